```python
import jax, jax.numpy as jnp
from jax import lax
import numpy as np

D_MODEL = 1024
BATCH = 8
SEQ = 8192
DEPTH = 2

GRID_W = 64
CTX_LEN = 256
CONV_WIDTH = D_MODEL
CONV_KERNEL = 31
N_HEADS = 16
N_KV_HEADS = 4
HEAD_DIM = 64
GROUP = N_HEADS // N_KV_HEADS
ATTN_WIDTH = N_HEADS * HEAD_DIM
KV_WIDTH = N_KV_HEADS * HEAD_DIM
Q_BLOCK = 128
ROPE_THETA = 10000.0
ROPE_AXIS_DIM = HEAD_DIM // 2
EPS = 1e-6
ATTN_SCALE = HEAD_DIM ** -0.5
IN_SPLITS = (2 * CONV_WIDTH, CONV_WIDTH, ATTN_WIDTH, KV_WIDTH, KV_WIDTH, ATTN_WIDTH, 2 * D_MODEL)
IN_WIDTH = sum(IN_SPLITS)
IN_OFFSETS = tuple(int(o) for o in np.cumsum(IN_SPLITS)[:-1])

kernel_name = "hybrid_conformer_gqa_prefix_dit_block"


def rms_norm(x, g):
    xf = x.astype(jnp.float32)
    y = xf * lax.rsqrt(jnp.mean(xf * xf, axis=-1, keepdims=True) + EPS)
    return (y * g.astype(jnp.float32)).astype(x.dtype)


def layer_norm(x, g, b):
    xf = x.astype(jnp.float32)
    mu = jnp.mean(xf, axis=-1, keepdims=True)
    var = jnp.mean(jnp.square(xf - mu), axis=-1, keepdims=True)
    y = (xf - mu) * lax.rsqrt(var + EPS)
    return (y * g.astype(jnp.float32) + b.astype(jnp.float32)).astype(x.dtype)


def axial_rope_tables(n_tokens):
    rows = n_tokens // GRID_W
    row = jnp.repeat(jnp.arange(rows, dtype=jnp.float32), GRID_W)
    col = jnp.tile(jnp.arange(GRID_W, dtype=jnp.float32), rows)
    inv_freq = ROPE_THETA ** (-jnp.arange(0, ROPE_AXIS_DIM, 2, dtype=jnp.float32) / ROPE_AXIS_DIM)
    ang = jnp.concatenate([row[:, None] * inv_freq, col[:, None] * inv_freq], axis=-1)
    return jnp.cos(ang), jnp.sin(ang)


def apply_rope(x, cos, sin):
    cos = cos.astype(x.dtype)[None, :, None, :]
    sin = sin.astype(x.dtype)[None, :, None, :]
    x1, x2 = jnp.split(x, 2, axis=-1)
    return jnp.concatenate([x1 * cos - x2 * sin, x2 * cos + x1 * sin], axis=-1)


def gqa_attend(qblk, k_all, v_all):
    s = jnp.einsum('bqkgd,bskd->bkgqs', qblk, k_all).astype(jnp.float32) * ATTN_SCALE
    p = jax.nn.softmax(s, axis=-1).astype(v_all.dtype)
    return jnp.einsum('bkgqs,bskd->bqkgd', p, v_all)


def latent_attention(q, k_all, v_all):
    b, n = q.shape[0], q.shape[1]
    nblk = n // Q_BLOCK
    qb = q.reshape(b, nblk, Q_BLOCK, N_KV_HEADS, GROUP, HEAD_DIM).transpose(1, 0, 2, 3, 4, 5)
    o = lax.map(lambda qblk: gqa_attend(qblk, k_all, v_all), qb)
    return o.transpose(1, 0, 2, 3, 4, 5).reshape(b, n, ATTN_WIDTH)


def conv_module(u, gate, conv_w, conv_b, ln_g, ln_b, w_conv_out):
    a, g = jnp.split(u, 2, axis=-1)
    y = a * jax.nn.sigmoid(g)
    pad = CONV_KERNEL // 2
    y = lax.conv_general_dilated(
        y, conv_w[:, None, :], window_strides=(1,), padding=[(pad, pad)],
        dimension_numbers=('NWC', 'WIO', 'NWC'), feature_group_count=CONV_WIDTH) + conv_b
    y = jax.nn.silu(layer_norm(y, ln_g, ln_b))
    y = y * jax.nn.silu(gate)
    return y @ w_conv_out


def split_proj(p):
    return jnp.split(p, IN_OFFSETS, axis=-1)


def merge_branches(y_conv, y_attn, gm, w_out):
    ga, gb = jnp.split(gm, 2, axis=-1)
    return (jax.nn.sigmoid(ga) * y_conv + jax.nn.sigmoid(gb) * y_attn) @ w_out


def _fwd_setup_inputs(seed: int = 0) -> dict:
    key = jax.random.key(seed)
    ks = jax.random.split(key, 20)
    f32 = jnp.float32

    def nrm(k, shape, scale):
        return jax.random.normal(k, shape, f32) * scale

    L, D = DEPTH, D_MODEL
    return {
        "x": nrm(ks[0], (BATCH, SEQ, D), 1.0),
        "c": nrm(ks[1], (BATCH, D), 1.0),
        "ctx": nrm(ks[2], (BATCH, CTX_LEN, D), 1.0),
        "c_ctx": nrm(ks[3], (D,), 1.0),
        "w_mod": nrm(ks[4], (L, D, 3 * D), 0.5 * D ** -0.5),
        "b_mod": nrm(ks[5], (L, 3 * D), 0.02),
        "g_pre": 1.0 + nrm(ks[6], (L, D), 0.02),
        "g_post": 1.0 + nrm(ks[7], (L, D), 0.02),
        "w_in": nrm(ks[8], (L, D, IN_WIDTH), D ** -0.5),
        "conv_w": nrm(ks[9], (L, CONV_KERNEL, CONV_WIDTH), CONV_KERNEL ** -0.5),
        "conv_b": nrm(ks[10], (L, CONV_WIDTH), 0.02),
        "ln_g": 1.0 + nrm(ks[11], (L, CONV_WIDTH), 0.02),
        "ln_b": nrm(ks[12], (L, CONV_WIDTH), 0.02),
        "w_conv_out": nrm(ks[13], (L, CONV_WIDTH, D), CONV_WIDTH ** -0.5),
        "q_norm_g": 1.0 + nrm(ks[14], (L, HEAD_DIM), 0.02),
        "k_norm_g": 1.0 + nrm(ks[15], (L, HEAD_DIM), 0.02),
        "w_attn_out": nrm(ks[16], (L, ATTN_WIDTH, D), ATTN_WIDTH ** -0.5),
        "w_out": nrm(ks[17], (L, D, D), D ** -0.5),
    }


def _fwd_reference(x, c, ctx, c_ctx, w_mod, b_mod, g_pre, g_post, w_in, conv_w, conv_b,
              ln_g, ln_b, w_conv_out, q_norm_g, k_norm_g, w_attn_out, w_out):
    b, n, _ = x.shape
    cos, sin = axial_rope_tables(n)

    for l in range(DEPTH):
        last = l == DEPTH - 1
        sh, sc, gt = jnp.split(jax.nn.silu(c) @ w_mod[l] + b_mod[l], 3, axis=-1)
        shc, scc, gtc = jnp.split(jax.nn.silu(c_ctx) @ w_mod[l] + b_mod[l], 3, axis=-1)

        h = rms_norm(x, g_pre[l]) * (1.0 + sc[:, None, :]) + sh[:, None, :]
        hc = rms_norm(ctx, g_pre[l]) * (1.0 + scc) + shc

        ua, gate_a, q, k, v, gate_b, gm = split_proj(h @ w_in[l])
        ua_c, gate_a_c, q_c, k_c, v_c, gate_b_c, gm_c = split_proj(hc @ w_in[l])

        q = apply_rope(rms_norm(q.reshape(b, n, N_HEADS, HEAD_DIM), q_norm_g[l]), cos, sin)
        k = apply_rope(rms_norm(k.reshape(b, n, N_KV_HEADS, HEAD_DIM), k_norm_g[l]), cos, sin)
        v = v.reshape(b, n, N_KV_HEADS, HEAD_DIM)
        k_c = rms_norm(k_c.reshape(b, CTX_LEN, N_KV_HEADS, HEAD_DIM), k_norm_g[l])
        v_c = v_c.reshape(b, CTX_LEN, N_KV_HEADS, HEAD_DIM)
        k_all = jnp.concatenate([k, k_c], axis=1)
        v_all = jnp.concatenate([v, v_c], axis=1)
        o = latent_attention(q, k_all, v_all)
        y_attn = (o * jax.nn.silu(gate_b)) @ w_attn_out[l]

        y_conv = conv_module(ua, gate_a, conv_w[l], conv_b[l], ln_g[l], ln_b[l], w_conv_out[l])

        out = merge_branches(y_conv, y_attn, gm, w_out[l])
        x_new = x + gt[:, None, :] * rms_norm(out, g_post[l])

        if not last:
            q_c = rms_norm(q_c.reshape(b, CTX_LEN, N_KV_HEADS, GROUP, HEAD_DIM), q_norm_g[l])
            o_c = gqa_attend(q_c, k_c, v_c).reshape(b, CTX_LEN, ATTN_WIDTH)
            y_attn_c = (o_c * jax.nn.silu(gate_b_c)) @ w_attn_out[l]
            y_conv_c = conv_module(ua_c, gate_a_c, conv_w[l], conv_b[l], ln_g[l], ln_b[l], w_conv_out[l])
            out_c = merge_branches(y_conv_c, y_attn_c, gm_c, w_out[l])
            ctx = ctx + gtc * rms_norm(out_c, g_post[l])
        x = x_new

    return x


import jax as _jax
import jax.numpy as _jnp

TWIN_FORMAT = 'train_step'
FWD_PARAMS = ['x', 'c', 'ctx', 'c_ctx', 'w_mod', 'b_mod', 'g_pre', 'g_post', 'w_in', 'conv_w', 'conv_b', 'ln_g', 'ln_b', 'w_conv_out', 'q_norm_g', 'k_norm_g', 'w_attn_out', 'w_out']
TWIN_WEIGHTS = ['c_ctx', 'w_mod', 'b_mod', 'g_pre', 'g_post', 'w_in', 'conv_w', 'conv_b', 'ln_g', 'ln_b', 'w_conv_out', 'q_norm_g', 'k_norm_g', 'w_attn_out', 'w_out']
TWIN_DIFF_INPUT = 'x'
TWIN_INPUTS = ['x', 'c', 'ctx', 'c_ctx', 'w_mod', 'b_mod', 'g_pre', 'g_post', 'w_in', 'conv_w', 'conv_b', 'ln_g', 'ln_b', 'w_conv_out', 'q_norm_g', 'k_norm_g', 'w_attn_out', 'w_out', 'loss_target', 'm_c_ctx', 'm_w_mod', 'm_b_mod', 'm_g_pre', 'm_g_post', 'm_w_in', 'm_conv_w', 'm_conv_b', 'm_ln_g', 'm_ln_b', 'm_w_conv_out', 'm_q_norm_g', 'm_k_norm_g', 'm_w_attn_out', 'm_w_out', 'v_c_ctx', 'v_w_mod', 'v_b_mod', 'v_g_pre', 'v_g_post', 'v_w_in', 'v_conv_w', 'v_conv_b', 'v_ln_g', 'v_ln_b', 'v_w_conv_out', 'v_q_norm_g', 'v_k_norm_g', 'v_w_attn_out', 'v_w_out']
TWIN_OUTPUTS = ['loss', 'grad_x', 'grad_c_ctx', 'grad_w_mod', 'grad_b_mod', 'grad_g_pre', 'grad_g_post', 'grad_w_in', 'grad_conv_w', 'grad_conv_b', 'grad_ln_g', 'grad_ln_b', 'grad_w_conv_out', 'grad_q_norm_g', 'grad_k_norm_g', 'grad_w_attn_out', 'grad_w_out', 'delta_c_ctx', 'delta_w_mod', 'delta_b_mod', 'delta_g_pre', 'delta_g_post', 'delta_w_in', 'delta_conv_w', 'delta_conv_b', 'delta_ln_g', 'delta_ln_b', 'delta_w_conv_out', 'delta_q_norm_g', 'delta_k_norm_g', 'delta_w_attn_out', 'delta_w_out', 'new_m_c_ctx', 'new_m_w_mod', 'new_m_b_mod', 'new_m_g_pre', 'new_m_g_post', 'new_m_w_in', 'new_m_conv_w', 'new_m_conv_b', 'new_m_ln_g', 'new_m_ln_b', 'new_m_w_conv_out', 'new_m_q_norm_g', 'new_m_k_norm_g', 'new_m_w_attn_out', 'new_m_w_out', 'new_v_c_ctx', 'new_v_w_mod', 'new_v_b_mod', 'new_v_g_pre', 'new_v_g_post', 'new_v_w_in', 'new_v_conv_w', 'new_v_conv_b', 'new_v_ln_g', 'new_v_ln_b', 'new_v_w_conv_out', 'new_v_q_norm_g', 'new_v_k_norm_g', 'new_v_w_attn_out', 'new_v_w_out']
TWIN_LEAF_KINDS = {'loss': 'loss', 'grad_x': 'grad_x', 'grad_c_ctx': 'grad_w', 'grad_w_mod': 'grad_w', 'grad_b_mod': 'grad_w', 'grad_g_pre': 'grad_w', 'grad_g_post': 'grad_w', 'grad_w_in': 'grad_w', 'grad_conv_w': 'grad_w', 'grad_conv_b': 'grad_w', 'grad_ln_g': 'grad_w', 'grad_ln_b': 'grad_w', 'grad_w_conv_out': 'grad_w', 'grad_q_norm_g': 'grad_w', 'grad_k_norm_g': 'grad_w', 'grad_w_attn_out': 'grad_w', 'grad_w_out': 'grad_w', 'delta_c_ctx': 'delta_w', 'delta_w_mod': 'delta_w', 'delta_b_mod': 'delta_w', 'delta_g_pre': 'delta_w', 'delta_g_post': 'delta_w', 'delta_w_in': 'delta_w', 'delta_conv_w': 'delta_w', 'delta_conv_b': 'delta_w', 'delta_ln_g': 'delta_w', 'delta_ln_b': 'delta_w', 'delta_w_conv_out': 'delta_w', 'delta_q_norm_g': 'delta_w', 'delta_k_norm_g': 'delta_w', 'delta_w_attn_out': 'delta_w', 'delta_w_out': 'delta_w', 'new_m_c_ctx': 'new_m', 'new_m_w_mod': 'new_m', 'new_m_b_mod': 'new_m', 'new_m_g_pre': 'new_m', 'new_m_g_post': 'new_m', 'new_m_w_in': 'new_m', 'new_m_conv_w': 'new_m', 'new_m_conv_b': 'new_m', 'new_m_ln_g': 'new_m', 'new_m_ln_b': 'new_m', 'new_m_w_conv_out': 'new_m', 'new_m_q_norm_g': 'new_m', 'new_m_k_norm_g': 'new_m', 'new_m_w_attn_out': 'new_m', 'new_m_w_out': 'new_m', 'new_v_c_ctx': 'new_v', 'new_v_w_mod': 'new_v', 'new_v_b_mod': 'new_v', 'new_v_g_pre': 'new_v', 'new_v_g_post': 'new_v', 'new_v_w_in': 'new_v', 'new_v_conv_w': 'new_v', 'new_v_conv_b': 'new_v', 'new_v_ln_g': 'new_v', 'new_v_ln_b': 'new_v', 'new_v_w_conv_out': 'new_v', 'new_v_q_norm_g': 'new_v', 'new_v_k_norm_g': 'new_v', 'new_v_w_attn_out': 'new_v', 'new_v_w_out': 'new_v'}


def _forward(args):
    return _fwd_reference(*[args[k] for k in FWD_PARAMS])


def _output_shape():
    def fwd():
        inp = _fwd_setup_inputs(0)
        return _fwd_reference(*[inp[k] for k in FWD_PARAMS])
    out = _jax.eval_shape(fwd)
    return out.shape, out.dtype

N_MICROBATCH = 1
ADAM_LR = 0.001
ADAM_B1 = 0.9
ADAM_B2 = 0.999
ADAM_EPS = 1e-08
ADAM_WD = 0.01
ADAM_STEP = 10
PER_EXAMPLE_BATCH_AXIS = {'x': 0, 'c': 0, 'ctx': 0, 'loss_target': 0}
SHARED_INPUTS = []
_WEIGHT_DTYPES = {'c_ctx': _jnp.float32, 'w_mod': _jnp.float32, 'b_mod': _jnp.float32, 'g_pre': _jnp.float32, 'g_post': _jnp.float32, 'w_in': _jnp.float32, 'conv_w': _jnp.float32, 'conv_b': _jnp.float32, 'ln_g': _jnp.float32, 'ln_b': _jnp.float32, 'w_conv_out': _jnp.float32, 'q_norm_g': _jnp.float32, 'k_norm_g': _jnp.float32, 'w_attn_out': _jnp.float32, 'w_out': _jnp.float32}
MOMENT_SCALE = {'c_ctx': 5.077065e-02, 'w_mod': 2.438904e+00, 'b_mod': 5.270998e+00, 'g_pre': 1.939815e-01, 'g_post': 6.658040e+00, 'w_in': 8.295197e-02, 'conv_w': 1.198664e-01, 'conv_b': 3.540777e-01, 'ln_g': 1.860229e-01, 'ln_b': 2.140793e-01, 'w_conv_out': 1.406626e-01, 'q_norm_g': 5.921712e-02, 'k_norm_g': 5.784178e-02, 'w_attn_out': 1.259002e-01, 'w_out': 1.909654e-01}


def _to_microbatches(a, axis):
    t = _jnp.moveaxis(a, axis, 0)
    t = t.reshape((N_MICROBATCH, t.shape[0] // N_MICROBATCH) + t.shape[1:])
    return _jnp.moveaxis(t, 1, axis + 1)


def setup_inputs(seed: int = 0) -> dict:
    inp = _fwd_setup_inputs(seed)
    key = _jax.random.fold_in(_jax.random.key(seed), 7919)
    shape, _ = _output_shape()
    out = dict(inp)
    out["loss_target"] = _jax.random.normal(_jax.random.fold_in(key, 0), shape, _jnp.float32)
    for i, name in enumerate(TWIN_WEIGHTS):
        w = inp[name].astype(_jnp.float32)
        if MOMENT_SCALE is None:
            s = _jnp.sqrt(_jnp.mean(_jnp.square(w)) + 1e-30)
        else:
            s = MOMENT_SCALE[name]
        km, kv = _jax.random.split(_jax.random.fold_in(key, i + 1))
        out[name] = w
        out["m_" + name] = s * _jax.random.normal(km, w.shape, _jnp.float32)
        out["v_" + name] = (s * s) * _jax.random.uniform(kv, w.shape, _jnp.float32, 0.5, 1.5)
    if N_MICROBATCH > 1:
        for name, axis in PER_EXAMPLE_BATCH_AXIS.items():
            out[name] = _to_microbatches(out[name], axis)
    return {'x': out['x'], 'c': out['c'], 'ctx': out['ctx'], 'c_ctx': out['c_ctx'], 'w_mod': out['w_mod'], 'b_mod': out['b_mod'], 'g_pre': out['g_pre'], 'g_post': out['g_post'], 'w_in': out['w_in'], 'conv_w': out['conv_w'], 'conv_b': out['conv_b'], 'ln_g': out['ln_g'], 'ln_b': out['ln_b'], 'w_conv_out': out['w_conv_out'], 'q_norm_g': out['q_norm_g'], 'k_norm_g': out['k_norm_g'], 'w_attn_out': out['w_attn_out'], 'w_out': out['w_out'], 'loss_target': out['loss_target'], 'm_c_ctx': out['m_c_ctx'], 'm_w_mod': out['m_w_mod'], 'm_b_mod': out['m_b_mod'], 'm_g_pre': out['m_g_pre'], 'm_g_post': out['m_g_post'], 'm_w_in': out['m_w_in'], 'm_conv_w': out['m_conv_w'], 'm_conv_b': out['m_conv_b'], 'm_ln_g': out['m_ln_g'], 'm_ln_b': out['m_ln_b'], 'm_w_conv_out': out['m_w_conv_out'], 'm_q_norm_g': out['m_q_norm_g'], 'm_k_norm_g': out['m_k_norm_g'], 'm_w_attn_out': out['m_w_attn_out'], 'm_w_out': out['m_w_out'], 'v_c_ctx': out['v_c_ctx'], 'v_w_mod': out['v_w_mod'], 'v_b_mod': out['v_b_mod'], 'v_g_pre': out['v_g_pre'], 'v_g_post': out['v_g_post'], 'v_w_in': out['v_w_in'], 'v_conv_w': out['v_conv_w'], 'v_conv_b': out['v_conv_b'], 'v_ln_g': out['v_ln_g'], 'v_ln_b': out['v_ln_b'], 'v_w_conv_out': out['v_w_conv_out'], 'v_q_norm_g': out['v_q_norm_g'], 'v_k_norm_g': out['v_k_norm_g'], 'v_w_attn_out': out['v_w_attn_out'], 'v_w_out': out['v_w_out']}


def _loss(weights, diff, rest, loss_target):
    with _jax.named_scope("forward"):
        args = {**rest, TWIN_DIFF_INPUT: diff, **{k: w.astype(_WEIGHT_DTYPES[k]) for k, w in weights.items()}}
        y = _forward(args)
    with _jax.named_scope("loss_head"):
        err = _jnp.square(y.astype(_jnp.float32) - loss_target)
        return 0.5 * _jnp.sum(_jnp.mean(err, axis=-1)) if err.ndim else 0.5 * err


def _adamw(w, g, m, v):
    m = ADAM_B1 * m + (1.0 - ADAM_B1) * g
    v = ADAM_B2 * v + (1.0 - ADAM_B2) * _jnp.square(g)
    m_hat = m / (1.0 - ADAM_B1 ** ADAM_STEP)
    v_hat = v / (1.0 - ADAM_B2 ** ADAM_STEP)
    delta = -ADAM_LR * (m_hat / (_jnp.sqrt(v_hat) + ADAM_EPS) + ADAM_WD * w)
    return delta, m, v


def reference(x, c, ctx, c_ctx, w_mod, b_mod, g_pre, g_post, w_in, conv_w, conv_b, ln_g, ln_b, w_conv_out, q_norm_g, k_norm_g, w_attn_out, w_out, loss_target, m_c_ctx, m_w_mod, m_b_mod, m_g_pre, m_g_post, m_w_in, m_conv_w, m_conv_b, m_ln_g, m_ln_b, m_w_conv_out, m_q_norm_g, m_k_norm_g, m_w_attn_out, m_w_out, v_c_ctx, v_w_mod, v_b_mod, v_g_pre, v_g_post, v_w_in, v_conv_w, v_conv_b, v_ln_g, v_ln_b, v_w_conv_out, v_q_norm_g, v_k_norm_g, v_w_attn_out, v_w_out):
    given = dict(x=x, c=c, ctx=ctx, c_ctx=c_ctx, w_mod=w_mod, b_mod=b_mod, g_pre=g_pre, g_post=g_post, w_in=w_in, conv_w=conv_w, conv_b=conv_b, ln_g=ln_g, ln_b=ln_b, w_conv_out=w_conv_out, q_norm_g=q_norm_g, k_norm_g=k_norm_g, w_attn_out=w_attn_out, w_out=w_out, loss_target=loss_target, m_c_ctx=m_c_ctx, m_w_mod=m_w_mod, m_b_mod=m_b_mod, m_g_pre=m_g_pre, m_g_post=m_g_post, m_w_in=m_w_in, m_conv_w=m_conv_w, m_conv_b=m_conv_b, m_ln_g=m_ln_g, m_ln_b=m_ln_b, m_w_conv_out=m_w_conv_out, m_q_norm_g=m_q_norm_g, m_k_norm_g=m_k_norm_g, m_w_attn_out=m_w_attn_out, m_w_out=m_w_out, v_c_ctx=v_c_ctx, v_w_mod=v_w_mod, v_b_mod=v_b_mod, v_g_pre=v_g_pre, v_g_post=v_g_post, v_w_in=v_w_in, v_conv_w=v_conv_w, v_conv_b=v_conv_b, v_ln_g=v_ln_g, v_ln_b=v_ln_b, v_w_conv_out=v_w_conv_out, v_q_norm_g=v_q_norm_g, v_k_norm_g=v_k_norm_g, v_w_attn_out=v_w_attn_out, v_w_out=v_w_out)
    weights = {n: given[n] for n in TWIN_WEIGHTS}
    shared = {n: given[n] for n in SHARED_INPUTS}
    per_example = {n: given[n] for n in ['x', 'c', 'ctx']}
    grad_fn = _jax.value_and_grad(_loss, argnums=(0, 1))

    def one_microbatch(ex, loss_target):
        ex = dict(ex)
        diff = ex.pop(TWIN_DIFF_INPUT)
        return grad_fn(weights, diff, {**shared, **ex}, loss_target)

    if N_MICROBATCH == 1:
        loss, (grad_w, grad_x) = one_microbatch(per_example, given["loss_target"])
    else:
        def body(carry, xs):
            loss_sum, grad_sum = carry
            l_k, (gw_k, gx_k) = one_microbatch(xs[0], xs[1])
            with _jax.named_scope("update"):
                return (loss_sum + l_k, _jax.tree.map(_jnp.add, grad_sum, gw_k)), gx_k

        init = (_jnp.zeros((), _jnp.float32), _jax.tree.map(_jnp.zeros_like, weights))
        (loss, grad_w), grad_x = _jax.lax.scan(body, init, (per_example, given["loss_target"]))
    with _jax.named_scope("update"):
        delta_w, new_m, new_v = {}, {}, {}
        for n in TWIN_WEIGHTS:
            delta_w[n], new_m[n], new_v[n] = _adamw(weights[n], grad_w[n], given["m_" + n], given["v_" + n])
    return (loss, grad_x, *[grad_w[n] for n in TWIN_WEIGHTS], *[delta_w[n] for n in TWIN_WEIGHTS],
            *[new_m[n] for n in TWIN_WEIGHTS], *[new_v[n] for n in TWIN_WEIGHTS])
```

```python
import functools
from typing import NamedTuple

import jax
import jax.numpy as jnp
import numpy as np
from jax import lax
from jax.experimental import pallas as pl
from jax.experimental.pallas import tpu as pltpu

F32 = jnp.float32
BF16 = jnp.bfloat16
EPS = 1e-6
GRID_W = 64
ROPE_THETA = 10000.0
CONV_KERNEL = 31
CONV_PAD = CONV_KERNEL // 2
HALO = 16
LANES = 128
ADAM_LR, ADAM_B1, ADAM_B2, ADAM_EPS, ADAM_WD, ADAM_STEP = 0.001, 0.9, 0.999, 1e-08, 0.01, 10
VMEM_LIMIT = 56 * 1024 * 1024
MESH = pl.DeviceIdType.MESH
N_CHIPS = 4
N_DEV = 8


class Dims(NamedTuple):
    N: int
    CTX: int
    R: int
    D: int
    C: int
    A: int
    KVW: int
    HD: int
    NH: int
    NKV: int
    G: int
    W: int
    TR: int


def _pick(n, prefs):
    for p in prefs:
        if n % p == 0:
            return p
    raise ValueError(f"no tile for {n} in {prefs}")


def _params(sem):
    return pltpu.CompilerParams(dimension_semantics=sem, vmem_limit_bytes=VMEM_LIMIT)


def _rows(tr, w, col=0):
    return pl.BlockSpec((tr, w), lambda i, col=col: (i, col))


def _whole(shape):
    return pl.BlockSpec(shape, lambda *_: (0,) * len(shape))


def _col(off, w):
    assert off % w == 0, (off, w)
    return off // w


_DN = {"nn": (((1,), (0,)), ((), ())), "nt": (((1,), (1,)), ((), ())), "tn": (((0,), (0,)), ((), ()))}


def _matmul(a, b, mode, name, out_dtype=F32):
    if mode == "nn":
        (m, k), n = a.shape, b.shape[1]
    elif mode == "nt":
        (m, k), n = a.shape, b.shape[0]
    else:
        (k, m), n = a.shape, b.shape[1]
    tm = _pick(m, (768, 512, 256, 128, 8))
    tn = _pick(n, (1280, 1024, 768, 512, 256, 128))
    tk = _pick(k, (1536, 1280, 1024, 768, 512, 256, 128))
    nk = k // tk
    dn = _DN[mode]

    def body(a_ref, b_ref, o_ref, acc_ref):
        kk = pl.program_id(2)

        @pl.when(kk == 0)
        def _():
            acc_ref[...] = jnp.zeros_like(acc_ref)

        acc_ref[...] += lax.dot_general(a_ref[...], b_ref[...], dn, preferred_element_type=F32)

        @pl.when(kk == nk - 1)
        def _():
            o_ref[...] = acc_ref[...].astype(o_ref.dtype)

    a_spec = pl.BlockSpec((tk, tm), lambda i, j, kk: (kk, i)) if mode == "tn" else pl.BlockSpec((tm, tk), lambda i, j, kk: (i, kk))
    b_spec = pl.BlockSpec((tn, tk), lambda i, j, kk: (j, kk)) if mode == "nt" else pl.BlockSpec((tk, tn), lambda i, j, kk: (kk, j))
    return pl.pallas_call(
        body, name=name, grid=(m // tm, n // tn, nk),
        in_specs=[a_spec, b_spec], out_specs=pl.BlockSpec((tm, tn), lambda i, j, kk: (i, j)),
        out_shape=jax.ShapeDtypeStruct((m, n), out_dtype),
        scratch_shapes=[pltpu.VMEM((tm, tn), F32)],
        compiler_params=_params(("parallel", "parallel", "arbitrary")),
    )(a, b)


def _rms(x, g):
    return x * lax.rsqrt(jnp.mean(x * x, axis=-1, keepdims=True) + EPS) * g


def _prenorm_fn(x, g, sc, sh):
    return _rms(x, g) * (1.0 + sc) + sh


def _post_fn(outp, g, gt):
    return gt * _rms(outp, g)


def _convact_fn(y1, gate, ln_g, ln_b):
    mu = jnp.mean(y1, axis=-1, keepdims=True)
    var = jnp.mean(jnp.square(y1 - mu), axis=-1, keepdims=True)
    y = (y1 - mu) * lax.rsqrt(var + EPS) * ln_g + ln_b
    return jax.nn.silu(y) * jax.nn.silu(gate)


def _dsilu(x):
    s = jax.nn.sigmoid(x)
    return s * (1.0 + x * (1.0 - s))


def _seg_select(is_ctx, mod_ref, lo, w):
    return jnp.where(is_ctx, mod_ref[1:2, lo:lo + w], mod_ref[0:1, lo:lo + w])


def _seg_rows(is_ctx, v):
    row = lax.broadcasted_iota(jnp.int32, (8, v.shape[1]), 0)
    return jnp.where(row == is_ctx.astype(jnp.int32), v, 0.0)


def _mod_fwd(cv, w_mod_l, b_mod_l):
    def body(cv_ref, w_ref, b_ref, o_ref):
        s = jax.nn.silu(cv_ref[...]).astype(BF16)
        o_ref[...] = jnp.dot(s, w_ref[...], preferred_element_type=F32) + b_ref[...]

    return pl.pallas_call(
        body, name="mod_fwd", out_shape=jax.ShapeDtypeStruct((8, w_mod_l.shape[1]), F32),
        compiler_params=pltpu.CompilerParams(vmem_limit_bytes=VMEM_LIMIT),
    )(cv, w_mod_l, b_mod_l)


def _mod_bwd(cv, w_mod_l, dmod):
    d = cv.shape[1]

    def body(cv_ref, w_ref, dm_ref, dw_ref, db_ref, dcv_ref):
        cvv = cv_ref[...]
        s = jax.nn.silu(cvv).astype(BF16)
        dm = dm_ref[...]
        dmb = dm.astype(BF16)
        dw_ref[...] = lax.dot_general(s, dmb, _DN["tn"], preferred_element_type=F32)
        db_ref[...] = jnp.sum(dm, axis=0, keepdims=True)
        ds = lax.dot_general(dmb, w_ref[...], _DN["nt"], preferred_element_type=F32)
        dcv_ref[...] = ds * _dsilu(cvv)

    return pl.pallas_call(
        body, name="mod_bwd",
        out_shape=(jax.ShapeDtypeStruct(w_mod_l.shape, F32), jax.ShapeDtypeStruct((1, w_mod_l.shape[1]), F32),
                   jax.ShapeDtypeStruct((8, d), F32)),
        compiler_params=pltpu.CompilerParams(vmem_limit_bytes=VMEM_LIMIT),
    )(cv, w_mod_l, dmod)


def _prenorm(dm, xc, g_pre_l, mod):
    tr, d, n = dm.TR, dm.D, dm.N

    def body(x_ref, g_ref, mod_ref, h_ref):
        is_ctx = pl.program_id(0) * tr >= n
        sh = _seg_select(is_ctx, mod_ref, 0, d)
        sc = _seg_select(is_ctx, mod_ref, d, d)
        h_ref[...] = _prenorm_fn(x_ref[...], g_ref[...], sc, sh).astype(BF16)

    return pl.pallas_call(
        body, name="prenorm", grid=(dm.R // tr,),
        in_specs=[_rows(tr, d), _whole((1, d)), _whole((8, 3 * d))], out_specs=_rows(tr, d),
        out_shape=jax.ShapeDtypeStruct((dm.R, d), BF16), compiler_params=_params(("parallel",)),
    )(xc, g_pre_l, mod)


def _prenorm_bwd(dm, xc, g_pre_l, mod, dh, dxn):
    tr, d, n = dm.TR, dm.D, dm.N

    def body(x_ref, g_ref, mod_ref, dh_ref, dxn_ref, dx_ref, dg_ref, dss_ref):
        i = pl.program_id(0)
        is_ctx = i * tr >= n
        sh = _seg_select(is_ctx, mod_ref, 0, d)
        sc = _seg_select(is_ctx, mod_ref, d, d)
        _, vjp = jax.vjp(_prenorm_fn, x_ref[...], g_ref[...], sc, sh)
        dx, dg, dsc, dsh = vjp(dh_ref[...])
        dx_ref[...] = dxn_ref[...] + dx

        @pl.when(i == 0)
        def _():
            dg_ref[...] = jnp.zeros_like(dg_ref)
            dss_ref[...] = jnp.zeros_like(dss_ref)

        dg_ref[...] += dg
        dss_ref[...] += _seg_rows(is_ctx, jnp.concatenate([dsh, dsc], axis=1))

    return pl.pallas_call(
        body, name="prenorm_bwd", grid=(dm.R // tr,),
        in_specs=[_rows(tr, d), _whole((1, d)), _whole((8, 3 * d)), _rows(tr, d), _rows(tr, d)],
        out_specs=(_rows(tr, d), _whole((1, d)), _whole((8, 2 * d))),
        out_shape=(jax.ShapeDtypeStruct((dm.R, d), F32), jax.ShapeDtypeStruct((1, d), F32),
                   jax.ShapeDtypeStruct((8, 2 * d), F32)),
        compiler_params=_params(("arbitrary",)),
    )(xc, g_pre_l, mod, dh, dxn)


def _post(dm, xc, outp, g_post_l, mod):
    tr, d, n = dm.TR, dm.D, dm.N

    def body(x_ref, o_ref, g_ref, mod_ref, y_ref):
        is_ctx = pl.program_id(0) * tr >= n
        gt = _seg_select(is_ctx, mod_ref, 2 * d, d)
        y_ref[...] = x_ref[...] + _post_fn(o_ref[...], g_ref[...], gt)

    return pl.pallas_call(
        body, name="post", grid=(dm.R // tr,),
        in_specs=[_rows(tr, d), _rows(tr, d), _whole((1, d)), _whole((8, 3 * d))], out_specs=_rows(tr, d),
        out_shape=jax.ShapeDtypeStruct((dm.R, d), F32), compiler_params=_params(("parallel",)),
    )(xc, outp, g_post_l, mod)


def _post_bwd(dm, outp, g_post_l, mod, dxn):
    tr, d, n = dm.TR, dm.D, dm.N

    def body(o_ref, g_ref, mod_ref, dxn_ref, do_ref, dg_ref, dgt_ref):
        i = pl.program_id(0)
        is_ctx = i * tr >= n
        gt = _seg_select(is_ctx, mod_ref, 2 * d, d)
        _, vjp = jax.vjp(_post_fn, o_ref[...], g_ref[...], gt)
        do, dg, dgt = vjp(dxn_ref[...])
        do_ref[...] = do.astype(BF16)

        @pl.when(i == 0)
        def _():
            dg_ref[...] = jnp.zeros_like(dg_ref)
            dgt_ref[...] = jnp.zeros_like(dgt_ref)

        dg_ref[...] += dg
        dgt_ref[...] += _seg_rows(is_ctx, dgt)

    return pl.pallas_call(
        body, name="post_bwd", grid=(dm.R // tr,),
        in_specs=[_rows(tr, d), _whole((1, d)), _whole((8, 3 * d)), _rows(tr, d)],
        out_specs=(_rows(tr, d), _whole((1, d)), _whole((8, d))),
        out_shape=(jax.ShapeDtypeStruct((dm.R, d), BF16), jax.ShapeDtypeStruct((1, d), F32),
                   jax.ShapeDtypeStruct((8, d), F32)),
        compiler_params=_params(("arbitrary",)),
    )(outp, g_post_l, mod, dxn)


def _group_sum(x, bd):
    hi = x.astype(BF16)
    lo = (x - hi.astype(F32)).astype(BF16)
    return jnp.dot(hi, bd, preferred_element_type=F32) + jnp.dot(lo, bd, preferred_element_type=F32)


def _swap_halves(x):
    lane = lax.broadcasted_iota(jnp.int32, x.shape, 1)
    return jnp.where((lane & 32) == 0, pltpu.roll(x, LANES - 32, 1), pltpu.roll(x, 32, 1))


def _qk_prep(dm, p, cos, sins, gq, gk, bd):
    tr, hd = dm.TR, dm.HD
    oq, ok, ov = 3 * dm.C, 3 * dm.C + dm.A, 3 * dm.C + dm.A + dm.KVW
    scale = hd ** -0.5

    def body(q_ref, k_ref, v_ref, cos_ref, sin_ref, gq_ref, gk_ref, bd_ref, qh_ref, kh_ref, vh_ref):
        cosv, sinv, bdv = cos_ref[...], sin_ref[...], bd_ref[...]

        def chunk(x, g):
            rstd = lax.rsqrt(_group_sum(x * x, bdv) * (1.0 / hd) + EPS)
            xn = x * rstd * g
            return xn * cosv + _swap_halves(xn) * sinv

        for c in range(dm.A // LANES):
            y = chunk(q_ref[:, c * LANES:(c + 1) * LANES], gq_ref[...]) * scale
            qh_ref[2 * c] = y[:, :hd].astype(BF16)
            qh_ref[2 * c + 1] = y[:, hd:].astype(BF16)
        for c in range(dm.KVW // LANES):
            y = chunk(k_ref[:, c * LANES:(c + 1) * LANES], gk_ref[...])
            kh_ref[2 * c] = y[:, :hd].astype(BF16)
            kh_ref[2 * c + 1] = y[:, hd:].astype(BF16)
            v = v_ref[:, c * LANES:(c + 1) * LANES]
            vh_ref[2 * c] = v[:, :hd].astype(BF16)
            vh_ref[2 * c + 1] = v[:, hd:].astype(BF16)

    hspec = lambda nh: pl.BlockSpec((nh, tr, hd), lambda i: (0, i, 0))
    return pl.pallas_call(
        body, name="qk_prep", grid=(dm.R // tr,),
        in_specs=[_rows(tr, dm.A, _col(oq, dm.A)), _rows(tr, dm.KVW, _col(ok, dm.KVW)), _rows(tr, dm.KVW, _col(ov, dm.KVW)),
                  _rows(tr, LANES), _rows(tr, LANES), _whole((1, LANES)), _whole((1, LANES)), _whole((LANES, LANES))],
        out_specs=(hspec(dm.NH), hspec(dm.NKV), hspec(dm.NKV)),
        out_shape=(jax.ShapeDtypeStruct((dm.NH, dm.R, hd), BF16), jax.ShapeDtypeStruct((dm.NKV, dm.R, hd), BF16),
                   jax.ShapeDtypeStruct((dm.NKV, dm.R, hd), BF16)),
        compiler_params=_params(("parallel",)),
    )(p, p, p, cos, sins, gq, gk, bd)


def _qk_bwd(dm, p, dq, dkh, dvh, cos, sins, gq, gk, bd):
    tr, hd = dm.TR, dm.HD
    oq, ok = 3 * dm.C, 3 * dm.C + dm.A
    scale = hd ** -0.5

    def body(q_ref, k_ref, dq_ref, dkh_ref, dvh_ref, cos_ref, sin_ref, gq_ref, gk_ref, bd_ref,
             dpq_ref, dpkv_ref, dgq_ref, dgk_ref):
        i = pl.program_id(0)
        cosv, sinv, bdv = cos_ref[...], sin_ref[...], bd_ref[...]

        def chunk(x, g, dy):
            rstd = lax.rsqrt(_group_sum(x * x, bdv) * (1.0 / hd) + EPS)
            xhat = x * rstd
            dxn = dy * cosv + _swap_halves(dy * sinv)
            dxhat = dxn * g
            dx = rstd * (dxhat - xhat * (_group_sum(dxhat * xhat, bdv) * (1.0 / hd)))
            return dx, jnp.sum(dxn * xhat, axis=0, keepdims=True)

        @pl.when(i == 0)
        def _():
            dgq_ref[...] = jnp.zeros_like(dgq_ref)
            dgk_ref[...] = jnp.zeros_like(dgk_ref)

        dgq = jnp.zeros((1, LANES), F32)
        for c in range(dm.A // LANES):
            sl = slice(c * LANES, (c + 1) * LANES)
            dx, dg = chunk(q_ref[:, sl], gq_ref[...], dq_ref[:, sl] * scale)
            dpq_ref[:, sl] = dx.astype(BF16)
            dgq = dgq + dg
        dgq_ref[...] += dgq + pltpu.roll(dgq, hd, 1)
        dgk = jnp.zeros((1, LANES), F32)
        for c in range(dm.KVW // LANES):
            sl = slice(c * LANES, (c + 1) * LANES)
            dy = jnp.concatenate([dkh_ref[2 * c], dkh_ref[2 * c + 1]], axis=1)
            dx, dg = chunk(k_ref[:, sl], gk_ref[...], dy)
            dpkv_ref[:, sl] = dx.astype(BF16)
            dgk = dgk + dg
            dv = jnp.concatenate([dvh_ref[2 * c], dvh_ref[2 * c + 1]], axis=1)
            dpkv_ref[:, dm.KVW + c * LANES:dm.KVW + (c + 1) * LANES] = dv.astype(BF16)
        dgk_ref[...] += dgk + pltpu.roll(dgk, hd, 1)

    hspec = pl.BlockSpec((dm.NKV, tr, hd), lambda i: (0, i, 0))
    return pl.pallas_call(
        body, name="qk_bwd", grid=(dm.R // tr,),
        in_specs=[_rows(tr, dm.A, _col(oq, dm.A)), _rows(tr, dm.KVW, _col(ok, dm.KVW)), _rows(tr, dm.A), hspec, hspec,
                  _rows(tr, LANES), _rows(tr, LANES), _whole((1, LANES)), _whole((1, LANES)), _whole((LANES, LANES))],
        out_specs=(_rows(tr, dm.A), _rows(tr, 2 * dm.KVW), _whole((1, LANES)), _whole((1, LANES))),
        out_shape=(jax.ShapeDtypeStruct((dm.R, dm.A), BF16), jax.ShapeDtypeStruct((dm.R, 2 * dm.KVW), BF16),
                   jax.ShapeDtypeStruct((1, LANES), F32), jax.ShapeDtypeStruct((1, LANES), F32)),
        compiler_params=_params(("arbitrary",)),
    )(p, p, dq, dkh, dvh, cos, sins, gq, gk, bd)


def _flash_fwd(dm, qh, kh, vh):
    tq = tk = dm.TR
    g, hd = dm.G, dm.HD
    nq, nk = dm.R // tq, dm.R // tk
    nq_lat, nk_lat = dm.N // tq, dm.N // tk

    def body(q_ref, k_ref, v_ref, o_ref, lse_ref, m_ref, l_ref, acc_ref):
        qi, ki = pl.program_id(1), pl.program_id(2)

        @pl.when(ki == 0)
        def _():
            m_ref[...] = jnp.full_like(m_ref, -1e30)
            l_ref[...] = jnp.zeros_like(l_ref)
            acc_ref[...] = jnp.zeros_like(acc_ref)

        @pl.when(jnp.logical_or(qi < nq_lat, ki >= nk_lat))
        def _():
            q = q_ref[...].reshape(g * tq, hd)
            s = lax.dot_general(q, k_ref[0], _DN["nt"], preferred_element_type=F32)
            m_old = m_ref[...]
            m_new = jnp.maximum(m_old, jnp.max(s, axis=-1, keepdims=True))
            p = jnp.exp(s - m_new)
            alpha = jnp.exp(m_old - m_new)
            l_ref[...] = alpha * l_ref[...] + jnp.sum(p, axis=-1, keepdims=True)
            acc_ref[...] = alpha * acc_ref[...] + jnp.dot(p.astype(BF16), v_ref[0], preferred_element_type=F32)
            m_ref[...] = m_new

        @pl.when(ki == nk - 1)
        def _():
            l = l_ref[...]
            o = acc_ref[...] / l
            for gg in range(g):
                o_ref[:, gg * hd:(gg + 1) * hd] = o[gg * tq:(gg + 1) * tq]
            lse_ref[...] = (m_ref[...] + jnp.log(l)).reshape(g, tq, 1)

    return pl.pallas_call(
        body, name="flash_fwd", grid=(dm.NKV, nq, nk),
        in_specs=[pl.BlockSpec((g, tq, hd), lambda h, qi, ki: (h, qi, 0)),
                  pl.BlockSpec((1, tk, hd), lambda h, qi, ki: (h, ki, 0)),
                  pl.BlockSpec((1, tk, hd), lambda h, qi, ki: (h, ki, 0))],
        out_specs=(pl.BlockSpec((tq, g * hd), lambda h, qi, ki: (qi, h)),
                   pl.BlockSpec((g, tq, 1), lambda h, qi, ki: (h, qi, 0))),
        out_shape=(jax.ShapeDtypeStruct((dm.R, dm.A), F32), jax.ShapeDtypeStruct((dm.NH, dm.R, 1), F32)),
        scratch_shapes=[pltpu.VMEM((g * tq, 1), F32), pltpu.VMEM((g * tq, 1), F32), pltpu.VMEM((g * tq, hd), F32)],
        compiler_params=_params(("parallel", "parallel", "arbitrary")),
    )(qh, kh, vh)


def _flash_bwd(dm, qh, kh, vh, doh, lse, delta):
    tq = tk = dm.TR
    g, hd = dm.G, dm.HD
    nq, nk = dm.R // tq, dm.R // tk
    nq_lat, nk_lat = dm.N // tq, dm.N // tk

    def body(q_ref, k_ref, v_ref, do_ref, lse_ref, dl_ref, dq_ref, dk_ref, dv_ref, dk_acc, dv_acc):
        ki, qi = pl.program_id(1), pl.program_id(2)

        @pl.when(jnp.logical_and(ki == 0, qi == 0))
        def _():
            dq_ref[...] = jnp.zeros_like(dq_ref)

        @pl.when(qi == 0)
        def _():
            dk_acc[...] = jnp.zeros_like(dk_acc)
            dv_acc[...] = jnp.zeros_like(dv_acc)

        @pl.when(jnp.logical_or(qi < nq_lat, ki >= nk_lat))
        def _():
            q = q_ref[...].reshape(g * tq, hd)
            do = do_ref[...].reshape(g * tq, hd)
            k, v = k_ref[0], v_ref[0]
            s = lax.dot_general(q, k, _DN["nt"], preferred_element_type=F32)
            p = jnp.exp(s - lse_ref[...].reshape(g * tq, 1))
            dv_acc[...] += lax.dot_general(p.astype(BF16), do, _DN["tn"], preferred_element_type=F32)
            dp = lax.dot_general(do, v, _DN["nt"], preferred_element_type=F32)
            ds = (p * (dp - dl_ref[...].reshape(g * tq, 1))).astype(BF16)
            dk_acc[...] += lax.dot_general(ds, q, _DN["tn"], preferred_element_type=F32)
            dqs = jnp.dot(ds, k, preferred_element_type=F32)
            r0 = pl.multiple_of(qi * tq, tq)
            for gg in range(g):
                dq_ref[pl.ds(r0, tq), gg * hd:(gg + 1) * hd] += dqs[gg * tq:(gg + 1) * tq]

        @pl.when(qi == nq - 1)
        def _():
            dk_ref[0] = dk_acc[...]
            dv_ref[0] = dv_acc[...]

    qspec = pl.BlockSpec((g, tq, hd), lambda h, ki, qi: (h, qi, 0))
    kspec = pl.BlockSpec((1, tk, hd), lambda h, ki, qi: (h, ki, 0))
    sspec = pl.BlockSpec((g, tq, 1), lambda h, ki, qi: (h, qi, 0))
    return pl.pallas_call(
        body, name="flash_bwd", grid=(dm.NKV, nk, nq),
        in_specs=[qspec, kspec, kspec, qspec, sspec, sspec],
        out_specs=(pl.BlockSpec((dm.R, g * hd), lambda h, ki, qi: (0, h)), kspec, kspec),
        out_shape=(jax.ShapeDtypeStruct((dm.R, dm.A), F32), jax.ShapeDtypeStruct((dm.NKV, dm.R, hd), F32),
                   jax.ShapeDtypeStruct((dm.NKV, dm.R, hd), F32)),
        scratch_shapes=[pltpu.VMEM((tk, hd), F32), pltpu.VMEM((tk, hd), F32)],
        compiler_params=_params(("parallel", "arbitrary", "arbitrary")),
    )(qh, kh, vh, doh, lse, delta)


def _gate_b_specs(dm):
    ogb = 3 * dm.C + dm.A + 2 * dm.KVW
    half = dm.A // 2
    return [_rows(dm.TR, half, _col(ogb, half)), _rows(dm.TR, half, _col(ogb, half) + 1)]


def _attn_gate(dm, o, p):
    tr = dm.TR

    def body(o_ref, g0_ref, g1_ref, za_ref):
        gate = jnp.concatenate([g0_ref[...], g1_ref[...]], axis=1)
        za_ref[...] = (o_ref[...] * jax.nn.silu(gate)).astype(BF16)

    return pl.pallas_call(
        body, name="attn_gate", grid=(dm.R // tr,),
        in_specs=[_rows(tr, dm.A)] + _gate_b_specs(dm), out_specs=_rows(tr, dm.A),
        out_shape=jax.ShapeDtypeStruct((dm.R, dm.A), BF16), compiler_params=_params(("parallel",)),
    )(o, p, p)


def _attn_gate_bwd(dm, dza, o, p):
    tr, hd = dm.TR, dm.HD

    def body(dza_ref, o_ref, g0_ref, g1_ref, doh_ref, dl_ref, dgb_ref):
        gate = jnp.concatenate([g0_ref[...], g1_ref[...]], axis=1)
        dza, o = dza_ref[...], o_ref[...]
        do = dza * jax.nn.silu(gate)
        dgb_ref[...] = (dza * o * _dsilu(gate)).astype(BF16)
        prod = do * o
        for h in range(dm.NH):
            sl = slice(h * hd, (h + 1) * hd)
            doh_ref[h] = do[:, sl].astype(BF16)
            dl_ref[h] = jnp.sum(prod[:, sl], axis=-1, keepdims=True)

    return pl.pallas_call(
        body, name="attn_gate_bwd", grid=(dm.R // tr,),
        in_specs=[_rows(tr, dm.A), _rows(tr, dm.A)] + _gate_b_specs(dm),
        out_specs=(pl.BlockSpec((dm.NH, tr, hd), lambda i: (0, i, 0)), pl.BlockSpec((dm.NH, tr, 1), lambda i: (0, i, 0)),
                   _rows(tr, dm.A)),
        out_shape=(jax.ShapeDtypeStruct((dm.NH, dm.R, hd), BF16), jax.ShapeDtypeStruct((dm.NH, dm.R, 1), F32),
                   jax.ShapeDtypeStruct((dm.R, dm.A), BF16)),
        compiler_params=_params(("parallel",)),
    )(dza, o, p, p)


def _halo_specs(dm, col):
    per = dm.TR // HALO
    last = dm.R // HALO - 1
    prev = pl.BlockSpec((HALO, dm.C), lambda i: (jnp.maximum(i * per - 1, 0), col))
    nxt = pl.BlockSpec((HALO, dm.C), lambda i: (jnp.minimum((i + 1) * per, last), col))
    return prev, nxt


def _seg_valid(dm, i):
    tr = dm.TR
    r = i * tr - HALO + lax.broadcasted_iota(jnp.int32, (tr + 2 * HALO, 1), 0)
    is_ctx = i * tr >= dm.N
    lo = jnp.where(is_ctx, dm.N, 0)
    hi = jnp.where(is_ctx, dm.R, dm.N)
    return jnp.logical_and(r >= lo, r < hi)


def _conv_fwd(dm, p, conv_w_l, conv_b_l, ln_g_l, ln_b_l):
    tr, c = dm.TR, dm.C

    def body(a_ref, g_ref, gate_ref, ap_ref, an_ref, gp_ref, gn_ref, w_ref, b_ref, lg_ref, lb_ref,
             zc_ref, y1_ref, u_ref):
        i = pl.program_id(0)
        a = jnp.concatenate([ap_ref[...], a_ref[...], an_ref[...]], axis=0)
        g = jnp.concatenate([gp_ref[...], g_ref[...], gn_ref[...]], axis=0)
        u_ref[...] = jnp.where(_seg_valid(dm, i), a * jax.nn.sigmoid(g), 0.0)
        for cc in range(c // LANES):
            sl = slice(cc * LANES, (cc + 1) * LANES)
            uc = u_ref[:, sl]
            acc = jnp.zeros((tr, LANES), F32)
            for j in range(CONV_KERNEL):
                off = HALO - CONV_PAD + j
                acc = acc + w_ref[j:j + 1, sl] * uc[off:off + tr]
            y1_ref[:, sl] = acc + b_ref[:, sl]
        zc_ref[...] = _convact_fn(y1_ref[...], gate_ref[...], lg_ref[...], lb_ref[...]).astype(BF16)

    ap, an = _halo_specs(dm, 0)
    gp, gn = _halo_specs(dm, 1)
    return pl.pallas_call(
        body, name="conv_fwd", grid=(dm.R // tr,),
        in_specs=[_rows(tr, c, 0), _rows(tr, c, 1), _rows(tr, c, 2), ap, an, gp, gn,
                  _whole((32, c)), _whole((1, c)), _whole((1, c)), _whole((1, c))],
        out_specs=(_rows(tr, c), _rows(tr, c)),
        out_shape=(jax.ShapeDtypeStruct((dm.R, c), BF16), jax.ShapeDtypeStruct((dm.R, c), F32)),
        scratch_shapes=[pltpu.VMEM((tr + 2 * HALO, c), F32)],
        compiler_params=_params(("parallel",)),
    )(p, p, p, p, p, p, p, conv_w_l, conv_b_l, ln_g_l, ln_b_l)


def _conv_act_bwd(dm, dzc, y1, p, ln_g_l, ln_b_l):
    tr, c = dm.TR, dm.C

    def body(dz_ref, y1_ref, gate_ref, lg_ref, lb_ref, dy_ref, dgate_ref, dlg_ref, dlb_ref):
        i = pl.program_id(0)
        _, vjp = jax.vjp(_convact_fn, y1_ref[...], gate_ref[...], lg_ref[...], lb_ref[...])
        dy, dgate, dlg, dlb = vjp(dz_ref[...])
        dy_ref[...] = dy
        dgate_ref[...] = dgate.astype(BF16)

        @pl.when(i == 0)
        def _():
            dlg_ref[...] = jnp.zeros_like(dlg_ref)
            dlb_ref[...] = jnp.zeros_like(dlb_ref)

        dlg_ref[...] += dlg
        dlb_ref[...] += dlb

    return pl.pallas_call(
        body, name="conv_act_bwd", grid=(dm.R // tr,),
        in_specs=[_rows(tr, c), _rows(tr, c), _rows(tr, c, 2), _whole((1, c)), _whole((1, c))],
        out_specs=(_rows(tr, c), _rows(tr, c), _whole((1, c)), _whole((1, c))),
        out_shape=(jax.ShapeDtypeStruct((dm.R, c), F32), jax.ShapeDtypeStruct((dm.R, c), BF16),
                   jax.ShapeDtypeStruct((1, c), F32), jax.ShapeDtypeStruct((1, c), F32)),
        compiler_params=_params(("arbitrary",)),
    )(dzc, y1, p, ln_g_l, ln_b_l)


def _conv_bwd(dm, dy1, p, conv_w_l):
    tr, c = dm.TR, dm.C

    def body(dy_ref, dyp_ref, dyn_ref, a_ref, g_ref, ap_ref, an_ref, gp_ref, gn_ref, w_ref,
             dag_ref, dw_ref, db_ref, u_ref, dyp_scr, du_ref):
        i = pl.program_id(0)
        valid = _seg_valid(dm, i)
        a = jnp.concatenate([ap_ref[...], a_ref[...], an_ref[...]], axis=0)
        g = jnp.concatenate([gp_ref[...], g_ref[...], gn_ref[...]], axis=0)
        u_ref[...] = jnp.where(valid, a * jax.nn.sigmoid(g), 0.0)
        dyp_scr[...] = jnp.where(valid, jnp.concatenate([dyp_ref[...], dy_ref[...], dyn_ref[...]], axis=0), 0.0)

        @pl.when(i == 0)
        def _():
            dw_ref[...] = jnp.zeros_like(dw_ref)
            db_ref[...] = jnp.zeros_like(db_ref)

        db_ref[...] += jnp.sum(dy_ref[...], axis=0, keepdims=True)
        for cc in range(c // LANES):
            sl = slice(cc * LANES, (cc + 1) * LANES)
            uc = u_ref[:, sl]
            dyc = dyp_scr[:, sl]
            dy_cur = dyc[HALO:HALO + tr]
            acc = jnp.zeros((tr, LANES), F32)
            for j in range(CONV_KERNEL):
                off = HALO + CONV_PAD - j
                acc = acc + w_ref[j:j + 1, sl] * dyc[off:off + tr]
                offu = HALO - CONV_PAD + j
                dw_ref[j:j + 1, sl] += jnp.sum(dy_cur * uc[offu:offu + tr], axis=0, keepdims=True)
            du_ref[:, sl] = acc
        du = du_ref[...]
        sg = jax.nn.sigmoid(g_ref[...])
        dag_ref[:, :c] = (du * sg).astype(BF16)
        dag_ref[:, c:] = (du * a_ref[...] * sg * (1.0 - sg)).astype(BF16)

    ap, an = _halo_specs(dm, 0)
    gp, gn = _halo_specs(dm, 1)
    dyp, dyn = _halo_specs(dm, 0)
    return pl.pallas_call(
        body, name="conv_bwd", grid=(dm.R // tr,),
        in_specs=[_rows(tr, c), dyp, dyn, _rows(tr, c, 0), _rows(tr, c, 1), ap, an, gp, gn, _whole((32, c))],
        out_specs=(_rows(tr, 2 * c), _whole((32, c)), _whole((1, c))),
        out_shape=(jax.ShapeDtypeStruct((dm.R, 2 * c), BF16), jax.ShapeDtypeStruct((32, c), F32),
                   jax.ShapeDtypeStruct((1, c), F32)),
        scratch_shapes=[pltpu.VMEM((tr + 2 * HALO, c), F32), pltpu.VMEM((tr + 2 * HALO, c), F32), pltpu.VMEM((tr, c), F32)],
        compiler_params=_params(("arbitrary",)),
    )(dy1, dy1, dy1, p, p, p, p, p, p, conv_w_l)


def _gm_specs(dm):
    half = dm.D // 2
    first = _col(3 * dm.C + 2 * dm.A + 2 * dm.KVW, half)
    return [_rows(dm.TR, half, first + k) for k in range(4)]


def _gm_load(a0, a1, b0, b1):
    return jnp.concatenate([a0[...], a1[...]], axis=1), jnp.concatenate([b0[...], b1[...]], axis=1)


def _merge(dm, yc, ya, p):
    tr, d = dm.TR, dm.D

    def body(yc_ref, ya_ref, a0, a1, b0, b1, m_ref):
        ga, gb = _gm_load(a0, a1, b0, b1)
        m_ref[...] = (jax.nn.sigmoid(ga) * yc_ref[...] + jax.nn.sigmoid(gb) * ya_ref[...]).astype(BF16)

    return pl.pallas_call(
        body, name="merge", grid=(dm.R // tr,),
        in_specs=[_rows(tr, d), _rows(tr, d)] + _gm_specs(dm), out_specs=_rows(tr, d),
        out_shape=jax.ShapeDtypeStruct((dm.R, d), BF16), compiler_params=_params(("parallel",)),
    )(yc, ya, p, p, p, p)


def _merge_bwd(dm, dmerged, yc, ya, p):
    tr, d = dm.TR, dm.D

    def body(dm_ref, yc_ref, ya_ref, a0, a1, b0, b1, dyc_ref, dya_ref, dgm_ref):
        dmv = dm_ref[...]
        ga, gb = _gm_load(a0, a1, b0, b1)
        sa, sb = jax.nn.sigmoid(ga), jax.nn.sigmoid(gb)
        dyc_ref[...] = (dmv * sa).astype(BF16)
        dya_ref[...] = (dmv * sb).astype(BF16)
        dgm_ref[:, :d] = (dmv * yc_ref[...] * sa * (1.0 - sa)).astype(BF16)
        dgm_ref[:, d:] = (dmv * ya_ref[...] * sb * (1.0 - sb)).astype(BF16)

    return pl.pallas_call(
        body, name="merge_bwd", grid=(dm.R // tr,),
        in_specs=[_rows(tr, d), _rows(tr, d), _rows(tr, d)] + _gm_specs(dm),
        out_specs=(_rows(tr, d), _rows(tr, d), _rows(tr, 2 * d)),
        out_shape=(jax.ShapeDtypeStruct((dm.R, d), BF16), jax.ShapeDtypeStruct((dm.R, d), BF16),
                   jax.ShapeDtypeStruct((dm.R, 2 * d), BF16)),
        compiler_params=_params(("parallel",)),
    )(dmerged, yc, ya, p, p, p, p)


def _loss_head(dm, xc, target):
    tr, d = dm.TR, dm.D
    n_lat = dm.N // tr

    def body(x_ref, t_ref, dy_ref, l_ref):
        i = pl.program_id(0)

        @pl.when(i == 0)
        def _():
            l_ref[...] = jnp.zeros_like(l_ref)

        @pl.when(i < n_lat)
        def _():
            e = x_ref[...] - t_ref[...]
            dy_ref[...] = e * (1.0 / d)
            l_ref[...] += 0.5 * jnp.sum(jnp.mean(e * e, axis=-1, keepdims=True), axis=0, keepdims=True)

        @pl.when(i >= n_lat)
        def _():
            dy_ref[...] = jnp.zeros_like(dy_ref)

    return pl.pallas_call(
        body, name="loss_head", grid=(dm.R // tr,),
        in_specs=[_rows(tr, d), pl.BlockSpec((tr, d), lambda i: (jnp.minimum(i, n_lat - 1), 0))],
        out_specs=(_rows(tr, d), _whole((1, 1))),
        out_shape=(jax.ShapeDtypeStruct((dm.R, d), F32), jax.ShapeDtypeStruct((1, 1), F32)),
        compiler_params=_params(("arbitrary",)),
    )(xc, target)


def _adamw(w, g, m, v, name):
    rows, cols = w.shape
    tr = _pick(rows, (512, 256, 128, 64, 32, 16, 8, rows))

    def body(w_ref, g_ref, m_ref, v_ref, d_ref, mo_ref, vo_ref):
        gv = g_ref[...]
        mn = ADAM_B1 * m_ref[...] + (1.0 - ADAM_B1) * gv
        vn = ADAM_B2 * v_ref[...] + (1.0 - ADAM_B2) * jnp.square(gv)
        m_hat = mn / (1.0 - ADAM_B1 ** ADAM_STEP)
        v_hat = vn / (1.0 - ADAM_B2 ** ADAM_STEP)
        d_ref[...] = -ADAM_LR * (m_hat / (jnp.sqrt(v_hat) + ADAM_EPS) + ADAM_WD * w_ref[...])
        mo_ref[...] = mn
        vo_ref[...] = vn

    spec = _rows(tr, cols)
    shp = jax.ShapeDtypeStruct(w.shape, F32)
    return pl.pallas_call(
        body, name=name, grid=(rows // tr,), in_specs=[spec] * 4, out_specs=(spec,) * 3, out_shape=(shp,) * 3,
        compiler_params=_params(("parallel",)),
    )(w, g, m, v)


def _rope_tables(dm):
    n = dm.N
    rows = n // GRID_W
    row = jnp.repeat(jnp.arange(rows, dtype=F32), GRID_W)
    col = jnp.tile(jnp.arange(GRID_W, dtype=F32), rows)
    half = dm.HD // 2
    inv_freq = ROPE_THETA ** (-jnp.arange(0, half, 2, dtype=F32) / half)
    ang = jnp.concatenate([row[:, None] * inv_freq, col[:, None] * inv_freq], axis=-1)
    cos, sin = jnp.cos(ang), jnp.sin(ang)
    cos = jnp.concatenate([cos, jnp.ones((dm.CTX, half), F32)], axis=0)
    sin = jnp.concatenate([sin, jnp.zeros((dm.CTX, half), F32)], axis=0)
    reps = LANES // dm.HD
    return jnp.tile(jnp.concatenate([cos, cos], axis=1), (1, reps)), jnp.tile(jnp.concatenate([-sin, sin], axis=1), (1, reps))


def _lane_vec(g, hd):
    return jnp.tile(g.reshape(1, hd), (1, LANES // hd))


def _layer_fwd(dm, xc, cv, wl, tabs):
    mod = _mod_fwd(cv, wl["w_mod"], wl["b_mod"])
    h = _prenorm(dm, xc, wl["g_pre"], mod)
    p = _matmul(h, wl["w_in"], "nn", "in_proj")
    qh, kh, vh = _qk_prep(dm, p, tabs["cos"], tabs["sins"], wl["gq"], wl["gk"], tabs["bd"])
    o, lse = _flash_fwd(dm, qh, kh, vh)
    za = _attn_gate(dm, o, p)
    zc, y1 = _conv_fwd(dm, p, wl["conv_w"], wl["conv_b"], wl["ln_g"], wl["ln_b"])
    yc = _matmul(zc, wl["w_conv_out"], "nn", "conv_out")
    ya = _matmul(za, wl["w_attn_out"], "nn", "attn_out")
    merged = _merge(dm, yc, ya, p)
    outp = _matmul(merged, wl["w_out"], "nn", "out_proj")
    x_new = _post(dm, xc, outp, wl["g_post"], mod)
    saved = dict(xc=xc, mod=mod, h=h, p=p, qh=qh, kh=kh, vh=vh, o=o, lse=lse, za=za, zc=zc, y1=y1, yc=yc, ya=ya,
                 merged=merged, outp=outp)
    return x_new, saved


def _layer_bwd(dm, dxn, cv, wl, tabs, s):
    p = s["p"]
    d_outp, dg_post, dgt = _post_bwd(dm, s["outp"], wl["g_post"], s["mod"], dxn)
    dmerged = _matmul(d_outp, wl["w_out"], "nt", "out_proj_dx")
    dw_out = _matmul(s["merged"], d_outp, "tn", "out_proj_dw")
    dyc, dya, dgm = _merge_bwd(dm, dmerged, s["yc"], s["ya"], p)
    dzc = _matmul(dyc, wl["w_conv_out"], "nt", "conv_out_dx")
    dw_conv_out = _matmul(s["zc"], dyc, "tn", "conv_out_dw")
    dza = _matmul(dya, wl["w_attn_out"], "nt", "attn_out_dx")
    dw_attn_out = _matmul(s["za"], dya, "tn", "attn_out_dw")
    doh, delta, dgate_b = _attn_gate_bwd(dm, dza, s["o"], p)
    dq, dkh, dvh = _flash_bwd(dm, s["qh"], s["kh"], s["vh"], doh, s["lse"], delta)
    dpq, dpkv, dgq, dgk = _qk_bwd(dm, p, dq, dkh, dvh, tabs["cos"], tabs["sins"], wl["gq"], wl["gk"], tabs["bd"])
    dy1, dgate_a, dln_g, dln_b = _conv_act_bwd(dm, dzc, s["y1"], p, wl["ln_g"], wl["ln_b"])
    dpag, dconv_w, dconv_b = _conv_bwd(dm, dy1, p, wl["conv_w"])
    dp = jnp.concatenate([dpag, dgate_a, dpq, dpkv, dgate_b, dgm], axis=1)
    dh = _matmul(dp, wl["w_in"], "nt", "in_proj_dx")
    dw_in = _matmul(s["h"], dp, "tn", "in_proj_dw")
    dxc, dg_pre, dshsc = _prenorm_bwd(dm, s["xc"], wl["g_pre"], s["mod"], dh, dxn)
    dmod = jnp.concatenate([dshsc, dgt], axis=1)
    dw_mod, db_mod, dcv = _mod_bwd(cv, wl["w_mod"], dmod)
    grads = dict(w_mod=dw_mod, b_mod=db_mod[0], g_pre=dg_pre[0], g_post=dg_post[0], w_in=dw_in,
                 conv_w=dconv_w[:CONV_KERNEL], conv_b=dconv_b[0], ln_g=dln_g[0], ln_b=dln_b[0],
                 w_conv_out=dw_conv_out, q_norm_g=dgq[0, :dm.HD], k_norm_g=dgk[0, :dm.HD],
                 w_attn_out=dw_attn_out, w_out=dw_out, c_ctx=dcv[1])
    return dxc, grads


def _local_step(dm, x, c, ctx, c_ctx, target, wf):
    depth = wf["w_in"].shape[0]
    cos, sins = _rope_tables(dm)
    lane = np.arange(LANES)
    bd = jnp.asarray((lane[:, None] // dm.HD == lane[None, :] // dm.HD), dtype=BF16)
    tabs = dict(cos=cos, sins=sins, bd=bd)
    cv = jnp.concatenate([c.reshape(1, dm.D), c_ctx.reshape(1, dm.D), jnp.zeros((6, dm.D), F32)], axis=0)
    xc = jnp.concatenate([x, ctx], axis=0)
    layers, saved = [], []
    for l in range(depth):
        wl = dict(
            w_mod=wf["w_mod"][l], b_mod=wf["b_mod"][l].reshape(1, -1), g_pre=wf["g_pre"][l].reshape(1, -1),
            g_post=wf["g_post"][l].reshape(1, -1), w_in=wf["w_in"][l],
            conv_w=jnp.concatenate([wf["conv_w"][l], jnp.zeros((1, dm.C), F32)], axis=0),
            conv_b=wf["conv_b"][l].reshape(1, -1), ln_g=wf["ln_g"][l].reshape(1, -1), ln_b=wf["ln_b"][l].reshape(1, -1),
            w_conv_out=wf["w_conv_out"][l], gq=_lane_vec(wf["q_norm_g"][l], dm.HD), gk=_lane_vec(wf["k_norm_g"][l], dm.HD),
            w_attn_out=wf["w_attn_out"][l], w_out=wf["w_out"][l])
        layers.append(wl)
        xc, s = _layer_fwd(dm, xc, cv, wl, tabs)
        saved.append(s)
    dxc, loss = _loss_head(dm, xc, target)
    grads = [None] * depth
    for l in reversed(range(depth)):
        dxc, grads[l] = _layer_bwd(dm, dxc, cv, layers[l], tabs, saved[l])
    gw = {k: jnp.stack([g[k] for g in grads]) for k in grads[0] if k != "c_ctx"}
    gw["c_ctx"] = grads[0]["c_ctx"] + grads[1]["c_ctx"] if depth == 2 else sum(g["c_ctx"] for g in grads)
    return loss[0, 0], dxc[:dm.N], gw


def _dims(x, ctx, w_in_full_cols, conv_c, attn_w, hd):
    n, d = x.shape
    ctx_len = ctx.shape[0]
    c, a = conv_c, attn_w
    kvw = (w_in_full_cols - 3 * c - 2 * a - 2 * d) // 2
    r = n + ctx_len
    tr = _pick(np.gcd(n, ctx_len), (256, 128))
    return Dims(N=n, CTX=ctx_len, R=r, D=d, C=c, A=a, KVW=kvw, HD=hd, NH=a // hd, NKV=kvw // hd,
                G=a // kvw, W=w_in_full_cols, TR=tr)


_HBM = pl.BlockSpec(memory_space=pltpu.HBM)
_VMEM = pl.BlockSpec(memory_space=pltpu.VMEM)


def _place():
    x, y, c = lax.axis_index("x"), lax.axis_index("y"), lax.axis_index("c")
    chips = [(1 - x, y), (x, 1 - y), (1 - x, 1 - y)]
    return x, y, c, chips


def _allgather_chips(own, name):
    _, h, w = own.shape

    def body(own_ref, out_ref, send_sems, recv_sems, local_sem):
        x, y, c, chips = _place()
        s_me = 2 * x + y
        sibling = (x, y, 1 - c)

        def copy(k, src, dst, to):
            return pltpu.make_async_remote_copy(src_ref=src, dst_ref=dst, send_sem=send_sems.at[k], recv_sem=recv_sems.at[k],
                                                device_id=to, device_id_type=MESH)

        local = pltpu.make_async_copy(own_ref, out_ref.at[s_me], local_sem)
        local.start()
        sends = [copy(j, own_ref.at[c], out_ref.at[s_me, c], (*chip, c)) for j, chip in enumerate(chips)]
        for cp in sends:
            cp.start()
        for j, (cx, cy) in enumerate(chips):
            landed = out_ref.at[2 * cx + cy, c]
            copy(j, landed, landed, (cx, cy, c)).wait_recv()
            fwd = copy(3 + j, landed, landed, sibling)
            fwd.start()
            sends.append(fwd)
        for j, (cx, cy) in enumerate(chips):
            theirs = out_ref.at[2 * cx + cy, 1 - c]
            copy(3 + j, theirs, theirs, sibling).wait_recv()
        for cp in sends:
            cp.wait_send()
        local.wait()

    return pl.pallas_call(
        body, name=name, in_specs=[_HBM], out_specs=_HBM,
        out_shape=jax.ShapeDtypeStruct((N_CHIPS, 2, h, w), own.dtype),
        scratch_shapes=[pltpu.SemaphoreType.DMA((6,)), pltpu.SemaphoreType.DMA((6,)), pltpu.SemaphoreType.DMA],
    )(own)


def _pair_exchange(buf, name):
    def body(buf_ref, out_ref, send_sem, recv_sem, local_sem):
        x, y, c, _ = _place()
        local = pltpu.make_async_copy(buf_ref.at[c], out_ref.at[c], local_sem)
        local.start()
        cp = pltpu.make_async_remote_copy(src_ref=buf_ref.at[1 - c], dst_ref=out_ref.at[c], send_sem=send_sem,
                                          recv_sem=recv_sem, device_id=(x, y, 1 - c), device_id_type=MESH)
        cp.start()
        cp.wait()
        local.wait()

    return pl.pallas_call(
        body, name=name, in_specs=[_HBM], out_specs=_HBM, out_shape=jax.ShapeDtypeStruct(buf.shape, buf.dtype),
        scratch_shapes=[pltpu.SemaphoreType.DMA, pltpu.SemaphoreType.DMA, pltpu.SemaphoreType.DMA],
    )(buf)


def _pair_share(t, name):
    def body(t_ref, out_ref, send_sem, recv_sem, local_sem):
        x, y, c, _ = _place()
        local = pltpu.make_async_copy(t_ref, out_ref.at[c], local_sem)
        local.start()
        cp = pltpu.make_async_remote_copy(src_ref=t_ref, dst_ref=out_ref.at[c], send_sem=send_sem,
                                          recv_sem=recv_sem, device_id=(x, y, 1 - c), device_id_type=MESH)
        cp.start()
        cp.wait()
        local.wait()

    return pl.pallas_call(
        body, name=name, in_specs=[_HBM], out_specs=_HBM, out_shape=jax.ShapeDtypeStruct((2,) + t.shape, t.dtype),
        scratch_shapes=[pltpu.SemaphoreType.DMA, pltpu.SemaphoreType.DMA, pltpu.SemaphoreType.DMA],
    )(t)


def _scatter_chips(s, name):
    def body(s_ref, out_ref, send_sems, recv_sems, local_sem):
        x, y, c, chips = _place()
        s_me = 2 * x + y
        local = pltpu.make_async_copy(s_ref.at[s_me], out_ref.at[s_me], local_sem)
        local.start()
        sends = []
        for j, (cx, cy) in enumerate(chips):
            cp = pltpu.make_async_remote_copy(src_ref=s_ref.at[2 * cx + cy], dst_ref=out_ref.at[s_me], send_sem=send_sems.at[j],
                                              recv_sem=recv_sems.at[j], device_id=(cx, cy, c), device_id_type=MESH)
            cp.start()
            sends.append(cp)
        for j, (cx, cy) in enumerate(chips):
            slot = out_ref.at[2 * cx + cy]
            pltpu.make_async_remote_copy(src_ref=slot, dst_ref=slot, send_sem=send_sems.at[j], recv_sem=recv_sems.at[j],
                                         device_id=(cx, cy, c), device_id_type=MESH).wait_recv()
        for cp in sends:
            cp.wait_send()
        local.wait()

    return pl.pallas_call(
        body, name=name, in_specs=[_HBM], out_specs=_HBM, out_shape=jax.ShapeDtypeStruct(s.shape, s.dtype),
        scratch_shapes=[pltpu.SemaphoreType.DMA((3,)), pltpu.SemaphoreType.DMA((3,)), pltpu.SemaphoreType.DMA],
    )(s)


def _sum_slots(a, name):
    k, m, w = a.shape
    tr = _pick(m, (512, 432, 256, 128, 64, 32, 16, 8))

    def body(a_ref, o_ref):
        acc = a_ref[0]
        for i in range(1, k):
            acc = acc + a_ref[i]
        o_ref[...] = acc

    return pl.pallas_call(
        body, name=name, grid=(m // tr,), in_specs=[pl.BlockSpec((k, tr, w), lambda i: (0, i, 0))],
        out_specs=_rows(tr, w), out_shape=jax.ShapeDtypeStruct((m, w), F32), compiler_params=_params(("parallel",)),
    )(a)


def _allreduce_small(own, name):
    m, w = own.shape

    def body(own_ref, sum_ref, all_ref, send_sems, recv_sems):
        x, y, c, _ = _place()
        me = 4 * x + 2 * y + c
        all_ref[me] = own_ref[...]
        sends = []
        for k in range(1, N_DEV):
            peer = (x ^ ((k >> 2) & 1), y ^ ((k >> 1) & 1), c ^ (k & 1))
            cp = pltpu.make_async_remote_copy(src_ref=own_ref, dst_ref=all_ref.at[me], send_sem=send_sems.at[k - 1],
                                              recv_sem=recv_sems.at[k - 1], device_id=peer, device_id_type=MESH)
            cp.start()
            sends.append(cp)
        for k in range(1, N_DEV):
            px, py, pc = x ^ ((k >> 2) & 1), y ^ ((k >> 1) & 1), c ^ (k & 1)
            slot = all_ref.at[4 * px + 2 * py + pc]
            pltpu.make_async_remote_copy(src_ref=slot, dst_ref=slot, send_sem=send_sems.at[k - 1], recv_sem=recv_sems.at[k - 1],
                                         device_id=(px, py, pc), device_id_type=MESH).wait_recv()
        for cp in sends:
            cp.wait_send()
        acc = all_ref[0]
        for d in range(1, N_DEV):
            acc = acc + all_ref[d]
        sum_ref[...] = acc

    return pl.pallas_call(
        body, name=name, in_specs=[_VMEM], out_specs=_VMEM, out_shape=jax.ShapeDtypeStruct((m, w), F32),
        scratch_shapes=[pltpu.VMEM((N_DEV, m, w), F32), pltpu.SemaphoreType.DMA((N_DEV - 1,)), pltpu.SemaphoreType.DMA((N_DEV - 1,))],
        compiler_params=pltpu.CompilerParams(vmem_limit_bytes=VMEM_LIMIT),
    )(own)


_COL_SHARDED = ("w_mod", "w_in")
_ROW_SHARDED = ("w_conv_out", "w_attn_out", "w_out")
_BIG = _COL_SHARDED + _ROW_SHARDED
_SMALL = ("c_ctx", "b_mod", "g_pre", "g_post", "conv_b", "ln_g", "ln_b", "q_norm_g", "k_norm_g")
FLAT_W = 1024


def _flatten_shards(shards):
    flat = jnp.concatenate([shards[k].reshape(-1) for k in _BIG])
    return flat.reshape(-1, FLAT_W)


def _unflatten_shard(flat, like):
    out, off = {}, 0
    flat = flat.reshape(-1)
    for k in _BIG:
        n = int(np.prod(like[k].shape))
        out[k] = flat[off:off + n].reshape(like[k].shape)
        off += n
    return out


def _by_shard(k, full):
    l = full.shape[0]
    if k in _COL_SHARDED:
        d, cols = full.shape[1:]
        return full.reshape(l, d, N_CHIPS, cols // N_CHIPS).transpose(2, 0, 1, 3).reshape(N_CHIPS, -1)
    rows, d = full.shape[1:]
    return full.reshape(l, N_CHIPS, rows // N_CHIPS, d).transpose(1, 0, 2, 3).reshape(N_CHIPS, -1)


def _from_shards(k, flat4, like):
    l = like.shape[0]
    if k in _COL_SHARDED:
        d, cs = like.shape[1:]
        return flat4.reshape(N_CHIPS, l, d, cs).transpose(1, 2, 0, 3).reshape(l, d, N_CHIPS * cs)
    rs, d = like.shape[1:]
    return flat4.reshape(N_CHIPS, l, rs, d).transpose(1, 0, 2, 3).reshape(l, N_CHIPS * rs, d)


def _pad_rows(flat, mult=8):
    n = flat.shape[0]
    rows = -(-n // LANES)
    rows = -(-rows // mult) * mult
    return jnp.concatenate([flat, jnp.zeros((rows * LANES - n,), F32)]).reshape(rows, LANES)


def kernel(x, c, ctx, c_ctx, w_mod, b_mod, g_pre, g_post, w_in, conv_w, conv_b, ln_g, ln_b, w_conv_out, q_norm_g, k_norm_g, w_attn_out, w_out, loss_target, m_c_ctx, m_w_mod, m_b_mod, m_g_pre, m_g_post, m_w_in, m_conv_w, m_conv_b, m_ln_g, m_ln_b, m_w_conv_out, m_q_norm_g, m_k_norm_g, m_w_attn_out, m_w_out, v_c_ctx, v_w_mod, v_b_mod, v_g_pre, v_g_post, v_w_in, v_conv_w, v_conv_b, v_ln_g, v_ln_b, v_w_conv_out, v_q_norm_g, v_k_norm_g, v_w_attn_out, v_w_out):
    w = dict(c_ctx=c_ctx, w_mod=w_mod, b_mod=b_mod, g_pre=g_pre, g_post=g_post, w_in=w_in, conv_w=conv_w, conv_b=conv_b,
             ln_g=ln_g, ln_b=ln_b, w_conv_out=w_conv_out, q_norm_g=q_norm_g, k_norm_g=k_norm_g, w_attn_out=w_attn_out, w_out=w_out)
    m = dict(c_ctx=m_c_ctx, w_mod=m_w_mod, b_mod=m_b_mod, g_pre=m_g_pre, g_post=m_g_post, w_in=m_w_in, conv_w=m_conv_w,
             conv_b=m_conv_b, ln_g=m_ln_g, ln_b=m_ln_b, w_conv_out=m_w_conv_out, q_norm_g=m_q_norm_g, k_norm_g=m_k_norm_g,
             w_attn_out=m_w_attn_out, w_out=m_w_out)
    v = dict(c_ctx=v_c_ctx, w_mod=v_w_mod, b_mod=v_b_mod, g_pre=v_g_pre, g_post=v_g_post, w_in=v_w_in, conv_w=v_conv_w,
             conv_b=v_conv_b, ln_g=v_ln_g, ln_b=v_ln_b, w_conv_out=v_w_conv_out, q_norm_g=v_q_norm_g, k_norm_g=v_k_norm_g,
             w_attn_out=v_w_attn_out, w_out=v_w_out)
    order = ("c_ctx", "w_mod", "b_mod", "g_pre", "g_post", "w_in", "conv_w", "conv_b", "ln_g", "ln_b", "w_conv_out",
             "q_norm_g", "k_norm_g", "w_attn_out", "w_out")
    my_chip = 2 * lax.axis_index("x") + lax.axis_index("y")
    my_core = lax.axis_index("c")
    depth, _, conv_cs = conv_w.shape
    conv_c = conv_cs * N_CHIPS
    dm = _dims(x[0], ctx[0], w_in.shape[2] * N_CHIPS, conv_c, w_attn_out.shape[1] * N_CHIPS, q_norm_g.shape[1])

    w_flat = _flatten_shards(w)
    rows2 = w_flat.shape[0]
    own = w_flat.astype(BF16).reshape(2, rows2 // 2, FLAT_W)
    gathered = _allgather_chips(own, "gather_weights").reshape(N_CHIPS, -1)
    wf, off = {}, 0
    for k in _BIG:
        n = int(np.prod(w[k].shape))
        wf[k] = _from_shards(k, gathered[:, off:off + n], w[k])
        off += n
    cw = jnp.zeros((depth, CONV_KERNEL, N_CHIPS, conv_cs), F32)
    cw = lax.dynamic_update_slice(cw, conv_w[:, :, None, :], (0, 0, my_chip, 0))
    cw = jnp.where(my_core == 0, cw, 0.0).reshape(-1)
    wf["conv_w"] = _allreduce_small(_pad_rows(cw), "gather_conv_w").reshape(-1)[:cw.shape[0]].reshape(depth, CONV_KERNEL, conv_c)
    for k in _SMALL:
        wf[k] = w[k]

    loss_part, grad_x, gw = _local_step(dm, x[0], c, ctx[0], c_ctx, loss_target[0], wf)
    loss = lax.psum(loss_part, ("x", "y", "c"))

    g4 = jnp.concatenate([_by_shard(k, gw[k]) for k in _BIG], axis=1)
    hrows = rows2 // 2
    g24 = g4.reshape(N_CHIPS, 2, hrows, FLAT_W).transpose(1, 0, 2, 3).reshape(2, N_CHIPS * hrows, FLAT_W)
    pair = _pair_exchange(g24, "reduce_pair")
    chip_sum = _sum_slots(pair, "reduce_pair_add").reshape(N_CHIPS, hrows, FLAT_W)
    quad = _scatter_chips(chip_sum, "reduce_chips")
    mine = _sum_slots(quad, "reduce_chips_add")
    g_flat = _pair_share(mine, "reduce_share").reshape(rows2, FLAT_W)

    small = jnp.concatenate([gw[k].reshape(-1) for k in _SMALL] + [gw["conv_w"].reshape(-1)])
    small_sum = _allreduce_small(_pad_rows(small), "reduce_small").reshape(-1)
    gs, off = {}, 0
    for k in _SMALL:
        n = int(np.prod(w[k].shape))
        gs[k] = small_sum[off:off + n].reshape(w[k].shape)
        off += n
    gcw = small_sum[off:off + depth * CONV_KERNEL * conv_c].reshape(depth, CONV_KERNEL, N_CHIPS, conv_cs)
    gs["conv_w"] = lax.dynamic_index_in_dim(gcw, my_chip, axis=2, keepdims=False)

    d_flat, m_flat, v_flat = _adamw(w_flat, g_flat, _flatten_shards(m), _flatten_shards(v), "adamw_big")
    grads = _unflatten_shard(g_flat, w)
    delta = _unflatten_shard(d_flat, w)
    new_m = _unflatten_shard(m_flat, w)
    new_v = _unflatten_shard(v_flat, w)
    small_keys = _SMALL + ("conv_w",)
    join = lambda t: _pad_rows(jnp.concatenate([t[k].reshape(-1) for k in small_keys]))
    d_s, m_s, v_s = _adamw(join(w), join(gs), join(m), join(v), "adamw_small")
    off = 0
    for k in small_keys:
        n = int(np.prod(w[k].shape))
        grads[k] = gs[k]
        for dst, src in ((delta, d_s), (new_m, m_s), (new_v, v_s)):
            dst[k] = src.reshape(-1)[off:off + n].reshape(w[k].shape)
        off += n
    return (loss, grad_x[None], *[grads[k] for k in order], *[delta[k] for k in order],
            *[new_m[k] for k in order], *[new_v[k] for k in order])
```

```python
import functools
from typing import NamedTuple

import jax
import jax.numpy as jnp
import numpy as np
from jax import lax
from jax.experimental import pallas as pl
from jax.experimental.pallas import tpu as pltpu

F32 = jnp.float32
BF16 = jnp.bfloat16
EPS = 1e-6
GRID_W = 64
ROPE_THETA = 10000.0
CONV_KERNEL = 31
CONV_PAD = CONV_KERNEL // 2
HALO = 16
LANES = 128
ADAM_LR, ADAM_B1, ADAM_B2, ADAM_EPS, ADAM_WD, ADAM_STEP = 0.001, 0.9, 0.999, 1e-08, 0.01, 10
VMEM_LIMIT = 56 * 1024 * 1024
MESH = pl.DeviceIdType.MESH
N_CHIPS = 4
N_DEV = 8


class Dims(NamedTuple):
    N: int
    CTX: int
    R: int
    D: int
    C: int
    A: int
    KVW: int
    HD: int
    NH: int
    NKV: int
    G: int
    W: int
    TR: int
    TK: int


def _pick(n, prefs):
    for p in prefs:
        if n % p == 0:
            return p
    raise ValueError(f"no tile for {n} in {prefs}")


def _params(sem):
    return pltpu.CompilerParams(dimension_semantics=sem, vmem_limit_bytes=VMEM_LIMIT)


def _rows(tr, w, col=0):
    return pl.BlockSpec((tr, w), lambda i, col=col: (i, col))


def _whole(shape):
    return pl.BlockSpec(shape, lambda *_: (0,) * len(shape))


def _col(off, w):
    assert off % w == 0, (off, w)
    return off // w


_DN = {"nn": (((1,), (0,)), ((), ())), "nt": (((1,), (1,)), ((), ())), "tn": (((0,), (0,)), ((), ()))}


def _matmul(a, b, mode, name, out_dtype=F32):
    if mode == "nn":
        (m, k), n = a.shape, b.shape[1]
    elif mode == "nt":
        (m, k), n = a.shape, b.shape[0]
    else:
        (k, m), n = a.shape, b.shape[1]
    tm = _pick(m, (768, 512, 256, 128, 8))
    tn = _pick(n, (1280, 1024, 768, 512, 256, 128))
    tk = _pick(k, (1536, 1280, 1024, 768, 512, 256, 128))
    nk = k // tk
    dn = _DN[mode]

    def body(a_ref, b_ref, o_ref, acc_ref):
        kk = pl.program_id(2)

        @pl.when(kk == 0)
        def _():
            acc_ref[...] = jnp.zeros_like(acc_ref)

        acc_ref[...] += lax.dot_general(a_ref[...], b_ref[...], dn, preferred_element_type=F32)

        @pl.when(kk == nk - 1)
        def _():
            o_ref[...] = acc_ref[...].astype(o_ref.dtype)

    a_spec = pl.BlockSpec((tk, tm), lambda i, j, kk: (kk, i)) if mode == "tn" else pl.BlockSpec((tm, tk), lambda i, j, kk: (i, kk))
    b_spec = pl.BlockSpec((tn, tk), lambda i, j, kk: (j, kk)) if mode == "nt" else pl.BlockSpec((tk, tn), lambda i, j, kk: (kk, j))
    return pl.pallas_call(
        body, name=name, grid=(m // tm, n // tn, nk),
        in_specs=[a_spec, b_spec], out_specs=pl.BlockSpec((tm, tn), lambda i, j, kk: (i, j)),
        out_shape=jax.ShapeDtypeStruct((m, n), out_dtype),
        scratch_shapes=[pltpu.VMEM((tm, tn), F32)],
        compiler_params=_params(("parallel", "parallel", "arbitrary")),
    )(a, b)


def _rms(x, g):
    return x * lax.rsqrt(jnp.mean(x * x, axis=-1, keepdims=True) + EPS) * g


def _prenorm_fn(x, g, sc, sh):
    return _rms(x, g) * (1.0 + sc) + sh


def _post_fn(outp, g, gt):
    return gt * _rms(outp, g)


def _convact_fn(y1, gate, ln_g, ln_b):
    mu = jnp.mean(y1, axis=-1, keepdims=True)
    var = jnp.mean(jnp.square(y1 - mu), axis=-1, keepdims=True)
    y = (y1 - mu) * lax.rsqrt(var + EPS) * ln_g + ln_b
    return jax.nn.silu(y) * jax.nn.silu(gate)


def _dsilu(x):
    s = jax.nn.sigmoid(x)
    return s * (1.0 + x * (1.0 - s))


def _seg_select(is_ctx, mod_ref, lo, w):
    return jnp.where(is_ctx, mod_ref[1:2, lo:lo + w], mod_ref[0:1, lo:lo + w])


def _seg_rows(is_ctx, v):
    row = lax.broadcasted_iota(jnp.int32, (8, v.shape[1]), 0)
    return jnp.where(row == is_ctx.astype(jnp.int32), v, 0.0)


def _mod_fwd(cv, w_mod_l, b_mod_l):
    def body(cv_ref, w_ref, b_ref, o_ref):
        s = jax.nn.silu(cv_ref[...]).astype(BF16)
        o_ref[...] = jnp.dot(s, w_ref[...], preferred_element_type=F32) + b_ref[...]

    return pl.pallas_call(
        body, name="mod_fwd", out_shape=jax.ShapeDtypeStruct((8, w_mod_l.shape[1]), F32),
        compiler_params=pltpu.CompilerParams(vmem_limit_bytes=VMEM_LIMIT),
    )(cv, w_mod_l, b_mod_l)


def _mod_bwd(cv, w_mod_l, dmod):
    d = cv.shape[1]

    def body(cv_ref, w_ref, dm_ref, dw_ref, db_ref, dcv_ref):
        cvv = cv_ref[...]
        s = jax.nn.silu(cvv).astype(BF16)
        dm = dm_ref[...]
        dmb = dm.astype(BF16)
        dw_ref[...] = lax.dot_general(s, dmb, _DN["tn"], preferred_element_type=F32)
        db_ref[...] = jnp.sum(dm, axis=0, keepdims=True)
        ds = lax.dot_general(dmb, w_ref[...], _DN["nt"], preferred_element_type=F32)
        dcv_ref[...] = ds * _dsilu(cvv)

    return pl.pallas_call(
        body, name="mod_bwd",
        out_shape=(jax.ShapeDtypeStruct(w_mod_l.shape, F32), jax.ShapeDtypeStruct((1, w_mod_l.shape[1]), F32),
                   jax.ShapeDtypeStruct((8, d), F32)),
        compiler_params=pltpu.CompilerParams(vmem_limit_bytes=VMEM_LIMIT),
    )(cv, w_mod_l, dmod)


def _prenorm(dm, xc, g_pre_l, mod):
    tr, d, n = dm.TR, dm.D, dm.N

    def body(x_ref, g_ref, mod_ref, h_ref):
        is_ctx = pl.program_id(0) * tr >= n
        sh = _seg_select(is_ctx, mod_ref, 0, d)
        sc = _seg_select(is_ctx, mod_ref, d, d)
        h_ref[...] = _prenorm_fn(x_ref[...], g_ref[...], sc, sh).astype(BF16)

    return pl.pallas_call(
        body, name="prenorm", grid=(dm.R // tr,),
        in_specs=[_rows(tr, d), _whole((1, d)), _whole((8, 3 * d))], out_specs=_rows(tr, d),
        out_shape=jax.ShapeDtypeStruct((dm.R, d), BF16), compiler_params=_params(("parallel",)),
    )(xc, g_pre_l, mod)


def _prenorm_bwd(dm, xc, g_pre_l, mod, dh, dxn):
    tr, d, n = dm.TR, dm.D, dm.N

    def body(x_ref, g_ref, mod_ref, dh_ref, dxn_ref, dx_ref, dg_ref, dss_ref):
        i = pl.program_id(0)
        is_ctx = i * tr >= n
        sh = _seg_select(is_ctx, mod_ref, 0, d)
        sc = _seg_select(is_ctx, mod_ref, d, d)
        _, vjp = jax.vjp(_prenorm_fn, x_ref[...], g_ref[...], sc, sh)
        dx, dg, dsc, dsh = vjp(dh_ref[...])
        dx_ref[...] = dxn_ref[...] + dx

        @pl.when(i == 0)
        def _():
            dg_ref[...] = jnp.zeros_like(dg_ref)
            dss_ref[...] = jnp.zeros_like(dss_ref)

        dg_ref[...] += dg
        dss_ref[...] += _seg_rows(is_ctx, jnp.concatenate([dsh, dsc], axis=1))

    return pl.pallas_call(
        body, name="prenorm_bwd", grid=(dm.R // tr,),
        in_specs=[_rows(tr, d), _whole((1, d)), _whole((8, 3 * d)), _rows(tr, d), _rows(tr, d)],
        out_specs=(_rows(tr, d), _whole((1, d)), _whole((8, 2 * d))),
        out_shape=(jax.ShapeDtypeStruct((dm.R, d), F32), jax.ShapeDtypeStruct((1, d), F32),
                   jax.ShapeDtypeStruct((8, 2 * d), F32)),
        compiler_params=_params(("arbitrary",)),
    )(xc, g_pre_l, mod, dh, dxn)


def _post(dm, xc, outp, g_post_l, mod):
    tr, d, n = dm.TR, dm.D, dm.N

    def body(x_ref, o_ref, g_ref, mod_ref, y_ref):
        is_ctx = pl.program_id(0) * tr >= n
        gt = _seg_select(is_ctx, mod_ref, 2 * d, d)
        y_ref[...] = x_ref[...] + _post_fn(o_ref[...], g_ref[...], gt)

    return pl.pallas_call(
        body, name="post", grid=(dm.R // tr,),
        in_specs=[_rows(tr, d), _rows(tr, d), _whole((1, d)), _whole((8, 3 * d))], out_specs=_rows(tr, d),
        out_shape=jax.ShapeDtypeStruct((dm.R, d), F32), compiler_params=_params(("parallel",)),
    )(xc, outp, g_post_l, mod)


def _post_bwd(dm, outp, g_post_l, mod, dxn):
    tr, d, n = dm.TR, dm.D, dm.N

    def body(o_ref, g_ref, mod_ref, dxn_ref, do_ref, dg_ref, dgt_ref):
        i = pl.program_id(0)
        is_ctx = i * tr >= n
        gt = _seg_select(is_ctx, mod_ref, 2 * d, d)
        _, vjp = jax.vjp(_post_fn, o_ref[...], g_ref[...], gt)
        do, dg, dgt = vjp(dxn_ref[...])
        do_ref[...] = do.astype(BF16)

        @pl.when(i == 0)
        def _():
            dg_ref[...] = jnp.zeros_like(dg_ref)
            dgt_ref[...] = jnp.zeros_like(dgt_ref)

        dg_ref[...] += dg
        dgt_ref[...] += _seg_rows(is_ctx, dgt)

    return pl.pallas_call(
        body, name="post_bwd", grid=(dm.R // tr,),
        in_specs=[_rows(tr, d), _whole((1, d)), _whole((8, 3 * d)), _rows(tr, d)],
        out_specs=(_rows(tr, d), _whole((1, d)), _whole((8, d))),
        out_shape=(jax.ShapeDtypeStruct((dm.R, d), BF16), jax.ShapeDtypeStruct((1, d), F32),
                   jax.ShapeDtypeStruct((8, d), F32)),
        compiler_params=_params(("arbitrary",)),
    )(outp, g_post_l, mod, dxn)


def _group_sum(x, bd):
    hi = x.astype(BF16)
    lo = (x - hi.astype(F32)).astype(BF16)
    return jnp.dot(hi, bd, preferred_element_type=F32) + jnp.dot(lo, bd, preferred_element_type=F32)


def _swap_halves(x):
    lane = lax.broadcasted_iota(jnp.int32, x.shape, 1)
    return jnp.where((lane & 32) == 0, pltpu.roll(x, LANES - 32, 1), pltpu.roll(x, 32, 1))


def _qk_prep(dm, p, cos, sins, gq, gk, bd):
    tr, hd = dm.TR, dm.HD
    oq, ok, ov = 3 * dm.C, 3 * dm.C + dm.A, 3 * dm.C + dm.A + dm.KVW
    scale = hd ** -0.5

    def body(q_ref, k_ref, v_ref, cos_ref, sin_ref, gq_ref, gk_ref, bd_ref, qh_ref, qt_ref, kh_ref, vh_ref):
        cosv, sinv, bdv = cos_ref[...], sin_ref[...], bd_ref[...]

        def chunk(x, g):
            rstd = lax.rsqrt(_group_sum(x * x, bdv) * (1.0 / hd) + EPS)
            xn = x * rstd * g
            return xn * cosv + _swap_halves(xn) * sinv

        for c in range(dm.A // LANES):
            y = chunk(q_ref[:, c * LANES:(c + 1) * LANES], gq_ref[...]) * scale
            qh_ref[2 * c] = y[:, :hd].astype(BF16)
            qh_ref[2 * c + 1] = y[:, hd:].astype(BF16)
            qt_ref[c * LANES:(c + 1) * LANES, :] = y.T.astype(BF16)
        for c in range(dm.KVW // LANES):
            y = chunk(k_ref[:, c * LANES:(c + 1) * LANES], gk_ref[...])
            kh_ref[2 * c] = y[:, :hd].astype(BF16)
            kh_ref[2 * c + 1] = y[:, hd:].astype(BF16)
            v = v_ref[:, c * LANES:(c + 1) * LANES]
            vh_ref[2 * c] = v[:, :hd].astype(BF16)
            vh_ref[2 * c + 1] = v[:, hd:].astype(BF16)

    hspec = lambda nh: pl.BlockSpec((nh, tr, hd), lambda i: (0, i, 0))
    return pl.pallas_call(
        body, name="qk_prep", grid=(dm.R // tr,),
        in_specs=[_rows(tr, dm.A, _col(oq, dm.A)), _rows(tr, dm.KVW, _col(ok, dm.KVW)), _rows(tr, dm.KVW, _col(ov, dm.KVW)),
                  _rows(tr, LANES), _rows(tr, LANES), _whole((1, LANES)), _whole((1, LANES)), _whole((LANES, LANES))],
        out_specs=(hspec(dm.NH), pl.BlockSpec((dm.A, tr), lambda i: (0, i)), hspec(dm.NKV), hspec(dm.NKV)),
        out_shape=(jax.ShapeDtypeStruct((dm.NH, dm.R, hd), BF16), jax.ShapeDtypeStruct((dm.A, dm.R), BF16),
                   jax.ShapeDtypeStruct((dm.NKV, dm.R, hd), BF16), jax.ShapeDtypeStruct((dm.NKV, dm.R, hd), BF16)),
        compiler_params=_params(("parallel",)),
    )(p, p, p, cos, sins, gq, gk, bd)


def _qk_bwd(dm, p, dq, dkt, dvt, cos, sins, gq, gk, bd):
    tr, hd = dm.TR, dm.HD
    oq, ok = 3 * dm.C, 3 * dm.C + dm.A
    scale = hd ** -0.5

    def body(q_ref, k_ref, dq_ref, dkh_ref, dvh_ref, cos_ref, sin_ref, gq_ref, gk_ref, bd_ref,
             dpq_ref, dpkv_ref, dgq_ref, dgk_ref):
        i = pl.program_id(0)
        cosv, sinv, bdv = cos_ref[...], sin_ref[...], bd_ref[...]

        def chunk(x, g, dy):
            rstd = lax.rsqrt(_group_sum(x * x, bdv) * (1.0 / hd) + EPS)
            xhat = x * rstd
            dxn = dy * cosv + _swap_halves(dy * sinv)
            dxhat = dxn * g
            dx = rstd * (dxhat - xhat * (_group_sum(dxhat * xhat, bdv) * (1.0 / hd)))
            return dx, jnp.sum(dxn * xhat, axis=0, keepdims=True)

        @pl.when(i == 0)
        def _():
            dgq_ref[...] = jnp.zeros_like(dgq_ref)
            dgk_ref[...] = jnp.zeros_like(dgk_ref)

        dgq = jnp.zeros((1, LANES), F32)
        for c in range(dm.A // LANES):
            sl = slice(c * LANES, (c + 1) * LANES)
            dx, dg = chunk(q_ref[:, sl], gq_ref[...], dq_ref[:, sl] * scale)
            dpq_ref[:, sl] = dx.astype(BF16)
            dgq = dgq + dg
        dgq_ref[...] += dgq + pltpu.roll(dgq, hd, 1)
        dgk = jnp.zeros((1, LANES), F32)
        for c in range(dm.KVW // LANES):
            sl = slice(c * LANES, (c + 1) * LANES)
            dx, dg = chunk(k_ref[:, sl], gk_ref[...], dkh_ref[sl, :].T)
            dpkv_ref[:, sl] = dx.astype(BF16)
            dgk = dgk + dg
            dpkv_ref[:, dm.KVW + c * LANES:dm.KVW + (c + 1) * LANES] = dvh_ref[sl, :].T.astype(BF16)
        dgk_ref[...] += dgk + pltpu.roll(dgk, hd, 1)

    hspec = pl.BlockSpec((dm.KVW, tr), lambda i: (0, i))
    return pl.pallas_call(
        body, name="qk_bwd", grid=(dm.R // tr,),
        in_specs=[_rows(tr, dm.A, _col(oq, dm.A)), _rows(tr, dm.KVW, _col(ok, dm.KVW)), _rows(tr, dm.A), hspec, hspec,
                  _rows(tr, LANES), _rows(tr, LANES), _whole((1, LANES)), _whole((1, LANES)), _whole((LANES, LANES))],
        out_specs=(_rows(tr, dm.A), _rows(tr, 2 * dm.KVW), _whole((1, LANES)), _whole((1, LANES))),
        out_shape=(jax.ShapeDtypeStruct((dm.R, dm.A), BF16), jax.ShapeDtypeStruct((dm.R, 2 * dm.KVW), BF16),
                   jax.ShapeDtypeStruct((1, LANES), F32), jax.ShapeDtypeStruct((1, LANES), F32)),
        compiler_params=_params(("arbitrary",)),
    )(p, p, dq, dkt, dvt, cos, sins, gq, gk, bd)


def _flash_fwd(dm, qh, kh, vh):
    tq, ks, tk = dm.TR, dm.TR, dm.TK
    rs = min(tq, LANES)
    g, hd = dm.G, dm.HD
    nq, nk = dm.R // tq, dm.R // tk
    nq_lat = dm.N // tq
    n_chunks, ctx_chunks = tk // ks, dm.CTX // ks

    def body(q_ref, k_ref, v_ref, o_ref, lse_ref, m_ref, l_ref, acc_ref):
        qi, ki = pl.program_id(1), pl.program_id(2)

        @pl.when(ki == 0)
        def _():
            m_ref[...] = jnp.full_like(m_ref, -1e30)
            l_ref[...] = jnp.zeros_like(l_ref)
            acc_ref[...] = jnp.zeros_like(acc_ref)

        def attend(chunks):
            for gg in range(g):
                for r in range(tq // rs):
                    rows = slice(gg * tq + r * rs, gg * tq + (r + 1) * rs)
                    q = q_ref[gg, r * rs:(r + 1) * rs, :]
                    m_prev, l_prev, acc = m_ref[rows, :], l_ref[rows, :], acc_ref[rows, :]
                    for kc in chunks:
                        keys = slice(kc * ks, (kc + 1) * ks)
                        s = lax.dot_general(q, k_ref[0, keys, :], _DN["nt"], preferred_element_type=F32)
                        m_new = jnp.maximum(m_prev, jnp.max(s, axis=-1, keepdims=True))
                        p = jnp.exp(s - jnp.tile(m_new, (1, ks // LANES)))
                        alpha = jnp.exp(m_prev - m_new)
                        l_prev = alpha * l_prev + jnp.sum(p, axis=-1, keepdims=True)
                        acc = acc * alpha[:, :hd] + jnp.dot(p.astype(BF16), v_ref[0, keys, :], preferred_element_type=F32)
                        m_prev = m_new
                    m_ref[rows, :], l_ref[rows, :], acc_ref[rows, :] = m_prev, l_prev, acc

        @pl.when(qi < nq_lat)
        def _():
            attend(range(n_chunks))

        @pl.when(jnp.logical_and(qi >= nq_lat, ki == nk - 1))
        def _():
            attend(range(n_chunks - ctx_chunks, n_chunks))

        @pl.when(ki == nk - 1)
        def _():
            for gg in range(g):
                rows = slice(gg * tq, (gg + 1) * tq)
                l = l_ref[rows, :]
                o_ref[:, gg * hd:(gg + 1) * hd] = acc_ref[rows, :] / l[:, :hd]
                lse_ref[gg] = (m_ref[rows, :] + jnp.log(l))[:, :1]

    return pl.pallas_call(
        body, name="flash_fwd", grid=(dm.NKV, nq, nk),
        in_specs=[pl.BlockSpec((g, tq, hd), lambda h, qi, ki: (h, qi, 0)),
                  pl.BlockSpec((1, tk, hd), lambda h, qi, ki: (h, ki, 0)),
                  pl.BlockSpec((1, tk, hd), lambda h, qi, ki: (h, ki, 0))],
        out_specs=(pl.BlockSpec((tq, g * hd), lambda h, qi, ki: (qi, h)),
                   pl.BlockSpec((g, tq, 1), lambda h, qi, ki: (h, qi, 0))),
        out_shape=(jax.ShapeDtypeStruct((dm.R, dm.A), F32), jax.ShapeDtypeStruct((dm.NH, dm.R, 1), F32)),
        scratch_shapes=[pltpu.VMEM((g * tq, LANES), F32), pltpu.VMEM((g * tq, LANES), F32), pltpu.VMEM((g * tq, hd), F32)],
        compiler_params=_params(("parallel", "parallel", "arbitrary")),
    )(qh, kh, vh)


def _flash_bwd(dm, qh, qt, kh, vh, doh, dot, lse, delta):
    tq, ks, tk = dm.TR, dm.TR, dm.TK
    g, hd = dm.G, dm.HD
    nq, nk = dm.R // tq, dm.R // tk
    nq_lat = dm.N // tq
    n_chunks, ctx_chunks = tk // ks, dm.CTX // ks

    def body(q_ref, qt_ref, k_ref, v_ref, do_ref, dot_ref, lse_ref, dl_ref, dq_ref, dk_ref, dv_ref, dk_acc, dv_acc):
        ki, qi = pl.program_id(1), pl.program_id(2)

        @pl.when(jnp.logical_and(ki == 0, qi == 0))
        def _():
            dq_ref[...] = jnp.zeros_like(dq_ref)

        @pl.when(qi == 0)
        def _():
            dk_acc[...] = jnp.zeros_like(dk_acc)
            dv_acc[...] = jnp.zeros_like(dv_acc)

        def attend(chunks):
            q = q_ref[...].reshape(g * tq, hd)
            do = do_ref[...].reshape(g * tq, hd)
            q_t = jnp.concatenate([qt_ref[gg * hd:(gg + 1) * hd, :] for gg in range(g)], axis=1)
            do_t = jnp.concatenate([dot_ref[gg * hd:(gg + 1) * hd, :] for gg in range(g)], axis=1)
            lse_c = lse_ref[...].reshape(g * tq, 1)
            dl_c = dl_ref[...].reshape(g * tq, 1)
            dqs = jnp.zeros((g * tq, hd), F32)
            for kc in chunks:
                keys = slice(kc * ks, (kc + 1) * ks)
                k, v = k_ref[0, keys, :], v_ref[0, keys, :]
                s = lax.dot_general(q, k, _DN["nt"], preferred_element_type=F32)
                p = jnp.exp(s - lse_c)
                dp = lax.dot_general(do, v, _DN["nt"], preferred_element_type=F32)
                ds = (p * (dp - dl_c)).astype(BF16)
                dv_acc[:, keys] += jnp.dot(do_t, p.astype(BF16), preferred_element_type=F32)
                dk_acc[:, keys] += jnp.dot(q_t, ds, preferred_element_type=F32)
                dqs = dqs + jnp.dot(ds, k, preferred_element_type=F32)
            r0 = pl.multiple_of(qi * tq, tq)
            for gg in range(g):
                dq_ref[pl.ds(r0, tq), gg * hd:(gg + 1) * hd] += dqs[gg * tq:(gg + 1) * tq]

        @pl.when(qi < nq_lat)
        def _():
            attend(range(n_chunks))

        @pl.when(jnp.logical_and(qi >= nq_lat, ki == nk - 1))
        def _():
            attend(range(n_chunks - ctx_chunks, n_chunks))

        @pl.when(qi == nq - 1)
        def _():
            dk_ref[...] = dk_acc[...]
            dv_ref[...] = dv_acc[...]

    qspec = pl.BlockSpec((g, tq, hd), lambda h, ki, qi: (h, qi, 0))
    tspec = pl.BlockSpec((g * hd, tq), lambda h, ki, qi: (h, qi))
    kspec = pl.BlockSpec((1, tk, hd), lambda h, ki, qi: (h, ki, 0))
    sspec = pl.BlockSpec((g, tq, 1), lambda h, ki, qi: (h, qi, 0))
    ospec = pl.BlockSpec((hd, tk), lambda h, ki, qi: (h, ki))
    return pl.pallas_call(
        body, name="flash_bwd", grid=(dm.NKV, nk, nq),
        in_specs=[qspec, tspec, kspec, kspec, qspec, tspec, sspec, sspec],
        out_specs=(pl.BlockSpec((dm.R, g * hd), lambda h, ki, qi: (0, h)), ospec, ospec),
        out_shape=(jax.ShapeDtypeStruct((dm.R, dm.A), F32), jax.ShapeDtypeStruct((dm.KVW, dm.R), F32),
                   jax.ShapeDtypeStruct((dm.KVW, dm.R), F32)),
        scratch_shapes=[pltpu.VMEM((hd, tk), F32), pltpu.VMEM((hd, tk), F32)],
        compiler_params=_params(("parallel", "arbitrary", "arbitrary")),
    )(qh, qt, kh, vh, doh, dot, lse, delta)


def _gate_b_specs(dm):
    ogb = 3 * dm.C + dm.A + 2 * dm.KVW
    half = dm.A // 2
    return [_rows(dm.TR, half, _col(ogb, half)), _rows(dm.TR, half, _col(ogb, half) + 1)]


def _attn_gate(dm, o, p):
    tr = dm.TR

    def body(o_ref, g0_ref, g1_ref, za_ref):
        gate = jnp.concatenate([g0_ref[...], g1_ref[...]], axis=1)
        za_ref[...] = (o_ref[...] * jax.nn.silu(gate)).astype(BF16)

    return pl.pallas_call(
        body, name="attn_gate", grid=(dm.R // tr,),
        in_specs=[_rows(tr, dm.A)] + _gate_b_specs(dm), out_specs=_rows(tr, dm.A),
        out_shape=jax.ShapeDtypeStruct((dm.R, dm.A), BF16), compiler_params=_params(("parallel",)),
    )(o, p, p)


def _attn_gate_bwd(dm, dza, o, p):
    tr, hd = dm.TR, dm.HD

    def body(dza_ref, o_ref, g0_ref, g1_ref, doh_ref, dot_ref, dl_ref, dgb_ref):
        gate = jnp.concatenate([g0_ref[...], g1_ref[...]], axis=1)
        dza, o = dza_ref[...], o_ref[...]
        do = dza * jax.nn.silu(gate)
        dgb_ref[...] = (dza * o * _dsilu(gate)).astype(BF16)
        dot_ref[...] = do.T.astype(BF16)
        prod = do * o
        for h in range(dm.NH):
            sl = slice(h * hd, (h + 1) * hd)
            doh_ref[h] = do[:, sl].astype(BF16)
            dl_ref[h] = jnp.sum(prod[:, sl], axis=-1, keepdims=True)

    return pl.pallas_call(
        body, name="attn_gate_bwd", grid=(dm.R // tr,),
        in_specs=[_rows(tr, dm.A), _rows(tr, dm.A)] + _gate_b_specs(dm),
        out_specs=(pl.BlockSpec((dm.NH, tr, hd), lambda i: (0, i, 0)), pl.BlockSpec((dm.A, tr), lambda i: (0, i)),
                   pl.BlockSpec((dm.NH, tr, 1), lambda i: (0, i, 0)), _rows(tr, dm.A)),
        out_shape=(jax.ShapeDtypeStruct((dm.NH, dm.R, hd), BF16), jax.ShapeDtypeStruct((dm.A, dm.R), BF16),
                   jax.ShapeDtypeStruct((dm.NH, dm.R, 1), F32), jax.ShapeDtypeStruct((dm.R, dm.A), BF16)),
        compiler_params=_params(("parallel",)),
    )(dza, o, p, p)


def _halo_specs(dm, col):
    per = dm.TR // HALO
    last = dm.R // HALO - 1
    prev = pl.BlockSpec((HALO, dm.C), lambda i: (jnp.maximum(i * per - 1, 0), col))
    nxt = pl.BlockSpec((HALO, dm.C), lambda i: (jnp.minimum((i + 1) * per, last), col))
    return prev, nxt


def _seg_valid(dm, i):
    tr = dm.TR
    r = i * tr - HALO + lax.broadcasted_iota(jnp.int32, (tr + 2 * HALO, 1), 0)
    is_ctx = i * tr >= dm.N
    lo = jnp.where(is_ctx, dm.N, 0)
    hi = jnp.where(is_ctx, dm.R, dm.N)
    return jnp.logical_and(r >= lo, r < hi)


def _conv_fwd(dm, p, conv_w_l, conv_b_l, ln_g_l, ln_b_l):
    tr, c = dm.TR, dm.C

    def body(a_ref, g_ref, gate_ref, ap_ref, an_ref, gp_ref, gn_ref, w_ref, b_ref, lg_ref, lb_ref,
             zc_ref, y1_ref, u_ref):
        i = pl.program_id(0)
        a = jnp.concatenate([ap_ref[...], a_ref[...], an_ref[...]], axis=0)
        g = jnp.concatenate([gp_ref[...], g_ref[...], gn_ref[...]], axis=0)
        u_ref[...] = jnp.where(_seg_valid(dm, i), a * jax.nn.sigmoid(g), 0.0)
        for cc in range(c // LANES):
            sl = slice(cc * LANES, (cc + 1) * LANES)
            uc = u_ref[:, sl]
            acc = jnp.zeros((tr, LANES), F32)
            for j in range(CONV_KERNEL):
                off = HALO - CONV_PAD + j
                acc = acc + w_ref[j:j + 1, sl] * uc[off:off + tr]
            y1_ref[:, sl] = acc + b_ref[:, sl]
        zc_ref[...] = _convact_fn(y1_ref[...], gate_ref[...], lg_ref[...], lb_ref[...]).astype(BF16)

    ap, an = _halo_specs(dm, 0)
    gp, gn = _halo_specs(dm, 1)
    return pl.pallas_call(
        body, name="conv_fwd", grid=(dm.R // tr,),
        in_specs=[_rows(tr, c, 0), _rows(tr, c, 1), _rows(tr, c, 2), ap, an, gp, gn,
                  _whole((32, c)), _whole((1, c)), _whole((1, c)), _whole((1, c))],
        out_specs=(_rows(tr, c), _rows(tr, c)),
        out_shape=(jax.ShapeDtypeStruct((dm.R, c), BF16), jax.ShapeDtypeStruct((dm.R, c), F32)),
        scratch_shapes=[pltpu.VMEM((tr + 2 * HALO, c), F32)],
        compiler_params=_params(("parallel",)),
    )(p, p, p, p, p, p, p, conv_w_l, conv_b_l, ln_g_l, ln_b_l)


def _conv_act_bwd(dm, dzc, y1, p, ln_g_l, ln_b_l):
    tr, c = dm.TR, dm.C

    def body(dz_ref, y1_ref, gate_ref, lg_ref, lb_ref, dy_ref, dgate_ref, dlg_ref, dlb_ref):
        i = pl.program_id(0)
        _, vjp = jax.vjp(_convact_fn, y1_ref[...], gate_ref[...], lg_ref[...], lb_ref[...])
        dy, dgate, dlg, dlb = vjp(dz_ref[...])
        dy_ref[...] = dy
        dgate_ref[...] = dgate.astype(BF16)

        @pl.when(i == 0)
        def _():
            dlg_ref[...] = jnp.zeros_like(dlg_ref)
            dlb_ref[...] = jnp.zeros_like(dlb_ref)

        dlg_ref[...] += dlg
        dlb_ref[...] += dlb

    return pl.pallas_call(
        body, name="conv_act_bwd", grid=(dm.R // tr,),
        in_specs=[_rows(tr, c), _rows(tr, c), _rows(tr, c, 2), _whole((1, c)), _whole((1, c))],
        out_specs=(_rows(tr, c), _rows(tr, c), _whole((1, c)), _whole((1, c))),
        out_shape=(jax.ShapeDtypeStruct((dm.R, c), F32), jax.ShapeDtypeStruct((dm.R, c), BF16),
                   jax.ShapeDtypeStruct((1, c), F32), jax.ShapeDtypeStruct((1, c), F32)),
        compiler_params=_params(("arbitrary",)),
    )(dzc, y1, p, ln_g_l, ln_b_l)


def _conv_bwd(dm, dy1, p, conv_w_l):
    tr, c = dm.TR, dm.C

    def body(dy_ref, dyp_ref, dyn_ref, a_ref, g_ref, ap_ref, an_ref, gp_ref, gn_ref, w_ref,
             dag_ref, dw_ref, db_ref, u_ref, dyp_scr, du_ref):
        i = pl.program_id(0)
        valid = _seg_valid(dm, i)
        a = jnp.concatenate([ap_ref[...], a_ref[...], an_ref[...]], axis=0)
        g = jnp.concatenate([gp_ref[...], g_ref[...], gn_ref[...]], axis=0)
        u_ref[...] = jnp.where(valid, a * jax.nn.sigmoid(g), 0.0)
        dyp_scr[...] = jnp.where(valid, jnp.concatenate([dyp_ref[...], dy_ref[...], dyn_ref[...]], axis=0), 0.0)

        @pl.when(i == 0)
        def _():
            dw_ref[...] = jnp.zeros_like(dw_ref)
            db_ref[...] = jnp.zeros_like(db_ref)

        db_ref[...] += jnp.sum(dy_ref[...], axis=0, keepdims=True)
        for cc in range(c // LANES):
            sl = slice(cc * LANES, (cc + 1) * LANES)
            uc = u_ref[:, sl]
            dyc = dyp_scr[:, sl]
            dy_cur = dyc[HALO:HALO + tr]
            acc = jnp.zeros((tr, LANES), F32)
            for j in range(CONV_KERNEL):
                off = HALO + CONV_PAD - j
                acc = acc + w_ref[j:j + 1, sl] * dyc[off:off + tr]
                offu = HALO - CONV_PAD + j
                dw_ref[j:j + 1, sl] += jnp.sum(dy_cur * uc[offu:offu + tr], axis=0, keepdims=True)
            du_ref[:, sl] = acc
        du = du_ref[...]
        sg = jax.nn.sigmoid(g_ref[...])
        dag_ref[:, :c] = (du * sg).astype(BF16)
        dag_ref[:, c:] = (du * a_ref[...] * sg * (1.0 - sg)).astype(BF16)

    ap, an = _halo_specs(dm, 0)
    gp, gn = _halo_specs(dm, 1)
    dyp, dyn = _halo_specs(dm, 0)
    return pl.pallas_call(
        body, name="conv_bwd", grid=(dm.R // tr,),
        in_specs=[_rows(tr, c), dyp, dyn, _rows(tr, c, 0), _rows(tr, c, 1), ap, an, gp, gn, _whole((32, c))],
        out_specs=(_rows(tr, 2 * c), _whole((32, c)), _whole((1, c))),
        out_shape=(jax.ShapeDtypeStruct((dm.R, 2 * c), BF16), jax.ShapeDtypeStruct((32, c), F32),
                   jax.ShapeDtypeStruct((1, c), F32)),
        scratch_shapes=[pltpu.VMEM((tr + 2 * HALO, c), F32), pltpu.VMEM((tr + 2 * HALO, c), F32), pltpu.VMEM((tr, c), F32)],
        compiler_params=_params(("arbitrary",)),
    )(dy1, dy1, dy1, p, p, p, p, p, p, conv_w_l)


def _gm_specs(dm):
    half = dm.D // 2
    first = _col(3 * dm.C + 2 * dm.A + 2 * dm.KVW, half)
    return [_rows(dm.TR, half, first + k) for k in range(4)]


def _gm_load(a0, a1, b0, b1):
    return jnp.concatenate([a0[...], a1[...]], axis=1), jnp.concatenate([b0[...], b1[...]], axis=1)


def _merge(dm, yc, ya, p):
    tr, d = dm.TR, dm.D

    def body(yc_ref, ya_ref, a0, a1, b0, b1, m_ref):
        ga, gb = _gm_load(a0, a1, b0, b1)
        m_ref[...] = (jax.nn.sigmoid(ga) * yc_ref[...] + jax.nn.sigmoid(gb) * ya_ref[...]).astype(BF16)

    return pl.pallas_call(
        body, name="merge", grid=(dm.R // tr,),
        in_specs=[_rows(tr, d), _rows(tr, d)] + _gm_specs(dm), out_specs=_rows(tr, d),
        out_shape=jax.ShapeDtypeStruct((dm.R, d), BF16), compiler_params=_params(("parallel",)),
    )(yc, ya, p, p, p, p)


def _merge_bwd(dm, dmerged, yc, ya, p):
    tr, d = dm.TR, dm.D

    def body(dm_ref, yc_ref, ya_ref, a0, a1, b0, b1, dyc_ref, dya_ref, dgm_ref):
        dmv = dm_ref[...]
        ga, gb = _gm_load(a0, a1, b0, b1)
        sa, sb = jax.nn.sigmoid(ga), jax.nn.sigmoid(gb)
        dyc_ref[...] = (dmv * sa).astype(BF16)
        dya_ref[...] = (dmv * sb).astype(BF16)
        dgm_ref[:, :d] = (dmv * yc_ref[...] * sa * (1.0 - sa)).astype(BF16)
        dgm_ref[:, d:] = (dmv * ya_ref[...] * sb * (1.0 - sb)).astype(BF16)

    return pl.pallas_call(
        body, name="merge_bwd", grid=(dm.R // tr,),
        in_specs=[_rows(tr, d), _rows(tr, d), _rows(tr, d)] + _gm_specs(dm),
        out_specs=(_rows(tr, d), _rows(tr, d), _rows(tr, 2 * d)),
        out_shape=(jax.ShapeDtypeStruct((dm.R, d), BF16), jax.ShapeDtypeStruct((dm.R, d), BF16),
                   jax.ShapeDtypeStruct((dm.R, 2 * d), BF16)),
        compiler_params=_params(("parallel",)),
    )(dmerged, yc, ya, p, p, p, p)


def _loss_head(dm, xc, target):
    tr, d = dm.TR, dm.D
    n_lat = dm.N // tr

    def body(x_ref, t_ref, dy_ref, l_ref):
        i = pl.program_id(0)

        @pl.when(i == 0)
        def _():
            l_ref[...] = jnp.zeros_like(l_ref)

        @pl.when(i < n_lat)
        def _():
            e = x_ref[...] - t_ref[...]
            dy_ref[...] = e * (1.0 / d)
            l_ref[...] += 0.5 * jnp.sum(jnp.mean(e * e, axis=-1, keepdims=True), axis=0, keepdims=True)

        @pl.when(i >= n_lat)
        def _():
            dy_ref[...] = jnp.zeros_like(dy_ref)

    return pl.pallas_call(
        body, name="loss_head", grid=(dm.R // tr,),
        in_specs=[_rows(tr, d), pl.BlockSpec((tr, d), lambda i: (jnp.minimum(i, n_lat - 1), 0))],
        out_specs=(_rows(tr, d), _whole((1, 1))),
        out_shape=(jax.ShapeDtypeStruct((dm.R, d), F32), jax.ShapeDtypeStruct((1, 1), F32)),
        compiler_params=_params(("arbitrary",)),
    )(xc, target)


def _adamw(w, g, m, v, name):
    rows, cols = w.shape
    tr = _pick(rows, (512, 256, 128, 64, 32, 16, 8, rows))

    def body(w_ref, g_ref, m_ref, v_ref, d_ref, mo_ref, vo_ref):
        gv = g_ref[...]
        mn = ADAM_B1 * m_ref[...] + (1.0 - ADAM_B1) * gv
        vn = ADAM_B2 * v_ref[...] + (1.0 - ADAM_B2) * jnp.square(gv)
        m_hat = mn / (1.0 - ADAM_B1 ** ADAM_STEP)
        v_hat = vn / (1.0 - ADAM_B2 ** ADAM_STEP)
        d_ref[...] = -ADAM_LR * (m_hat / (jnp.sqrt(v_hat) + ADAM_EPS) + ADAM_WD * w_ref[...])
        mo_ref[...] = mn
        vo_ref[...] = vn

    spec = _rows(tr, cols)
    shp = jax.ShapeDtypeStruct(w.shape, F32)
    return pl.pallas_call(
        body, name=name, grid=(rows // tr,), in_specs=[spec] * 4, out_specs=(spec,) * 3, out_shape=(shp,) * 3,
        compiler_params=_params(("parallel",)),
    )(w, g, m, v)


def _rope_tables(dm):
    n = dm.N
    rows = n // GRID_W
    row = jnp.repeat(jnp.arange(rows, dtype=F32), GRID_W)
    col = jnp.tile(jnp.arange(GRID_W, dtype=F32), rows)
    half = dm.HD // 2
    inv_freq = ROPE_THETA ** (-jnp.arange(0, half, 2, dtype=F32) / half)
    ang = jnp.concatenate([row[:, None] * inv_freq, col[:, None] * inv_freq], axis=-1)
    cos, sin = jnp.cos(ang), jnp.sin(ang)
    cos = jnp.concatenate([cos, jnp.ones((dm.CTX, half), F32)], axis=0)
    sin = jnp.concatenate([sin, jnp.zeros((dm.CTX, half), F32)], axis=0)
    reps = LANES // dm.HD
    return jnp.tile(jnp.concatenate([cos, cos], axis=1), (1, reps)), jnp.tile(jnp.concatenate([-sin, sin], axis=1), (1, reps))


def _lane_vec(g, hd):
    return jnp.tile(g.reshape(1, hd), (1, LANES // hd))


def _layer_fwd(dm, xc, cv, wl, tabs):
    mod = _mod_fwd(cv, wl["w_mod"], wl["b_mod"])
    h = _prenorm(dm, xc, wl["g_pre"], mod)
    p = _matmul(h, wl["w_in"], "nn", "in_proj")
    qh, qt, kh, vh = _qk_prep(dm, p, tabs["cos"], tabs["sins"], wl["gq"], wl["gk"], tabs["bd"])
    o, lse = _flash_fwd(dm, qh, kh, vh)
    za = _attn_gate(dm, o, p)
    zc, y1 = _conv_fwd(dm, p, wl["conv_w"], wl["conv_b"], wl["ln_g"], wl["ln_b"])
    yc = _matmul(zc, wl["w_conv_out"], "nn", "conv_out")
    ya = _matmul(za, wl["w_attn_out"], "nn", "attn_out")
    merged = _merge(dm, yc, ya, p)
    outp = _matmul(merged, wl["w_out"], "nn", "out_proj")
    x_new = _post(dm, xc, outp, wl["g_post"], mod)
    saved = dict(xc=xc, mod=mod, h=h, p=p, qh=qh, qt=qt, kh=kh, vh=vh, o=o, lse=lse, za=za, zc=zc, y1=y1, yc=yc, ya=ya,
                 merged=merged, outp=outp)
    return x_new, saved


def _layer_bwd(dm, dxn, cv, wl, tabs, s):
    p = s["p"]
    d_outp, dg_post, dgt = _post_bwd(dm, s["outp"], wl["g_post"], s["mod"], dxn)
    dmerged = _matmul(d_outp, wl["w_out"], "nt", "out_proj_dx")
    dw_out = _matmul(s["merged"], d_outp, "tn", "out_proj_dw")
    dyc, dya, dgm = _merge_bwd(dm, dmerged, s["yc"], s["ya"], p)
    dzc = _matmul(dyc, wl["w_conv_out"], "nt", "conv_out_dx")
    dw_conv_out = _matmul(s["zc"], dyc, "tn", "conv_out_dw")
    dza = _matmul(dya, wl["w_attn_out"], "nt", "attn_out_dx")
    dw_attn_out = _matmul(s["za"], dya, "tn", "attn_out_dw")
    doh, dot, delta, dgate_b = _attn_gate_bwd(dm, dza, s["o"], p)
    dq, dkt, dvt = _flash_bwd(dm, s["qh"], s["qt"], s["kh"], s["vh"], doh, dot, s["lse"], delta)
    dpq, dpkv, dgq, dgk = _qk_bwd(dm, p, dq, dkt, dvt, tabs["cos"], tabs["sins"], wl["gq"], wl["gk"], tabs["bd"])
    dy1, dgate_a, dln_g, dln_b = _conv_act_bwd(dm, dzc, s["y1"], p, wl["ln_g"], wl["ln_b"])
    dpag, dconv_w, dconv_b = _conv_bwd(dm, dy1, p, wl["conv_w"])
    dp = jnp.concatenate([dpag, dgate_a, dpq, dpkv, dgate_b, dgm], axis=1)
    dh = _matmul(dp, wl["w_in"], "nt", "in_proj_dx")
    dw_in = _matmul(s["h"], dp, "tn", "in_proj_dw")
    dxc, dg_pre, dshsc = _prenorm_bwd(dm, s["xc"], wl["g_pre"], s["mod"], dh, dxn)
    dmod = jnp.concatenate([dshsc, dgt], axis=1)
    dw_mod, db_mod, dcv = _mod_bwd(cv, wl["w_mod"], dmod)
    grads = dict(w_mod=dw_mod, b_mod=db_mod[0], g_pre=dg_pre[0], g_post=dg_post[0], w_in=dw_in,
                 conv_w=dconv_w[:CONV_KERNEL], conv_b=dconv_b[0], ln_g=dln_g[0], ln_b=dln_b[0],
                 w_conv_out=dw_conv_out, q_norm_g=dgq[0, :dm.HD], k_norm_g=dgk[0, :dm.HD],
                 w_attn_out=dw_attn_out, w_out=dw_out, c_ctx=dcv[1])
    return dxc, grads


def _local_step(dm, x, c, ctx, c_ctx, target, wf):
    depth = wf["w_in"].shape[0]
    cos, sins = _rope_tables(dm)
    lane = np.arange(LANES)
    bd = jnp.asarray((lane[:, None] // dm.HD == lane[None, :] // dm.HD), dtype=BF16)
    tabs = dict(cos=cos, sins=sins, bd=bd)
    cv = jnp.concatenate([c.reshape(1, dm.D), c_ctx.reshape(1, dm.D), jnp.zeros((6, dm.D), F32)], axis=0)
    xc = jnp.concatenate([x, ctx], axis=0)
    layers, saved = [], []
    for l in range(depth):
        wl = dict(
            w_mod=wf["w_mod"][l], b_mod=wf["b_mod"][l].reshape(1, -1), g_pre=wf["g_pre"][l].reshape(1, -1),
            g_post=wf["g_post"][l].reshape(1, -1), w_in=wf["w_in"][l],
            conv_w=jnp.concatenate([wf["conv_w"][l], jnp.zeros((1, dm.C), F32)], axis=0),
            conv_b=wf["conv_b"][l].reshape(1, -1), ln_g=wf["ln_g"][l].reshape(1, -1), ln_b=wf["ln_b"][l].reshape(1, -1),
            w_conv_out=wf["w_conv_out"][l], gq=_lane_vec(wf["q_norm_g"][l], dm.HD), gk=_lane_vec(wf["k_norm_g"][l], dm.HD),
            w_attn_out=wf["w_attn_out"][l], w_out=wf["w_out"][l])
        layers.append(wl)
        xc, s = _layer_fwd(dm, xc, cv, wl, tabs)
        saved.append(s)
    dxc, loss = _loss_head(dm, xc, target)
    grads = [None] * depth
    for l in reversed(range(depth)):
        dxc, grads[l] = _layer_bwd(dm, dxc, cv, layers[l], tabs, saved[l])
    gw = {k: jnp.stack([g[k] for g in grads]) for k in grads[0] if k != "c_ctx"}
    gw["c_ctx"] = grads[0]["c_ctx"] + grads[1]["c_ctx"] if depth == 2 else sum(g["c_ctx"] for g in grads)
    return loss[0, 0], dxc[:dm.N], gw


def _dims(x, ctx, w_in_full_cols, conv_c, attn_w, hd):
    n, d = x.shape
    ctx_len = ctx.shape[0]
    c, a = conv_c, attn_w
    kvw = (w_in_full_cols - 3 * c - 2 * a - 2 * d) // 2
    r = n + ctx_len
    tr = _pick(np.gcd(n, ctx_len), (256, 128))
    tk = tr * _pick(r // tr, (3, 5, 1))
    assert ctx_len <= tk and ctx_len % tr == 0
    return Dims(N=n, CTX=ctx_len, R=r, D=d, C=c, A=a, KVW=kvw, HD=hd, NH=a // hd, NKV=kvw // hd,
                G=a // kvw, W=w_in_full_cols, TR=tr, TK=tk)


_HBM = pl.BlockSpec(memory_space=pltpu.HBM)
_VMEM = pl.BlockSpec(memory_space=pltpu.VMEM)


def _place():
    x, y, c = lax.axis_index("x"), lax.axis_index("y"), lax.axis_index("c")
    chips = [(1 - x, y), (x, 1 - y), (1 - x, 1 - y)]
    return x, y, c, chips


def _row_pieces(rows, want):
    n = max(k for k in range(1, want + 1) if rows % (16 * k) == 0)
    return [pl.ds(i * (rows // n), rows // n) for i in range(n)]


def _allgather_chips(own, name):
    _, h, w = own.shape
    pieces = _row_pieces(h, 4)
    npc = len(pieces)

    def body(own_ref, out_ref, send_sems, recv_sems, local_sem):
        x, y, c, chips = _place()
        s_me = 2 * x + y
        sibling = (x, y, 1 - c)

        def copy(k, src, dst, to):
            return pltpu.make_async_remote_copy(src_ref=src, dst_ref=dst, send_sem=send_sems.at[k], recv_sem=recv_sems.at[k],
                                                device_id=to, device_id_type=MESH)

        local = pltpu.make_async_copy(own_ref, out_ref.at[s_me], local_sem)
        local.start()
        sends = []
        for i, rows in enumerate(pieces):
            for j, chip in enumerate(chips):
                sends.append(copy(i * 3 + j, own_ref.at[c, rows], out_ref.at[s_me, c, rows], (*chip, c)))
        for cp in sends:
            cp.start()
        for i, rows in enumerate(pieces):
            for j, (cx, cy) in enumerate(chips):
                landed = out_ref.at[2 * cx + cy, c, rows]
                copy(i * 3 + j, landed, landed, (cx, cy, c)).wait_recv()
                fwd = copy(3 * npc + i * 3 + j, landed, landed, sibling)
                fwd.start()
                sends.append(fwd)
        for i, rows in enumerate(pieces):
            for j, (cx, cy) in enumerate(chips):
                theirs = out_ref.at[2 * cx + cy, 1 - c, rows]
                copy(3 * npc + i * 3 + j, theirs, theirs, sibling).wait_recv()
        for cp in sends:
            cp.wait_send()
        local.wait()

    return pl.pallas_call(
        body, name=name, in_specs=[_HBM], out_specs=_HBM,
        out_shape=jax.ShapeDtypeStruct((N_CHIPS, 2, h, w), own.dtype),
        scratch_shapes=[pltpu.SemaphoreType.DMA((6 * npc,)), pltpu.SemaphoreType.DMA((6 * npc,)), pltpu.SemaphoreType.DMA],
    )(own)


def _pair_exchange(buf, name):
    _, m, w = buf.shape
    pieces = _row_pieces(m, 16)

    def body(buf_ref, out_ref, send_sems, recv_sems):
        x, y, c, _ = _place()
        cps = [pltpu.make_async_remote_copy(src_ref=buf_ref.at[1 - c, rows], dst_ref=out_ref.at[rows], send_sem=send_sems.at[i],
                                            recv_sem=recv_sems.at[i], device_id=(x, y, 1 - c), device_id_type=MESH)
               for i, rows in enumerate(pieces)]
        for cp in cps:
            cp.start()
        for cp in cps:
            cp.wait()

    return pl.pallas_call(
        body, name=name, in_specs=[_HBM], out_specs=_HBM, out_shape=jax.ShapeDtypeStruct((m, w), buf.dtype),
        scratch_shapes=[pltpu.SemaphoreType.DMA((len(pieces),)), pltpu.SemaphoreType.DMA((len(pieces),))],
    )(buf)


def _pair_share(t, name):
    m, w = t.shape
    pieces = _row_pieces(m, 8)
    n = len(pieces)

    def body(t_ref, out_ref, send_sems, recv_sems, local_sems):
        x, y, c, _ = _place()
        cps = [pltpu.make_async_remote_copy(src_ref=t_ref.at[rows], dst_ref=out_ref.at[c, rows], send_sem=send_sems.at[i],
                                            recv_sem=recv_sems.at[i], device_id=(x, y, 1 - c), device_id_type=MESH)
               for i, rows in enumerate(pieces)]
        locs = [pltpu.make_async_copy(t_ref.at[rows], out_ref.at[c, rows], local_sems.at[i]) for i, rows in enumerate(pieces)]
        for cp in cps + locs:
            cp.start()
        for cp in cps + locs:
            cp.wait()

    return pl.pallas_call(
        body, name=name, in_specs=[_HBM], out_specs=_HBM, out_shape=jax.ShapeDtypeStruct((2, m, w), t.dtype),
        scratch_shapes=[pltpu.SemaphoreType.DMA((n,)), pltpu.SemaphoreType.DMA((n,)), pltpu.SemaphoreType.DMA((n,))],
    )(t)


def _scatter_chips(s, name):
    _, h, w = s.shape

    def body(s_ref, out_ref, send_sems, recv_sems):
        _, _, c, chips = _place()
        cps = [pltpu.make_async_remote_copy(src_ref=s_ref.at[2 * cx + cy], dst_ref=out_ref.at[j], send_sem=send_sems.at[j],
                                            recv_sem=recv_sems.at[j], device_id=(cx, cy, c), device_id_type=MESH)
               for j, (cx, cy) in enumerate(chips)]
        for cp in cps:
            cp.start()
        for cp in cps:
            cp.wait()

    return pl.pallas_call(
        body, name=name, in_specs=[_HBM], out_specs=_HBM, out_shape=jax.ShapeDtypeStruct((3, h, w), s.dtype),
        scratch_shapes=[pltpu.SemaphoreType.DMA((3,)), pltpu.SemaphoreType.DMA((3,))],
    )(s)


def _sum_pair(buf, recv, core, name):
    m, w = recv.shape
    tr = _pick(m, (512, 432, 256, 128, 64, 32, 16))

    def body(c_ref, b_ref, r_ref, o_ref, ob_ref):
        v = b_ref[0] + r_ref[...]
        o_ref[...] = v
        ob_ref[...] = v.astype(BF16)

    spec = pl.BlockSpec((tr, w), lambda i, c: (i, 0))
    return pl.pallas_call(
        body, name=name,
        grid_spec=pltpu.PrefetchScalarGridSpec(
            num_scalar_prefetch=1, grid=(m // tr,),
            in_specs=[pl.BlockSpec((1, tr, w), lambda i, c: (c[0], i, 0)), spec], out_specs=(spec, spec)),
        out_shape=(jax.ShapeDtypeStruct((m, w), F32), jax.ShapeDtypeStruct((m, w), BF16)),
        compiler_params=_params(("parallel",)),
    )(core, buf, recv)


def _sum_chips(own, recv, chip, name):
    _, h, w = own.shape
    tr = _pick(h, (512, 432, 256, 128, 64, 32, 16))

    def body(s_ref, a_ref, r_ref, o_ref):
        acc = a_ref[0]
        for j in range(3):
            acc = acc + r_ref[j].astype(F32)
        o_ref[...] = acc

    return pl.pallas_call(
        body, name=name,
        grid_spec=pltpu.PrefetchScalarGridSpec(
            num_scalar_prefetch=1, grid=(h // tr,),
            in_specs=[pl.BlockSpec((1, tr, w), lambda i, s: (s[0], i, 0)), pl.BlockSpec((3, tr, w), lambda i, s: (0, i, 0))],
            out_specs=pl.BlockSpec((tr, w), lambda i, s: (i, 0))),
        out_shape=jax.ShapeDtypeStruct((h, w), F32), compiler_params=_params(("parallel",)),
    )(chip, own, recv)


def _allreduce_small(own, name):
    m, w = own.shape

    def body(own_ref, sum_ref, all_ref, send_sems, recv_sems):
        x, y, c, _ = _place()
        me = 4 * x + 2 * y + c
        all_ref[me] = own_ref[...]
        sends = []
        for k in range(1, N_DEV):
            peer = (x ^ ((k >> 2) & 1), y ^ ((k >> 1) & 1), c ^ (k & 1))
            cp = pltpu.make_async_remote_copy(src_ref=own_ref, dst_ref=all_ref.at[me], send_sem=send_sems.at[k - 1],
                                              recv_sem=recv_sems.at[k - 1], device_id=peer, device_id_type=MESH)
            cp.start()
            sends.append(cp)
        for k in range(1, N_DEV):
            px, py, pc = x ^ ((k >> 2) & 1), y ^ ((k >> 1) & 1), c ^ (k & 1)
            slot = all_ref.at[4 * px + 2 * py + pc]
            pltpu.make_async_remote_copy(src_ref=slot, dst_ref=slot, send_sem=send_sems.at[k - 1], recv_sem=recv_sems.at[k - 1],
                                         device_id=(px, py, pc), device_id_type=MESH).wait_recv()
        for cp in sends:
            cp.wait_send()
        acc = all_ref[0]
        for d in range(1, N_DEV):
            acc = acc + all_ref[d]
        sum_ref[...] = acc

    return pl.pallas_call(
        body, name=name, in_specs=[_VMEM], out_specs=_VMEM, out_shape=jax.ShapeDtypeStruct((m, w), F32),
        scratch_shapes=[pltpu.VMEM((N_DEV, m, w), F32), pltpu.SemaphoreType.DMA((N_DEV - 1,)), pltpu.SemaphoreType.DMA((N_DEV - 1,))],
        compiler_params=pltpu.CompilerParams(vmem_limit_bytes=VMEM_LIMIT),
    )(own)


_COL_SHARDED = ("w_mod", "w_in")
_ROW_SHARDED = ("w_conv_out", "w_attn_out", "w_out")
_BIG = _COL_SHARDED + _ROW_SHARDED
_SMALL = ("c_ctx", "b_mod", "g_pre", "g_post", "conv_b", "ln_g", "ln_b", "q_norm_g", "k_norm_g")
FLAT_W = 1024


def _flatten_shards(shards):
    flat = jnp.concatenate([shards[k].reshape(-1) for k in _BIG])
    return flat.reshape(-1, FLAT_W)


def _unflatten_shard(flat, like):
    out, off = {}, 0
    flat = flat.reshape(-1)
    for k in _BIG:
        n = int(np.prod(like[k].shape))
        out[k] = flat[off:off + n].reshape(like[k].shape)
        off += n
    return out


def _by_shard(k, full):
    l = full.shape[0]
    if k in _COL_SHARDED:
        d, cols = full.shape[1:]
        return full.reshape(l, d, N_CHIPS, cols // N_CHIPS).transpose(2, 0, 1, 3).reshape(N_CHIPS, -1)
    rows, d = full.shape[1:]
    return full.reshape(l, N_CHIPS, rows // N_CHIPS, d).transpose(1, 0, 2, 3).reshape(N_CHIPS, -1)


def _from_shards(k, flat4, like):
    l = like.shape[0]
    if k in _COL_SHARDED:
        d, cs = like.shape[1:]
        return flat4.reshape(N_CHIPS, l, d, cs).transpose(1, 2, 0, 3).reshape(l, d, N_CHIPS * cs)
    rs, d = like.shape[1:]
    return flat4.reshape(N_CHIPS, l, rs, d).transpose(1, 0, 2, 3).reshape(l, N_CHIPS * rs, d)


def _pad_rows(flat, mult=8):
    n = flat.shape[0]
    rows = -(-n // LANES)
    rows = -(-rows // mult) * mult
    return jnp.concatenate([flat, jnp.zeros((rows * LANES - n,), F32)]).reshape(rows, LANES)


def kernel(x, c, ctx, c_ctx, w_mod, b_mod, g_pre, g_post, w_in, conv_w, conv_b, ln_g, ln_b, w_conv_out, q_norm_g, k_norm_g, w_attn_out, w_out, loss_target, m_c_ctx, m_w_mod, m_b_mod, m_g_pre, m_g_post, m_w_in, m_conv_w, m_conv_b, m_ln_g, m_ln_b, m_w_conv_out, m_q_norm_g, m_k_norm_g, m_w_attn_out, m_w_out, v_c_ctx, v_w_mod, v_b_mod, v_g_pre, v_g_post, v_w_in, v_conv_w, v_conv_b, v_ln_g, v_ln_b, v_w_conv_out, v_q_norm_g, v_k_norm_g, v_w_attn_out, v_w_out):
    w = dict(c_ctx=c_ctx, w_mod=w_mod, b_mod=b_mod, g_pre=g_pre, g_post=g_post, w_in=w_in, conv_w=conv_w, conv_b=conv_b,
             ln_g=ln_g, ln_b=ln_b, w_conv_out=w_conv_out, q_norm_g=q_norm_g, k_norm_g=k_norm_g, w_attn_out=w_attn_out, w_out=w_out)
    m = dict(c_ctx=m_c_ctx, w_mod=m_w_mod, b_mod=m_b_mod, g_pre=m_g_pre, g_post=m_g_post, w_in=m_w_in, conv_w=m_conv_w,
             conv_b=m_conv_b, ln_g=m_ln_g, ln_b=m_ln_b, w_conv_out=m_w_conv_out, q_norm_g=m_q_norm_g, k_norm_g=m_k_norm_g,
             w_attn_out=m_w_attn_out, w_out=m_w_out)
    v = dict(c_ctx=v_c_ctx, w_mod=v_w_mod, b_mod=v_b_mod, g_pre=v_g_pre, g_post=v_g_post, w_in=v_w_in, conv_w=v_conv_w,
             conv_b=v_conv_b, ln_g=v_ln_g, ln_b=v_ln_b, w_conv_out=v_w_conv_out, q_norm_g=v_q_norm_g, k_norm_g=v_k_norm_g,
             w_attn_out=v_w_attn_out, w_out=v_w_out)
    order = ("c_ctx", "w_mod", "b_mod", "g_pre", "g_post", "w_in", "conv_w", "conv_b", "ln_g", "ln_b", "w_conv_out",
             "q_norm_g", "k_norm_g", "w_attn_out", "w_out")
    my_chip = 2 * lax.axis_index("x") + lax.axis_index("y")
    my_core = lax.axis_index("c")
    depth, _, conv_cs = conv_w.shape
    conv_c = conv_cs * N_CHIPS
    dm = _dims(x[0], ctx[0], w_in.shape[2] * N_CHIPS, conv_c, w_attn_out.shape[1] * N_CHIPS, q_norm_g.shape[1])

    w_flat = _flatten_shards(w)
    rows2 = w_flat.shape[0]
    own = w_flat.astype(BF16).reshape(2, rows2 // 2, FLAT_W)
    gathered = _allgather_chips(own, "gather_weights").reshape(N_CHIPS, -1)
    wf, off = {}, 0
    for k in _BIG:
        n = int(np.prod(w[k].shape))
        wf[k] = _from_shards(k, gathered[:, off:off + n], w[k])
        off += n
    cw = jnp.zeros((depth, CONV_KERNEL, N_CHIPS, conv_cs), F32)
    cw = lax.dynamic_update_slice(cw, conv_w[:, :, None, :], (0, 0, my_chip, 0))
    cw = jnp.where(my_core == 0, cw, 0.0).reshape(-1)
    wf["conv_w"] = _allreduce_small(_pad_rows(cw), "gather_conv_w").reshape(-1)[:cw.shape[0]].reshape(depth, CONV_KERNEL, conv_c)
    for k in _SMALL:
        wf[k] = w[k]

    loss_part, grad_x, gw = _local_step(dm, x[0], c, ctx[0], c_ctx, loss_target[0], wf)
    loss = lax.psum(loss_part, ("x", "y", "c"))

    g4 = jnp.concatenate([_by_shard(k, gw[k]) for k in _BIG], axis=1)
    hrows = rows2 // 2
    g24 = g4.reshape(N_CHIPS, 2, hrows, FLAT_W).transpose(1, 0, 2, 3).reshape(2, N_CHIPS * hrows, FLAT_W)
    core_i = my_core.astype(jnp.int32).reshape(1)
    chip_i = my_chip.astype(jnp.int32).reshape(1)
    theirs = _pair_exchange(g24, "reduce_pair")
    chip_sum, chip_sum_bf = _sum_pair(g24, theirs, core_i, "reduce_pair_add")
    others = _scatter_chips(chip_sum_bf.reshape(N_CHIPS, hrows, FLAT_W), "reduce_chips")
    mine = _sum_chips(chip_sum.reshape(N_CHIPS, hrows, FLAT_W), others, chip_i, "reduce_chips_add")
    g_flat = _pair_share(mine, "reduce_share").reshape(rows2, FLAT_W)

    small = jnp.concatenate([gw[k].reshape(-1) for k in _SMALL] + [gw["conv_w"].reshape(-1)])
    small_sum = _allreduce_small(_pad_rows(small), "reduce_small").reshape(-1)
    gs, off = {}, 0
    for k in _SMALL:
        n = int(np.prod(w[k].shape))
        gs[k] = small_sum[off:off + n].reshape(w[k].shape)
        off += n
    gcw = small_sum[off:off + depth * CONV_KERNEL * conv_c].reshape(depth, CONV_KERNEL, N_CHIPS, conv_cs)
    gs["conv_w"] = lax.dynamic_index_in_dim(gcw, my_chip, axis=2, keepdims=False)

    d_flat, m_flat, v_flat = _adamw(w_flat, g_flat, _flatten_shards(m), _flatten_shards(v), "adamw_big")
    grads = _unflatten_shard(g_flat, w)
    delta = _unflatten_shard(d_flat, w)
    new_m = _unflatten_shard(m_flat, w)
    new_v = _unflatten_shard(v_flat, w)
    small_keys = _SMALL + ("conv_w",)
    join = lambda t: _pad_rows(jnp.concatenate([t[k].reshape(-1) for k in small_keys]))
    d_s, m_s, v_s = _adamw(join(w), join(gs), join(m), join(v), "adamw_small")
    off = 0
    for k in small_keys:
        n = int(np.prod(w[k].shape))
        grads[k] = gs[k]
        for dst, src in ((delta, d_s), (new_m, m_s), (new_v, v_s)):
            dst[k] = src.reshape(-1)[off:off + n].reshape(w[k].shape)
        off += n
    return (loss, grad_x[None], *[grads[k] for k in order], *[delta[k] for k in order],
            *[new_m[k] for k in order], *[new_v[k] for k in order])
```

```python
import functools
from typing import NamedTuple

import jax
import jax.numpy as jnp
import numpy as np
from jax import lax
from jax.experimental import pallas as pl
from jax.experimental.pallas import tpu as pltpu

F32 = jnp.float32
BF16 = jnp.bfloat16
EPS = 1e-6
GRID_W = 64
ROPE_THETA = 10000.0
CONV_KERNEL = 31
CONV_PAD = CONV_KERNEL // 2
HALO = 16
LANES = 128
ADAM_LR, ADAM_B1, ADAM_B2, ADAM_EPS, ADAM_WD, ADAM_STEP = 0.001, 0.9, 0.999, 1e-08, 0.01, 10
VMEM_LIMIT = 56 * 1024 * 1024
LOG2E = 1.4426950408889634
LN2 = 0.6931471805599453
MESH = pl.DeviceIdType.MESH
N_CHIPS = 4
N_DEV = 8


class Dims(NamedTuple):
    N: int
    CTX: int
    R: int
    D: int
    C: int
    A: int
    KVW: int
    HD: int
    NH: int
    NKV: int
    G: int
    W: int
    TR: int


def _pick(n, prefs):
    for p in prefs:
        if n % p == 0:
            return p
    raise ValueError(f"no tile for {n} in {prefs}")


def _params(sem):
    return pltpu.CompilerParams(dimension_semantics=sem, vmem_limit_bytes=VMEM_LIMIT)


def _rows(tr, w, col=0):
    return pl.BlockSpec((tr, w), lambda i, col=col: (i, col))


def _whole(shape):
    return pl.BlockSpec(shape, lambda *_: (0,) * len(shape))


def _col(off, w):
    assert off % w == 0, (off, w)
    return off // w


_DN = {"nn": (((1,), (0,)), ((), ())), "nt": (((1,), (1,)), ((), ())), "tn": (((0,), (0,)), ((), ()))}


def _matmul(a, b, mode, name, out_dtype=F32):
    if mode == "nn":
        (m, k), n = a.shape, b.shape[1]
    elif mode == "nt":
        (m, k), n = a.shape, b.shape[0]
    else:
        (k, m), n = a.shape, b.shape[1]
    tm = _pick(m, (768, 512, 256, 128, 8))
    tn = _pick(n, (1280, 1024, 768, 512, 256, 128))
    tk = _pick(k, (1536, 1280, 1024, 768, 512, 256, 128))
    nk = k // tk
    dn = _DN[mode]

    def body(a_ref, b_ref, o_ref, acc_ref):
        kk = pl.program_id(2)

        @pl.when(kk == 0)
        def _():
            acc_ref[...] = jnp.zeros_like(acc_ref)

        acc_ref[...] += lax.dot_general(a_ref[...], b_ref[...], dn, preferred_element_type=F32)

        @pl.when(kk == nk - 1)
        def _():
            o_ref[...] = acc_ref[...].astype(o_ref.dtype)

    a_spec = pl.BlockSpec((tk, tm), lambda i, j, kk: (kk, i)) if mode == "tn" else pl.BlockSpec((tm, tk), lambda i, j, kk: (i, kk))
    b_spec = pl.BlockSpec((tn, tk), lambda i, j, kk: (j, kk)) if mode == "nt" else pl.BlockSpec((tk, tn), lambda i, j, kk: (kk, j))
    return pl.pallas_call(
        body, name=name, grid=(m // tm, n // tn, nk),
        in_specs=[a_spec, b_spec], out_specs=pl.BlockSpec((tm, tn), lambda i, j, kk: (i, j)),
        out_shape=jax.ShapeDtypeStruct((m, n), out_dtype),
        scratch_shapes=[pltpu.VMEM((tm, tn), F32)],
        compiler_params=_params(("parallel", "parallel", "arbitrary")),
    )(a, b)


def _rms(x, g):
    return x * lax.rsqrt(jnp.mean(x * x, axis=-1, keepdims=True) + EPS) * g


def _prenorm_fn(x, g, sc, sh):
    return _rms(x, g) * (1.0 + sc) + sh


def _post_fn(outp, g, gt):
    return gt * _rms(outp, g)


def _convact_fn(y1, gate, ln_g, ln_b):
    mu = jnp.mean(y1, axis=-1, keepdims=True)
    var = jnp.mean(jnp.square(y1 - mu), axis=-1, keepdims=True)
    y = (y1 - mu) * lax.rsqrt(var + EPS) * ln_g + ln_b
    return jax.nn.silu(y) * jax.nn.silu(gate)


def _dsilu(x):
    s = jax.nn.sigmoid(x)
    return s * (1.0 + x * (1.0 - s))


def _seg_select(is_ctx, mod_ref, lo, w):
    return jnp.where(is_ctx, mod_ref[1:2, lo:lo + w], mod_ref[0:1, lo:lo + w])


def _seg_rows(is_ctx, v):
    row = lax.broadcasted_iota(jnp.int32, (8, v.shape[1]), 0)
    return jnp.where(row == is_ctx.astype(jnp.int32), v, 0.0)


def _mod_fwd(cv, w_mod_l, b_mod_l):
    def body(cv_ref, w_ref, b_ref, o_ref):
        s = jax.nn.silu(cv_ref[...]).astype(BF16)
        o_ref[...] = jnp.dot(s, w_ref[...], preferred_element_type=F32) + b_ref[...]

    return pl.pallas_call(
        body, name="mod_fwd", out_shape=jax.ShapeDtypeStruct((8, w_mod_l.shape[1]), F32),
        compiler_params=pltpu.CompilerParams(vmem_limit_bytes=VMEM_LIMIT),
    )(cv, w_mod_l, b_mod_l)


def _mod_bwd(cv, w_mod_l, dmod):
    d = cv.shape[1]

    def body(cv_ref, w_ref, dm_ref, dw_ref, db_ref, dcv_ref):
        cvv = cv_ref[...]
        s = jax.nn.silu(cvv).astype(BF16)
        dm = dm_ref[...]
        dmb = dm.astype(BF16)
        dw_ref[...] = lax.dot_general(s, dmb, _DN["tn"], preferred_element_type=F32)
        db_ref[...] = jnp.sum(dm, axis=0, keepdims=True)
        ds = lax.dot_general(dmb, w_ref[...], _DN["nt"], preferred_element_type=F32)
        dcv_ref[...] = ds * _dsilu(cvv)

    return pl.pallas_call(
        body, name="mod_bwd",
        out_shape=(jax.ShapeDtypeStruct(w_mod_l.shape, F32), jax.ShapeDtypeStruct((1, w_mod_l.shape[1]), F32),
                   jax.ShapeDtypeStruct((8, d), F32)),
        compiler_params=pltpu.CompilerParams(vmem_limit_bytes=VMEM_LIMIT),
    )(cv, w_mod_l, dmod)


def _prenorm(dm, xc, g_pre_l, mod):
    tr, d, n = dm.TR, dm.D, dm.N

    def body(x_ref, g_ref, mod_ref, h_ref):
        is_ctx = pl.program_id(0) * tr >= n
        sh = _seg_select(is_ctx, mod_ref, 0, d)
        sc = _seg_select(is_ctx, mod_ref, d, d)
        h_ref[...] = _prenorm_fn(x_ref[...], g_ref[...], sc, sh).astype(BF16)

    return pl.pallas_call(
        body, name="prenorm", grid=(dm.R // tr,),
        in_specs=[_rows(tr, d), _whole((1, d)), _whole((8, 3 * d))], out_specs=_rows(tr, d),
        out_shape=jax.ShapeDtypeStruct((dm.R, d), BF16), compiler_params=_params(("parallel",)),
    )(xc, g_pre_l, mod)


def _prenorm_bwd(dm, xc, g_pre_l, mod, dh, dxn):
    tr, d, n = dm.TR, dm.D, dm.N

    def body(x_ref, g_ref, mod_ref, dh_ref, dxn_ref, dx_ref, dg_ref, dss_ref):
        i = pl.program_id(0)
        is_ctx = i * tr >= n
        sh = _seg_select(is_ctx, mod_ref, 0, d)
        sc = _seg_select(is_ctx, mod_ref, d, d)
        _, vjp = jax.vjp(_prenorm_fn, x_ref[...], g_ref[...], sc, sh)
        dx, dg, dsc, dsh = vjp(dh_ref[...])
        dx_ref[...] = dxn_ref[...] + dx

        @pl.when(i == 0)
        def _():
            dg_ref[...] = jnp.zeros_like(dg_ref)
            dss_ref[...] = jnp.zeros_like(dss_ref)

        dg_ref[...] += dg
        dss_ref[...] += _seg_rows(is_ctx, jnp.concatenate([dsh, dsc], axis=1))

    return pl.pallas_call(
        body, name="prenorm_bwd", grid=(dm.R // tr,),
        in_specs=[_rows(tr, d), _whole((1, d)), _whole((8, 3 * d)), _rows(tr, d), _rows(tr, d)],
        out_specs=(_rows(tr, d), _whole((1, d)), _whole((8, 2 * d))),
        out_shape=(jax.ShapeDtypeStruct((dm.R, d), F32), jax.ShapeDtypeStruct((1, d), F32),
                   jax.ShapeDtypeStruct((8, 2 * d), F32)),
        compiler_params=_params(("arbitrary",)),
    )(xc, g_pre_l, mod, dh, dxn)


def _post(dm, xc, outp, g_post_l, mod):
    tr, d, n = dm.TR, dm.D, dm.N

    def body(x_ref, o_ref, g_ref, mod_ref, y_ref):
        is_ctx = pl.program_id(0) * tr >= n
        gt = _seg_select(is_ctx, mod_ref, 2 * d, d)
        y_ref[...] = x_ref[...] + _post_fn(o_ref[...], g_ref[...], gt)

    return pl.pallas_call(
        body, name="post", grid=(dm.R // tr,),
        in_specs=[_rows(tr, d), _rows(tr, d), _whole((1, d)), _whole((8, 3 * d))], out_specs=_rows(tr, d),
        out_shape=jax.ShapeDtypeStruct((dm.R, d), F32), compiler_params=_params(("parallel",)),
    )(xc, outp, g_post_l, mod)


def _post_bwd(dm, outp, g_post_l, mod, dxn):
    tr, d, n = dm.TR, dm.D, dm.N

    def body(o_ref, g_ref, mod_ref, dxn_ref, do_ref, dg_ref, dgt_ref):
        i = pl.program_id(0)
        is_ctx = i * tr >= n
        gt = _seg_select(is_ctx, mod_ref, 2 * d, d)
        _, vjp = jax.vjp(_post_fn, o_ref[...], g_ref[...], gt)
        do, dg, dgt = vjp(dxn_ref[...])
        do_ref[...] = do.astype(BF16)

        @pl.when(i == 0)
        def _():
            dg_ref[...] = jnp.zeros_like(dg_ref)
            dgt_ref[...] = jnp.zeros_like(dgt_ref)

        dg_ref[...] += dg
        dgt_ref[...] += _seg_rows(is_ctx, dgt)

    return pl.pallas_call(
        body, name="post_bwd", grid=(dm.R // tr,),
        in_specs=[_rows(tr, d), _whole((1, d)), _whole((8, 3 * d)), _rows(tr, d)],
        out_specs=(_rows(tr, d), _whole((1, d)), _whole((8, d))),
        out_shape=(jax.ShapeDtypeStruct((dm.R, d), BF16), jax.ShapeDtypeStruct((1, d), F32),
                   jax.ShapeDtypeStruct((8, d), F32)),
        compiler_params=_params(("arbitrary",)),
    )(outp, g_post_l, mod, dxn)


def _group_sum(x, bd):
    hi = x.astype(BF16)
    lo = (x - hi.astype(F32)).astype(BF16)
    return jnp.dot(hi, bd, preferred_element_type=F32) + jnp.dot(lo, bd, preferred_element_type=F32)


def _swap_halves(x):
    lane = lax.broadcasted_iota(jnp.int32, x.shape, 1)
    return jnp.where((lane & 32) == 0, pltpu.roll(x, LANES - 32, 1), pltpu.roll(x, 32, 1))


def _qk_prep(dm, p, cos, sins, gq, gk, bd):
    tr, hd = dm.TR, dm.HD
    oq, ok, ov = 3 * dm.C, 3 * dm.C + dm.A, 3 * dm.C + dm.A + dm.KVW
    scale = hd ** -0.5 * LOG2E

    def body(q_ref, k_ref, v_ref, cos_ref, sin_ref, gq_ref, gk_ref, bd_ref, qh_ref, qt_ref, kh_ref, vh_ref):
        cosv, sinv, bdv = cos_ref[...], sin_ref[...], bd_ref[...]

        def chunk(x, g):
            rstd = lax.rsqrt(_group_sum(x * x, bdv) * (1.0 / hd) + EPS)
            xn = x * rstd * g
            return xn * cosv + _swap_halves(xn) * sinv

        for c in range(dm.A // LANES):
            y = chunk(q_ref[:, c * LANES:(c + 1) * LANES], gq_ref[...]) * scale
            qh_ref[2 * c] = y[:, :hd].astype(BF16)
            qh_ref[2 * c + 1] = y[:, hd:].astype(BF16)
            qt_ref[c * LANES:(c + 1) * LANES, :] = y.T.astype(BF16)
        for c in range(dm.KVW // LANES):
            y = chunk(k_ref[:, c * LANES:(c + 1) * LANES], gk_ref[...])
            kh_ref[2 * c] = y[:, :hd].astype(BF16)
            kh_ref[2 * c + 1] = y[:, hd:].astype(BF16)
            v = v_ref[:, c * LANES:(c + 1) * LANES]
            vh_ref[2 * c] = v[:, :hd].astype(BF16)
            vh_ref[2 * c + 1] = v[:, hd:].astype(BF16)

    hspec = lambda nh: pl.BlockSpec((nh, tr, hd), lambda i: (0, i, 0))
    return pl.pallas_call(
        body, name="qk_prep", grid=(dm.R // tr,),
        in_specs=[_rows(tr, dm.A, _col(oq, dm.A)), _rows(tr, dm.KVW, _col(ok, dm.KVW)), _rows(tr, dm.KVW, _col(ov, dm.KVW)),
                  _rows(tr, LANES), _rows(tr, LANES), _whole((1, LANES)), _whole((1, LANES)), _whole((LANES, LANES))],
        out_specs=(hspec(dm.NH), pl.BlockSpec((dm.A, tr), lambda i: (0, i)), hspec(dm.NKV), hspec(dm.NKV)),
        out_shape=(jax.ShapeDtypeStruct((dm.NH, dm.R, hd), BF16), jax.ShapeDtypeStruct((dm.A, dm.R), BF16),
                   jax.ShapeDtypeStruct((dm.NKV, dm.R, hd), BF16), jax.ShapeDtypeStruct((dm.NKV, dm.R, hd), BF16)),
        compiler_params=_params(("parallel",)),
    )(p, p, p, cos, sins, gq, gk, bd)


def _qk_bwd(dm, p, dq, dkt, dvt, cos, sins, gq, gk, bd):
    tr, hd = dm.TR, dm.HD
    oq, ok = 3 * dm.C, 3 * dm.C + dm.A
    scale = hd ** -0.5

    def body(q_ref, k_ref, dq_ref, dkh_ref, dvh_ref, cos_ref, sin_ref, gq_ref, gk_ref, bd_ref,
             dpq_ref, dpkv_ref, dgq_ref, dgk_ref):
        i = pl.program_id(0)
        cosv, sinv, bdv = cos_ref[...], sin_ref[...], bd_ref[...]

        def chunk(x, g, dy):
            rstd = lax.rsqrt(_group_sum(x * x, bdv) * (1.0 / hd) + EPS)
            xhat = x * rstd
            dxn = dy * cosv + _swap_halves(dy * sinv)
            dxhat = dxn * g
            dx = rstd * (dxhat - xhat * (_group_sum(dxhat * xhat, bdv) * (1.0 / hd)))
            return dx, jnp.sum(dxn * xhat, axis=0, keepdims=True)

        @pl.when(i == 0)
        def _():
            dgq_ref[...] = jnp.zeros_like(dgq_ref)
            dgk_ref[...] = jnp.zeros_like(dgk_ref)

        dgq = jnp.zeros((1, LANES), F32)
        for c in range(dm.A // LANES):
            sl = slice(c * LANES, (c + 1) * LANES)
            dx, dg = chunk(q_ref[:, sl], gq_ref[...], dq_ref[:, sl] * scale)
            dpq_ref[:, sl] = dx.astype(BF16)
            dgq = dgq + dg
        dgq_ref[...] += dgq + pltpu.roll(dgq, hd, 1)
        dgk = jnp.zeros((1, LANES), F32)
        for c in range(dm.KVW // LANES):
            sl = slice(c * LANES, (c + 1) * LANES)
            dx, dg = chunk(k_ref[:, sl], gk_ref[...], dkh_ref[sl, :].T)
            dpkv_ref[:, sl] = dx.astype(BF16)
            dgk = dgk + dg
            dpkv_ref[:, dm.KVW + c * LANES:dm.KVW + (c + 1) * LANES] = dvh_ref[sl, :].T.astype(BF16)
        dgk_ref[...] += dgk + pltpu.roll(dgk, hd, 1)

    hspec = pl.BlockSpec((dm.KVW, tr), lambda i: (0, i))
    return pl.pallas_call(
        body, name="qk_bwd", grid=(dm.R // tr,),
        in_specs=[_rows(tr, dm.A, _col(oq, dm.A)), _rows(tr, dm.KVW, _col(ok, dm.KVW)), _rows(tr, dm.A), hspec, hspec,
                  _rows(tr, LANES), _rows(tr, LANES), _whole((1, LANES)), _whole((1, LANES)), _whole((LANES, LANES))],
        out_specs=(_rows(tr, dm.A), _rows(tr, 2 * dm.KVW), _whole((1, LANES)), _whole((1, LANES))),
        out_shape=(jax.ShapeDtypeStruct((dm.R, dm.A), BF16), jax.ShapeDtypeStruct((dm.R, 2 * dm.KVW), BF16),
                   jax.ShapeDtypeStruct((1, LANES), F32), jax.ShapeDtypeStruct((1, LANES), F32)),
        compiler_params=_params(("arbitrary",)),
    )(p, p, dq, dkt, dvt, cos, sins, gq, gk, bd)


def _flash_fwd(dm, qh, kh, vh):
    tq, ks = dm.TR, dm.TR
    rs = min(tq, LANES)
    g, hd = dm.G, dm.HD
    nq, nq_lat = dm.R // tq, dm.N // tq
    n_chunks, ctx_chunks = dm.R // ks, dm.CTX // ks

    def body(q_ref, k_ref, v_ref, o_ref, lse_ref):
        qi = pl.program_id(1)

        def attend(chunks):
            for gg in range(g):
                for r in range(tq // rs):
                    rows = slice(r * rs, (r + 1) * rs)
                    q = q_ref[gg, rows, :]
                    m_prev = jnp.full((rs, LANES), -1e30, F32)
                    l_prev = jnp.zeros((rs, LANES), F32)
                    acc = jnp.zeros((rs, hd), F32)
                    for kc in chunks:
                        keys = slice(kc * ks, (kc + 1) * ks)
                        s = lax.dot_general(q, k_ref[0, keys, :], _DN["nt"], preferred_element_type=F32)
                        m_new = jnp.maximum(m_prev, jnp.max(s, axis=-1, keepdims=True))
                        p = jnp.exp2(s - jnp.tile(m_new, (1, ks // LANES)))
                        alpha = jnp.exp2(m_prev - m_new)
                        l_prev = alpha * l_prev + jnp.sum(p, axis=-1, keepdims=True)
                        acc = acc * alpha[:, :hd] + jnp.dot(p.astype(BF16), v_ref[0, keys, :], preferred_element_type=F32)
                        m_prev = m_new
                    o_ref[rows, gg * hd:(gg + 1) * hd] = acc / l_prev[:, :hd]
                    lse_ref[gg, rows, :] = (m_prev + jnp.log2(l_prev))[:, :1]

        @pl.when(qi < nq_lat)
        def _():
            attend(range(n_chunks))

        @pl.when(qi >= nq_lat)
        def _():
            attend(range(n_chunks - ctx_chunks, n_chunks))

    kv_spec = pl.BlockSpec((1, dm.R, hd), lambda h, qi: (h, 0, 0))
    return pl.pallas_call(
        body, name="flash_fwd", grid=(dm.NKV, nq),
        in_specs=[pl.BlockSpec((g, tq, hd), lambda h, qi: (h, qi, 0)), kv_spec, kv_spec],
        out_specs=(pl.BlockSpec((tq, g * hd), lambda h, qi: (qi, h)), pl.BlockSpec((g, tq, 1), lambda h, qi: (h, qi, 0))),
        out_shape=(jax.ShapeDtypeStruct((dm.R, dm.A), F32), jax.ShapeDtypeStruct((dm.NH, dm.R, 1), F32)),
        compiler_params=_params(("parallel", "parallel")),
    )(qh, kh, vh)


def _flash_bwd(dm, qh, qt, kh, vh, doh, dot, lse, delta):
    tq, ks = dm.TR, dm.TR
    g, hd = dm.G, dm.HD
    nq, nq_lat = dm.R // tq, dm.N // tq
    n_chunks, ctx_chunks = dm.R // ks, dm.CTX // ks

    def body(q_ref, qt_ref, k_ref, v_ref, do_ref, dot_ref, lse_ref, dl_ref, dq_ref, dk_acc, dv_acc):
        qi = pl.program_id(1)

        @pl.when(qi == 0)
        def _():
            dk_acc[...] = jnp.zeros_like(dk_acc)
            dv_acc[...] = jnp.zeros_like(dv_acc)

        def attend(chunks):
            q = q_ref[...].reshape(g * tq, hd)
            do = do_ref[...].reshape(g * tq, hd)
            q_t = jnp.concatenate([qt_ref[gg * hd:(gg + 1) * hd, :] for gg in range(g)], axis=1)
            do_t = jnp.concatenate([dot_ref[gg * hd:(gg + 1) * hd, :] for gg in range(g)], axis=1)
            lse_c = lse_ref[...].reshape(g * tq, 1)
            dl_c = dl_ref[...].reshape(g * tq, 1)
            dqs = jnp.zeros((g * tq, hd), F32)
            for kc in chunks:
                keys = slice(kc * ks, (kc + 1) * ks)
                k, v = k_ref[0, keys, :], v_ref[0, keys, :]
                s = lax.dot_general(q, k, _DN["nt"], preferred_element_type=F32)
                p = jnp.exp2(s - lse_c)
                dp = lax.dot_general(do, v, _DN["nt"], preferred_element_type=F32)
                ds = (p * (dp - dl_c)).astype(BF16)
                dv_acc[:, keys] += jnp.dot(do_t, p.astype(BF16), preferred_element_type=F32)
                dk_acc[:, keys] += jnp.dot(q_t, ds, preferred_element_type=F32)
                dqs = dqs + jnp.dot(ds, k, preferred_element_type=F32)
            for gg in range(g):
                dq_ref[:, gg * hd:(gg + 1) * hd] = dqs[gg * tq:(gg + 1) * tq]

        @pl.when(qi < nq_lat)
        def _():
            attend(range(n_chunks))

        @pl.when(qi >= nq_lat)
        def _():
            attend(range(n_chunks - ctx_chunks, n_chunks))

        @pl.when(qi == nq - 1)
        def _():
            dk_acc[...] = dk_acc[...] * LN2

    qspec = pl.BlockSpec((g, tq, hd), lambda h, qi: (h, qi, 0))
    tspec = pl.BlockSpec((g * hd, tq), lambda h, qi: (h, qi))
    kspec = pl.BlockSpec((1, dm.R, hd), lambda h, qi: (h, 0, 0))
    sspec = pl.BlockSpec((g, tq, 1), lambda h, qi: (h, qi, 0))
    ospec = pl.BlockSpec((hd, dm.R), lambda h, qi: (h, 0))
    return pl.pallas_call(
        body, name="flash_bwd", grid=(dm.NKV, nq),
        in_specs=[qspec, tspec, kspec, kspec, qspec, tspec, sspec, sspec],
        out_specs=(pl.BlockSpec((tq, g * hd), lambda h, qi: (qi, h)), ospec, ospec),
        out_shape=(jax.ShapeDtypeStruct((dm.R, dm.A), F32), jax.ShapeDtypeStruct((dm.KVW, dm.R), F32),
                   jax.ShapeDtypeStruct((dm.KVW, dm.R), F32)),
        compiler_params=_params(("parallel", "arbitrary")),
    )(qh, qt, kh, vh, doh, dot, lse, delta)


def _gate_b_specs(dm):
    ogb = 3 * dm.C + dm.A + 2 * dm.KVW
    half = dm.A // 2
    return [_rows(dm.TR, half, _col(ogb, half)), _rows(dm.TR, half, _col(ogb, half) + 1)]


def _attn_gate(dm, o, p):
    tr = dm.TR

    def body(o_ref, g0_ref, g1_ref, za_ref):
        gate = jnp.concatenate([g0_ref[...], g1_ref[...]], axis=1)
        za_ref[...] = (o_ref[...] * jax.nn.silu(gate)).astype(BF16)

    return pl.pallas_call(
        body, name="attn_gate", grid=(dm.R // tr,),
        in_specs=[_rows(tr, dm.A)] + _gate_b_specs(dm), out_specs=_rows(tr, dm.A),
        out_shape=jax.ShapeDtypeStruct((dm.R, dm.A), BF16), compiler_params=_params(("parallel",)),
    )(o, p, p)


def _attn_gate_bwd(dm, dza, o, p):
    tr, hd = dm.TR, dm.HD

    def body(dza_ref, o_ref, g0_ref, g1_ref, doh_ref, dot_ref, dl_ref, dgb_ref):
        gate = jnp.concatenate([g0_ref[...], g1_ref[...]], axis=1)
        dza, o = dza_ref[...], o_ref[...]
        do = dza * jax.nn.silu(gate)
        dgb_ref[...] = (dza * o * _dsilu(gate)).astype(BF16)
        dot_ref[...] = do.T.astype(BF16)
        prod = do * o
        for h in range(dm.NH):
            sl = slice(h * hd, (h + 1) * hd)
            doh_ref[h] = do[:, sl].astype(BF16)
            dl_ref[h] = jnp.sum(prod[:, sl], axis=-1, keepdims=True)

    return pl.pallas_call(
        body, name="attn_gate_bwd", grid=(dm.R // tr,),
        in_specs=[_rows(tr, dm.A), _rows(tr, dm.A)] + _gate_b_specs(dm),
        out_specs=(pl.BlockSpec((dm.NH, tr, hd), lambda i: (0, i, 0)), pl.BlockSpec((dm.A, tr), lambda i: (0, i)),
                   pl.BlockSpec((dm.NH, tr, 1), lambda i: (0, i, 0)), _rows(tr, dm.A)),
        out_shape=(jax.ShapeDtypeStruct((dm.NH, dm.R, hd), BF16), jax.ShapeDtypeStruct((dm.A, dm.R), BF16),
                   jax.ShapeDtypeStruct((dm.NH, dm.R, 1), F32), jax.ShapeDtypeStruct((dm.R, dm.A), BF16)),
        compiler_params=_params(("parallel",)),
    )(dza, o, p, p)


def _halo_specs(dm, col):
    per = dm.TR // HALO
    last = dm.R // HALO - 1
    prev = pl.BlockSpec((HALO, dm.C), lambda i: (jnp.maximum(i * per - 1, 0), col))
    nxt = pl.BlockSpec((HALO, dm.C), lambda i: (jnp.minimum((i + 1) * per, last), col))
    return prev, nxt


def _seg_valid(dm, i):
    tr = dm.TR
    r = i * tr - HALO + lax.broadcasted_iota(jnp.int32, (tr + 2 * HALO, 1), 0)
    is_ctx = i * tr >= dm.N
    lo = jnp.where(is_ctx, dm.N, 0)
    hi = jnp.where(is_ctx, dm.R, dm.N)
    return jnp.logical_and(r >= lo, r < hi)


def _conv_fwd(dm, p, conv_w_l, conv_b_l, ln_g_l, ln_b_l):
    tr, c = dm.TR, dm.C

    def body(a_ref, g_ref, gate_ref, ap_ref, an_ref, gp_ref, gn_ref, w_ref, b_ref, lg_ref, lb_ref,
             zc_ref, y1_ref, u_ref):
        i = pl.program_id(0)
        a = jnp.concatenate([ap_ref[...], a_ref[...], an_ref[...]], axis=0)
        g = jnp.concatenate([gp_ref[...], g_ref[...], gn_ref[...]], axis=0)
        u_ref[...] = jnp.where(_seg_valid(dm, i), a * jax.nn.sigmoid(g), 0.0)
        for cc in range(c // LANES):
            sl = slice(cc * LANES, (cc + 1) * LANES)
            uc = u_ref[:, sl]
            acc = jnp.zeros((tr, LANES), F32)
            for j in range(CONV_KERNEL):
                off = HALO - CONV_PAD + j
                acc = acc + w_ref[j:j + 1, sl] * uc[off:off + tr]
            y1_ref[:, sl] = acc + b_ref[:, sl]
        zc_ref[...] = _convact_fn(y1_ref[...], gate_ref[...], lg_ref[...], lb_ref[...]).astype(BF16)

    ap, an = _halo_specs(dm, 0)
    gp, gn = _halo_specs(dm, 1)
    return pl.pallas_call(
        body, name="conv_fwd", grid=(dm.R // tr,),
        in_specs=[_rows(tr, c, 0), _rows(tr, c, 1), _rows(tr, c, 2), ap, an, gp, gn,
                  _whole((32, c)), _whole((1, c)), _whole((1, c)), _whole((1, c))],
        out_specs=(_rows(tr, c), _rows(tr, c)),
        out_shape=(jax.ShapeDtypeStruct((dm.R, c), BF16), jax.ShapeDtypeStruct((dm.R, c), F32)),
        scratch_shapes=[pltpu.VMEM((tr + 2 * HALO, c), F32)],
        compiler_params=_params(("parallel",)),
    )(p, p, p, p, p, p, p, conv_w_l, conv_b_l, ln_g_l, ln_b_l)


def _conv_act_bwd(dm, dzc, y1, p, ln_g_l, ln_b_l):
    tr, c = dm.TR, dm.C

    def body(dz_ref, y1_ref, gate_ref, lg_ref, lb_ref, dy_ref, dgate_ref, dlg_ref, dlb_ref):
        i = pl.program_id(0)
        _, vjp = jax.vjp(_convact_fn, y1_ref[...], gate_ref[...], lg_ref[...], lb_ref[...])
        dy, dgate, dlg, dlb = vjp(dz_ref[...])
        dy_ref[...] = dy
        dgate_ref[...] = dgate.astype(BF16)

        @pl.when(i == 0)
        def _():
            dlg_ref[...] = jnp.zeros_like(dlg_ref)
            dlb_ref[...] = jnp.zeros_like(dlb_ref)

        dlg_ref[...] += dlg
        dlb_ref[...] += dlb

    return pl.pallas_call(
        body, name="conv_act_bwd", grid=(dm.R // tr,),
        in_specs=[_rows(tr, c), _rows(tr, c), _rows(tr, c, 2), _whole((1, c)), _whole((1, c))],
        out_specs=(_rows(tr, c), _rows(tr, c), _whole((1, c)), _whole((1, c))),
        out_shape=(jax.ShapeDtypeStruct((dm.R, c), F32), jax.ShapeDtypeStruct((dm.R, c), BF16),
                   jax.ShapeDtypeStruct((1, c), F32), jax.ShapeDtypeStruct((1, c), F32)),
        compiler_params=_params(("arbitrary",)),
    )(dzc, y1, p, ln_g_l, ln_b_l)


def _conv_bwd(dm, dy1, p, conv_w_l):
    tr, c = dm.TR, dm.C

    def body(dy_ref, dyp_ref, dyn_ref, a_ref, g_ref, ap_ref, an_ref, gp_ref, gn_ref, w_ref,
             dag_ref, dw_ref, db_ref, u_ref, dyp_scr, du_ref):
        i = pl.program_id(0)
        valid = _seg_valid(dm, i)
        a = jnp.concatenate([ap_ref[...], a_ref[...], an_ref[...]], axis=0)
        g = jnp.concatenate([gp_ref[...], g_ref[...], gn_ref[...]], axis=0)
        u_ref[...] = jnp.where(valid, a * jax.nn.sigmoid(g), 0.0)
        dyp_scr[...] = jnp.where(valid, jnp.concatenate([dyp_ref[...], dy_ref[...], dyn_ref[...]], axis=0), 0.0)

        @pl.when(i == 0)
        def _():
            dw_ref[...] = jnp.zeros_like(dw_ref)
            db_ref[...] = jnp.zeros_like(db_ref)

        db_ref[...] += jnp.sum(dy_ref[...], axis=0, keepdims=True)
        for cc in range(c // LANES):
            sl = slice(cc * LANES, (cc + 1) * LANES)
            uc = u_ref[:, sl]
            dyc = dyp_scr[:, sl]
            dy_cur = dyc[HALO:HALO + tr]
            acc = jnp.zeros((tr, LANES), F32)
            for j in range(CONV_KERNEL):
                off = HALO + CONV_PAD - j
                acc = acc + w_ref[j:j + 1, sl] * dyc[off:off + tr]
                offu = HALO - CONV_PAD + j
                dw_ref[j:j + 1, sl] += jnp.sum(dy_cur * uc[offu:offu + tr], axis=0, keepdims=True)
            du_ref[:, sl] = acc
        du = du_ref[...]
        sg = jax.nn.sigmoid(g_ref[...])
        dag_ref[:, :c] = (du * sg).astype(BF16)
        dag_ref[:, c:] = (du * a_ref[...] * sg * (1.0 - sg)).astype(BF16)

    ap, an = _halo_specs(dm, 0)
    gp, gn = _halo_specs(dm, 1)
    dyp, dyn = _halo_specs(dm, 0)
    return pl.pallas_call(
        body, name="conv_bwd", grid=(dm.R // tr,),
        in_specs=[_rows(tr, c), dyp, dyn, _rows(tr, c, 0), _rows(tr, c, 1), ap, an, gp, gn, _whole((32, c))],
        out_specs=(_rows(tr, 2 * c), _whole((32, c)), _whole((1, c))),
        out_shape=(jax.ShapeDtypeStruct((dm.R, 2 * c), BF16), jax.ShapeDtypeStruct((32, c), F32),
                   jax.ShapeDtypeStruct((1, c), F32)),
        scratch_shapes=[pltpu.VMEM((tr + 2 * HALO, c), F32), pltpu.VMEM((tr + 2 * HALO, c), F32), pltpu.VMEM((tr, c), F32)],
        compiler_params=_params(("arbitrary",)),
    )(dy1, dy1, dy1, p, p, p, p, p, p, conv_w_l)


def _gm_specs(dm):
    half = dm.D // 2
    first = _col(3 * dm.C + 2 * dm.A + 2 * dm.KVW, half)
    return [_rows(dm.TR, half, first + k) for k in range(4)]


def _gm_load(a0, a1, b0, b1):
    return jnp.concatenate([a0[...], a1[...]], axis=1), jnp.concatenate([b0[...], b1[...]], axis=1)


def _merge(dm, yc, ya, p):
    tr, d = dm.TR, dm.D

    def body(yc_ref, ya_ref, a0, a1, b0, b1, m_ref):
        ga, gb = _gm_load(a0, a1, b0, b1)
        m_ref[...] = (jax.nn.sigmoid(ga) * yc_ref[...] + jax.nn.sigmoid(gb) * ya_ref[...]).astype(BF16)

    return pl.pallas_call(
        body, name="merge", grid=(dm.R // tr,),
        in_specs=[_rows(tr, d), _rows(tr, d)] + _gm_specs(dm), out_specs=_rows(tr, d),
        out_shape=jax.ShapeDtypeStruct((dm.R, d), BF16), compiler_params=_params(("parallel",)),
    )(yc, ya, p, p, p, p)


def _merge_bwd(dm, dmerged, yc, ya, p):
    tr, d = dm.TR, dm.D

    def body(dm_ref, yc_ref, ya_ref, a0, a1, b0, b1, dyc_ref, dya_ref, dgm_ref):
        dmv = dm_ref[...]
        ga, gb = _gm_load(a0, a1, b0, b1)
        sa, sb = jax.nn.sigmoid(ga), jax.nn.sigmoid(gb)
        dyc_ref[...] = (dmv * sa).astype(BF16)
        dya_ref[...] = (dmv * sb).astype(BF16)
        dgm_ref[:, :d] = (dmv * yc_ref[...] * sa * (1.0 - sa)).astype(BF16)
        dgm_ref[:, d:] = (dmv * ya_ref[...] * sb * (1.0 - sb)).astype(BF16)

    return pl.pallas_call(
        body, name="merge_bwd", grid=(dm.R // tr,),
        in_specs=[_rows(tr, d), _rows(tr, d), _rows(tr, d)] + _gm_specs(dm),
        out_specs=(_rows(tr, d), _rows(tr, d), _rows(tr, 2 * d)),
        out_shape=(jax.ShapeDtypeStruct((dm.R, d), BF16), jax.ShapeDtypeStruct((dm.R, d), BF16),
                   jax.ShapeDtypeStruct((dm.R, 2 * d), BF16)),
        compiler_params=_params(("parallel",)),
    )(dmerged, yc, ya, p, p, p, p)


def _loss_head(dm, xc, target):
    tr, d = dm.TR, dm.D
    n_lat = dm.N // tr

    def body(x_ref, t_ref, dy_ref, l_ref):
        i = pl.program_id(0)

        @pl.when(i == 0)
        def _():
            l_ref[...] = jnp.zeros_like(l_ref)

        @pl.when(i < n_lat)
        def _():
            e = x_ref[...] - t_ref[...]
            dy_ref[...] = e * (1.0 / d)
            l_ref[...] += 0.5 * jnp.sum(jnp.mean(e * e, axis=-1, keepdims=True), axis=0, keepdims=True)

        @pl.when(i >= n_lat)
        def _():
            dy_ref[...] = jnp.zeros_like(dy_ref)

    return pl.pallas_call(
        body, name="loss_head", grid=(dm.R // tr,),
        in_specs=[_rows(tr, d), pl.BlockSpec((tr, d), lambda i: (jnp.minimum(i, n_lat - 1), 0))],
        out_specs=(_rows(tr, d), _whole((1, 1))),
        out_shape=(jax.ShapeDtypeStruct((dm.R, d), F32), jax.ShapeDtypeStruct((1, 1), F32)),
        compiler_params=_params(("arbitrary",)),
    )(xc, target)


def _adamw(w, g, m, v, name):
    rows, cols = w.shape
    tr = _pick(rows, (512, 256, 128, 64, 32, 16, 8, rows))

    def body(w_ref, g_ref, m_ref, v_ref, d_ref, mo_ref, vo_ref):
        d_ref[...], mo_ref[...], vo_ref[...] = _adam_update(w_ref[...], g_ref[...], m_ref[...], v_ref[...])

    spec = _rows(tr, cols)
    shp = jax.ShapeDtypeStruct(w.shape, F32)
    return pl.pallas_call(
        body, name=name, grid=(rows // tr,), in_specs=[spec] * 4, out_specs=(spec,) * 3, out_shape=(shp,) * 3,
        compiler_params=_params(("parallel",)),
    )(w, g, m, v)


def _adam_update(w, g, m, v):
    mn = ADAM_B1 * m + (1.0 - ADAM_B1) * g
    vn = ADAM_B2 * v + (1.0 - ADAM_B2) * jnp.square(g)
    m_hat = mn / (1.0 - ADAM_B1 ** ADAM_STEP)
    v_hat = vn / (1.0 - ADAM_B2 ** ADAM_STEP)
    return -ADAM_LR * (m_hat / (jnp.sqrt(v_hat) + ADAM_EPS) + ADAM_WD * w), mn, vn


def _adamw_halves(w, mine, theirs, core, m, v, name):
    rows, cols = w.shape
    h = rows // 2
    tr = _pick(h, (256, 128, 64, 32, 16, 8))
    nb = h // tr

    def body(c_ref, w_ref, a_ref, b_ref, m_ref, v_ref, g_ref, d_ref, mo_ref, vo_ref):
        gv = jnp.where(pl.program_id(0) // nb == c_ref[0], a_ref[...], b_ref[...])
        g_ref[...] = gv
        d_ref[...], mo_ref[...], vo_ref[...] = _adam_update(w_ref[...], gv, m_ref[...], v_ref[...])

    full = pl.BlockSpec((tr, cols), lambda i, c: (i, 0))
    half = pl.BlockSpec((tr, cols), lambda i, c: (i % nb, 0))
    shp = jax.ShapeDtypeStruct(w.shape, F32)
    return pl.pallas_call(
        body, name=name,
        grid_spec=pltpu.PrefetchScalarGridSpec(num_scalar_prefetch=1, grid=(rows // tr,),
                                               in_specs=[full, half, half, full, full], out_specs=(full,) * 4),
        out_shape=(shp,) * 4, compiler_params=_params(("parallel",)),
    )(core, w, mine, theirs, m, v)


def _rope_tables(dm):
    n = dm.N
    rows = n // GRID_W
    row = jnp.repeat(jnp.arange(rows, dtype=F32), GRID_W)
    col = jnp.tile(jnp.arange(GRID_W, dtype=F32), rows)
    half = dm.HD // 2
    inv_freq = ROPE_THETA ** (-jnp.arange(0, half, 2, dtype=F32) / half)
    ang = jnp.concatenate([row[:, None] * inv_freq, col[:, None] * inv_freq], axis=-1)
    cos, sin = jnp.cos(ang), jnp.sin(ang)
    cos = jnp.concatenate([cos, jnp.ones((dm.CTX, half), F32)], axis=0)
    sin = jnp.concatenate([sin, jnp.zeros((dm.CTX, half), F32)], axis=0)
    reps = LANES // dm.HD
    return jnp.tile(jnp.concatenate([cos, cos], axis=1), (1, reps)), jnp.tile(jnp.concatenate([-sin, sin], axis=1), (1, reps))


def _lane_vec(g, hd):
    return jnp.tile(g.reshape(1, hd), (1, LANES // hd))


def _layer_fwd(dm, xc, cv, wl, tabs):
    mod = _mod_fwd(cv, wl["w_mod"], wl["b_mod"])
    h = _prenorm(dm, xc, wl["g_pre"], mod)
    p = _matmul(h, wl["w_in"], "nn", "in_proj")
    qh, qt, kh, vh = _qk_prep(dm, p, tabs["cos"], tabs["sins"], wl["gq"], wl["gk"], tabs["bd"])
    o, lse = _flash_fwd(dm, qh, kh, vh)
    za = _attn_gate(dm, o, p)
    zc, y1 = _conv_fwd(dm, p, wl["conv_w"], wl["conv_b"], wl["ln_g"], wl["ln_b"])
    yc = _matmul(zc, wl["w_conv_out"], "nn", "conv_out")
    ya = _matmul(za, wl["w_attn_out"], "nn", "attn_out")
    merged = _merge(dm, yc, ya, p)
    outp = _matmul(merged, wl["w_out"], "nn", "out_proj")
    x_new = _post(dm, xc, outp, wl["g_post"], mod)
    saved = dict(xc=xc, mod=mod, h=h, p=p, qh=qh, qt=qt, kh=kh, vh=vh, o=o, lse=lse, za=za, zc=zc, y1=y1, yc=yc, ya=ya,
                 merged=merged, outp=outp)
    return x_new, saved


def _layer_bwd(dm, dxn, cv, wl, tabs, s):
    p = s["p"]
    d_outp, dg_post, dgt = _post_bwd(dm, s["outp"], wl["g_post"], s["mod"], dxn)
    dmerged = _matmul(d_outp, wl["w_out"], "nt", "out_proj_dx")
    dw_out = _matmul(s["merged"], d_outp, "tn", "out_proj_dw")
    dyc, dya, dgm = _merge_bwd(dm, dmerged, s["yc"], s["ya"], p)
    dzc = _matmul(dyc, wl["w_conv_out"], "nt", "conv_out_dx")
    dw_conv_out = _matmul(s["zc"], dyc, "tn", "conv_out_dw")
    dza = _matmul(dya, wl["w_attn_out"], "nt", "attn_out_dx")
    dw_attn_out = _matmul(s["za"], dya, "tn", "attn_out_dw")
    doh, dot, delta, dgate_b = _attn_gate_bwd(dm, dza, s["o"], p)
    dq, dkt, dvt = _flash_bwd(dm, s["qh"], s["qt"], s["kh"], s["vh"], doh, dot, s["lse"], delta)
    dpq, dpkv, dgq, dgk = _qk_bwd(dm, p, dq, dkt, dvt, tabs["cos"], tabs["sins"], wl["gq"], wl["gk"], tabs["bd"])
    dy1, dgate_a, dln_g, dln_b = _conv_act_bwd(dm, dzc, s["y1"], p, wl["ln_g"], wl["ln_b"])
    dpag, dconv_w, dconv_b = _conv_bwd(dm, dy1, p, wl["conv_w"])
    dp = jnp.concatenate([dpag, dgate_a, dpq, dpkv, dgate_b, dgm], axis=1)
    dh = _matmul(dp, wl["w_in"], "nt", "in_proj_dx")
    dw_in = _matmul(s["h"], dp, "tn", "in_proj_dw")
    dxc, dg_pre, dshsc = _prenorm_bwd(dm, s["xc"], wl["g_pre"], s["mod"], dh, dxn)
    dmod = jnp.concatenate([dshsc, dgt], axis=1)
    dw_mod, db_mod, dcv = _mod_bwd(cv, wl["w_mod"], dmod)
    grads = dict(w_mod=dw_mod, b_mod=db_mod[0], g_pre=dg_pre[0], g_post=dg_post[0], w_in=dw_in,
                 conv_w=dconv_w[:CONV_KERNEL], conv_b=dconv_b[0], ln_g=dln_g[0], ln_b=dln_b[0],
                 w_conv_out=dw_conv_out, q_norm_g=dgq[0, :dm.HD], k_norm_g=dgk[0, :dm.HD],
                 w_attn_out=dw_attn_out, w_out=dw_out, c_ctx=dcv[1])
    return dxc, grads


def _local_step(dm, x, c, ctx, c_ctx, target, wf):
    depth = wf["w_in"].shape[0]
    cos, sins = _rope_tables(dm)
    lane = np.arange(LANES)
    bd = jnp.asarray((lane[:, None] // dm.HD == lane[None, :] // dm.HD), dtype=BF16)
    tabs = dict(cos=cos, sins=sins, bd=bd)
    cv = jnp.concatenate([c.reshape(1, dm.D), c_ctx.reshape(1, dm.D), jnp.zeros((6, dm.D), F32)], axis=0)
    xc = jnp.concatenate([x, ctx], axis=0)
    layers, saved = [], []
    for l in range(depth):
        wl = dict(
            w_mod=wf["w_mod"][l], b_mod=wf["b_mod"][l].reshape(1, -1), g_pre=wf["g_pre"][l].reshape(1, -1),
            g_post=wf["g_post"][l].reshape(1, -1), w_in=wf["w_in"][l],
            conv_w=jnp.concatenate([wf["conv_w"][l], jnp.zeros((1, dm.C), F32)], axis=0),
            conv_b=wf["conv_b"][l].reshape(1, -1), ln_g=wf["ln_g"][l].reshape(1, -1), ln_b=wf["ln_b"][l].reshape(1, -1),
            w_conv_out=wf["w_conv_out"][l], gq=_lane_vec(wf["q_norm_g"][l], dm.HD), gk=_lane_vec(wf["k_norm_g"][l], dm.HD),
            w_attn_out=wf["w_attn_out"][l], w_out=wf["w_out"][l])
        layers.append(wl)
        xc, s = _layer_fwd(dm, xc, cv, wl, tabs)
        saved.append(s)
    dxc, loss = _loss_head(dm, xc, target)
    grads = [None] * depth
    for l in reversed(range(depth)):
        dxc, grads[l] = _layer_bwd(dm, dxc, cv, layers[l], tabs, saved[l])
    gw = {k: jnp.stack([g[k] for g in grads]) for k in grads[0] if k != "c_ctx"}
    gw["c_ctx"] = grads[0]["c_ctx"] + grads[1]["c_ctx"] if depth == 2 else sum(g["c_ctx"] for g in grads)
    return loss[0, 0], dxc[:dm.N], gw


def _dims(x, ctx, w_in_full_cols, conv_c, attn_w, hd):
    n, d = x.shape
    ctx_len = ctx.shape[0]
    c, a = conv_c, attn_w
    kvw = (w_in_full_cols - 3 * c - 2 * a - 2 * d) // 2
    r = n + ctx_len
    tr = _pick(np.gcd(n, ctx_len), (256, 128))
    return Dims(N=n, CTX=ctx_len, R=r, D=d, C=c, A=a, KVW=kvw, HD=hd, NH=a // hd, NKV=kvw // hd,
                G=a // kvw, W=w_in_full_cols, TR=tr)


_HBM = pl.BlockSpec(memory_space=pltpu.HBM)
_VMEM = pl.BlockSpec(memory_space=pltpu.VMEM)


def _place():
    x, y, c = lax.axis_index("x"), lax.axis_index("y"), lax.axis_index("c")
    chips = [(1 - x, y), (x, 1 - y), (1 - x, 1 - y)]
    return x, y, c, chips


def _row_pieces(rows, want):
    n = max(k for k in range(1, want + 1) if rows % (16 * k) == 0)
    return [pl.ds(i * (rows // n), rows // n) for i in range(n)]


def _allgather_chips(own, name):
    _, h, w = own.shape
    pieces = _row_pieces(h, 4)
    npc = len(pieces)

    def body(own_ref, out_ref, send_sems, recv_sems):
        x, y, c, chips = _place()
        s_me = 2 * x + y
        sibling = (x, y, 1 - c)

        def copy(k, src, dst, to):
            return pltpu.make_async_remote_copy(src_ref=src, dst_ref=dst, send_sem=send_sems.at[k], recv_sem=recv_sems.at[k],
                                                device_id=to, device_id_type=MESH)

        sends = []
        for i, rows in enumerate(pieces):
            for j, chip in enumerate(chips):
                sends.append(copy(i * 3 + j, own_ref.at[c, rows], out_ref.at[s_me, c, rows], (*chip, c)))
        for cp in sends:
            cp.start()
        for i, rows in enumerate(pieces):
            for j, (cx, cy) in enumerate(chips):
                landed = out_ref.at[2 * cx + cy, c, rows]
                copy(i * 3 + j, landed, landed, (cx, cy, c)).wait_recv()
                fwd = copy(3 * npc + i * 3 + j, landed, landed, sibling)
                fwd.start()
                sends.append(fwd)
        for i, rows in enumerate(pieces):
            for j, (cx, cy) in enumerate(chips):
                theirs = out_ref.at[2 * cx + cy, 1 - c, rows]
                copy(3 * npc + i * 3 + j, theirs, theirs, sibling).wait_recv()
        for cp in sends:
            cp.wait_send()

    return pl.pallas_call(
        body, name=name, in_specs=[_HBM], out_specs=_HBM,
        out_shape=jax.ShapeDtypeStruct((N_CHIPS, 2, h, w), own.dtype),
        scratch_shapes=[pltpu.SemaphoreType.DMA((6 * npc,)), pltpu.SemaphoreType.DMA((6 * npc,))],
    )(own)


def _pair_exchange(buf, name):
    _, m, w = buf.shape
    pieces = _row_pieces(m, 16)

    def body(buf_ref, out_ref, send_sems, recv_sems):
        x, y, c, _ = _place()
        cps = [pltpu.make_async_remote_copy(src_ref=buf_ref.at[1 - c, rows], dst_ref=out_ref.at[rows], send_sem=send_sems.at[i],
                                            recv_sem=recv_sems.at[i], device_id=(x, y, 1 - c), device_id_type=MESH)
               for i, rows in enumerate(pieces)]
        for cp in cps:
            cp.start()
        for cp in cps:
            cp.wait()

    return pl.pallas_call(
        body, name=name, in_specs=[_HBM], out_specs=_HBM, out_shape=jax.ShapeDtypeStruct((m, w), buf.dtype),
        scratch_shapes=[pltpu.SemaphoreType.DMA((len(pieces),)), pltpu.SemaphoreType.DMA((len(pieces),))],
    )(buf)


def _pair_swap(t, name):
    m, w = t.shape
    pieces = _row_pieces(m, 8)
    n = len(pieces)

    def body(t_ref, out_ref, send_sems, recv_sems):
        x, y, c, _ = _place()
        cps = [pltpu.make_async_remote_copy(src_ref=t_ref.at[rows], dst_ref=out_ref.at[rows], send_sem=send_sems.at[i],
                                            recv_sem=recv_sems.at[i], device_id=(x, y, 1 - c), device_id_type=MESH)
               for i, rows in enumerate(pieces)]
        for cp in cps:
            cp.start()
        for cp in cps:
            cp.wait()

    return pl.pallas_call(
        body, name=name, in_specs=[_HBM], out_specs=_HBM, out_shape=jax.ShapeDtypeStruct((m, w), t.dtype),
        scratch_shapes=[pltpu.SemaphoreType.DMA((n,)), pltpu.SemaphoreType.DMA((n,))],
    )(t)


def _scatter_chips(s, name):
    _, h, w = s.shape

    def body(s_ref, out_ref, send_sems, recv_sems):
        _, _, c, chips = _place()
        cps = [pltpu.make_async_remote_copy(src_ref=s_ref.at[2 * cx + cy], dst_ref=out_ref.at[j], send_sem=send_sems.at[j],
                                            recv_sem=recv_sems.at[j], device_id=(cx, cy, c), device_id_type=MESH)
               for j, (cx, cy) in enumerate(chips)]
        for cp in cps:
            cp.start()
        for cp in cps:
            cp.wait()

    return pl.pallas_call(
        body, name=name, in_specs=[_HBM], out_specs=_HBM, out_shape=jax.ShapeDtypeStruct((3, h, w), s.dtype),
        scratch_shapes=[pltpu.SemaphoreType.DMA((3,)), pltpu.SemaphoreType.DMA((3,))],
    )(s)


def _sum_pair(buf, recv, core, name):
    m, w = recv.shape
    tr = _pick(m, (512, 432, 256, 128, 64, 32, 16))

    def body(c_ref, b_ref, r_ref, o_ref, ob_ref):
        v = b_ref[0] + r_ref[...]
        o_ref[...] = v
        ob_ref[...] = v.astype(BF16)

    spec = pl.BlockSpec((tr, w), lambda i, c: (i, 0))
    return pl.pallas_call(
        body, name=name,
        grid_spec=pltpu.PrefetchScalarGridSpec(
            num_scalar_prefetch=1, grid=(m // tr,),
            in_specs=[pl.BlockSpec((1, tr, w), lambda i, c: (c[0], i, 0)), spec], out_specs=(spec, spec)),
        out_shape=(jax.ShapeDtypeStruct((m, w), F32), jax.ShapeDtypeStruct((m, w), BF16)),
        compiler_params=_params(("parallel",)),
    )(core, buf, recv)


def _sum_chips(own, recv, chip, name):
    _, h, w = own.shape
    tr = _pick(h, (512, 432, 256, 128, 64, 32, 16))

    def body(s_ref, a_ref, r_ref, o_ref):
        acc = a_ref[0]
        for j in range(3):
            acc = acc + r_ref[j].astype(F32)
        o_ref[...] = acc

    return pl.pallas_call(
        body, name=name,
        grid_spec=pltpu.PrefetchScalarGridSpec(
            num_scalar_prefetch=1, grid=(h // tr,),
            in_specs=[pl.BlockSpec((1, tr, w), lambda i, s: (s[0], i, 0)), pl.BlockSpec((3, tr, w), lambda i, s: (0, i, 0))],
            out_specs=pl.BlockSpec((tr, w), lambda i, s: (i, 0))),
        out_shape=jax.ShapeDtypeStruct((h, w), F32), compiler_params=_params(("parallel",)),
    )(chip, own, recv)


def _allreduce_small(own, name):
    m, w = own.shape

    def body(own_ref, sum_ref, all_ref, send_sems, recv_sems):
        x, y, c, _ = _place()
        me = 4 * x + 2 * y + c
        all_ref[me] = own_ref[...]
        sends = []
        for k in range(1, N_DEV):
            peer = (x ^ ((k >> 2) & 1), y ^ ((k >> 1) & 1), c ^ (k & 1))
            cp = pltpu.make_async_remote_copy(src_ref=own_ref, dst_ref=all_ref.at[me], send_sem=send_sems.at[k - 1],
                                              recv_sem=recv_sems.at[k - 1], device_id=peer, device_id_type=MESH)
            cp.start()
            sends.append(cp)
        for k in range(1, N_DEV):
            px, py, pc = x ^ ((k >> 2) & 1), y ^ ((k >> 1) & 1), c ^ (k & 1)
            slot = all_ref.at[4 * px + 2 * py + pc]
            pltpu.make_async_remote_copy(src_ref=slot, dst_ref=slot, send_sem=send_sems.at[k - 1], recv_sem=recv_sems.at[k - 1],
                                         device_id=(px, py, pc), device_id_type=MESH).wait_recv()
        for cp in sends:
            cp.wait_send()
        acc = all_ref[0]
        for d in range(1, N_DEV):
            acc = acc + all_ref[d]
        sum_ref[...] = acc

    return pl.pallas_call(
        body, name=name, in_specs=[_VMEM], out_specs=_VMEM, out_shape=jax.ShapeDtypeStruct((m, w), F32),
        scratch_shapes=[pltpu.VMEM((N_DEV, m, w), F32), pltpu.SemaphoreType.DMA((N_DEV - 1,)), pltpu.SemaphoreType.DMA((N_DEV - 1,))],
        compiler_params=pltpu.CompilerParams(vmem_limit_bytes=VMEM_LIMIT),
    )(own)


_COL_SHARDED = ("w_mod", "w_in")
_ROW_SHARDED = ("w_conv_out", "w_attn_out", "w_out")
_BIG = _COL_SHARDED + _ROW_SHARDED
_SMALL = ("c_ctx", "b_mod", "g_pre", "g_post", "conv_b", "ln_g", "ln_b", "q_norm_g", "k_norm_g")
FLAT_W = 1024


def _flatten_shards(shards):
    flat = jnp.concatenate([shards[k].reshape(-1) for k in _BIG])
    return flat.reshape(-1, FLAT_W)


def _unflatten_shard(flat, like):
    out, off = {}, 0
    flat = flat.reshape(-1)
    for k in _BIG:
        n = int(np.prod(like[k].shape))
        out[k] = flat[off:off + n].reshape(like[k].shape)
        off += n
    return out


def _by_shard(k, full):
    l = full.shape[0]
    if k in _COL_SHARDED:
        d, cols = full.shape[1:]
        return full.reshape(l, d, N_CHIPS, cols // N_CHIPS).transpose(2, 0, 1, 3).reshape(N_CHIPS, -1)
    rows, d = full.shape[1:]
    return full.reshape(l, N_CHIPS, rows // N_CHIPS, d).transpose(1, 0, 2, 3).reshape(N_CHIPS, -1)


def _from_shards(k, flat4, like):
    l = like.shape[0]
    if k in _COL_SHARDED:
        d, cs = like.shape[1:]
        return flat4.reshape(N_CHIPS, l, d, cs).transpose(1, 2, 0, 3).reshape(l, d, N_CHIPS * cs)
    rs, d = like.shape[1:]
    return flat4.reshape(N_CHIPS, l, rs, d).transpose(1, 0, 2, 3).reshape(l, N_CHIPS * rs, d)


def _pad_rows(flat, mult=8):
    n = flat.shape[0]
    rows = -(-n // LANES)
    rows = -(-rows // mult) * mult
    return jnp.concatenate([flat, jnp.zeros((rows * LANES - n,), F32)]).reshape(rows, LANES)


def kernel(x, c, ctx, c_ctx, w_mod, b_mod, g_pre, g_post, w_in, conv_w, conv_b, ln_g, ln_b, w_conv_out, q_norm_g, k_norm_g, w_attn_out, w_out, loss_target, m_c_ctx, m_w_mod, m_b_mod, m_g_pre, m_g_post, m_w_in, m_conv_w, m_conv_b, m_ln_g, m_ln_b, m_w_conv_out, m_q_norm_g, m_k_norm_g, m_w_attn_out, m_w_out, v_c_ctx, v_w_mod, v_b_mod, v_g_pre, v_g_post, v_w_in, v_conv_w, v_conv_b, v_ln_g, v_ln_b, v_w_conv_out, v_q_norm_g, v_k_norm_g, v_w_attn_out, v_w_out):
    w = dict(c_ctx=c_ctx, w_mod=w_mod, b_mod=b_mod, g_pre=g_pre, g_post=g_post, w_in=w_in, conv_w=conv_w, conv_b=conv_b,
             ln_g=ln_g, ln_b=ln_b, w_conv_out=w_conv_out, q_norm_g=q_norm_g, k_norm_g=k_norm_g, w_attn_out=w_attn_out, w_out=w_out)
    m = dict(c_ctx=m_c_ctx, w_mod=m_w_mod, b_mod=m_b_mod, g_pre=m_g_pre, g_post=m_g_post, w_in=m_w_in, conv_w=m_conv_w,
             conv_b=m_conv_b, ln_g=m_ln_g, ln_b=m_ln_b, w_conv_out=m_w_conv_out, q_norm_g=m_q_norm_g, k_norm_g=m_k_norm_g,
             w_attn_out=m_w_attn_out, w_out=m_w_out)
    v = dict(c_ctx=v_c_ctx, w_mod=v_w_mod, b_mod=v_b_mod, g_pre=v_g_pre, g_post=v_g_post, w_in=v_w_in, conv_w=v_conv_w,
             conv_b=v_conv_b, ln_g=v_ln_g, ln_b=v_ln_b, w_conv_out=v_w_conv_out, q_norm_g=v_q_norm_g, k_norm_g=v_k_norm_g,
             w_attn_out=v_w_attn_out, w_out=v_w_out)
    order = ("c_ctx", "w_mod", "b_mod", "g_pre", "g_post", "w_in", "conv_w", "conv_b", "ln_g", "ln_b", "w_conv_out",
             "q_norm_g", "k_norm_g", "w_attn_out", "w_out")
    my_chip = 2 * lax.axis_index("x") + lax.axis_index("y")
    my_core = lax.axis_index("c")
    depth, _, conv_cs = conv_w.shape
    conv_c = conv_cs * N_CHIPS
    dm = _dims(x[0], ctx[0], w_in.shape[2] * N_CHIPS, conv_c, w_attn_out.shape[1] * N_CHIPS, q_norm_g.shape[1])

    w_flat = _flatten_shards(w)
    rows2 = w_flat.shape[0]
    own = w_flat.astype(BF16).reshape(2, rows2 // 2, FLAT_W)
    gathered = _allgather_chips(own, "gather_weights").reshape(N_CHIPS, -1)
    is_mine = (jnp.arange(N_CHIPS) == my_chip)[:, None]
    gathered = jnp.where(is_mine, own.reshape(1, -1), gathered)
    wf, off = {}, 0
    for k in _BIG:
        n = int(np.prod(w[k].shape))
        wf[k] = _from_shards(k, gathered[:, off:off + n], w[k])
        off += n
    cw = jnp.zeros((depth, CONV_KERNEL, N_CHIPS, conv_cs), F32)
    cw = lax.dynamic_update_slice(cw, conv_w[:, :, None, :], (0, 0, my_chip, 0))
    cw = jnp.where(my_core == 0, cw, 0.0).reshape(-1)
    wf["conv_w"] = _allreduce_small(_pad_rows(cw), "gather_conv_w").reshape(-1)[:cw.shape[0]].reshape(depth, CONV_KERNEL, conv_c)
    for k in _SMALL:
        wf[k] = w[k]

    loss_part, grad_x, gw = _local_step(dm, x[0], c, ctx[0], c_ctx, loss_target[0], wf)
    loss = lax.psum(loss_part, ("x", "y", "c"))

    g4 = jnp.concatenate([_by_shard(k, gw[k]) for k in _BIG], axis=1)
    hrows = rows2 // 2
    g24 = g4.reshape(N_CHIPS, 2, hrows, FLAT_W).transpose(1, 0, 2, 3).reshape(2, N_CHIPS * hrows, FLAT_W)
    core_i = my_core.astype(jnp.int32).reshape(1)
    chip_i = my_chip.astype(jnp.int32).reshape(1)
    theirs = _pair_exchange(g24, "reduce_pair")
    chip_sum, chip_sum_bf = _sum_pair(g24, theirs, core_i, "reduce_pair_add")
    others = _scatter_chips(chip_sum_bf.reshape(N_CHIPS, hrows, FLAT_W), "reduce_chips")
    mine = _sum_chips(chip_sum.reshape(N_CHIPS, hrows, FLAT_W), others, chip_i, "reduce_chips_add")
    sibling_half = _pair_swap(mine, "reduce_share")

    small = jnp.concatenate([gw[k].reshape(-1) for k in _SMALL] + [gw["conv_w"].reshape(-1)])
    small_sum = _allreduce_small(_pad_rows(small), "reduce_small").reshape(-1)
    gs, off = {}, 0
    for k in _SMALL:
        n = int(np.prod(w[k].shape))
        gs[k] = small_sum[off:off + n].reshape(w[k].shape)
        off += n
    gcw = small_sum[off:off + depth * CONV_KERNEL * conv_c].reshape(depth, CONV_KERNEL, N_CHIPS, conv_cs)
    gs["conv_w"] = lax.dynamic_index_in_dim(gcw, my_chip, axis=2, keepdims=False)

    g_flat, d_flat, m_flat, v_flat = _adamw_halves(w_flat, mine, sibling_half, core_i, _flatten_shards(m), _flatten_shards(v),
                                                   "adamw_big")
    grads = _unflatten_shard(g_flat, w)
    delta = _unflatten_shard(d_flat, w)
    new_m = _unflatten_shard(m_flat, w)
    new_v = _unflatten_shard(v_flat, w)
    small_keys = _SMALL + ("conv_w",)
    join = lambda t: _pad_rows(jnp.concatenate([t[k].reshape(-1) for k in small_keys]))
    d_s, m_s, v_s = _adamw(join(w), join(gs), join(m), join(v), "adamw_small")
    off = 0
    for k in small_keys:
        n = int(np.prod(w[k].shape))
        grads[k] = gs[k]
        for dst, src in ((delta, d_s), (new_m, m_s), (new_v, v_s)):
            dst[k] = src.reshape(-1)[off:off + n].reshape(w[k].shape)
        off += n
    return (loss, grad_x[None], *[grads[k] for k in order], *[delta[k] for k in order],
            *[new_m[k] for k in order], *[new_v[k] for k in order])
```

```python
import functools
from typing import NamedTuple

import jax
import jax.numpy as jnp
import numpy as np
from jax import lax
from jax.experimental import pallas as pl
from jax.experimental.pallas import tpu as pltpu

F32 = jnp.float32
BF16 = jnp.bfloat16
EPS = 1e-6
GRID_W = 64
ROPE_THETA = 10000.0
CONV_KERNEL = 31
CONV_PAD = CONV_KERNEL // 2
HALO = 16
LANES = 128
SUBLANES = 8
ADAM_LR, ADAM_B1, ADAM_B2, ADAM_EPS, ADAM_WD, ADAM_STEP = 0.001, 0.9, 0.999, 1e-08, 0.01, 10
VMEM_LIMIT = 56 * 1024 * 1024
ADAM_BLOCK_BYTES = 1024 * 1024
LOG2E = 1.4426950408889634
LN2 = 0.6931471805599453
MESH = pl.DeviceIdType.MESH
N_CHIPS = 4
N_DEV = 8


class Dims(NamedTuple):
    N: int
    CTX: int
    R: int
    D: int
    C: int
    A: int
    KVW: int
    HD: int
    NH: int
    NKV: int
    G: int
    W: int
    TR: int


def _pick(n, prefs):
    for p in prefs:
        if n % p == 0:
            return p
    raise ValueError(f"no tile for {n} in {prefs}")


def _params(sem):
    return pltpu.CompilerParams(dimension_semantics=sem, vmem_limit_bytes=VMEM_LIMIT)


def _rows(tr, w, col=0):
    return pl.BlockSpec((tr, w), lambda i, col=col: (i, col))


def _whole(shape):
    return pl.BlockSpec(shape, lambda *_: (0,) * len(shape))


def _col(off, w):
    assert off % w == 0, (off, w)
    return off // w


_DN = {"nn": (((1,), (0,)), ((), ())), "nt": (((1,), (1,)), ((), ())), "tn": (((0,), (0,)), ((), ()))}


def _matmul(a, b, mode, name, out_dtype=F32):
    if mode == "nn":
        (m, k), n = a.shape, b.shape[1]
    elif mode == "nt":
        (m, k), n = a.shape, b.shape[0]
    else:
        (k, m), n = a.shape, b.shape[1]
    tm = _pick(m, (768, 512, 256, 128, 8))
    tn = _pick(n, (1280, 1024, 768, 512, 256, 128))
    tk = _pick(k, (1536, 1280, 1024, 768, 512, 256, 128))
    nk = k // tk
    dn = _DN[mode]

    def body(a_ref, b_ref, o_ref, acc_ref):
        kk = pl.program_id(2)

        @pl.when(kk == 0)
        def _():
            acc_ref[...] = jnp.zeros_like(acc_ref)

        acc_ref[...] += lax.dot_general(a_ref[...], b_ref[...], dn, preferred_element_type=F32)

        @pl.when(kk == nk - 1)
        def _():
            o_ref[...] = acc_ref[...].astype(o_ref.dtype)

    a_spec = pl.BlockSpec((tk, tm), lambda i, j, kk: (kk, i)) if mode == "tn" else pl.BlockSpec((tm, tk), lambda i, j, kk: (i, kk))
    b_spec = pl.BlockSpec((tn, tk), lambda i, j, kk: (j, kk)) if mode == "nt" else pl.BlockSpec((tk, tn), lambda i, j, kk: (kk, j))
    return pl.pallas_call(
        body, name=name, grid=(m // tm, n // tn, nk),
        in_specs=[a_spec, b_spec], out_specs=pl.BlockSpec((tm, tn), lambda i, j, kk: (i, j)),
        out_shape=jax.ShapeDtypeStruct((m, n), out_dtype),
        scratch_shapes=[pltpu.VMEM((tm, tn), F32)],
        compiler_params=_params(("parallel", "parallel", "arbitrary")),
    )(a, b)


def _rms(x, g):
    return x * lax.rsqrt(jnp.mean(x * x, axis=-1, keepdims=True) + EPS) * g


def _prenorm_fn(x, g, sc, sh):
    return _rms(x, g) * (1.0 + sc) + sh


def _post_fn(outp, g, gt):
    return gt * _rms(outp, g)


def _convact_fn(y1, gate, ln_g, ln_b):
    mu = jnp.mean(y1, axis=-1, keepdims=True)
    var = jnp.mean(jnp.square(y1 - mu), axis=-1, keepdims=True)
    y = (y1 - mu) * lax.rsqrt(var + EPS) * ln_g + ln_b
    return jax.nn.silu(y) * jax.nn.silu(gate)


def _dsilu(x):
    s = jax.nn.sigmoid(x)
    return s * (1.0 + x * (1.0 - s))


def _seg_select(is_ctx, mod_ref, lo, w):
    return jnp.where(is_ctx, mod_ref[1:2, lo:lo + w], mod_ref[0:1, lo:lo + w])


def _seg_rows(is_ctx, v):
    row = lax.broadcasted_iota(jnp.int32, (8, v.shape[1]), 0)
    return jnp.where(row == is_ctx.astype(jnp.int32), v, 0.0)


def _mod_fwd(cv, w_mod_l, b_mod_l):
    def body(cv_ref, w_ref, b_ref, o_ref):
        s = jax.nn.silu(cv_ref[...]).astype(BF16)
        o_ref[...] = jnp.dot(s, w_ref[...], preferred_element_type=F32) + b_ref[...]

    return pl.pallas_call(
        body, name="mod_fwd", out_shape=jax.ShapeDtypeStruct((8, w_mod_l.shape[1]), F32),
        compiler_params=pltpu.CompilerParams(vmem_limit_bytes=VMEM_LIMIT),
    )(cv, w_mod_l, b_mod_l)


def _mod_bwd(cv, w_mod_l, dmod):
    d = cv.shape[1]

    def body(cv_ref, w_ref, dm_ref, dw_ref, db_ref, dcv_ref):
        cvv = cv_ref[...]
        s = jax.nn.silu(cvv).astype(BF16)
        dm = dm_ref[...]
        dmb = dm.astype(BF16)
        dw_ref[...] = lax.dot_general(s, dmb, _DN["tn"], preferred_element_type=F32)
        db_ref[...] = jnp.sum(dm, axis=0, keepdims=True)
        ds = lax.dot_general(dmb, w_ref[...], _DN["nt"], preferred_element_type=F32)
        dcv_ref[...] = ds * _dsilu(cvv)

    return pl.pallas_call(
        body, name="mod_bwd",
        out_shape=(jax.ShapeDtypeStruct(w_mod_l.shape, F32), jax.ShapeDtypeStruct((1, w_mod_l.shape[1]), F32),
                   jax.ShapeDtypeStruct((8, d), F32)),
        compiler_params=pltpu.CompilerParams(vmem_limit_bytes=VMEM_LIMIT),
    )(cv, w_mod_l, dmod)


def _prenorm(dm, xc, g_pre_l, mod):
    tr, d, n = dm.TR, dm.D, dm.N

    def body(x_ref, g_ref, mod_ref, h_ref):
        is_ctx = pl.program_id(0) * tr >= n
        sh = _seg_select(is_ctx, mod_ref, 0, d)
        sc = _seg_select(is_ctx, mod_ref, d, d)
        h_ref[...] = _prenorm_fn(x_ref[...], g_ref[...], sc, sh).astype(BF16)

    return pl.pallas_call(
        body, name="prenorm", grid=(dm.R // tr,),
        in_specs=[_rows(tr, d), _whole((1, d)), _whole((8, 3 * d))], out_specs=_rows(tr, d),
        out_shape=jax.ShapeDtypeStruct((dm.R, d), BF16), compiler_params=_params(("parallel",)),
    )(xc, g_pre_l, mod)


def _prenorm_bwd(dm, xc, g_pre_l, mod, dh, dxn):
    tr, d, n = dm.TR, dm.D, dm.N

    def body(x_ref, g_ref, mod_ref, dh_ref, dxn_ref, dx_ref, dg_ref, dss_ref):
        i = pl.program_id(0)
        is_ctx = i * tr >= n
        sh = _seg_select(is_ctx, mod_ref, 0, d)
        sc = _seg_select(is_ctx, mod_ref, d, d)
        _, vjp = jax.vjp(_prenorm_fn, x_ref[...], g_ref[...], sc, sh)
        dx, dg, dsc, dsh = vjp(dh_ref[...])
        dx_ref[...] = dxn_ref[...] + dx

        @pl.when(i == 0)
        def _():
            dg_ref[...] = jnp.zeros_like(dg_ref)
            dss_ref[...] = jnp.zeros_like(dss_ref)

        dg_ref[...] += dg
        dss_ref[...] += _seg_rows(is_ctx, jnp.concatenate([dsh, dsc], axis=1))

    return pl.pallas_call(
        body, name="prenorm_bwd", grid=(dm.R // tr,),
        in_specs=[_rows(tr, d), _whole((1, d)), _whole((8, 3 * d)), _rows(tr, d), _rows(tr, d)],
        out_specs=(_rows(tr, d), _whole((1, d)), _whole((8, 2 * d))),
        out_shape=(jax.ShapeDtypeStruct((dm.R, d), F32), jax.ShapeDtypeStruct((1, d), F32),
                   jax.ShapeDtypeStruct((8, 2 * d), F32)),
        compiler_params=_params(("arbitrary",)),
    )(xc, g_pre_l, mod, dh, dxn)


def _post(dm, xc, outp, g_post_l, mod):
    tr, d, n = dm.TR, dm.D, dm.N

    def body(x_ref, o_ref, g_ref, mod_ref, y_ref):
        is_ctx = pl.program_id(0) * tr >= n
        gt = _seg_select(is_ctx, mod_ref, 2 * d, d)
        y_ref[...] = x_ref[...] + _post_fn(o_ref[...], g_ref[...], gt)

    return pl.pallas_call(
        body, name="post", grid=(dm.R // tr,),
        in_specs=[_rows(tr, d), _rows(tr, d), _whole((1, d)), _whole((8, 3 * d))], out_specs=_rows(tr, d),
        out_shape=jax.ShapeDtypeStruct((dm.R, d), F32), compiler_params=_params(("parallel",)),
    )(xc, outp, g_post_l, mod)


def _post_bwd(dm, outp, g_post_l, mod, dxn):
    tr, d, n = dm.TR, dm.D, dm.N

    def body(o_ref, g_ref, mod_ref, dxn_ref, do_ref, dg_ref, dgt_ref):
        i = pl.program_id(0)
        is_ctx = i * tr >= n
        gt = _seg_select(is_ctx, mod_ref, 2 * d, d)
        _, vjp = jax.vjp(_post_fn, o_ref[...], g_ref[...], gt)
        do, dg, dgt = vjp(dxn_ref[...])
        do_ref[...] = do.astype(BF16)

        @pl.when(i == 0)
        def _():
            dg_ref[...] = jnp.zeros_like(dg_ref)
            dgt_ref[...] = jnp.zeros_like(dgt_ref)

        dg_ref[...] += dg
        dgt_ref[...] += _seg_rows(is_ctx, dgt)

    return pl.pallas_call(
        body, name="post_bwd", grid=(dm.R // tr,),
        in_specs=[_rows(tr, d), _whole((1, d)), _whole((8, 3 * d)), _rows(tr, d)],
        out_specs=(_rows(tr, d), _whole((1, d)), _whole((8, d))),
        out_shape=(jax.ShapeDtypeStruct((dm.R, d), BF16), jax.ShapeDtypeStruct((1, d), F32),
                   jax.ShapeDtypeStruct((8, d), F32)),
        compiler_params=_params(("arbitrary",)),
    )(outp, g_post_l, mod, dxn)


def _group_sum(x, bd):
    hi = x.astype(BF16)
    lo = (x - hi.astype(F32)).astype(BF16)
    return jnp.dot(hi, bd, preferred_element_type=F32) + jnp.dot(lo, bd, preferred_element_type=F32)


def _swap_halves(x):
    lane = lax.broadcasted_iota(jnp.int32, x.shape, 1)
    return jnp.where((lane & 32) == 0, pltpu.roll(x, LANES - 32, 1), pltpu.roll(x, 32, 1))


def _qk_prep(dm, p, cos, sins, gq, gk, bd):
    tr, hd = dm.TR, dm.HD
    oq, ok, ov = 3 * dm.C, 3 * dm.C + dm.A, 3 * dm.C + dm.A + dm.KVW
    scale = hd ** -0.5 * LOG2E

    def body(q_ref, k_ref, v_ref, cos_ref, sin_ref, gq_ref, gk_ref, bd_ref, qh_ref, qt_ref, kh_ref, vh_ref, vx_ref):
        cosv, sinv, bdv = cos_ref[...], sin_ref[...], bd_ref[...]

        def chunk(x, g):
            rstd = lax.rsqrt(_group_sum(x * x, bdv) * (1.0 / hd) + EPS)
            xn = x * rstd * g
            return xn * cosv + _swap_halves(xn) * sinv

        for c in range(dm.A // LANES):
            y = chunk(q_ref[:, c * LANES:(c + 1) * LANES], gq_ref[...]) * scale
            qh_ref[2 * c] = y[:, :hd].astype(BF16)
            qh_ref[2 * c + 1] = y[:, hd:].astype(BF16)
            qt_ref[c * LANES:(c + 1) * LANES, :] = y.T.astype(BF16)
        for c in range(dm.KVW // LANES):
            y = chunk(k_ref[:, c * LANES:(c + 1) * LANES], gk_ref[...])
            kh_ref[2 * c] = y[:, :hd].astype(BF16)
            kh_ref[2 * c + 1] = y[:, hd:].astype(BF16)
            v = v_ref[:, c * LANES:(c + 1) * LANES]
            vh_ref[2 * c] = v[:, :hd].astype(BF16)
            vh_ref[2 * c + 1] = v[:, hd:].astype(BF16)
            lane = lax.broadcasted_iota(jnp.int32, v.shape, 1)
            one_at_hd = (lane == hd).astype(F32)
            vx_ref[2 * c] = jnp.where(lane < hd, v, one_at_hd).astype(BF16)
            vx_ref[2 * c + 1] = jnp.where(lane < hd, pltpu.roll(v, hd, 1), one_at_hd).astype(BF16)

    hspec = lambda nh, w=hd: pl.BlockSpec((nh, tr, w), lambda i: (0, i, 0))
    return pl.pallas_call(
        body, name="qk_prep", grid=(dm.R // tr,),
        in_specs=[_rows(tr, dm.A, _col(oq, dm.A)), _rows(tr, dm.KVW, _col(ok, dm.KVW)), _rows(tr, dm.KVW, _col(ov, dm.KVW)),
                  _rows(tr, LANES), _rows(tr, LANES), _whole((1, LANES)), _whole((1, LANES)), _whole((LANES, LANES))],
        out_specs=(hspec(dm.NH), pl.BlockSpec((dm.A, tr), lambda i: (0, i)), hspec(dm.NKV), hspec(dm.NKV), hspec(dm.NKV, LANES)),
        out_shape=(jax.ShapeDtypeStruct((dm.NH, dm.R, hd), BF16), jax.ShapeDtypeStruct((dm.A, dm.R), BF16),
                   jax.ShapeDtypeStruct((dm.NKV, dm.R, hd), BF16), jax.ShapeDtypeStruct((dm.NKV, dm.R, hd), BF16),
                   jax.ShapeDtypeStruct((dm.NKV, dm.R, LANES), BF16)),
        compiler_params=_params(("parallel",)),
    )(p, p, p, cos, sins, gq, gk, bd)


def _qk_bwd(dm, p, dq, dkt, dvt, cos, sins, gq, gk, bd):
    tr, hd = dm.TR, dm.HD
    oq, ok = 3 * dm.C, 3 * dm.C + dm.A
    scale = hd ** -0.5

    def body(q_ref, k_ref, dq_ref, dkh_ref, dvh_ref, cos_ref, sin_ref, gq_ref, gk_ref, bd_ref,
             dpq_ref, dpkv_ref, dgq_ref, dgk_ref):
        i = pl.program_id(0)
        cosv, sinv, bdv = cos_ref[...], sin_ref[...], bd_ref[...]

        def chunk(x, g, dy):
            rstd = lax.rsqrt(_group_sum(x * x, bdv) * (1.0 / hd) + EPS)
            xhat = x * rstd
            dxn = dy * cosv + _swap_halves(dy * sinv)
            dxhat = dxn * g
            dx = rstd * (dxhat - xhat * (_group_sum(dxhat * xhat, bdv) * (1.0 / hd)))
            return dx, jnp.sum(dxn * xhat, axis=0, keepdims=True)

        @pl.when(i == 0)
        def _():
            dgq_ref[...] = jnp.zeros_like(dgq_ref)
            dgk_ref[...] = jnp.zeros_like(dgk_ref)

        dgq = jnp.zeros((1, LANES), F32)
        for c in range(dm.A // LANES):
            sl = slice(c * LANES, (c + 1) * LANES)
            dx, dg = chunk(q_ref[:, sl], gq_ref[...], dq_ref[:, sl] * scale)
            dpq_ref[:, sl] = dx.astype(BF16)
            dgq = dgq + dg
        dgq_ref[...] += dgq + pltpu.roll(dgq, hd, 1)
        dgk = jnp.zeros((1, LANES), F32)
        for c in range(dm.KVW // LANES):
            sl = slice(c * LANES, (c + 1) * LANES)
            dx, dg = chunk(k_ref[:, sl], gk_ref[...], dkh_ref[sl, :].T)
            dpkv_ref[:, sl] = dx.astype(BF16)
            dgk = dgk + dg
            dpkv_ref[:, dm.KVW + c * LANES:dm.KVW + (c + 1) * LANES] = dvh_ref[sl, :].T.astype(BF16)
        dgk_ref[...] += dgk + pltpu.roll(dgk, hd, 1)

    hspec = pl.BlockSpec((dm.KVW, tr), lambda i: (0, i))
    return pl.pallas_call(
        body, name="qk_bwd", grid=(dm.R // tr,),
        in_specs=[_rows(tr, dm.A, _col(oq, dm.A)), _rows(tr, dm.KVW, _col(ok, dm.KVW)), _rows(tr, dm.A), hspec, hspec,
                  _rows(tr, LANES), _rows(tr, LANES), _whole((1, LANES)), _whole((1, LANES)), _whole((LANES, LANES))],
        out_specs=(_rows(tr, dm.A), _rows(tr, 2 * dm.KVW), _whole((1, LANES)), _whole((1, LANES))),
        out_shape=(jax.ShapeDtypeStruct((dm.R, dm.A), BF16), jax.ShapeDtypeStruct((dm.R, 2 * dm.KVW), BF16),
                   jax.ShapeDtypeStruct((1, LANES), F32), jax.ShapeDtypeStruct((1, LANES), F32)),
        compiler_params=_params(("arbitrary",)),
    )(p, p, dq, dkt, dvt, cos, sins, gq, gk, bd)


def _flash_fwd(dm, qh, kh, vx):
    tq, ks = dm.TR, dm.TR
    rs = min(tq, LANES)
    g, hd = dm.G, dm.HD
    nq, nq_lat = dm.R // tq, dm.N // tq
    n_chunks, ctx_chunks = dm.R // ks, dm.CTX // ks

    def body(q_ref, k_ref, v_ref, o_ref, lse_ref):
        qi = pl.program_id(1)

        def attend(chunks):
            for gg in range(g):
                for r in range(tq // rs):
                    rows = slice(r * rs, (r + 1) * rs)
                    q = q_ref[gg, rows, :]
                    m_prev = jnp.full((rs, LANES), -1e30, F32)
                    acc = jnp.zeros((rs, LANES), F32)
                    for kc in chunks:
                        keys = slice(kc * ks, (kc + 1) * ks)
                        s = lax.dot_general(q, k_ref[0, keys, :], _DN["nt"], preferred_element_type=F32)
                        m_new = jnp.maximum(m_prev, jnp.max(s, axis=-1, keepdims=True))
                        p = jnp.exp2(s - jnp.tile(m_new, (1, ks // LANES)))
                        acc = acc * jnp.exp2(m_prev - m_new) + jnp.dot(p.astype(BF16), v_ref[0, keys, :],
                                                                        preferred_element_type=F32)
                        m_prev = m_new
                    l = acc[:, hd:hd + 1]
                    o_ref[rows, gg * hd:(gg + 1) * hd] = acc[:, :hd] / l
                    lse_ref[gg, rows, :] = m_prev[:, :1] + jnp.log2(l)

        @pl.when(qi < nq_lat)
        def _():
            attend(range(n_chunks))

        @pl.when(qi >= nq_lat)
        def _():
            attend(range(n_chunks - ctx_chunks, n_chunks))

    return pl.pallas_call(
        body, name="flash_fwd", grid=(dm.NKV, nq),
        in_specs=[pl.BlockSpec((g, tq, hd), lambda h, qi: (h, qi, 0)), pl.BlockSpec((1, dm.R, hd), lambda h, qi: (h, 0, 0)),
                  pl.BlockSpec((1, dm.R, LANES), lambda h, qi: (h, 0, 0))],
        out_specs=(pl.BlockSpec((tq, g * hd), lambda h, qi: (qi, h)), pl.BlockSpec((g, tq, 1), lambda h, qi: (h, qi, 0))),
        out_shape=(jax.ShapeDtypeStruct((dm.R, dm.A), F32), jax.ShapeDtypeStruct((dm.NH, dm.R, 1), F32)),
        compiler_params=_params(("parallel", "parallel")),
    )(qh, kh, vx)


def _flash_bwd(dm, qh, qt, kh, vh, doh, dot, lse, delta):
    tq, ks = dm.TR, dm.TR
    g, hd = dm.G, dm.HD
    nq, nq_lat = dm.R // tq, dm.N // tq
    n_chunks, ctx_chunks = dm.R // ks, dm.CTX // ks

    def body(q_ref, qt_ref, k_ref, v_ref, do_ref, dot_ref, lse_ref, dl_ref, dq_ref, dk_acc, dv_acc):
        qi = pl.program_id(1)

        @pl.when(qi == 0)
        def _():
            dk_acc[...] = jnp.zeros_like(dk_acc)
            dv_acc[...] = jnp.zeros_like(dv_acc)

        def attend(chunks):
            q = q_ref[...].reshape(g * tq, hd)
            do = do_ref[...].reshape(g * tq, hd)
            q_t = jnp.concatenate([qt_ref[gg * hd:(gg + 1) * hd, :] for gg in range(g)], axis=1)
            do_t = jnp.concatenate([dot_ref[gg * hd:(gg + 1) * hd, :] for gg in range(g)], axis=1)
            lse_c = lse_ref[...].reshape(g * tq, 1)
            dl_c = dl_ref[...].reshape(g * tq, 1)
            dqs = jnp.zeros((g * tq, hd), F32)
            for kc in chunks:
                keys = slice(kc * ks, (kc + 1) * ks)
                k, v = k_ref[0, keys, :], v_ref[0, keys, :]
                s = lax.dot_general(q, k, _DN["nt"], preferred_element_type=F32)
                p = jnp.exp2(s - lse_c)
                dp = lax.dot_general(do, v, _DN["nt"], preferred_element_type=F32)
                ds = (p * (dp - dl_c)).astype(BF16)
                dv_acc[:, keys] += jnp.dot(do_t, p.astype(BF16), preferred_element_type=F32)
                dk_acc[:, keys] += jnp.dot(q_t, ds, preferred_element_type=F32)
                dqs = dqs + jnp.dot(ds, k, preferred_element_type=F32)
            for gg in range(g):
                dq_ref[:, gg * hd:(gg + 1) * hd] = dqs[gg * tq:(gg + 1) * tq]

        @pl.when(qi < nq_lat)
        def _():
            attend(range(n_chunks))

        @pl.when(qi >= nq_lat)
        def _():
            attend(range(n_chunks - ctx_chunks, n_chunks))

        @pl.when(qi == nq - 1)
        def _():
            dk_acc[...] = dk_acc[...] * LN2

    qspec = pl.BlockSpec((g, tq, hd), lambda h, qi: (h, qi, 0))
    tspec = pl.BlockSpec((g * hd, tq), lambda h, qi: (h, qi))
    kspec = pl.BlockSpec((1, dm.R, hd), lambda h, qi: (h, 0, 0))
    sspec = pl.BlockSpec((g, tq, 1), lambda h, qi: (h, qi, 0))
    ospec = pl.BlockSpec((hd, dm.R), lambda h, qi: (h, 0))
    return pl.pallas_call(
        body, name="flash_bwd", grid=(dm.NKV, nq),
        in_specs=[qspec, tspec, kspec, kspec, qspec, tspec, sspec, sspec],
        out_specs=(pl.BlockSpec((tq, g * hd), lambda h, qi: (qi, h)), ospec, ospec),
        out_shape=(jax.ShapeDtypeStruct((dm.R, dm.A), F32), jax.ShapeDtypeStruct((dm.KVW, dm.R), F32),
                   jax.ShapeDtypeStruct((dm.KVW, dm.R), F32)),
        compiler_params=_params(("parallel", "arbitrary")),
    )(qh, qt, kh, vh, doh, dot, lse, delta)


def _gate_b_specs(dm):
    ogb = 3 * dm.C + dm.A + 2 * dm.KVW
    half = dm.A // 2
    return [_rows(dm.TR, half, _col(ogb, half)), _rows(dm.TR, half, _col(ogb, half) + 1)]


def _attn_gate(dm, o, p):
    tr = dm.TR

    def body(o_ref, g0_ref, g1_ref, za_ref):
        gate = jnp.concatenate([g0_ref[...], g1_ref[...]], axis=1)
        za_ref[...] = (o_ref[...] * jax.nn.silu(gate)).astype(BF16)

    return pl.pallas_call(
        body, name="attn_gate", grid=(dm.R // tr,),
        in_specs=[_rows(tr, dm.A)] + _gate_b_specs(dm), out_specs=_rows(tr, dm.A),
        out_shape=jax.ShapeDtypeStruct((dm.R, dm.A), BF16), compiler_params=_params(("parallel",)),
    )(o, p, p)


def _attn_gate_bwd(dm, dza, o, p):
    tr, hd = dm.TR, dm.HD

    def body(dza_ref, o_ref, g0_ref, g1_ref, doh_ref, dot_ref, dl_ref, dgb_ref):
        gate = jnp.concatenate([g0_ref[...], g1_ref[...]], axis=1)
        dza, o = dza_ref[...], o_ref[...]
        do = dza * jax.nn.silu(gate)
        dgb_ref[...] = (dza * o * _dsilu(gate)).astype(BF16)
        dot_ref[...] = do.T.astype(BF16)
        prod = do * o
        for h in range(dm.NH):
            sl = slice(h * hd, (h + 1) * hd)
            doh_ref[h] = do[:, sl].astype(BF16)
            dl_ref[h] = jnp.sum(prod[:, sl], axis=-1, keepdims=True)

    return pl.pallas_call(
        body, name="attn_gate_bwd", grid=(dm.R // tr,),
        in_specs=[_rows(tr, dm.A), _rows(tr, dm.A)] + _gate_b_specs(dm),
        out_specs=(pl.BlockSpec((dm.NH, tr, hd), lambda i: (0, i, 0)), pl.BlockSpec((dm.A, tr), lambda i: (0, i)),
                   pl.BlockSpec((dm.NH, tr, 1), lambda i: (0, i, 0)), _rows(tr, dm.A)),
        out_shape=(jax.ShapeDtypeStruct((dm.NH, dm.R, hd), BF16), jax.ShapeDtypeStruct((dm.A, dm.R), BF16),
                   jax.ShapeDtypeStruct((dm.NH, dm.R, 1), F32), jax.ShapeDtypeStruct((dm.R, dm.A), BF16)),
        compiler_params=_params(("parallel",)),
    )(dza, o, p, p)


def _halo_specs(dm, col):
    per = dm.TR // HALO
    last = dm.R // HALO - 1
    prev = pl.BlockSpec((HALO, dm.C), lambda i: (jnp.maximum(i * per - 1, 0), col))
    nxt = pl.BlockSpec((HALO, dm.C), lambda i: (jnp.minimum((i + 1) * per, last), col))
    return prev, nxt


def _seg_valid(dm, i):
    tr = dm.TR
    r = i * tr - HALO + lax.broadcasted_iota(jnp.int32, (tr + 2 * HALO, 1), 0)
    is_ctx = i * tr >= dm.N
    lo = jnp.where(is_ctx, dm.N, 0)
    hi = jnp.where(is_ctx, dm.R, dm.N)
    return jnp.logical_and(r >= lo, r < hi)


CONV_RB = 64


def _shifted_copies(src_ref, sl, sh_ref, tr):
    rows = tr + 2 * HALO - SUBLANES
    padded = src_ref[:, sl]
    for ph in range(SUBLANES):
        sh_ref[ph, 0:rows, :] = padded[ph:ph + rows]


def _tap_rows(sh_ref, off, n):
    ph = off % SUBLANES
    return sh_ref[ph, off - ph:off - ph + n, :]


def _conv_fwd(dm, p, conv_w_l, conv_b_l, ln_g_l, ln_b_l):
    tr, c = dm.TR, dm.C

    def body(a_ref, g_ref, gate_ref, ap_ref, an_ref, gp_ref, gn_ref, w_ref, b_ref, lg_ref, lb_ref,
             zc_ref, y1_ref, u_ref, sh_ref):
        i = pl.program_id(0)
        a = jnp.concatenate([ap_ref[...], a_ref[...], an_ref[...]], axis=0)
        g = jnp.concatenate([gp_ref[...], g_ref[...], gn_ref[...]], axis=0)
        u_ref[...] = jnp.where(_seg_valid(dm, i), a * jax.nn.sigmoid(g), 0.0)
        for cc in range(c // LANES):
            sl = slice(cc * LANES, (cc + 1) * LANES)
            _shifted_copies(u_ref, sl, sh_ref, tr)
            for r0 in range(0, tr, CONV_RB):
                acc = jnp.zeros((CONV_RB, LANES), F32)
                for j in range(CONV_KERNEL):
                    acc = acc + w_ref[j:j + 1, sl] * _tap_rows(sh_ref, r0 + HALO - CONV_PAD + j, CONV_RB)
                y1_ref[r0:r0 + CONV_RB, sl] = acc + b_ref[:, sl]
        zc_ref[...] = _convact_fn(y1_ref[...], gate_ref[...], lg_ref[...], lb_ref[...]).astype(BF16)

    ap, an = _halo_specs(dm, 0)
    gp, gn = _halo_specs(dm, 1)
    return pl.pallas_call(
        body, name="conv_fwd", grid=(dm.R // tr,),
        in_specs=[_rows(tr, c, 0), _rows(tr, c, 1), _rows(tr, c, 2), ap, an, gp, gn,
                  _whole((32, c)), _whole((1, c)), _whole((1, c)), _whole((1, c))],
        out_specs=(_rows(tr, c), _rows(tr, c)),
        out_shape=(jax.ShapeDtypeStruct((dm.R, c), BF16), jax.ShapeDtypeStruct((dm.R, c), F32)),
        scratch_shapes=[pltpu.VMEM((tr + 2 * HALO, c), F32), pltpu.VMEM((SUBLANES, tr + 2 * HALO, LANES), F32)],
        compiler_params=_params(("parallel",)),
    )(p, p, p, p, p, p, p, conv_w_l, conv_b_l, ln_g_l, ln_b_l)


def _conv_act_bwd(dm, dzc, y1, p, ln_g_l, ln_b_l):
    tr, c = dm.TR, dm.C

    def body(dz_ref, y1_ref, gate_ref, lg_ref, lb_ref, dy_ref, dgate_ref, dlg_ref, dlb_ref):
        i = pl.program_id(0)
        _, vjp = jax.vjp(_convact_fn, y1_ref[...], gate_ref[...], lg_ref[...], lb_ref[...])
        dy, dgate, dlg, dlb = vjp(dz_ref[...])
        dy_ref[...] = dy
        dgate_ref[...] = dgate.astype(BF16)

        @pl.when(i == 0)
        def _():
            dlg_ref[...] = jnp.zeros_like(dlg_ref)
            dlb_ref[...] = jnp.zeros_like(dlb_ref)

        dlg_ref[...] += dlg
        dlb_ref[...] += dlb

    return pl.pallas_call(
        body, name="conv_act_bwd", grid=(dm.R // tr,),
        in_specs=[_rows(tr, c), _rows(tr, c), _rows(tr, c, 2), _whole((1, c)), _whole((1, c))],
        out_specs=(_rows(tr, c), _rows(tr, c), _whole((1, c)), _whole((1, c))),
        out_shape=(jax.ShapeDtypeStruct((dm.R, c), F32), jax.ShapeDtypeStruct((dm.R, c), BF16),
                   jax.ShapeDtypeStruct((1, c), F32), jax.ShapeDtypeStruct((1, c), F32)),
        compiler_params=_params(("arbitrary",)),
    )(dzc, y1, p, ln_g_l, ln_b_l)


def _conv_bwd(dm, dy1, p, conv_w_l):
    tr, c = dm.TR, dm.C

    def body(dy_ref, dyp_ref, dyn_ref, a_ref, g_ref, ap_ref, an_ref, gp_ref, gn_ref, w_ref,
             dag_ref, dw_ref, db_ref, u_ref, dyp_scr, du_ref, dwp_ref, shu_ref, shd_ref):
        i = pl.program_id(0)
        valid = _seg_valid(dm, i)
        a = jnp.concatenate([ap_ref[...], a_ref[...], an_ref[...]], axis=0)
        g = jnp.concatenate([gp_ref[...], g_ref[...], gn_ref[...]], axis=0)
        u_ref[...] = jnp.where(valid, a * jax.nn.sigmoid(g), 0.0)
        dyp_scr[...] = jnp.where(valid, jnp.concatenate([dyp_ref[...], dy_ref[...], dyn_ref[...]], axis=0), 0.0)

        @pl.when(i == 0)
        def _():
            dwp_ref[...] = jnp.zeros_like(dwp_ref)
            db_ref[...] = jnp.zeros_like(db_ref)

        db_ref[...] += jnp.sum(dy_ref[...], axis=0, keepdims=True)
        for cc in range(c // LANES):
            sl = slice(cc * LANES, (cc + 1) * LANES)
            _shifted_copies(u_ref, sl, shu_ref, tr)
            _shifted_copies(dyp_scr, sl, shd_ref, tr)
            for r0 in range(0, tr, CONV_RB):
                dy_cur = dy_ref[r0:r0 + CONV_RB, sl]
                acc = jnp.zeros((CONV_RB, LANES), F32)
                for j in range(CONV_KERNEL):
                    acc = acc + w_ref[j:j + 1, sl] * _tap_rows(shd_ref, r0 + HALO + CONV_PAD - j, CONV_RB)
                    prod = dy_cur * _tap_rows(shu_ref, r0 + HALO - CONV_PAD + j, CONV_RB)
                    dwp_ref[j, :, sl] += jnp.sum(prod.reshape(CONV_RB // SUBLANES, SUBLANES, LANES), axis=0)
                du_ref[r0:r0 + CONV_RB, sl] = acc
        du = du_ref[...]
        sg = jax.nn.sigmoid(g_ref[...])
        dag_ref[:, :c] = (du * sg).astype(BF16)
        dag_ref[:, c:] = (du * a_ref[...] * sg * (1.0 - sg)).astype(BF16)

        @pl.when(i == dm.R // tr - 1)
        def _():
            dw_ref[...] = jnp.sum(dwp_ref[...], axis=1)

    ap, an = _halo_specs(dm, 0)
    gp, gn = _halo_specs(dm, 1)
    dyp, dyn = _halo_specs(dm, 0)
    return pl.pallas_call(
        body, name="conv_bwd", grid=(dm.R // tr,),
        in_specs=[_rows(tr, c), dyp, dyn, _rows(tr, c, 0), _rows(tr, c, 1), ap, an, gp, gn, _whole((32, c))],
        out_specs=(_rows(tr, 2 * c), _whole((32, c)), _whole((1, c))),
        out_shape=(jax.ShapeDtypeStruct((dm.R, 2 * c), BF16), jax.ShapeDtypeStruct((32, c), F32),
                   jax.ShapeDtypeStruct((1, c), F32)),
        scratch_shapes=[pltpu.VMEM((tr + 2 * HALO, c), F32), pltpu.VMEM((tr + 2 * HALO, c), F32), pltpu.VMEM((tr, c), F32),
                        pltpu.VMEM((32, SUBLANES, c), F32), pltpu.VMEM((SUBLANES, tr + 2 * HALO, LANES), F32),
                        pltpu.VMEM((SUBLANES, tr + 2 * HALO, LANES), F32)],
        compiler_params=_params(("arbitrary",)),
    )(dy1, dy1, dy1, p, p, p, p, p, p, conv_w_l)


def _gm_specs(dm):
    half = dm.D // 2
    first = _col(3 * dm.C + 2 * dm.A + 2 * dm.KVW, half)
    return [_rows(dm.TR, half, first + k) for k in range(4)]


def _gm_load(a0, a1, b0, b1):
    return jnp.concatenate([a0[...], a1[...]], axis=1), jnp.concatenate([b0[...], b1[...]], axis=1)


def _merge(dm, yc, ya, p):
    tr, d = dm.TR, dm.D

    def body(yc_ref, ya_ref, a0, a1, b0, b1, m_ref):
        ga, gb = _gm_load(a0, a1, b0, b1)
        m_ref[...] = (jax.nn.sigmoid(ga) * yc_ref[...] + jax.nn.sigmoid(gb) * ya_ref[...]).astype(BF16)

    return pl.pallas_call(
        body, name="merge", grid=(dm.R // tr,),
        in_specs=[_rows(tr, d), _rows(tr, d)] + _gm_specs(dm), out_specs=_rows(tr, d),
        out_shape=jax.ShapeDtypeStruct((dm.R, d), BF16), compiler_params=_params(("parallel",)),
    )(yc, ya, p, p, p, p)


def _merge_bwd(dm, dmerged, yc, ya, p):
    tr, d = dm.TR, dm.D

    def body(dm_ref, yc_ref, ya_ref, a0, a1, b0, b1, dyc_ref, dya_ref, dgm_ref):
        dmv = dm_ref[...]
        ga, gb = _gm_load(a0, a1, b0, b1)
        sa, sb = jax.nn.sigmoid(ga), jax.nn.sigmoid(gb)
        dyc_ref[...] = (dmv * sa).astype(BF16)
        dya_ref[...] = (dmv * sb).astype(BF16)
        dgm_ref[:, :d] = (dmv * yc_ref[...] * sa * (1.0 - sa)).astype(BF16)
        dgm_ref[:, d:] = (dmv * ya_ref[...] * sb * (1.0 - sb)).astype(BF16)

    return pl.pallas_call(
        body, name="merge_bwd", grid=(dm.R // tr,),
        in_specs=[_rows(tr, d), _rows(tr, d), _rows(tr, d)] + _gm_specs(dm),
        out_specs=(_rows(tr, d), _rows(tr, d), _rows(tr, 2 * d)),
        out_shape=(jax.ShapeDtypeStruct((dm.R, d), BF16), jax.ShapeDtypeStruct((dm.R, d), BF16),
                   jax.ShapeDtypeStruct((dm.R, 2 * d), BF16)),
        compiler_params=_params(("parallel",)),
    )(dmerged, yc, ya, p, p, p, p)


def _loss_head(dm, xc, target):
    tr, d = dm.TR, dm.D
    n_lat = dm.N // tr

    def body(x_ref, t_ref, dy_ref, l_ref):
        i = pl.program_id(0)

        @pl.when(i == 0)
        def _():
            l_ref[...] = jnp.zeros_like(l_ref)

        @pl.when(i < n_lat)
        def _():
            e = x_ref[...] - t_ref[...]
            dy_ref[...] = e * (1.0 / d)
            l_ref[...] += 0.5 * jnp.sum(jnp.mean(e * e, axis=-1, keepdims=True), axis=0, keepdims=True)

        @pl.when(i >= n_lat)
        def _():
            dy_ref[...] = jnp.zeros_like(dy_ref)

    return pl.pallas_call(
        body, name="loss_head", grid=(dm.R // tr,),
        in_specs=[_rows(tr, d), pl.BlockSpec((tr, d), lambda i: (jnp.minimum(i, n_lat - 1), 0))],
        out_specs=(_rows(tr, d), _whole((1, 1))),
        out_shape=(jax.ShapeDtypeStruct((dm.R, d), F32), jax.ShapeDtypeStruct((1, 1), F32)),
        compiler_params=_params(("arbitrary",)),
    )(xc, target)


def _adamw(w, g, m, v, name):
    rows, cols = w.shape
    fits = lambda t: t * cols * 4 <= ADAM_BLOCK_BYTES
    tr = rows if fits(rows) else _pick(rows, [t for t in (512, 256, 128, 64, 32, 16, 8) if fits(t)])

    def body(w_ref, g_ref, m_ref, v_ref, d_ref, mo_ref, vo_ref):
        d_ref[...], mo_ref[...], vo_ref[...] = _adam_update(w_ref[...], g_ref[...], m_ref[...], v_ref[...])

    spec = _rows(tr, cols)
    shp = jax.ShapeDtypeStruct(w.shape, F32)
    return pl.pallas_call(
        body, name=name, grid=(rows // tr,), in_specs=[spec] * 4, out_specs=(spec,) * 3, out_shape=(shp,) * 3,
        compiler_params=_params(("parallel",)),
    )(w, g, m, v)


def _adam_update(w, g, m, v):
    mn = ADAM_B1 * m + (1.0 - ADAM_B1) * g
    vn = ADAM_B2 * v + (1.0 - ADAM_B2) * jnp.square(g)
    m_hat = mn / (1.0 - ADAM_B1 ** ADAM_STEP)
    v_hat = vn / (1.0 - ADAM_B2 ** ADAM_STEP)
    return -ADAM_LR * (m_hat / (jnp.sqrt(v_hat) + ADAM_EPS) + ADAM_WD * w), mn, vn


def _rope_tables(dm):
    n = dm.N
    rows = n // GRID_W
    row = jnp.repeat(jnp.arange(rows, dtype=F32), GRID_W)
    col = jnp.tile(jnp.arange(GRID_W, dtype=F32), rows)
    half = dm.HD // 2
    inv_freq = ROPE_THETA ** (-jnp.arange(0, half, 2, dtype=F32) / half)
    ang = jnp.concatenate([row[:, None] * inv_freq, col[:, None] * inv_freq], axis=-1)
    cos, sin = jnp.cos(ang), jnp.sin(ang)
    cos = jnp.concatenate([cos, jnp.ones((dm.CTX, half), F32)], axis=0)
    sin = jnp.concatenate([sin, jnp.zeros((dm.CTX, half), F32)], axis=0)
    reps = LANES // dm.HD
    return jnp.tile(jnp.concatenate([cos, cos], axis=1), (1, reps)), jnp.tile(jnp.concatenate([-sin, sin], axis=1), (1, reps))


def _lane_vec(g, hd):
    return jnp.tile(g.reshape(1, hd), (1, LANES // hd))


def _layer_fwd(dm, xc, cv, wl, tabs):
    mod = _mod_fwd(cv, wl["w_mod"], wl["b_mod"])
    h = _prenorm(dm, xc, wl["g_pre"], mod)
    p = _matmul(h, wl["w_in"], "nn", "in_proj")
    qh, qt, kh, vh, vx = _qk_prep(dm, p, tabs["cos"], tabs["sins"], wl["gq"], wl["gk"], tabs["bd"])
    o, lse = _flash_fwd(dm, qh, kh, vx)
    za = _attn_gate(dm, o, p)
    zc, y1 = _conv_fwd(dm, p, wl["conv_w"], wl["conv_b"], wl["ln_g"], wl["ln_b"])
    yc = _matmul(zc, wl["w_conv_out"], "nn", "conv_out")
    ya = _matmul(za, wl["w_attn_out"], "nn", "attn_out")
    merged = _merge(dm, yc, ya, p)
    outp = _matmul(merged, wl["w_out"], "nn", "out_proj")
    x_new = _post(dm, xc, outp, wl["g_post"], mod)
    saved = dict(xc=xc, mod=mod, h=h, p=p, qh=qh, qt=qt, kh=kh, vh=vh, o=o, lse=lse, za=za, zc=zc, y1=y1, yc=yc, ya=ya,
                 merged=merged, outp=outp)
    return x_new, saved


def _layer_bwd(dm, dxn, cv, wl, tabs, s):
    p = s["p"]
    d_outp, dg_post, dgt = _post_bwd(dm, s["outp"], wl["g_post"], s["mod"], dxn)
    dmerged = _matmul(d_outp, wl["w_out"], "nt", "out_proj_dx")
    dw_out = _matmul(s["merged"], d_outp, "tn", "out_proj_dw")
    dyc, dya, dgm = _merge_bwd(dm, dmerged, s["yc"], s["ya"], p)
    dzc = _matmul(dyc, wl["w_conv_out"], "nt", "conv_out_dx")
    dw_conv_out = _matmul(s["zc"], dyc, "tn", "conv_out_dw")
    dza = _matmul(dya, wl["w_attn_out"], "nt", "attn_out_dx")
    dw_attn_out = _matmul(s["za"], dya, "tn", "attn_out_dw")
    doh, dot, delta, dgate_b = _attn_gate_bwd(dm, dza, s["o"], p)
    dq, dkt, dvt = _flash_bwd(dm, s["qh"], s["qt"], s["kh"], s["vh"], doh, dot, s["lse"], delta)
    dpq, dpkv, dgq, dgk = _qk_bwd(dm, p, dq, dkt, dvt, tabs["cos"], tabs["sins"], wl["gq"], wl["gk"], tabs["bd"])
    dy1, dgate_a, dln_g, dln_b = _conv_act_bwd(dm, dzc, s["y1"], p, wl["ln_g"], wl["ln_b"])
    dpag, dconv_w, dconv_b = _conv_bwd(dm, dy1, p, wl["conv_w"])
    dp = jnp.concatenate([dpag, dgate_a, dpq, dpkv, dgate_b, dgm], axis=1)
    dh = _matmul(dp, wl["w_in"], "nt", "in_proj_dx")
    dw_in = _matmul(s["h"], dp, "tn", "in_proj_dw")
    dxc, dg_pre, dshsc = _prenorm_bwd(dm, s["xc"], wl["g_pre"], s["mod"], dh, dxn)
    dmod = jnp.concatenate([dshsc, dgt], axis=1)
    dw_mod, db_mod, dcv = _mod_bwd(cv, wl["w_mod"], dmod)
    grads = dict(w_mod=dw_mod, b_mod=db_mod[0], g_pre=dg_pre[0], g_post=dg_post[0], w_in=dw_in,
                 conv_w=dconv_w[:CONV_KERNEL], conv_b=dconv_b[0], ln_g=dln_g[0], ln_b=dln_b[0],
                 w_conv_out=dw_conv_out, q_norm_g=dgq[0, :dm.HD], k_norm_g=dgk[0, :dm.HD],
                 w_attn_out=dw_attn_out, w_out=dw_out, c_ctx=dcv[1])
    return dxc, grads


def _local_step(dm, x, c, ctx, c_ctx, target, wf):
    depth = wf["w_in"].shape[0]
    cos, sins = _rope_tables(dm)
    lane = np.arange(LANES)
    bd = jnp.asarray((lane[:, None] // dm.HD == lane[None, :] // dm.HD), dtype=BF16)
    tabs = dict(cos=cos, sins=sins, bd=bd)
    cv = jnp.concatenate([c.reshape(1, dm.D), c_ctx.reshape(1, dm.D), jnp.zeros((6, dm.D), F32)], axis=0)
    xc = jnp.concatenate([x, ctx], axis=0)
    layers, saved = [], []
    for l in range(depth):
        wl = dict(
            w_mod=wf["w_mod"][l], b_mod=wf["b_mod"][l].reshape(1, -1), g_pre=wf["g_pre"][l].reshape(1, -1),
            g_post=wf["g_post"][l].reshape(1, -1), w_in=wf["w_in"][l],
            conv_w=jnp.concatenate([wf["conv_w"][l], jnp.zeros((1, dm.C), F32)], axis=0),
            conv_b=wf["conv_b"][l].reshape(1, -1), ln_g=wf["ln_g"][l].reshape(1, -1), ln_b=wf["ln_b"][l].reshape(1, -1),
            w_conv_out=wf["w_conv_out"][l], gq=_lane_vec(wf["q_norm_g"][l], dm.HD), gk=_lane_vec(wf["k_norm_g"][l], dm.HD),
            w_attn_out=wf["w_attn_out"][l], w_out=wf["w_out"][l])
        layers.append(wl)
        xc, s = _layer_fwd(dm, xc, cv, wl, tabs)
        saved.append(s)
    dxc, loss = _loss_head(dm, xc, target)
    grads = [None] * depth
    for l in reversed(range(depth)):
        dxc, grads[l] = _layer_bwd(dm, dxc, cv, layers[l], tabs, saved[l])
    big = ("w_mod", "w_in", "w_conv_out", "w_attn_out", "w_out")
    gw = {k: [g[k] for g in grads] if k in big else jnp.stack([g[k] for g in grads]) for k in grads[0] if k != "c_ctx"}
    gw["c_ctx"] = grads[0]["c_ctx"] + grads[1]["c_ctx"] if depth == 2 else sum(g["c_ctx"] for g in grads)
    return loss[0, 0], dxc[:dm.N], gw


def _dims(x, ctx, w_in_full_cols, conv_c, attn_w, hd):
    n, d = x.shape
    ctx_len = ctx.shape[0]
    c, a = conv_c, attn_w
    kvw = (w_in_full_cols - 3 * c - 2 * a - 2 * d) // 2
    r = n + ctx_len
    tr = _pick(np.gcd(n, ctx_len), (256, 128))
    return Dims(N=n, CTX=ctx_len, R=r, D=d, C=c, A=a, KVW=kvw, HD=hd, NH=a // hd, NKV=kvw // hd,
                G=a // kvw, W=w_in_full_cols, TR=tr)


_HBM = pl.BlockSpec(memory_space=pltpu.HBM)
_VMEM = pl.BlockSpec(memory_space=pltpu.VMEM)


def _place():
    x, y, c = lax.axis_index("x"), lax.axis_index("y"), lax.axis_index("c")
    chips = [(1 - x, y), (x, 1 - y), (1 - x, 1 - y)]
    return x, y, c, chips


def _row_pieces(rows, want):
    n = max(k for k in range(1, want + 1) if rows % (16 * k) == 0)
    return [pl.ds(i * (rows // n), rows // n) for i in range(n)]


def _allgather_chips(own, name):
    _, h, w = own.shape
    pieces = _row_pieces(h, 4)
    npc = len(pieces)

    def body(own_ref, out_ref, send_sems, recv_sems):
        x, y, c, chips = _place()
        s_me = 2 * x + y
        sibling = (x, y, 1 - c)

        def copy(k, src, dst, to):
            return pltpu.make_async_remote_copy(src_ref=src, dst_ref=dst, send_sem=send_sems.at[k], recv_sem=recv_sems.at[k],
                                                device_id=to, device_id_type=MESH)

        sends = []
        for i, rows in enumerate(pieces):
            for j, chip in enumerate(chips):
                sends.append(copy(i * 3 + j, own_ref.at[c, rows], out_ref.at[s_me, c, rows], (*chip, c)))
        for cp in sends:
            cp.start()
        for i, rows in enumerate(pieces):
            for j, (cx, cy) in enumerate(chips):
                landed = out_ref.at[2 * cx + cy, c, rows]
                copy(i * 3 + j, landed, landed, (cx, cy, c)).wait_recv()
                fwd = copy(3 * npc + i * 3 + j, landed, landed, sibling)
                fwd.start()
                sends.append(fwd)
        for i, rows in enumerate(pieces):
            for j, (cx, cy) in enumerate(chips):
                theirs = out_ref.at[2 * cx + cy, 1 - c, rows]
                copy(3 * npc + i * 3 + j, theirs, theirs, sibling).wait_recv()
        for cp in sends:
            cp.wait_send()

    return pl.pallas_call(
        body, name=name, in_specs=[_HBM], out_specs=_HBM,
        out_shape=jax.ShapeDtypeStruct((N_CHIPS, 2, h, w), own.dtype),
        scratch_shapes=[pltpu.SemaphoreType.DMA((6 * npc,)), pltpu.SemaphoreType.DMA((6 * npc,))],
    )(own)


def _pair_exchange(buf, name):
    _, m, w = buf.shape
    pieces = _row_pieces(m, 16)

    def body(buf_ref, out_ref, send_sems, recv_sems):
        x, y, c, _ = _place()
        cps = [pltpu.make_async_remote_copy(src_ref=buf_ref.at[1 - c, rows], dst_ref=out_ref.at[rows], send_sem=send_sems.at[i],
                                            recv_sem=recv_sems.at[i], device_id=(x, y, 1 - c), device_id_type=MESH)
               for i, rows in enumerate(pieces)]
        for cp in cps:
            cp.start()
        for cp in cps:
            cp.wait()

    return pl.pallas_call(
        body, name=name, in_specs=[_HBM], out_specs=_HBM, out_shape=jax.ShapeDtypeStruct((m, w), buf.dtype),
        scratch_shapes=[pltpu.SemaphoreType.DMA((len(pieces),)), pltpu.SemaphoreType.DMA((len(pieces),))],
    )(buf)


def _pair_swap(t, name):
    m, w = t.shape
    pieces = _row_pieces(m, 8)
    n = len(pieces)

    def body(t_ref, out_ref, send_sems, recv_sems):
        x, y, c, _ = _place()
        cps = [pltpu.make_async_remote_copy(src_ref=t_ref.at[rows], dst_ref=out_ref.at[rows], send_sem=send_sems.at[i],
                                            recv_sem=recv_sems.at[i], device_id=(x, y, 1 - c), device_id_type=MESH)
               for i, rows in enumerate(pieces)]
        for cp in cps:
            cp.start()
        for cp in cps:
            cp.wait()

    return pl.pallas_call(
        body, name=name, in_specs=[_HBM], out_specs=_HBM, out_shape=jax.ShapeDtypeStruct((m, w), t.dtype),
        scratch_shapes=[pltpu.SemaphoreType.DMA((n,)), pltpu.SemaphoreType.DMA((n,))],
    )(t)


def _scatter_chips(s, name):
    _, h, w = s.shape

    def body(s_ref, out_ref, send_sems, recv_sems):
        _, _, c, chips = _place()
        cps = [pltpu.make_async_remote_copy(src_ref=s_ref.at[2 * cx + cy], dst_ref=out_ref.at[j], send_sem=send_sems.at[j],
                                            recv_sem=recv_sems.at[j], device_id=(cx, cy, c), device_id_type=MESH)
               for j, (cx, cy) in enumerate(chips)]
        for cp in cps:
            cp.start()
        for cp in cps:
            cp.wait()

    return pl.pallas_call(
        body, name=name, in_specs=[_HBM], out_specs=_HBM, out_shape=jax.ShapeDtypeStruct((3, h, w), s.dtype),
        scratch_shapes=[pltpu.SemaphoreType.DMA((3,)), pltpu.SemaphoreType.DMA((3,))],
    )(s)


def _sum_pair(buf, recv, core, name):
    m, w = recv.shape
    tr = _pick(m, (512, 432, 256, 128, 64, 32, 16))

    def body(c_ref, b_ref, r_ref, o_ref, ob_ref):
        v = b_ref[0] + r_ref[...]
        o_ref[...] = v
        ob_ref[...] = v.astype(BF16)

    spec = pl.BlockSpec((tr, w), lambda i, c: (i, 0))
    return pl.pallas_call(
        body, name=name,
        grid_spec=pltpu.PrefetchScalarGridSpec(
            num_scalar_prefetch=1, grid=(m // tr,),
            in_specs=[pl.BlockSpec((1, tr, w), lambda i, c: (c[0], i, 0)), spec], out_specs=(spec, spec)),
        out_shape=(jax.ShapeDtypeStruct((m, w), F32), jax.ShapeDtypeStruct((m, w), BF16)),
        compiler_params=_params(("parallel",)),
    )(core, buf, recv)


def _sum_chips(own, recv, chip, name):
    _, h, w = own.shape
    tr = _pick(h, (512, 432, 256, 128, 64, 32, 16))

    def body(s_ref, a_ref, r_ref, o_ref):
        acc = a_ref[0]
        for j in range(3):
            acc = acc + r_ref[j].astype(F32)
        o_ref[...] = acc

    return pl.pallas_call(
        body, name=name,
        grid_spec=pltpu.PrefetchScalarGridSpec(
            num_scalar_prefetch=1, grid=(h // tr,),
            in_specs=[pl.BlockSpec((1, tr, w), lambda i, s: (s[0], i, 0)), pl.BlockSpec((3, tr, w), lambda i, s: (0, i, 0))],
            out_specs=pl.BlockSpec((tr, w), lambda i, s: (i, 0))),
        out_shape=jax.ShapeDtypeStruct((h, w), F32), compiler_params=_params(("parallel",)),
    )(chip, own, recv)


def _allreduce_small(own, name):
    m, w = own.shape

    def body(own_ref, sum_ref, all_ref, send_sems, recv_sems):
        x, y, c, _ = _place()
        me = 4 * x + 2 * y + c
        all_ref[me] = own_ref[...]
        sends = []
        for k in range(1, N_DEV):
            peer = (x ^ ((k >> 2) & 1), y ^ ((k >> 1) & 1), c ^ (k & 1))
            cp = pltpu.make_async_remote_copy(src_ref=own_ref, dst_ref=all_ref.at[me], send_sem=send_sems.at[k - 1],
                                              recv_sem=recv_sems.at[k - 1], device_id=peer, device_id_type=MESH)
            cp.start()
            sends.append(cp)
        for k in range(1, N_DEV):
            px, py, pc = x ^ ((k >> 2) & 1), y ^ ((k >> 1) & 1), c ^ (k & 1)
            slot = all_ref.at[4 * px + 2 * py + pc]
            pltpu.make_async_remote_copy(src_ref=slot, dst_ref=slot, send_sem=send_sems.at[k - 1], recv_sem=recv_sems.at[k - 1],
                                         device_id=(px, py, pc), device_id_type=MESH).wait_recv()
        for cp in sends:
            cp.wait_send()
        acc = all_ref[0]
        for d in range(1, N_DEV):
            acc = acc + all_ref[d]
        sum_ref[...] = acc

    return pl.pallas_call(
        body, name=name, in_specs=[_VMEM], out_specs=_VMEM, out_shape=jax.ShapeDtypeStruct((m, w), F32),
        scratch_shapes=[pltpu.VMEM((N_DEV, m, w), F32), pltpu.SemaphoreType.DMA((N_DEV - 1,)), pltpu.SemaphoreType.DMA((N_DEV - 1,))],
        compiler_params=pltpu.CompilerParams(vmem_limit_bytes=VMEM_LIMIT),
    )(own)


_COL_SHARDED = ("w_mod", "w_in")
_ROW_SHARDED = ("w_conv_out", "w_attn_out", "w_out")
_BIG = _COL_SHARDED + _ROW_SHARDED
_SMALL = ("c_ctx", "b_mod", "g_pre", "g_post", "conv_b", "ln_g", "ln_b", "q_norm_g", "k_norm_g")
FLAT_W = 1024


def _unflatten_shard(flat, like):
    out, off = {}, 0
    flat = flat.reshape(-1)
    for k in _BIG:
        n = int(np.prod(like[k].shape))
        out[k] = flat[off:off + n].reshape(like[k].shape)
        off += n
    return out


def _by_shard(k, layers):
    parts = []
    for full in layers:
        if k in _COL_SHARDED:
            d, cols = full.shape
            parts.append(full.reshape(d, N_CHIPS, cols // N_CHIPS).transpose(1, 0, 2).reshape(N_CHIPS, -1))
        else:
            parts.append(full.reshape(N_CHIPS, -1))
    return jnp.concatenate(parts, axis=1)


def _from_shards(k, flat4, like):
    l = like.shape[0]
    if k in _COL_SHARDED:
        d, cs = like.shape[1:]
        return flat4.reshape(N_CHIPS, l, d, cs).transpose(1, 2, 0, 3).reshape(l, d, N_CHIPS * cs)
    rs, d = like.shape[1:]
    return flat4.reshape(N_CHIPS, l, rs, d).transpose(1, 0, 2, 3).reshape(l, N_CHIPS * rs, d)


def _pad_rows(flat, mult=8):
    n = flat.shape[0]
    rows = -(-n // LANES)
    rows = -(-rows // mult) * mult
    return jnp.concatenate([flat, jnp.zeros((rows * LANES - n,), F32)]).reshape(rows, LANES)


def kernel(x, c, ctx, c_ctx, w_mod, b_mod, g_pre, g_post, w_in, conv_w, conv_b, ln_g, ln_b, w_conv_out, q_norm_g, k_norm_g, w_attn_out, w_out, loss_target, m_c_ctx, m_w_mod, m_b_mod, m_g_pre, m_g_post, m_w_in, m_conv_w, m_conv_b, m_ln_g, m_ln_b, m_w_conv_out, m_q_norm_g, m_k_norm_g, m_w_attn_out, m_w_out, v_c_ctx, v_w_mod, v_b_mod, v_g_pre, v_g_post, v_w_in, v_conv_w, v_conv_b, v_ln_g, v_ln_b, v_w_conv_out, v_q_norm_g, v_k_norm_g, v_w_attn_out, v_w_out):
    w = dict(c_ctx=c_ctx, w_mod=w_mod, b_mod=b_mod, g_pre=g_pre, g_post=g_post, w_in=w_in, conv_w=conv_w, conv_b=conv_b,
             ln_g=ln_g, ln_b=ln_b, w_conv_out=w_conv_out, q_norm_g=q_norm_g, k_norm_g=k_norm_g, w_attn_out=w_attn_out, w_out=w_out)
    m = dict(c_ctx=m_c_ctx, w_mod=m_w_mod, b_mod=m_b_mod, g_pre=m_g_pre, g_post=m_g_post, w_in=m_w_in, conv_w=m_conv_w,
             conv_b=m_conv_b, ln_g=m_ln_g, ln_b=m_ln_b, w_conv_out=m_w_conv_out, q_norm_g=m_q_norm_g, k_norm_g=m_k_norm_g,
             w_attn_out=m_w_attn_out, w_out=m_w_out)
    v = dict(c_ctx=v_c_ctx, w_mod=v_w_mod, b_mod=v_b_mod, g_pre=v_g_pre, g_post=v_g_post, w_in=v_w_in, conv_w=v_conv_w,
             conv_b=v_conv_b, ln_g=v_ln_g, ln_b=v_ln_b, w_conv_out=v_w_conv_out, q_norm_g=v_q_norm_g, k_norm_g=v_k_norm_g,
             w_attn_out=v_w_attn_out, w_out=v_w_out)
    order = ("c_ctx", "w_mod", "b_mod", "g_pre", "g_post", "w_in", "conv_w", "conv_b", "ln_g", "ln_b", "w_conv_out",
             "q_norm_g", "k_norm_g", "w_attn_out", "w_out")
    my_chip = 2 * lax.axis_index("x") + lax.axis_index("y")
    my_core = lax.axis_index("c")
    depth, _, conv_cs = conv_w.shape
    conv_c = conv_cs * N_CHIPS
    dm = _dims(x[0], ctx[0], w_in.shape[2] * N_CHIPS, conv_c, w_attn_out.shape[1] * N_CHIPS, q_norm_g.shape[1])

    own = jnp.concatenate([w[k].astype(BF16).reshape(-1) for k in _BIG])
    rows2 = own.shape[0] // FLAT_W
    own = own.reshape(2, rows2 // 2, FLAT_W)
    gathered = _allgather_chips(own, "gather_weights").reshape(N_CHIPS, -1)
    is_mine = (jnp.arange(N_CHIPS) == my_chip)[:, None]
    gathered = jnp.where(is_mine, own.reshape(1, -1), gathered)
    wf, off = {}, 0
    for k in _BIG:
        n = int(np.prod(w[k].shape))
        wf[k] = _from_shards(k, gathered[:, off:off + n], w[k])
        off += n
    cw = jnp.zeros((depth, CONV_KERNEL, N_CHIPS, conv_cs), F32)
    cw = lax.dynamic_update_slice(cw, conv_w[:, :, None, :], (0, 0, my_chip, 0))
    cw = jnp.where(my_core == 0, cw, 0.0).reshape(-1)
    wf["conv_w"] = _allreduce_small(_pad_rows(cw), "gather_conv_w").reshape(-1)[:cw.shape[0]].reshape(depth, CONV_KERNEL, conv_c)
    for k in _SMALL:
        wf[k] = w[k]

    loss_part, grad_x, gw = _local_step(dm, x[0], c, ctx[0], c_ctx, loss_target[0], wf)
    loss = lax.psum(loss_part, ("x", "y", "c"))

    g4 = jnp.concatenate([_by_shard(k, gw[k]) for k in _BIG], axis=1)
    hrows = rows2 // 2
    g24 = g4.reshape(N_CHIPS, 2, hrows, FLAT_W).transpose(1, 0, 2, 3).reshape(2, N_CHIPS * hrows, FLAT_W)
    core_i = my_core.astype(jnp.int32).reshape(1)
    chip_i = my_chip.astype(jnp.int32).reshape(1)
    theirs = _pair_exchange(g24, "reduce_pair")
    chip_sum, chip_sum_bf = _sum_pair(g24, theirs, core_i, "reduce_pair_add")
    others = _scatter_chips(chip_sum_bf.reshape(N_CHIPS, hrows, FLAT_W), "reduce_chips")
    mine = _sum_chips(chip_sum.reshape(N_CHIPS, hrows, FLAT_W), others, chip_i, "reduce_chips_add")
    sibling_half = _pair_swap(mine, "reduce_share")

    small = jnp.concatenate([gw[k].reshape(-1) for k in _SMALL] + [gw["conv_w"].reshape(-1)])
    small_sum = _allreduce_small(_pad_rows(small), "reduce_small").reshape(-1)
    gs, off = {}, 0
    for k in _SMALL:
        n = int(np.prod(w[k].shape))
        gs[k] = small_sum[off:off + n].reshape(w[k].shape)
        off += n
    gcw = small_sum[off:off + depth * CONV_KERNEL * conv_c].reshape(depth, CONV_KERNEL, N_CHIPS, conv_cs)
    gs["conv_w"] = lax.dynamic_index_in_dim(gcw, my_chip, axis=2, keepdims=False)

    g_flat = jnp.where(my_core == 0, jnp.concatenate([mine, sibling_half]), jnp.concatenate([sibling_half, mine]))
    grads = _unflatten_shard(g_flat, w)
    delta, new_m, new_v = {}, {}, {}
    for k in _BIG:
        as2d = lambda t: t.reshape(-1, t.shape[-1])
        d_k, m_k, v_k = _adamw(as2d(w[k]), as2d(grads[k]), as2d(m[k]), as2d(v[k]), "adamw_" + k)
        delta[k], new_m[k], new_v[k] = d_k.reshape(w[k].shape), m_k.reshape(w[k].shape), v_k.reshape(w[k].shape)
    small_keys = _SMALL + ("conv_w",)
    join = lambda t: _pad_rows(jnp.concatenate([t[k].reshape(-1) for k in small_keys]))
    d_s, m_s, v_s = _adamw(join(w), join(gs), join(m), join(v), "adamw_small")
    off = 0
    for k in small_keys:
        n = int(np.prod(w[k].shape))
        grads[k] = gs[k]
        for dst, src in ((delta, d_s), (new_m, m_s), (new_v, v_s)):
            dst[k] = src.reshape(-1)[off:off + n].reshape(w[k].shape)
        off += n
    return (loss, grad_x[None], *[grads[k] for k in order], *[delta[k] for k in order],
            *[new_m[k] for k in order], *[new_v[k] for k in order])
```

```python
import functools
from typing import NamedTuple

import jax
import jax.numpy as jnp
import numpy as np
from jax import lax
from jax.experimental import pallas as pl
from jax.experimental.pallas import tpu as pltpu

F32 = jnp.float32
BF16 = jnp.bfloat16
EPS = 1e-6
GRID_W = 64
ROPE_THETA = 10000.0
CONV_KERNEL = 31
CONV_PAD = CONV_KERNEL // 2
HALO = 16
LANES = 128
SUBLANES = 8
ADAM_LR, ADAM_B1, ADAM_B2, ADAM_EPS, ADAM_WD, ADAM_STEP = 0.001, 0.9, 0.999, 1e-08, 0.01, 10
VMEM_LIMIT = 56 * 1024 * 1024
ADAM_BLOCK_BYTES = 1024 * 1024
LOG2E = 1.4426950408889634
LN2 = 0.6931471805599453
MESH = pl.DeviceIdType.MESH
N_CHIPS = 4
N_DEV = 8


class Dims(NamedTuple):
    N: int
    CTX: int
    R: int
    D: int
    C: int
    A: int
    KVW: int
    HD: int
    NH: int
    NKV: int
    G: int
    W: int
    TR: int


def _pick(n, prefs):
    for p in prefs:
        if n % p == 0:
            return p
    raise ValueError(f"no tile for {n} in {prefs}")


def _params(sem):
    return pltpu.CompilerParams(dimension_semantics=sem, vmem_limit_bytes=VMEM_LIMIT)


def _rows(tr, w, col=0):
    return pl.BlockSpec((tr, w), lambda i, col=col: (i, col))


def _whole(shape):
    return pl.BlockSpec(shape, lambda *_: (0,) * len(shape))


def _col(off, w):
    assert off % w == 0, (off, w)
    return off // w


_DN = {"nn": (((1,), (0,)), ((), ())), "nt": (((1,), (1,)), ((), ())), "tn": (((0,), (0,)), ((), ()))}


def _matmul(a, b, mode, name):
    if mode == "nn":
        (m, k), n = a.shape, b.shape[1]
    elif mode == "nt":
        (m, k), n = a.shape, b.shape[0]
    else:
        (k, m), n = a.shape, b.shape[1]
    tm = _pick(m, (1408, 1024, 768, 512, 256, 128, 8))
    tn = _pick(n, (1280, 1024, 768, 512, 256, 128))
    tk = _pick(k, (1536, 1408, 1280, 1024, 768, 512, 256, 128))
    nk = k // tk
    dn = _DN[mode]

    def body(a_ref, b_ref, o_ref):
        part = lax.dot_general(a_ref[...], b_ref[...], dn, preferred_element_type=F32)
        if nk == 1:
            o_ref[...] = part
            return
        kk = pl.program_id(2)

        @pl.when(kk == 0)
        def _():
            o_ref[...] = part

        @pl.when(kk > 0)
        def _():
            o_ref[...] += part

    a_spec = pl.BlockSpec((tk, tm), lambda i, j, kk: (kk, i)) if mode == "tn" else pl.BlockSpec((tm, tk), lambda i, j, kk: (i, kk))
    b_spec = pl.BlockSpec((tn, tk), lambda i, j, kk: (j, kk)) if mode == "nt" else pl.BlockSpec((tk, tn), lambda i, j, kk: (kk, j))
    return pl.pallas_call(
        body, name=name, grid=(m // tm, n // tn, nk),
        in_specs=[a_spec, b_spec], out_specs=pl.BlockSpec((tm, tn), lambda i, j, kk: (i, j)),
        out_shape=jax.ShapeDtypeStruct((m, n), F32),
        compiler_params=_params(("parallel", "parallel", "arbitrary")),
    )(a, b)


def _rms(x, g):
    return x * lax.rsqrt(jnp.mean(x * x, axis=-1, keepdims=True) + EPS) * g


def _prenorm_fn(x, g, sc, sh):
    return _rms(x, g) * (1.0 + sc) + sh


def _post_fn(outp, g, gt):
    return gt * _rms(outp, g)


def _convact_fn(y1, gate, ln_g, ln_b):
    mu = jnp.mean(y1, axis=-1, keepdims=True)
    var = jnp.mean(jnp.square(y1 - mu), axis=-1, keepdims=True)
    y = (y1 - mu) * lax.rsqrt(var + EPS) * ln_g + ln_b
    return jax.nn.silu(y) * jax.nn.silu(gate)


def _dsilu(x):
    s = jax.nn.sigmoid(x)
    return s * (1.0 + x * (1.0 - s))


def _seg_select(is_ctx, mod_ref, lo, w):
    return jnp.where(is_ctx, mod_ref[1:2, lo:lo + w], mod_ref[0:1, lo:lo + w])


def _seg_rows(is_ctx, v):
    row = lax.broadcasted_iota(jnp.int32, (8, v.shape[1]), 0)
    return jnp.where(row == is_ctx.astype(jnp.int32), v, 0.0)


def _mod_fwd(cv, w_mod_l, b_mod_l):
    def body(cv_ref, w_ref, b_ref, o_ref):
        s = jax.nn.silu(cv_ref[...]).astype(BF16)
        o_ref[...] = jnp.dot(s, w_ref[...], preferred_element_type=F32) + b_ref[...]

    return pl.pallas_call(
        body, name="mod_fwd", out_shape=jax.ShapeDtypeStruct((8, w_mod_l.shape[1]), F32),
        compiler_params=pltpu.CompilerParams(vmem_limit_bytes=VMEM_LIMIT),
    )(cv, w_mod_l, b_mod_l)


def _mod_bwd(cv, w_mod_l, dmod):
    d = cv.shape[1]

    def body(cv_ref, w_ref, dm_ref, dw_ref, db_ref, dcv_ref):
        cvv = cv_ref[...]
        s = jax.nn.silu(cvv).astype(BF16)
        dm = dm_ref[...]
        dmb = dm.astype(BF16)
        dw_ref[...] = lax.dot_general(s, dmb, _DN["tn"], preferred_element_type=F32)
        db_ref[...] = jnp.sum(dm, axis=0, keepdims=True)
        ds = lax.dot_general(dmb, w_ref[...], _DN["nt"], preferred_element_type=F32)
        dcv_ref[...] = ds * _dsilu(cvv)

    return pl.pallas_call(
        body, name="mod_bwd",
        out_shape=(jax.ShapeDtypeStruct(w_mod_l.shape, F32), jax.ShapeDtypeStruct((1, w_mod_l.shape[1]), F32),
                   jax.ShapeDtypeStruct((8, d), F32)),
        compiler_params=pltpu.CompilerParams(vmem_limit_bytes=VMEM_LIMIT),
    )(cv, w_mod_l, dmod)


def _prenorm(dm, xc, g_pre_l, mod):
    tr, d, n = dm.TR, dm.D, dm.N

    def body(x_ref, g_ref, mod_ref, h_ref):
        is_ctx = pl.program_id(0) * tr >= n
        sh = _seg_select(is_ctx, mod_ref, 0, d)
        sc = _seg_select(is_ctx, mod_ref, d, d)
        h_ref[...] = _prenorm_fn(x_ref[...], g_ref[...], sc, sh).astype(BF16)

    return pl.pallas_call(
        body, name="prenorm", grid=(dm.R // tr,),
        in_specs=[_rows(tr, d), _whole((1, d)), _whole((8, 3 * d))], out_specs=_rows(tr, d),
        out_shape=jax.ShapeDtypeStruct((dm.R, d), BF16), compiler_params=_params(("parallel",)),
    )(xc, g_pre_l, mod)


def _prenorm_bwd(dm, xc, g_pre_l, mod, dh, dxn):
    tr, d, n = dm.TR, dm.D, dm.N

    def body(x_ref, g_ref, mod_ref, dh_ref, dxn_ref, dx_ref, dg_ref, dss_ref):
        i = pl.program_id(0)
        is_ctx = i * tr >= n
        sh = _seg_select(is_ctx, mod_ref, 0, d)
        sc = _seg_select(is_ctx, mod_ref, d, d)
        _, vjp = jax.vjp(_prenorm_fn, x_ref[...], g_ref[...], sc, sh)
        dx, dg, dsc, dsh = vjp(dh_ref[...])
        dx_ref[...] = dxn_ref[...] + dx

        @pl.when(i == 0)
        def _():
            dg_ref[...] = jnp.zeros_like(dg_ref)
            dss_ref[...] = jnp.zeros_like(dss_ref)

        dg_ref[...] += dg
        dss_ref[...] += _seg_rows(is_ctx, jnp.concatenate([dsh, dsc], axis=1))

    return pl.pallas_call(
        body, name="prenorm_bwd", grid=(dm.R // tr,),
        in_specs=[_rows(tr, d), _whole((1, d)), _whole((8, 3 * d)), _rows(tr, d), _rows(tr, d)],
        out_specs=(_rows(tr, d), _whole((1, d)), _whole((8, 2 * d))),
        out_shape=(jax.ShapeDtypeStruct((dm.R, d), F32), jax.ShapeDtypeStruct((1, d), F32),
                   jax.ShapeDtypeStruct((8, 2 * d), F32)),
        compiler_params=_params(("arbitrary",)),
    )(xc, g_pre_l, mod, dh, dxn)


def _post(dm, xc, outp, g_post_l, mod):
    tr, d, n = dm.TR, dm.D, dm.N

    def body(x_ref, o_ref, g_ref, mod_ref, y_ref):
        is_ctx = pl.program_id(0) * tr >= n
        gt = _seg_select(is_ctx, mod_ref, 2 * d, d)
        y_ref[...] = x_ref[...] + _post_fn(o_ref[...], g_ref[...], gt)

    return pl.pallas_call(
        body, name="post", grid=(dm.R // tr,),
        in_specs=[_rows(tr, d), _rows(tr, d), _whole((1, d)), _whole((8, 3 * d))], out_specs=_rows(tr, d),
        out_shape=jax.ShapeDtypeStruct((dm.R, d), F32), compiler_params=_params(("parallel",)),
    )(xc, outp, g_post_l, mod)


def _post_bwd(dm, outp, g_post_l, mod, dxn):
    tr, d, n = dm.TR, dm.D, dm.N

    def body(o_ref, g_ref, mod_ref, dxn_ref, do_ref, dg_ref, dgt_ref):
        i = pl.program_id(0)
        is_ctx = i * tr >= n
        gt = _seg_select(is_ctx, mod_ref, 2 * d, d)
        _, vjp = jax.vjp(_post_fn, o_ref[...], g_ref[...], gt)
        do, dg, dgt = vjp(dxn_ref[...])
        do_ref[...] = do.astype(BF16)

        @pl.when(i == 0)
        def _():
            dg_ref[...] = jnp.zeros_like(dg_ref)
            dgt_ref[...] = jnp.zeros_like(dgt_ref)

        dg_ref[...] += dg
        dgt_ref[...] += _seg_rows(is_ctx, dgt)

    return pl.pallas_call(
        body, name="post_bwd", grid=(dm.R // tr,),
        in_specs=[_rows(tr, d), _whole((1, d)), _whole((8, 3 * d)), _rows(tr, d)],
        out_specs=(_rows(tr, d), _whole((1, d)), _whole((8, d))),
        out_shape=(jax.ShapeDtypeStruct((dm.R, d), BF16), jax.ShapeDtypeStruct((1, d), F32),
                   jax.ShapeDtypeStruct((8, d), F32)),
        compiler_params=_params(("arbitrary",)),
    )(outp, g_post_l, mod, dxn)


def _group_sum(x, bd):
    hi = x.astype(BF16)
    lo = (x - hi.astype(F32)).astype(BF16)
    return jnp.dot(hi, bd, preferred_element_type=F32) + jnp.dot(lo, bd, preferred_element_type=F32)


def _swap_halves(x):
    lane = lax.broadcasted_iota(jnp.int32, x.shape, 1)
    return jnp.where((lane & 32) == 0, pltpu.roll(x, LANES - 32, 1), pltpu.roll(x, 32, 1))


def _qk_prep(dm, p, cos, sins, gq, gk, bd):
    tr, hd = dm.TR, dm.HD
    oq, ok, ov = 3 * dm.C, 3 * dm.C + dm.A, 3 * dm.C + dm.A + dm.KVW
    scale = hd ** -0.5 * LOG2E

    def body(q_ref, k_ref, v_ref, cos_ref, sin_ref, gq_ref, gk_ref, bd_ref, qh_ref, qt_ref, kh_ref, vh_ref, vx_ref):
        cosv, sinv, bdv = cos_ref[...], sin_ref[...], bd_ref[...]

        def chunk(x, g):
            rstd = lax.rsqrt(_group_sum(x * x, bdv) * (1.0 / hd) + EPS)
            xn = x * rstd * g
            return xn * cosv + _swap_halves(xn) * sinv

        for c in range(dm.A // LANES):
            y = chunk(q_ref[:, c * LANES:(c + 1) * LANES], gq_ref[...]) * scale
            qh_ref[2 * c] = y[:, :hd].astype(BF16)
            qh_ref[2 * c + 1] = y[:, hd:].astype(BF16)
            qt_ref[c * LANES:(c + 1) * LANES, :] = y.T.astype(BF16)
        for c in range(dm.KVW // LANES):
            y = chunk(k_ref[:, c * LANES:(c + 1) * LANES], gk_ref[...])
            kh_ref[2 * c] = y[:, :hd].astype(BF16)
            kh_ref[2 * c + 1] = y[:, hd:].astype(BF16)
            v = v_ref[:, c * LANES:(c + 1) * LANES]
            vh_ref[2 * c] = v[:, :hd].astype(BF16)
            vh_ref[2 * c + 1] = v[:, hd:].astype(BF16)
            lane = lax.broadcasted_iota(jnp.int32, v.shape, 1)
            one_at_hd = (lane == hd).astype(F32)
            vx_ref[2 * c] = jnp.where(lane < hd, v, one_at_hd).astype(BF16)
            vx_ref[2 * c + 1] = jnp.where(lane < hd, pltpu.roll(v, hd, 1), one_at_hd).astype(BF16)

    hspec = lambda nh, w=hd: pl.BlockSpec((nh, tr, w), lambda i: (0, i, 0))
    return pl.pallas_call(
        body, name="qk_prep", grid=(dm.R // tr,),
        in_specs=[_rows(tr, dm.A, _col(oq, dm.A)), _rows(tr, dm.KVW, _col(ok, dm.KVW)), _rows(tr, dm.KVW, _col(ov, dm.KVW)),
                  _rows(tr, LANES), _rows(tr, LANES), _whole((1, LANES)), _whole((1, LANES)), _whole((LANES, LANES))],
        out_specs=(hspec(dm.NH), pl.BlockSpec((dm.A, tr), lambda i: (0, i)), hspec(dm.NKV), hspec(dm.NKV), hspec(dm.NKV, LANES)),
        out_shape=(jax.ShapeDtypeStruct((dm.NH, dm.R, hd), BF16), jax.ShapeDtypeStruct((dm.A, dm.R), BF16),
                   jax.ShapeDtypeStruct((dm.NKV, dm.R, hd), BF16), jax.ShapeDtypeStruct((dm.NKV, dm.R, hd), BF16),
                   jax.ShapeDtypeStruct((dm.NKV, dm.R, LANES), BF16)),
        compiler_params=_params(("parallel",)),
    )(p, p, p, cos, sins, gq, gk, bd)


def _qk_bwd(dm, p, dq, dkt, dvt, cos, sins, gq, gk, bd):
    tr, hd = dm.TR, dm.HD
    oq, ok = 3 * dm.C, 3 * dm.C + dm.A
    scale = hd ** -0.5

    def body(q_ref, k_ref, dq_ref, dkh_ref, dvh_ref, cos_ref, sin_ref, gq_ref, gk_ref, bd_ref,
             dpq_ref, dpkv_ref, dgq_ref, dgk_ref):
        i = pl.program_id(0)
        cosv, sinv, bdv = cos_ref[...], sin_ref[...], bd_ref[...]

        def chunk(x, g, dy):
            rstd = lax.rsqrt(_group_sum(x * x, bdv) * (1.0 / hd) + EPS)
            xhat = x * rstd
            dxn = dy * cosv + _swap_halves(dy * sinv)
            dxhat = dxn * g
            dx = rstd * (dxhat - xhat * (_group_sum(dxhat * xhat, bdv) * (1.0 / hd)))
            return dx, jnp.sum(dxn * xhat, axis=0, keepdims=True)

        @pl.when(i == 0)
        def _():
            dgq_ref[...] = jnp.zeros_like(dgq_ref)
            dgk_ref[...] = jnp.zeros_like(dgk_ref)

        dgq = jnp.zeros((1, LANES), F32)
        for c in range(dm.A // LANES):
            sl = slice(c * LANES, (c + 1) * LANES)
            dx, dg = chunk(q_ref[:, sl], gq_ref[...], dq_ref[:, sl] * scale)
            dpq_ref[:, sl] = dx.astype(BF16)
            dgq = dgq + dg
        dgq_ref[...] += dgq + pltpu.roll(dgq, hd, 1)
        dgk = jnp.zeros((1, LANES), F32)
        for c in range(dm.KVW // LANES):
            sl = slice(c * LANES, (c + 1) * LANES)
            dx, dg = chunk(k_ref[:, sl], gk_ref[...], dkh_ref[sl, :].T)
            dpkv_ref[:, sl] = dx.astype(BF16)
            dgk = dgk + dg
            dpkv_ref[:, dm.KVW + c * LANES:dm.KVW + (c + 1) * LANES] = dvh_ref[sl, :].T.astype(BF16)
        dgk_ref[...] += dgk + pltpu.roll(dgk, hd, 1)

    hspec = pl.BlockSpec((dm.KVW, tr), lambda i: (0, i))
    return pl.pallas_call(
        body, name="qk_bwd", grid=(dm.R // tr,),
        in_specs=[_rows(tr, dm.A, _col(oq, dm.A)), _rows(tr, dm.KVW, _col(ok, dm.KVW)), _rows(tr, dm.A), hspec, hspec,
                  _rows(tr, LANES), _rows(tr, LANES), _whole((1, LANES)), _whole((1, LANES)), _whole((LANES, LANES))],
        out_specs=(_rows(tr, dm.A), _rows(tr, 2 * dm.KVW), _whole((1, LANES)), _whole((1, LANES))),
        out_shape=(jax.ShapeDtypeStruct((dm.R, dm.A), BF16), jax.ShapeDtypeStruct((dm.R, 2 * dm.KVW), BF16),
                   jax.ShapeDtypeStruct((1, LANES), F32), jax.ShapeDtypeStruct((1, LANES), F32)),
        compiler_params=_params(("arbitrary",)),
    )(p, p, dq, dkt, dvt, cos, sins, gq, gk, bd)


def _flash_fwd(dm, qh, kh, vx):
    tq, ks = dm.TR, dm.TR
    rs = min(tq, LANES)
    g, hd = dm.G, dm.HD
    nq, nq_lat = dm.R // tq, dm.N // tq
    n_chunks, ctx_chunks = dm.R // ks, dm.CTX // ks

    def body(q_ref, k_ref, v_ref, o_ref, lse_ref):
        qi = pl.program_id(1)

        def attend(chunks):
            for gg in range(g):
                for r in range(tq // rs):
                    rows = slice(r * rs, (r + 1) * rs)
                    q = q_ref[gg, rows, :]
                    m_prev = jnp.full((rs, LANES), -1e30, F32)
                    acc = jnp.zeros((rs, LANES), F32)
                    for kc in chunks:
                        keys = slice(kc * ks, (kc + 1) * ks)
                        s = lax.dot_general(q, k_ref[0, keys, :], _DN["nt"], preferred_element_type=F32)
                        m_new = jnp.maximum(m_prev, jnp.max(s, axis=-1, keepdims=True))
                        p = jnp.exp2(s - jnp.tile(m_new, (1, ks // LANES)))
                        acc = acc * jnp.exp2(m_prev - m_new) + jnp.dot(p.astype(BF16), v_ref[0, keys, :],
                                                                        preferred_element_type=F32)
                        m_prev = m_new
                    l = acc[:, hd:hd + 1]
                    o_ref[rows, gg * hd:(gg + 1) * hd] = acc[:, :hd] / l
                    lse_ref[gg, rows, :] = m_prev[:, :1] + jnp.log2(l)

        @pl.when(qi < nq_lat)
        def _():
            attend(range(n_chunks))

        @pl.when(qi >= nq_lat)
        def _():
            attend(range(n_chunks - ctx_chunks, n_chunks))

    return pl.pallas_call(
        body, name="flash_fwd", grid=(dm.NKV, nq),
        in_specs=[pl.BlockSpec((g, tq, hd), lambda h, qi: (h, qi, 0)), pl.BlockSpec((1, dm.R, hd), lambda h, qi: (h, 0, 0)),
                  pl.BlockSpec((1, dm.R, LANES), lambda h, qi: (h, 0, 0))],
        out_specs=(pl.BlockSpec((tq, g * hd), lambda h, qi: (qi, h)), pl.BlockSpec((g, tq, 1), lambda h, qi: (h, qi, 0))),
        out_shape=(jax.ShapeDtypeStruct((dm.R, dm.A), F32), jax.ShapeDtypeStruct((dm.NH, dm.R, 1), F32)),
        compiler_params=_params(("parallel", "parallel")),
    )(qh, kh, vx)


def _flash_bwd(dm, qh, qt, kh, vh, doh, dot, lse, delta):
    tq, ks = dm.TR, dm.TR
    g, hd = dm.G, dm.HD
    nq, nq_lat = dm.R // tq, dm.N // tq
    n_chunks, ctx_chunks = dm.R // ks, dm.CTX // ks

    def body(q_ref, qt_ref, k_ref, v_ref, do_ref, dot_ref, lse_ref, dl_ref, dq_ref, dk_acc, dv_acc):
        qi = pl.program_id(1)

        @pl.when(qi == 0)
        def _():
            dk_acc[...] = jnp.zeros_like(dk_acc)
            dv_acc[...] = jnp.zeros_like(dv_acc)

        def attend(chunks):
            q = q_ref[...].reshape(g * tq, hd)
            do = do_ref[...].reshape(g * tq, hd)
            q_t = jnp.concatenate([qt_ref[gg * hd:(gg + 1) * hd, :] for gg in range(g)], axis=1)
            do_t = jnp.concatenate([dot_ref[gg * hd:(gg + 1) * hd, :] for gg in range(g)], axis=1)
            lse_c = lse_ref[...].reshape(g * tq, 1)
            dl_c = dl_ref[...].reshape(g * tq, 1)
            dqs = jnp.zeros((g * tq, hd), F32)
            for kc in chunks:
                keys = slice(kc * ks, (kc + 1) * ks)
                k, v = k_ref[0, keys, :], v_ref[0, keys, :]
                s = lax.dot_general(q, k, _DN["nt"], preferred_element_type=F32)
                p = jnp.exp2(s - lse_c)
                dp = lax.dot_general(do, v, _DN["nt"], preferred_element_type=F32)
                ds = (p * (dp - dl_c)).astype(BF16)
                dv_acc[:, keys] += jnp.dot(do_t, p.astype(BF16), preferred_element_type=F32)
                dk_acc[:, keys] += jnp.dot(q_t, ds, preferred_element_type=F32)
                dqs = dqs + jnp.dot(ds, k, preferred_element_type=F32)
            for gg in range(g):
                dq_ref[:, gg * hd:(gg + 1) * hd] = dqs[gg * tq:(gg + 1) * tq]

        @pl.when(qi < nq_lat)
        def _():
            attend(range(n_chunks))

        @pl.when(qi >= nq_lat)
        def _():
            attend(range(n_chunks - ctx_chunks, n_chunks))

        @pl.when(qi == nq - 1)
        def _():
            dk_acc[...] = dk_acc[...] * LN2

    qspec = pl.BlockSpec((g, tq, hd), lambda h, qi: (h, qi, 0))
    tspec = pl.BlockSpec((g * hd, tq), lambda h, qi: (h, qi))
    kspec = pl.BlockSpec((1, dm.R, hd), lambda h, qi: (h, 0, 0))
    sspec = pl.BlockSpec((g, tq, 1), lambda h, qi: (h, qi, 0))
    ospec = pl.BlockSpec((hd, dm.R), lambda h, qi: (h, 0))
    return pl.pallas_call(
        body, name="flash_bwd", grid=(dm.NKV, nq),
        in_specs=[qspec, tspec, kspec, kspec, qspec, tspec, sspec, sspec],
        out_specs=(pl.BlockSpec((tq, g * hd), lambda h, qi: (qi, h)), ospec, ospec),
        out_shape=(jax.ShapeDtypeStruct((dm.R, dm.A), F32), jax.ShapeDtypeStruct((dm.KVW, dm.R), F32),
                   jax.ShapeDtypeStruct((dm.KVW, dm.R), F32)),
        compiler_params=_params(("parallel", "arbitrary")),
    )(qh, qt, kh, vh, doh, dot, lse, delta)


def _gate_b_specs(dm):
    ogb = 3 * dm.C + dm.A + 2 * dm.KVW
    half = dm.A // 2
    return [_rows(dm.TR, half, _col(ogb, half)), _rows(dm.TR, half, _col(ogb, half) + 1)]


def _attn_gate(dm, o, p):
    tr = dm.TR

    def body(o_ref, g0_ref, g1_ref, za_ref):
        gate = jnp.concatenate([g0_ref[...], g1_ref[...]], axis=1)
        za_ref[...] = (o_ref[...] * jax.nn.silu(gate)).astype(BF16)

    return pl.pallas_call(
        body, name="attn_gate", grid=(dm.R // tr,),
        in_specs=[_rows(tr, dm.A)] + _gate_b_specs(dm), out_specs=_rows(tr, dm.A),
        out_shape=jax.ShapeDtypeStruct((dm.R, dm.A), BF16), compiler_params=_params(("parallel",)),
    )(o, p, p)


def _attn_gate_bwd(dm, dza, o, p):
    tr, hd = dm.TR, dm.HD

    def body(dza_ref, o_ref, g0_ref, g1_ref, doh_ref, dot_ref, dl_ref, dgb_ref):
        gate = jnp.concatenate([g0_ref[...], g1_ref[...]], axis=1)
        dza, o = dza_ref[...], o_ref[...]
        do = dza * jax.nn.silu(gate)
        dgb_ref[...] = (dza * o * _dsilu(gate)).astype(BF16)
        dot_ref[...] = do.T.astype(BF16)
        prod = do * o
        for h in range(dm.NH):
            sl = slice(h * hd, (h + 1) * hd)
            doh_ref[h] = do[:, sl].astype(BF16)
            dl_ref[h] = jnp.sum(prod[:, sl], axis=-1, keepdims=True)

    return pl.pallas_call(
        body, name="attn_gate_bwd", grid=(dm.R // tr,),
        in_specs=[_rows(tr, dm.A), _rows(tr, dm.A)] + _gate_b_specs(dm),
        out_specs=(pl.BlockSpec((dm.NH, tr, hd), lambda i: (0, i, 0)), pl.BlockSpec((dm.A, tr), lambda i: (0, i)),
                   pl.BlockSpec((dm.NH, tr, 1), lambda i: (0, i, 0)), _rows(tr, dm.A)),
        out_shape=(jax.ShapeDtypeStruct((dm.NH, dm.R, hd), BF16), jax.ShapeDtypeStruct((dm.A, dm.R), BF16),
                   jax.ShapeDtypeStruct((dm.NH, dm.R, 1), F32), jax.ShapeDtypeStruct((dm.R, dm.A), BF16)),
        compiler_params=_params(("parallel",)),
    )(dza, o, p, p)


def _halo_specs(dm, col):
    per = dm.TR // HALO
    last = dm.R // HALO - 1
    prev = pl.BlockSpec((HALO, dm.C), lambda i: (jnp.maximum(i * per - 1, 0), col))
    nxt = pl.BlockSpec((HALO, dm.C), lambda i: (jnp.minimum((i + 1) * per, last), col))
    return prev, nxt


def _seg_valid(dm, i):
    tr = dm.TR
    r = i * tr - HALO + lax.broadcasted_iota(jnp.int32, (tr + 2 * HALO, 1), 0)
    is_ctx = i * tr >= dm.N
    lo = jnp.where(is_ctx, dm.N, 0)
    hi = jnp.where(is_ctx, dm.R, dm.N)
    return jnp.logical_and(r >= lo, r < hi)


CONV_RB = 64


def _shifted_copies(src_ref, sl, sh_ref, tr):
    rows = tr + 2 * HALO - SUBLANES
    padded = src_ref[:, sl]
    for ph in range(SUBLANES):
        sh_ref[ph, 0:rows, :] = padded[ph:ph + rows]


def _tap_rows(sh_ref, off, n):
    ph = off % SUBLANES
    return sh_ref[ph, off - ph:off - ph + n, :]


def _conv_fwd(dm, p, conv_w_l, conv_b_l, ln_g_l, ln_b_l):
    tr, c = dm.TR, dm.C

    def body(a_ref, g_ref, gate_ref, ap_ref, an_ref, gp_ref, gn_ref, w_ref, b_ref, lg_ref, lb_ref,
             zc_ref, y1_ref, u_ref, sh_ref):
        i = pl.program_id(0)
        a = jnp.concatenate([ap_ref[...], a_ref[...], an_ref[...]], axis=0)
        g = jnp.concatenate([gp_ref[...], g_ref[...], gn_ref[...]], axis=0)
        u_ref[...] = jnp.where(_seg_valid(dm, i), a * jax.nn.sigmoid(g), 0.0)
        for cc in range(c // LANES):
            sl = slice(cc * LANES, (cc + 1) * LANES)
            _shifted_copies(u_ref, sl, sh_ref, tr)
            for r0 in range(0, tr, CONV_RB):
                acc = jnp.zeros((CONV_RB, LANES), F32)
                for j in range(CONV_KERNEL):
                    acc = acc + w_ref[j:j + 1, sl] * _tap_rows(sh_ref, r0 + HALO - CONV_PAD + j, CONV_RB)
                y1_ref[r0:r0 + CONV_RB, sl] = acc + b_ref[:, sl]
        zc_ref[...] = _convact_fn(y1_ref[...], gate_ref[...], lg_ref[...], lb_ref[...]).astype(BF16)

    ap, an = _halo_specs(dm, 0)
    gp, gn = _halo_specs(dm, 1)
    return pl.pallas_call(
        body, name="conv_fwd", grid=(dm.R // tr,),
        in_specs=[_rows(tr, c, 0), _rows(tr, c, 1), _rows(tr, c, 2), ap, an, gp, gn,
                  _whole((32, c)), _whole((1, c)), _whole((1, c)), _whole((1, c))],
        out_specs=(_rows(tr, c), _rows(tr, c)),
        out_shape=(jax.ShapeDtypeStruct((dm.R, c), BF16), jax.ShapeDtypeStruct((dm.R, c), F32)),
        scratch_shapes=[pltpu.VMEM((tr + 2 * HALO, c), F32), pltpu.VMEM((SUBLANES, tr + 2 * HALO, LANES), F32)],
        compiler_params=_params(("parallel",)),
    )(p, p, p, p, p, p, p, conv_w_l, conv_b_l, ln_g_l, ln_b_l)


def _conv_act_bwd(dm, dzc, y1, p, ln_g_l, ln_b_l):
    tr, c = dm.TR, dm.C

    def body(dz_ref, y1_ref, gate_ref, lg_ref, lb_ref, dy_ref, dgate_ref, dlg_ref, dlb_ref):
        i = pl.program_id(0)
        _, vjp = jax.vjp(_convact_fn, y1_ref[...], gate_ref[...], lg_ref[...], lb_ref[...])
        dy, dgate, dlg, dlb = vjp(dz_ref[...])
        dy_ref[...] = dy
        dgate_ref[...] = dgate.astype(BF16)

        @pl.when(i == 0)
        def _():
            dlg_ref[...] = jnp.zeros_like(dlg_ref)
            dlb_ref[...] = jnp.zeros_like(dlb_ref)

        dlg_ref[...] += dlg
        dlb_ref[...] += dlb

    return pl.pallas_call(
        body, name="conv_act_bwd", grid=(dm.R // tr,),
        in_specs=[_rows(tr, c), _rows(tr, c), _rows(tr, c, 2), _whole((1, c)), _whole((1, c))],
        out_specs=(_rows(tr, c), _rows(tr, c), _whole((1, c)), _whole((1, c))),
        out_shape=(jax.ShapeDtypeStruct((dm.R, c), F32), jax.ShapeDtypeStruct((dm.R, c), BF16),
                   jax.ShapeDtypeStruct((1, c), F32), jax.ShapeDtypeStruct((1, c), F32)),
        compiler_params=_params(("arbitrary",)),
    )(dzc, y1, p, ln_g_l, ln_b_l)


def _conv_bwd(dm, dy1, p, conv_w_l):
    tr, c = dm.TR, dm.C

    def body(dy_ref, dyp_ref, dyn_ref, a_ref, g_ref, ap_ref, an_ref, gp_ref, gn_ref, w_ref,
             dag_ref, dw_ref, db_ref, u_ref, dyp_scr, du_ref, dwp_ref, shu_ref, shd_ref):
        i = pl.program_id(0)
        valid = _seg_valid(dm, i)
        a = jnp.concatenate([ap_ref[...], a_ref[...], an_ref[...]], axis=0)
        g = jnp.concatenate([gp_ref[...], g_ref[...], gn_ref[...]], axis=0)
        u_ref[...] = jnp.where(valid, a * jax.nn.sigmoid(g), 0.0)
        dyp_scr[...] = jnp.where(valid, jnp.concatenate([dyp_ref[...], dy_ref[...], dyn_ref[...]], axis=0), 0.0)

        @pl.when(i == 0)
        def _():
            dwp_ref[...] = jnp.zeros_like(dwp_ref)
            db_ref[...] = jnp.zeros_like(db_ref)

        db_ref[...] += jnp.sum(dy_ref[...], axis=0, keepdims=True)
        for cc in range(c // LANES):
            sl = slice(cc * LANES, (cc + 1) * LANES)
            _shifted_copies(u_ref, sl, shu_ref, tr)
            _shifted_copies(dyp_scr, sl, shd_ref, tr)
            for r0 in range(0, tr, CONV_RB):
                dy_cur = dy_ref[r0:r0 + CONV_RB, sl]
                acc = jnp.zeros((CONV_RB, LANES), F32)
                for j in range(CONV_KERNEL):
                    acc = acc + w_ref[j:j + 1, sl] * _tap_rows(shd_ref, r0 + HALO + CONV_PAD - j, CONV_RB)
                    prod = dy_cur * _tap_rows(shu_ref, r0 + HALO - CONV_PAD + j, CONV_RB)
                    dwp_ref[j, :, sl] += jnp.sum(prod.reshape(CONV_RB // SUBLANES, SUBLANES, LANES), axis=0)
                du_ref[r0:r0 + CONV_RB, sl] = acc
        du = du_ref[...]
        sg = jax.nn.sigmoid(g_ref[...])
        dag_ref[:, :c] = (du * sg).astype(BF16)
        dag_ref[:, c:] = (du * a_ref[...] * sg * (1.0 - sg)).astype(BF16)

        @pl.when(i == dm.R // tr - 1)
        def _():
            dw_ref[...] = jnp.sum(dwp_ref[...], axis=1)

    ap, an = _halo_specs(dm, 0)
    gp, gn = _halo_specs(dm, 1)
    dyp, dyn = _halo_specs(dm, 0)
    return pl.pallas_call(
        body, name="conv_bwd", grid=(dm.R // tr,),
        in_specs=[_rows(tr, c), dyp, dyn, _rows(tr, c, 0), _rows(tr, c, 1), ap, an, gp, gn, _whole((32, c))],
        out_specs=(_rows(tr, 2 * c), _whole((32, c)), _whole((1, c))),
        out_shape=(jax.ShapeDtypeStruct((dm.R, 2 * c), BF16), jax.ShapeDtypeStruct((32, c), F32),
                   jax.ShapeDtypeStruct((1, c), F32)),
        scratch_shapes=[pltpu.VMEM((tr + 2 * HALO, c), F32), pltpu.VMEM((tr + 2 * HALO, c), F32), pltpu.VMEM((tr, c), F32),
                        pltpu.VMEM((32, SUBLANES, c), F32), pltpu.VMEM((SUBLANES, tr + 2 * HALO, LANES), F32),
                        pltpu.VMEM((SUBLANES, tr + 2 * HALO, LANES), F32)],
        compiler_params=_params(("arbitrary",)),
    )(dy1, dy1, dy1, p, p, p, p, p, p, conv_w_l)


def _gm_specs(dm):
    half = dm.D // 2
    first = _col(3 * dm.C + 2 * dm.A + 2 * dm.KVW, half)
    return [_rows(dm.TR, half, first + k) for k in range(4)]


def _gm_load(a0, a1, b0, b1):
    return jnp.concatenate([a0[...], a1[...]], axis=1), jnp.concatenate([b0[...], b1[...]], axis=1)


def _merge(dm, yc, ya, p):
    tr, d = dm.TR, dm.D

    def body(yc_ref, ya_ref, a0, a1, b0, b1, m_ref):
        ga, gb = _gm_load(a0, a1, b0, b1)
        m_ref[...] = (jax.nn.sigmoid(ga) * yc_ref[...] + jax.nn.sigmoid(gb) * ya_ref[...]).astype(BF16)

    return pl.pallas_call(
        body, name="merge", grid=(dm.R // tr,),
        in_specs=[_rows(tr, d), _rows(tr, d)] + _gm_specs(dm), out_specs=_rows(tr, d),
        out_shape=jax.ShapeDtypeStruct((dm.R, d), BF16), compiler_params=_params(("parallel",)),
    )(yc, ya, p, p, p, p)


def _merge_bwd(dm, dmerged, yc, ya, p):
    tr, d = dm.TR, dm.D

    def body(dm_ref, yc_ref, ya_ref, a0, a1, b0, b1, dyc_ref, dya_ref, dgm_ref):
        dmv = dm_ref[...]
        ga, gb = _gm_load(a0, a1, b0, b1)
        sa, sb = jax.nn.sigmoid(ga), jax.nn.sigmoid(gb)
        dyc_ref[...] = (dmv * sa).astype(BF16)
        dya_ref[...] = (dmv * sb).astype(BF16)
        dgm_ref[:, :d] = (dmv * yc_ref[...] * sa * (1.0 - sa)).astype(BF16)
        dgm_ref[:, d:] = (dmv * ya_ref[...] * sb * (1.0 - sb)).astype(BF16)

    return pl.pallas_call(
        body, name="merge_bwd", grid=(dm.R // tr,),
        in_specs=[_rows(tr, d), _rows(tr, d), _rows(tr, d)] + _gm_specs(dm),
        out_specs=(_rows(tr, d), _rows(tr, d), _rows(tr, 2 * d)),
        out_shape=(jax.ShapeDtypeStruct((dm.R, d), BF16), jax.ShapeDtypeStruct((dm.R, d), BF16),
                   jax.ShapeDtypeStruct((dm.R, 2 * d), BF16)),
        compiler_params=_params(("parallel",)),
    )(dmerged, yc, ya, p, p, p, p)


def _loss_head(dm, xc, target):
    tr, d = dm.TR, dm.D
    n_lat = dm.N // tr

    def body(x_ref, t_ref, dy_ref, l_ref):
        i = pl.program_id(0)

        @pl.when(i == 0)
        def _():
            l_ref[...] = jnp.zeros_like(l_ref)

        @pl.when(i < n_lat)
        def _():
            e = x_ref[...] - t_ref[...]
            dy_ref[...] = e * (1.0 / d)
            l_ref[...] += 0.5 * jnp.sum(jnp.mean(e * e, axis=-1, keepdims=True), axis=0, keepdims=True)

        @pl.when(i >= n_lat)
        def _():
            dy_ref[...] = jnp.zeros_like(dy_ref)

    return pl.pallas_call(
        body, name="loss_head", grid=(dm.R // tr,),
        in_specs=[_rows(tr, d), pl.BlockSpec((tr, d), lambda i: (jnp.minimum(i, n_lat - 1), 0))],
        out_specs=(_rows(tr, d), _whole((1, 1))),
        out_shape=(jax.ShapeDtypeStruct((dm.R, d), F32), jax.ShapeDtypeStruct((1, 1), F32)),
        compiler_params=_params(("arbitrary",)),
    )(xc, target)


def _adamw(w, g, m, v, name):
    rows, cols = w.shape
    fits = lambda t: t * cols * 4 <= ADAM_BLOCK_BYTES
    tr = rows if fits(rows) else _pick(rows, [t for t in (512, 256, 128, 64, 32, 16, 8) if fits(t)])

    def body(w_ref, g_ref, m_ref, v_ref, d_ref, mo_ref, vo_ref):
        d_ref[...], mo_ref[...], vo_ref[...] = _adam_update(w_ref[...], g_ref[...], m_ref[...], v_ref[...])

    spec = _rows(tr, cols)
    shp = jax.ShapeDtypeStruct(w.shape, F32)
    return pl.pallas_call(
        body, name=name, grid=(rows // tr,), in_specs=[spec] * 4, out_specs=(spec,) * 3, out_shape=(shp,) * 3,
        compiler_params=_params(("parallel",)),
    )(w, g, m, v)


def _adam_update(w, g, m, v):
    mn = ADAM_B1 * m + (1.0 - ADAM_B1) * g
    vn = ADAM_B2 * v + (1.0 - ADAM_B2) * jnp.square(g)
    m_hat = mn / (1.0 - ADAM_B1 ** ADAM_STEP)
    v_hat = vn / (1.0 - ADAM_B2 ** ADAM_STEP)
    return -ADAM_LR * (m_hat / (jnp.sqrt(v_hat) + ADAM_EPS) + ADAM_WD * w), mn, vn


def _rope_tables(dm):
    n = dm.N
    rows = n // GRID_W
    row = jnp.repeat(jnp.arange(rows, dtype=F32), GRID_W)
    col = jnp.tile(jnp.arange(GRID_W, dtype=F32), rows)
    half = dm.HD // 2
    inv_freq = ROPE_THETA ** (-jnp.arange(0, half, 2, dtype=F32) / half)
    ang = jnp.concatenate([row[:, None] * inv_freq, col[:, None] * inv_freq], axis=-1)
    cos, sin = jnp.cos(ang), jnp.sin(ang)
    cos = jnp.concatenate([cos, jnp.ones((dm.CTX, half), F32)], axis=0)
    sin = jnp.concatenate([sin, jnp.zeros((dm.CTX, half), F32)], axis=0)
    reps = LANES // dm.HD
    return jnp.tile(jnp.concatenate([cos, cos], axis=1), (1, reps)), jnp.tile(jnp.concatenate([-sin, sin], axis=1), (1, reps))


def _lane_vec(g, hd):
    return jnp.tile(g.reshape(1, hd), (1, LANES // hd))


def _layer_fwd(dm, xc, cv, wl, tabs):
    mod = _mod_fwd(cv, wl["w_mod"], wl["b_mod"])
    h = _prenorm(dm, xc, wl["g_pre"], mod)
    p = _matmul(h, wl["w_in"], "nn", "in_proj")
    qh, qt, kh, vh, vx = _qk_prep(dm, p, tabs["cos"], tabs["sins"], wl["gq"], wl["gk"], tabs["bd"])
    o, lse = _flash_fwd(dm, qh, kh, vx)
    za = _attn_gate(dm, o, p)
    zc, y1 = _conv_fwd(dm, p, wl["conv_w"], wl["conv_b"], wl["ln_g"], wl["ln_b"])
    yc = _matmul(zc, wl["w_conv_out"], "nn", "conv_out")
    ya = _matmul(za, wl["w_attn_out"], "nn", "attn_out")
    merged = _merge(dm, yc, ya, p)
    outp = _matmul(merged, wl["w_out"], "nn", "out_proj")
    x_new = _post(dm, xc, outp, wl["g_post"], mod)
    saved = dict(xc=xc, mod=mod, h=h, p=p, qh=qh, qt=qt, kh=kh, vh=vh, o=o, lse=lse, za=za, zc=zc, y1=y1, yc=yc, ya=ya,
                 merged=merged, outp=outp)
    return x_new, saved


def _layer_bwd(dm, dxn, cv, wl, tabs, s):
    p = s["p"]
    d_outp, dg_post, dgt = _post_bwd(dm, s["outp"], wl["g_post"], s["mod"], dxn)
    dmerged = _matmul(d_outp, wl["w_out"], "nt", "out_proj_dx")
    dw_out = _matmul(s["merged"], d_outp, "tn", "out_proj_dw")
    dyc, dya, dgm = _merge_bwd(dm, dmerged, s["yc"], s["ya"], p)
    dzc = _matmul(dyc, wl["w_conv_out"], "nt", "conv_out_dx")
    dw_conv_out = _matmul(s["zc"], dyc, "tn", "conv_out_dw")
    dza = _matmul(dya, wl["w_attn_out"], "nt", "attn_out_dx")
    dw_attn_out = _matmul(s["za"], dya, "tn", "attn_out_dw")
    doh, dot, delta, dgate_b = _attn_gate_bwd(dm, dza, s["o"], p)
    dq, dkt, dvt = _flash_bwd(dm, s["qh"], s["qt"], s["kh"], s["vh"], doh, dot, s["lse"], delta)
    dpq, dpkv, dgq, dgk = _qk_bwd(dm, p, dq, dkt, dvt, tabs["cos"], tabs["sins"], wl["gq"], wl["gk"], tabs["bd"])
    dy1, dgate_a, dln_g, dln_b = _conv_act_bwd(dm, dzc, s["y1"], p, wl["ln_g"], wl["ln_b"])
    dpag, dconv_w, dconv_b = _conv_bwd(dm, dy1, p, wl["conv_w"])
    dp = jnp.concatenate([dpag, dgate_a, dpq, dpkv, dgate_b, dgm], axis=1)
    dh = _matmul(dp, wl["w_in"], "nt", "in_proj_dx")
    dw_in = _matmul(s["h"], dp, "tn", "in_proj_dw")
    dxc, dg_pre, dshsc = _prenorm_bwd(dm, s["xc"], wl["g_pre"], s["mod"], dh, dxn)
    dmod = jnp.concatenate([dshsc, dgt], axis=1)
    dw_mod, db_mod, dcv = _mod_bwd(cv, wl["w_mod"], dmod)
    grads = dict(w_mod=dw_mod, b_mod=db_mod[0], g_pre=dg_pre[0], g_post=dg_post[0], w_in=dw_in,
                 conv_w=dconv_w[:CONV_KERNEL], conv_b=dconv_b[0], ln_g=dln_g[0], ln_b=dln_b[0],
                 w_conv_out=dw_conv_out, q_norm_g=dgq[0, :dm.HD], k_norm_g=dgk[0, :dm.HD],
                 w_attn_out=dw_attn_out, w_out=dw_out, c_ctx=dcv[1])
    return dxc, grads


def _local_step(dm, x, c, ctx, c_ctx, target, wf):
    depth = wf["w_in"].shape[0]
    cos, sins = _rope_tables(dm)
    lane = np.arange(LANES)
    bd = jnp.asarray((lane[:, None] // dm.HD == lane[None, :] // dm.HD), dtype=BF16)
    tabs = dict(cos=cos, sins=sins, bd=bd)
    cv = jnp.concatenate([c.reshape(1, dm.D), c_ctx.reshape(1, dm.D), jnp.zeros((6, dm.D), F32)], axis=0)
    xc = jnp.concatenate([x, ctx], axis=0)
    layers, saved = [], []
    for l in range(depth):
        wl = dict(
            w_mod=wf["w_mod"][l], b_mod=wf["b_mod"][l].reshape(1, -1), g_pre=wf["g_pre"][l].reshape(1, -1),
            g_post=wf["g_post"][l].reshape(1, -1), w_in=wf["w_in"][l],
            conv_w=jnp.concatenate([wf["conv_w"][l], jnp.zeros((1, dm.C), F32)], axis=0),
            conv_b=wf["conv_b"][l].reshape(1, -1), ln_g=wf["ln_g"][l].reshape(1, -1), ln_b=wf["ln_b"][l].reshape(1, -1),
            w_conv_out=wf["w_conv_out"][l], gq=_lane_vec(wf["q_norm_g"][l], dm.HD), gk=_lane_vec(wf["k_norm_g"][l], dm.HD),
            w_attn_out=wf["w_attn_out"][l], w_out=wf["w_out"][l])
        layers.append(wl)
        xc, s = _layer_fwd(dm, xc, cv, wl, tabs)
        saved.append(s)
    dxc, loss = _loss_head(dm, xc, target)
    grads = [None] * depth
    for l in reversed(range(depth)):
        dxc, grads[l] = _layer_bwd(dm, dxc, cv, layers[l], tabs, saved[l])
    big = ("w_mod", "w_in", "w_conv_out", "w_attn_out", "w_out")
    gw = {k: [g[k] for g in grads] if k in big else jnp.stack([g[k] for g in grads]) for k in grads[0] if k != "c_ctx"}
    gw["c_ctx"] = grads[0]["c_ctx"] + grads[1]["c_ctx"] if depth == 2 else sum(g["c_ctx"] for g in grads)
    return loss[0, 0], dxc[:dm.N], gw


def _dims(x, ctx, w_in_full_cols, conv_c, attn_w, hd):
    n, d = x.shape
    ctx_len = ctx.shape[0]
    c, a = conv_c, attn_w
    kvw = (w_in_full_cols - 3 * c - 2 * a - 2 * d) // 2
    r = n + ctx_len
    tr = _pick(np.gcd(n, ctx_len), (256, 128))
    return Dims(N=n, CTX=ctx_len, R=r, D=d, C=c, A=a, KVW=kvw, HD=hd, NH=a // hd, NKV=kvw // hd,
                G=a // kvw, W=w_in_full_cols, TR=tr)


_HBM = pl.BlockSpec(memory_space=pltpu.HBM)
_VMEM = pl.BlockSpec(memory_space=pltpu.VMEM)


def _place():
    x, y, c = lax.axis_index("x"), lax.axis_index("y"), lax.axis_index("c")
    chips = [(1 - x, y), (x, 1 - y), (1 - x, 1 - y)]
    return x, y, c, chips


def _row_pieces(rows, want):
    n = max(k for k in range(1, want + 1) if rows % (16 * k) == 0)
    return [pl.ds(i * (rows // n), rows // n) for i in range(n)]


def _allgather_chips(own, name):
    _, h, w = own.shape
    pieces = _row_pieces(h, 4)
    npc = len(pieces)

    def body(own_ref, out_ref, send_sems, recv_sems):
        x, y, c, chips = _place()
        s_me = 2 * x + y
        sibling = (x, y, 1 - c)

        def copy(k, src, dst, to):
            return pltpu.make_async_remote_copy(src_ref=src, dst_ref=dst, send_sem=send_sems.at[k], recv_sem=recv_sems.at[k],
                                                device_id=to, device_id_type=MESH)

        sends = []
        for i, rows in enumerate(pieces):
            for j, chip in enumerate(chips):
                sends.append(copy(i * 3 + j, own_ref.at[c, rows], out_ref.at[s_me, c, rows], (*chip, c)))
        own_slots = [(6 * npc + hf * npc + i, hf, rows) for hf in range(2) for i, rows in enumerate(pieces)]
        for k, hf, rows in own_slots:
            sends.append(copy(k, own_ref.at[hf, rows], out_ref.at[s_me, hf, rows], sibling))
        for cp in sends:
            cp.start()
        for i, rows in enumerate(pieces):
            for j, (cx, cy) in enumerate(chips):
                landed = out_ref.at[2 * cx + cy, c, rows]
                copy(i * 3 + j, landed, landed, (cx, cy, c)).wait_recv()
                fwd = copy(3 * npc + i * 3 + j, landed, landed, sibling)
                fwd.start()
                sends.append(fwd)
        for i, rows in enumerate(pieces):
            for j, (cx, cy) in enumerate(chips):
                theirs = out_ref.at[2 * cx + cy, 1 - c, rows]
                copy(3 * npc + i * 3 + j, theirs, theirs, sibling).wait_recv()
        for k, hf, rows in own_slots:
            mine = out_ref.at[s_me, hf, rows]
            copy(k, mine, mine, sibling).wait_recv()
        for cp in sends:
            cp.wait_send()

    return pl.pallas_call(
        body, name=name, in_specs=[_HBM], out_specs=_HBM,
        out_shape=jax.ShapeDtypeStruct((N_CHIPS, 2, h, w), own.dtype),
        scratch_shapes=[pltpu.SemaphoreType.DMA((8 * npc,)), pltpu.SemaphoreType.DMA((8 * npc,))],
    )(own)


def _pair_exchange(buf, name):
    _, m, w = buf.shape
    pieces = _row_pieces(m, 16)

    def body(buf_ref, out_ref, send_sems, recv_sems):
        x, y, c, _ = _place()
        cps = [pltpu.make_async_remote_copy(src_ref=buf_ref.at[1 - c, rows], dst_ref=out_ref.at[rows], send_sem=send_sems.at[i],
                                            recv_sem=recv_sems.at[i], device_id=(x, y, 1 - c), device_id_type=MESH)
               for i, rows in enumerate(pieces)]
        for cp in cps:
            cp.start()
        for cp in cps:
            cp.wait()

    return pl.pallas_call(
        body, name=name, in_specs=[_HBM], out_specs=_HBM, out_shape=jax.ShapeDtypeStruct((m, w), buf.dtype),
        scratch_shapes=[pltpu.SemaphoreType.DMA((len(pieces),)), pltpu.SemaphoreType.DMA((len(pieces),))],
    )(buf)


def _pair_swap(t, name):
    m, w = t.shape
    pieces = _row_pieces(m, 8)
    n = len(pieces)

    def body(t_ref, out_ref, send_sems, recv_sems):
        x, y, c, _ = _place()
        cps = [pltpu.make_async_remote_copy(src_ref=t_ref.at[rows], dst_ref=out_ref.at[rows], send_sem=send_sems.at[i],
                                            recv_sem=recv_sems.at[i], device_id=(x, y, 1 - c), device_id_type=MESH)
               for i, rows in enumerate(pieces)]
        for cp in cps:
            cp.start()
        for cp in cps:
            cp.wait()

    return pl.pallas_call(
        body, name=name, in_specs=[_HBM], out_specs=_HBM, out_shape=jax.ShapeDtypeStruct((m, w), t.dtype),
        scratch_shapes=[pltpu.SemaphoreType.DMA((n,)), pltpu.SemaphoreType.DMA((n,))],
    )(t)


def _scatter_chips(s, name):
    _, h, w = s.shape

    def body(s_ref, out_ref, send_sems, recv_sems):
        _, _, c, chips = _place()
        cps = [pltpu.make_async_remote_copy(src_ref=s_ref.at[2 * cx + cy], dst_ref=out_ref.at[j], send_sem=send_sems.at[j],
                                            recv_sem=recv_sems.at[j], device_id=(cx, cy, c), device_id_type=MESH)
               for j, (cx, cy) in enumerate(chips)]
        for cp in cps:
            cp.start()
        for cp in cps:
            cp.wait()

    return pl.pallas_call(
        body, name=name, in_specs=[_HBM], out_specs=_HBM, out_shape=jax.ShapeDtypeStruct((3, h, w), s.dtype),
        scratch_shapes=[pltpu.SemaphoreType.DMA((3,)), pltpu.SemaphoreType.DMA((3,))],
    )(s)


def _sum_pair(buf, recv, core, name):
    m, w = recv.shape
    tr = _pick(m, (512, 432, 256, 128, 64, 32, 16))

    def body(c_ref, b_ref, r_ref, o_ref, ob_ref):
        v = b_ref[0] + r_ref[...]
        o_ref[...] = v
        ob_ref[...] = v.astype(BF16)

    spec = pl.BlockSpec((tr, w), lambda i, c: (i, 0))
    return pl.pallas_call(
        body, name=name,
        grid_spec=pltpu.PrefetchScalarGridSpec(
            num_scalar_prefetch=1, grid=(m // tr,),
            in_specs=[pl.BlockSpec((1, tr, w), lambda i, c: (c[0], i, 0)), spec], out_specs=(spec, spec)),
        out_shape=(jax.ShapeDtypeStruct((m, w), F32), jax.ShapeDtypeStruct((m, w), BF16)),
        compiler_params=_params(("parallel",)),
    )(core, buf, recv)


def _sum_chips(own, recv, chip, name):
    _, h, w = own.shape
    tr = _pick(h, (512, 432, 256, 128, 64, 32, 16))

    def body(s_ref, a_ref, r_ref, o_ref):
        acc = a_ref[0]
        for j in range(3):
            acc = acc + r_ref[j].astype(F32)
        o_ref[...] = acc

    return pl.pallas_call(
        body, name=name,
        grid_spec=pltpu.PrefetchScalarGridSpec(
            num_scalar_prefetch=1, grid=(h // tr,),
            in_specs=[pl.BlockSpec((1, tr, w), lambda i, s: (s[0], i, 0)), pl.BlockSpec((3, tr, w), lambda i, s: (0, i, 0))],
            out_specs=pl.BlockSpec((tr, w), lambda i, s: (i, 0))),
        out_shape=jax.ShapeDtypeStruct((h, w), F32), compiler_params=_params(("parallel",)),
    )(chip, own, recv)


def _allreduce_small(own, name):
    m, w = own.shape

    def body(own_ref, sum_ref, all_ref, send_sems, recv_sems):
        x, y, c, _ = _place()
        me = 4 * x + 2 * y + c
        all_ref[me] = own_ref[...]
        sends = []
        for k in range(1, N_DEV):
            peer = (x ^ ((k >> 2) & 1), y ^ ((k >> 1) & 1), c ^ (k & 1))
            cp = pltpu.make_async_remote_copy(src_ref=own_ref, dst_ref=all_ref.at[me], send_sem=send_sems.at[k - 1],
                                              recv_sem=recv_sems.at[k - 1], device_id=peer, device_id_type=MESH)
            cp.start()
            sends.append(cp)
        for k in range(1, N_DEV):
            px, py, pc = x ^ ((k >> 2) & 1), y ^ ((k >> 1) & 1), c ^ (k & 1)
            slot = all_ref.at[4 * px + 2 * py + pc]
            pltpu.make_async_remote_copy(src_ref=slot, dst_ref=slot, send_sem=send_sems.at[k - 1], recv_sem=recv_sems.at[k - 1],
                                         device_id=(px, py, pc), device_id_type=MESH).wait_recv()
        for cp in sends:
            cp.wait_send()
        acc = all_ref[0]
        for d in range(1, N_DEV):
            acc = acc + all_ref[d]
        sum_ref[...] = acc

    return pl.pallas_call(
        body, name=name, in_specs=[_VMEM], out_specs=_VMEM, out_shape=jax.ShapeDtypeStruct((m, w), F32),
        scratch_shapes=[pltpu.VMEM((N_DEV, m, w), F32), pltpu.SemaphoreType.DMA((N_DEV - 1,)), pltpu.SemaphoreType.DMA((N_DEV - 1,))],
        compiler_params=pltpu.CompilerParams(vmem_limit_bytes=VMEM_LIMIT),
    )(own)


_COL_SHARDED = ("w_mod", "w_in")
_ROW_SHARDED = ("w_conv_out", "w_attn_out", "w_out")
_BIG = _COL_SHARDED + _ROW_SHARDED
_SMALL = ("c_ctx", "b_mod", "g_pre", "g_post", "conv_b", "ln_g", "ln_b", "q_norm_g", "k_norm_g")
FLAT_W = 1024


def _unflatten_shard(flat, like):
    out, off = {}, 0
    flat = flat.reshape(-1)
    for k in _BIG:
        n = int(np.prod(like[k].shape))
        out[k] = flat[off:off + n].reshape(like[k].shape)
        off += n
    return out


def _by_shard(k, layers):
    parts = []
    for full in layers:
        if k in _COL_SHARDED:
            d, cols = full.shape
            parts.append(full.reshape(d, N_CHIPS, cols // N_CHIPS).transpose(1, 0, 2).reshape(N_CHIPS, -1))
        else:
            parts.append(full.reshape(N_CHIPS, -1))
    return jnp.concatenate(parts, axis=1)


def _from_shards(k, flat4, like):
    l = like.shape[0]
    if k in _COL_SHARDED:
        d, cs = like.shape[1:]
        return flat4.reshape(N_CHIPS, l, d, cs).transpose(1, 2, 0, 3).reshape(l, d, N_CHIPS * cs)
    rs, d = like.shape[1:]
    return flat4.reshape(N_CHIPS, l, rs, d).transpose(1, 0, 2, 3).reshape(l, N_CHIPS * rs, d)


def _pad_rows(flat, mult=8):
    n = flat.shape[0]
    rows = -(-n // LANES)
    rows = -(-rows // mult) * mult
    return jnp.concatenate([flat, jnp.zeros((rows * LANES - n,), F32)]).reshape(rows, LANES)


def kernel(x, c, ctx, c_ctx, w_mod, b_mod, g_pre, g_post, w_in, conv_w, conv_b, ln_g, ln_b, w_conv_out, q_norm_g, k_norm_g, w_attn_out, w_out, loss_target, m_c_ctx, m_w_mod, m_b_mod, m_g_pre, m_g_post, m_w_in, m_conv_w, m_conv_b, m_ln_g, m_ln_b, m_w_conv_out, m_q_norm_g, m_k_norm_g, m_w_attn_out, m_w_out, v_c_ctx, v_w_mod, v_b_mod, v_g_pre, v_g_post, v_w_in, v_conv_w, v_conv_b, v_ln_g, v_ln_b, v_w_conv_out, v_q_norm_g, v_k_norm_g, v_w_attn_out, v_w_out):
    w = dict(c_ctx=c_ctx, w_mod=w_mod, b_mod=b_mod, g_pre=g_pre, g_post=g_post, w_in=w_in, conv_w=conv_w, conv_b=conv_b,
             ln_g=ln_g, ln_b=ln_b, w_conv_out=w_conv_out, q_norm_g=q_norm_g, k_norm_g=k_norm_g, w_attn_out=w_attn_out, w_out=w_out)
    m = dict(c_ctx=m_c_ctx, w_mod=m_w_mod, b_mod=m_b_mod, g_pre=m_g_pre, g_post=m_g_post, w_in=m_w_in, conv_w=m_conv_w,
             conv_b=m_conv_b, ln_g=m_ln_g, ln_b=m_ln_b, w_conv_out=m_w_conv_out, q_norm_g=m_q_norm_g, k_norm_g=m_k_norm_g,
             w_attn_out=m_w_attn_out, w_out=m_w_out)
    v = dict(c_ctx=v_c_ctx, w_mod=v_w_mod, b_mod=v_b_mod, g_pre=v_g_pre, g_post=v_g_post, w_in=v_w_in, conv_w=v_conv_w,
             conv_b=v_conv_b, ln_g=v_ln_g, ln_b=v_ln_b, w_conv_out=v_w_conv_out, q_norm_g=v_q_norm_g, k_norm_g=v_k_norm_g,
             w_attn_out=v_w_attn_out, w_out=v_w_out)
    order = ("c_ctx", "w_mod", "b_mod", "g_pre", "g_post", "w_in", "conv_w", "conv_b", "ln_g", "ln_b", "w_conv_out",
             "q_norm_g", "k_norm_g", "w_attn_out", "w_out")
    my_chip = 2 * lax.axis_index("x") + lax.axis_index("y")
    my_core = lax.axis_index("c")
    depth, _, conv_cs = conv_w.shape
    conv_c = conv_cs * N_CHIPS
    dm = _dims(x[0], ctx[0], w_in.shape[2] * N_CHIPS, conv_c, w_attn_out.shape[1] * N_CHIPS, q_norm_g.shape[1])

    own = jnp.concatenate([w[k].astype(BF16).reshape(-1) for k in _BIG])
    rows2 = own.shape[0] // FLAT_W
    own = own.reshape(2, rows2 // 2, FLAT_W)
    gathered = _allgather_chips(own, "gather_weights").reshape(N_CHIPS, -1)
    wf, off = {}, 0
    for k in _BIG:
        n = int(np.prod(w[k].shape))
        wf[k] = _from_shards(k, gathered[:, off:off + n], w[k])
        off += n
    cw = jnp.zeros((depth, CONV_KERNEL, N_CHIPS, conv_cs), F32)
    cw = lax.dynamic_update_slice(cw, conv_w[:, :, None, :], (0, 0, my_chip, 0))
    cw = jnp.where(my_core == 0, cw, 0.0).reshape(-1)
    wf["conv_w"] = _allreduce_small(_pad_rows(cw), "gather_conv_w").reshape(-1)[:cw.shape[0]].reshape(depth, CONV_KERNEL, conv_c)
    for k in _SMALL:
        wf[k] = w[k]

    loss_part, grad_x, gw = _local_step(dm, x[0], c, ctx[0], c_ctx, loss_target[0], wf)

    g4 = jnp.concatenate([_by_shard(k, gw[k]) for k in _BIG], axis=1)
    hrows = rows2 // 2
    g24 = g4.reshape(N_CHIPS, 2, hrows, FLAT_W).transpose(1, 0, 2, 3).reshape(2, N_CHIPS * hrows, FLAT_W)
    core_i = my_core.astype(jnp.int32).reshape(1)
    chip_i = my_chip.astype(jnp.int32).reshape(1)
    theirs = _pair_exchange(g24, "reduce_pair")
    chip_sum, chip_sum_bf = _sum_pair(g24, theirs, core_i, "reduce_pair_add")
    others = _scatter_chips(chip_sum_bf.reshape(N_CHIPS, hrows, FLAT_W), "reduce_chips")
    mine = _sum_chips(chip_sum.reshape(N_CHIPS, hrows, FLAT_W), others, chip_i, "reduce_chips_add")
    sibling_half = _pair_swap(mine, "reduce_share")

    small = jnp.concatenate([gw[k].reshape(-1) for k in _SMALL] + [gw["conv_w"].reshape(-1), loss_part.reshape(1)])
    small_sum = _allreduce_small(_pad_rows(small), "reduce_small").reshape(-1)
    loss = small_sum[small.shape[0] - 1]
    gs, off = {}, 0
    for k in _SMALL:
        n = int(np.prod(w[k].shape))
        gs[k] = small_sum[off:off + n].reshape(w[k].shape)
        off += n
    gcw = small_sum[off:off + depth * CONV_KERNEL * conv_c].reshape(depth, CONV_KERNEL, N_CHIPS, conv_cs)
    gs["conv_w"] = lax.dynamic_index_in_dim(gcw, my_chip, axis=2, keepdims=False)

    g_flat = jnp.where(my_core == 0, jnp.concatenate([mine, sibling_half]), jnp.concatenate([sibling_half, mine]))
    grads = _unflatten_shard(g_flat, w)
    delta, new_m, new_v = {}, {}, {}
    for k in _BIG:
        as2d = lambda t: t.reshape(-1, t.shape[-1])
        d_k, m_k, v_k = _adamw(as2d(w[k]), as2d(grads[k]), as2d(m[k]), as2d(v[k]), "adamw_" + k)
        delta[k], new_m[k], new_v[k] = d_k.reshape(w[k].shape), m_k.reshape(w[k].shape), v_k.reshape(w[k].shape)
    small_keys = _SMALL + ("conv_w",)
    join = lambda t: _pad_rows(jnp.concatenate([t[k].reshape(-1) for k in small_keys]))
    d_s, m_s, v_s = _adamw(join(w), join(gs), join(m), join(v), "adamw_small")
    off = 0
    for k in small_keys:
        n = int(np.prod(w[k].shape))
        grads[k] = gs[k]
        for dst, src in ((delta, d_s), (new_m, m_s), (new_v, v_s)):
            dst[k] = src.reshape(-1)[off:off + n].reshape(w[k].shape)
        off += n
    return (loss, grad_x[None], *[grads[k] for k in order], *[delta[k] for k in order],
            *[new_m[k] for k in order], *[new_v[k] for k in order])
```

```python
import functools
from typing import NamedTuple

import jax
import jax.numpy as jnp
import numpy as np
from jax import lax
from jax.experimental import pallas as pl
from jax.experimental.pallas import tpu as pltpu

F32 = jnp.float32
BF16 = jnp.bfloat16
EPS = 1e-6
GRID_W = 64
ROPE_THETA = 10000.0
CONV_KERNEL = 31
CONV_PAD = CONV_KERNEL // 2
HALO = 16
LANES = 128
SUBLANES = 8
ADAM_LR, ADAM_B1, ADAM_B2, ADAM_EPS, ADAM_WD, ADAM_STEP = 0.001, 0.9, 0.999, 1e-08, 0.01, 10
VMEM_LIMIT = 56 * 1024 * 1024
ADAM_BLOCK_BYTES = 1024 * 1024
LOG2E = 1.4426950408889634
LN2 = 0.6931471805599453
MESH = pl.DeviceIdType.MESH
N_CHIPS = 4
N_DEV = 8


class Dims(NamedTuple):
    N: int
    CTX: int
    R: int
    D: int
    C: int
    A: int
    KVW: int
    HD: int
    NH: int
    NKV: int
    G: int
    W: int
    TR: int


def _pick(n, prefs):
    for p in prefs:
        if n % p == 0:
            return p
    raise ValueError(f"no tile for {n} in {prefs}")


def _params(sem):
    return pltpu.CompilerParams(dimension_semantics=sem, vmem_limit_bytes=VMEM_LIMIT)


def _rows(tr, w, col=0):
    return pl.BlockSpec((tr, w), lambda i, col=col: (i, col))


def _whole(shape):
    return pl.BlockSpec(shape, lambda *_: (0,) * len(shape))


def _col(off, w):
    assert off % w == 0, (off, w)
    return off // w


_DN = {"nn": (((1,), (0,)), ((), ())), "nt": (((1,), (1,)), ((), ())), "tn": (((0,), (0,)), ((), ()))}


def _matmul(a, b, mode, name):
    if mode == "nn":
        (m, k), n = a.shape, b.shape[1]
    elif mode == "nt":
        (m, k), n = a.shape, b.shape[0]
    else:
        (k, m), n = a.shape, b.shape[1]
    tm = _pick(m, (1408, 1024, 768, 512, 256, 128, 8))
    tn = _pick(n, (1280, 1024, 768, 512, 256, 128))
    tk = _pick(k, (1536, 1408, 1280, 1024, 768, 512, 256, 128))
    nk = k // tk
    dn = _DN[mode]

    def body(a_ref, b_ref, o_ref):
        part = lax.dot_general(a_ref[...], b_ref[...], dn, preferred_element_type=F32)
        if nk == 1:
            o_ref[...] = part
            return
        kk = pl.program_id(2)

        @pl.when(kk == 0)
        def _():
            o_ref[...] = part

        @pl.when(kk > 0)
        def _():
            o_ref[...] += part

    a_spec = pl.BlockSpec((tk, tm), lambda i, j, kk: (kk, i)) if mode == "tn" else pl.BlockSpec((tm, tk), lambda i, j, kk: (i, kk))
    b_spec = pl.BlockSpec((tn, tk), lambda i, j, kk: (j, kk)) if mode == "nt" else pl.BlockSpec((tk, tn), lambda i, j, kk: (kk, j))
    return pl.pallas_call(
        body, name=name, grid=(m // tm, n // tn, nk),
        in_specs=[a_spec, b_spec], out_specs=pl.BlockSpec((tm, tn), lambda i, j, kk: (i, j)),
        out_shape=jax.ShapeDtypeStruct((m, n), F32),
        compiler_params=_params(("parallel", "parallel", "arbitrary")),
    )(a, b)


def _rms(x, g):
    return x * lax.rsqrt(jnp.mean(x * x, axis=-1, keepdims=True) + EPS) * g


def _prenorm_fn(x, g, sc, sh):
    return _rms(x, g) * (1.0 + sc) + sh


def _post_fn(outp, g, gt):
    return gt * _rms(outp, g)


def _convact_fn(y1, gate, ln_g, ln_b):
    mu = jnp.mean(y1, axis=-1, keepdims=True)
    var = jnp.mean(jnp.square(y1 - mu), axis=-1, keepdims=True)
    y = (y1 - mu) * lax.rsqrt(var + EPS) * ln_g + ln_b
    return jax.nn.silu(y) * jax.nn.silu(gate)


def _dsilu(x):
    s = jax.nn.sigmoid(x)
    return s * (1.0 + x * (1.0 - s))


def _seg_select(is_ctx, mod_ref, lo, w):
    return jnp.where(is_ctx, mod_ref[1:2, lo:lo + w], mod_ref[0:1, lo:lo + w])


def _seg_rows(is_ctx, v):
    row = lax.broadcasted_iota(jnp.int32, (8, v.shape[1]), 0)
    return jnp.where(row == is_ctx.astype(jnp.int32), v, 0.0)


def _mod_fwd(cv, w_mod_l, b_mod_l):
    def body(cv_ref, w_ref, b_ref, o_ref):
        s = jax.nn.silu(cv_ref[...]).astype(BF16)
        o_ref[...] = jnp.dot(s, w_ref[...], preferred_element_type=F32) + b_ref[...]

    return pl.pallas_call(
        body, name="mod_fwd", out_shape=jax.ShapeDtypeStruct((8, w_mod_l.shape[1]), F32),
        compiler_params=pltpu.CompilerParams(vmem_limit_bytes=VMEM_LIMIT),
    )(cv, w_mod_l, b_mod_l)


def _mod_bwd(cv, w_mod_l, dmod):
    d = cv.shape[1]

    def body(cv_ref, w_ref, dm_ref, dw_ref, db_ref, dcv_ref):
        cvv = cv_ref[...]
        s = jax.nn.silu(cvv).astype(BF16)
        dm = dm_ref[...]
        dmb = dm.astype(BF16)
        dw_ref[...] = lax.dot_general(s, dmb, _DN["tn"], preferred_element_type=F32)
        db_ref[...] = jnp.sum(dm, axis=0, keepdims=True)
        ds = lax.dot_general(dmb, w_ref[...], _DN["nt"], preferred_element_type=F32)
        dcv_ref[...] = ds * _dsilu(cvv)

    return pl.pallas_call(
        body, name="mod_bwd",
        out_shape=(jax.ShapeDtypeStruct(w_mod_l.shape, F32), jax.ShapeDtypeStruct((1, w_mod_l.shape[1]), F32),
                   jax.ShapeDtypeStruct((8, d), F32)),
        compiler_params=pltpu.CompilerParams(vmem_limit_bytes=VMEM_LIMIT),
    )(cv, w_mod_l, dmod)


def _prenorm(dm, xc, g_pre_l, mod):
    tr, d, n = dm.TR, dm.D, dm.N

    def body(x_ref, g_ref, mod_ref, h_ref):
        is_ctx = pl.program_id(0) * tr >= n
        sh = _seg_select(is_ctx, mod_ref, 0, d)
        sc = _seg_select(is_ctx, mod_ref, d, d)
        h_ref[...] = _prenorm_fn(x_ref[...], g_ref[...], sc, sh).astype(BF16)

    return pl.pallas_call(
        body, name="prenorm", grid=(dm.R // tr,),
        in_specs=[_rows(tr, d), _whole((1, d)), _whole((8, 3 * d))], out_specs=_rows(tr, d),
        out_shape=jax.ShapeDtypeStruct((dm.R, d), BF16), compiler_params=_params(("parallel",)),
    )(xc, g_pre_l, mod)


def _prenorm_bwd(dm, xc, g_pre_l, mod, dh, dxn):
    tr, d, n = dm.TR, dm.D, dm.N

    def body(x_ref, g_ref, mod_ref, dh_ref, dxn_ref, dx_ref, dg_ref, dss_ref):
        i = pl.program_id(0)
        is_ctx = i * tr >= n
        sh = _seg_select(is_ctx, mod_ref, 0, d)
        sc = _seg_select(is_ctx, mod_ref, d, d)
        _, vjp = jax.vjp(_prenorm_fn, x_ref[...], g_ref[...], sc, sh)
        dx, dg, dsc, dsh = vjp(dh_ref[...])
        dx_ref[...] = dxn_ref[...] + dx

        @pl.when(i == 0)
        def _():
            dg_ref[...] = jnp.zeros_like(dg_ref)
            dss_ref[...] = jnp.zeros_like(dss_ref)

        dg_ref[...] += dg
        dss_ref[...] += _seg_rows(is_ctx, jnp.concatenate([dsh, dsc], axis=1))

    return pl.pallas_call(
        body, name="prenorm_bwd", grid=(dm.R // tr,),
        in_specs=[_rows(tr, d), _whole((1, d)), _whole((8, 3 * d)), _rows(tr, d), _rows(tr, d)],
        out_specs=(_rows(tr, d), _whole((1, d)), _whole((8, 2 * d))),
        out_shape=(jax.ShapeDtypeStruct((dm.R, d), F32), jax.ShapeDtypeStruct((1, d), F32),
                   jax.ShapeDtypeStruct((8, 2 * d), F32)),
        compiler_params=_params(("arbitrary",)),
    )(xc, g_pre_l, mod, dh, dxn)


def _post(dm, xc, outp, g_post_l, mod):
    tr, d, n = dm.TR, dm.D, dm.N

    def body(x_ref, o_ref, g_ref, mod_ref, y_ref):
        is_ctx = pl.program_id(0) * tr >= n
        gt = _seg_select(is_ctx, mod_ref, 2 * d, d)
        y_ref[...] = x_ref[...] + _post_fn(o_ref[...], g_ref[...], gt)

    return pl.pallas_call(
        body, name="post", grid=(dm.R // tr,),
        in_specs=[_rows(tr, d), _rows(tr, d), _whole((1, d)), _whole((8, 3 * d))], out_specs=_rows(tr, d),
        out_shape=jax.ShapeDtypeStruct((dm.R, d), F32), compiler_params=_params(("parallel",)),
    )(xc, outp, g_post_l, mod)


def _post_bwd(dm, outp, g_post_l, mod, dxn):
    tr, d, n = dm.TR, dm.D, dm.N

    def body(o_ref, g_ref, mod_ref, dxn_ref, do_ref, dg_ref, dgt_ref):
        i = pl.program_id(0)
        is_ctx = i * tr >= n
        gt = _seg_select(is_ctx, mod_ref, 2 * d, d)
        _, vjp = jax.vjp(_post_fn, o_ref[...], g_ref[...], gt)
        do, dg, dgt = vjp(dxn_ref[...])
        do_ref[...] = do.astype(BF16)

        @pl.when(i == 0)
        def _():
            dg_ref[...] = jnp.zeros_like(dg_ref)
            dgt_ref[...] = jnp.zeros_like(dgt_ref)

        dg_ref[...] += dg
        dgt_ref[...] += _seg_rows(is_ctx, dgt)

    return pl.pallas_call(
        body, name="post_bwd", grid=(dm.R // tr,),
        in_specs=[_rows(tr, d), _whole((1, d)), _whole((8, 3 * d)), _rows(tr, d)],
        out_specs=(_rows(tr, d), _whole((1, d)), _whole((8, d))),
        out_shape=(jax.ShapeDtypeStruct((dm.R, d), BF16), jax.ShapeDtypeStruct((1, d), F32),
                   jax.ShapeDtypeStruct((8, d), F32)),
        compiler_params=_params(("arbitrary",)),
    )(outp, g_post_l, mod, dxn)


def _group_sum(x, bd):
    hi = x.astype(BF16)
    lo = (x - hi.astype(F32)).astype(BF16)
    return jnp.dot(hi, bd, preferred_element_type=F32) + jnp.dot(lo, bd, preferred_element_type=F32)


def _swap_halves(x):
    lane = lax.broadcasted_iota(jnp.int32, x.shape, 1)
    return jnp.where((lane & 32) == 0, pltpu.roll(x, LANES - 32, 1), pltpu.roll(x, 32, 1))


def _qk_prep(dm, p, cos, sins, gq, gk, bd):
    tr, hd = dm.TR, dm.HD
    oq, ok, ov = 3 * dm.C, 3 * dm.C + dm.A, 3 * dm.C + dm.A + dm.KVW
    scale = hd ** -0.5 * LOG2E

    def body(q_ref, k_ref, v_ref, cos_ref, sin_ref, gq_ref, gk_ref, bd_ref, qh_ref, qt_ref, kh_ref, vh_ref, vx_ref):
        cosv, sinv, bdv = cos_ref[...], sin_ref[...], bd_ref[...]

        def chunk(x, g):
            rstd = lax.rsqrt(_group_sum(x * x, bdv) * (1.0 / hd) + EPS)
            xn = x * rstd * g
            return xn * cosv + _swap_halves(xn) * sinv

        for c in range(dm.A // LANES):
            y = chunk(q_ref[:, c * LANES:(c + 1) * LANES], gq_ref[...]) * scale
            qh_ref[2 * c] = y[:, :hd].astype(BF16)
            qh_ref[2 * c + 1] = y[:, hd:].astype(BF16)
            qt_ref[c * LANES:(c + 1) * LANES, :] = y.T.astype(BF16)
        for c in range(dm.KVW // LANES):
            y = chunk(k_ref[:, c * LANES:(c + 1) * LANES], gk_ref[...])
            kh_ref[2 * c] = y[:, :hd].astype(BF16)
            kh_ref[2 * c + 1] = y[:, hd:].astype(BF16)
            v = v_ref[:, c * LANES:(c + 1) * LANES]
            vh_ref[2 * c] = v[:, :hd].astype(BF16)
            vh_ref[2 * c + 1] = v[:, hd:].astype(BF16)
            lane = lax.broadcasted_iota(jnp.int32, v.shape, 1)
            one_at_hd = (lane == hd).astype(F32)
            vx_ref[2 * c] = jnp.where(lane < hd, v, one_at_hd).astype(BF16)
            vx_ref[2 * c + 1] = jnp.where(lane < hd, pltpu.roll(v, hd, 1), one_at_hd).astype(BF16)

    hspec = lambda nh, w=hd: pl.BlockSpec((nh, tr, w), lambda i: (0, i, 0))
    return pl.pallas_call(
        body, name="qk_prep", grid=(dm.R // tr,),
        in_specs=[_rows(tr, dm.A, _col(oq, dm.A)), _rows(tr, dm.KVW, _col(ok, dm.KVW)), _rows(tr, dm.KVW, _col(ov, dm.KVW)),
                  _rows(tr, LANES), _rows(tr, LANES), _whole((1, LANES)), _whole((1, LANES)), _whole((LANES, LANES))],
        out_specs=(hspec(dm.NH), pl.BlockSpec((dm.A, tr), lambda i: (0, i)), hspec(dm.NKV), hspec(dm.NKV), hspec(dm.NKV, LANES)),
        out_shape=(jax.ShapeDtypeStruct((dm.NH, dm.R, hd), BF16), jax.ShapeDtypeStruct((dm.A, dm.R), BF16),
                   jax.ShapeDtypeStruct((dm.NKV, dm.R, hd), BF16), jax.ShapeDtypeStruct((dm.NKV, dm.R, hd), BF16),
                   jax.ShapeDtypeStruct((dm.NKV, dm.R, LANES), BF16)),
        compiler_params=_params(("parallel",)),
    )(p, p, p, cos, sins, gq, gk, bd)


def _qk_bwd(dm, p, dq, dkt, dvt, cos, sins, gq, gk, bd):
    tr, hd = dm.TR, dm.HD
    oq, ok = 3 * dm.C, 3 * dm.C + dm.A
    scale = hd ** -0.5

    def body(q_ref, k_ref, dq_ref, dkh_ref, dvh_ref, cos_ref, sin_ref, gq_ref, gk_ref, bd_ref,
             dpq_ref, dpkv_ref, dgq_ref, dgk_ref):
        i = pl.program_id(0)
        cosv, sinv, bdv = cos_ref[...], sin_ref[...], bd_ref[...]

        def chunk(x, g, dy):
            rstd = lax.rsqrt(_group_sum(x * x, bdv) * (1.0 / hd) + EPS)
            xhat = x * rstd
            dxn = dy * cosv + _swap_halves(dy * sinv)
            dxhat = dxn * g
            dx = rstd * (dxhat - xhat * (_group_sum(dxhat * xhat, bdv) * (1.0 / hd)))
            return dx, jnp.sum(dxn * xhat, axis=0, keepdims=True)

        @pl.when(i == 0)
        def _():
            dgq_ref[...] = jnp.zeros_like(dgq_ref)
            dgk_ref[...] = jnp.zeros_like(dgk_ref)

        dgq = jnp.zeros((1, LANES), F32)
        for c in range(dm.A // LANES):
            sl = slice(c * LANES, (c + 1) * LANES)
            dx, dg = chunk(q_ref[:, sl], gq_ref[...], dq_ref[:, sl] * scale)
            dpq_ref[:, sl] = dx.astype(BF16)
            dgq = dgq + dg
        dgq_ref[...] += dgq + pltpu.roll(dgq, hd, 1)
        dgk = jnp.zeros((1, LANES), F32)
        for c in range(dm.KVW // LANES):
            sl = slice(c * LANES, (c + 1) * LANES)
            dx, dg = chunk(k_ref[:, sl], gk_ref[...], dkh_ref[sl, :].T)
            dpkv_ref[:, sl] = dx.astype(BF16)
            dgk = dgk + dg
            dpkv_ref[:, dm.KVW + c * LANES:dm.KVW + (c + 1) * LANES] = dvh_ref[sl, :].T.astype(BF16)
        dgk_ref[...] += dgk + pltpu.roll(dgk, hd, 1)

    hspec = pl.BlockSpec((dm.KVW, tr), lambda i: (0, i))
    return pl.pallas_call(
        body, name="qk_bwd", grid=(dm.R // tr,),
        in_specs=[_rows(tr, dm.A, _col(oq, dm.A)), _rows(tr, dm.KVW, _col(ok, dm.KVW)), _rows(tr, dm.A), hspec, hspec,
                  _rows(tr, LANES), _rows(tr, LANES), _whole((1, LANES)), _whole((1, LANES)), _whole((LANES, LANES))],
        out_specs=(_rows(tr, dm.A), _rows(tr, 2 * dm.KVW), _whole((1, LANES)), _whole((1, LANES))),
        out_shape=(jax.ShapeDtypeStruct((dm.R, dm.A), BF16), jax.ShapeDtypeStruct((dm.R, 2 * dm.KVW), BF16),
                   jax.ShapeDtypeStruct((1, LANES), F32), jax.ShapeDtypeStruct((1, LANES), F32)),
        compiler_params=_params(("arbitrary",)),
    )(p, p, dq, dkt, dvt, cos, sins, gq, gk, bd)


def _flash_fwd(dm, qh, kh, vx):
    tq, ks = dm.TR, dm.TR
    rs = min(tq, LANES)
    g, hd = dm.G, dm.HD
    nq, nq_lat = dm.R // tq, dm.N // tq
    n_chunks, ctx_chunks = dm.R // ks, dm.CTX // ks

    def body(q_ref, k_ref, v_ref, o_ref, lse_ref):
        qi = pl.program_id(1)

        def attend(chunks):
            for gg in range(g):
                for r in range(tq // rs):
                    rows = slice(r * rs, (r + 1) * rs)
                    q = q_ref[gg, rows, :]
                    m_prev = jnp.full((rs, LANES), -1e30, F32)
                    acc = jnp.zeros((rs, LANES), F32)
                    for kc in chunks:
                        keys = slice(kc * ks, (kc + 1) * ks)
                        s = lax.dot_general(q, k_ref[0, keys, :], _DN["nt"], preferred_element_type=F32)
                        m_new = jnp.maximum(m_prev, jnp.max(s, axis=-1, keepdims=True))
                        p = jnp.exp2(s - jnp.tile(m_new, (1, ks // LANES)))
                        acc = acc * jnp.exp2(m_prev - m_new) + jnp.dot(p.astype(BF16), v_ref[0, keys, :],
                                                                        preferred_element_type=F32)
                        m_prev = m_new
                    l = acc[:, hd:hd + 1]
                    o_ref[rows, gg * hd:(gg + 1) * hd] = acc[:, :hd] / l
                    lse_ref[gg, rows, :] = m_prev + jnp.log2(l)

        @pl.when(qi < nq_lat)
        def _():
            attend(range(n_chunks))

        @pl.when(qi >= nq_lat)
        def _():
            attend(range(n_chunks - ctx_chunks, n_chunks))

    return pl.pallas_call(
        body, name="flash_fwd", grid=(dm.NKV, nq),
        in_specs=[pl.BlockSpec((g, tq, hd), lambda h, qi: (h, qi, 0)), pl.BlockSpec((1, dm.R, hd), lambda h, qi: (h, 0, 0)),
                  pl.BlockSpec((1, dm.R, LANES), lambda h, qi: (h, 0, 0))],
        out_specs=(pl.BlockSpec((tq, g * hd), lambda h, qi: (qi, h)), pl.BlockSpec((g, tq, LANES), lambda h, qi: (h, qi, 0))),
        out_shape=(jax.ShapeDtypeStruct((dm.R, dm.A), F32), jax.ShapeDtypeStruct((dm.NH, dm.R, LANES), F32)),
        compiler_params=_params(("parallel", "parallel")),
    )(qh, kh, vx)


def _flash_bwd(dm, qh, qt, kh, vh, doh, dot, lse, delta):
    tq, ks = dm.TR, dm.TR
    g, hd = dm.G, dm.HD
    nq, nq_lat = dm.R // tq, dm.N // tq
    n_chunks, ctx_chunks = dm.R // ks, dm.CTX // ks

    def body(q_ref, qt_ref, k_ref, v_ref, do_ref, dot_ref, lse_ref, dl_ref, dq_ref, dk_acc, dv_acc):
        qi = pl.program_id(1)

        @pl.when(qi == 0)
        def _():
            dk_acc[...] = jnp.zeros_like(dk_acc)
            dv_acc[...] = jnp.zeros_like(dv_acc)

        def attend(chunks):
            q = q_ref[...].reshape(g * tq, hd)
            do = do_ref[...].reshape(g * tq, hd)
            q_t = jnp.concatenate([qt_ref[gg * hd:(gg + 1) * hd, :] for gg in range(g)], axis=1)
            do_t = jnp.concatenate([dot_ref[gg * hd:(gg + 1) * hd, :] for gg in range(g)], axis=1)
            wide = lambda ref: jnp.tile(ref[...].reshape(g * tq, LANES), (1, ks // LANES))
            dqs = jnp.zeros((g * tq, hd), F32)
            for kc in chunks:
                keys = slice(kc * ks, (kc + 1) * ks)
                k, v = k_ref[0, keys, :], v_ref[0, keys, :]
                s = lax.dot_general(q, k, _DN["nt"], preferred_element_type=F32)
                p = jnp.exp2(s - wide(lse_ref))
                dp = lax.dot_general(do, v, _DN["nt"], preferred_element_type=F32)
                ds = (p * (dp - wide(dl_ref))).astype(BF16)
                dv_acc[:, keys] += jnp.dot(do_t, p.astype(BF16), preferred_element_type=F32)
                dk_acc[:, keys] += jnp.dot(q_t, ds, preferred_element_type=F32)
                dqs = dqs + jnp.dot(ds, k, preferred_element_type=F32)
            for gg in range(g):
                dq_ref[:, gg * hd:(gg + 1) * hd] = dqs[gg * tq:(gg + 1) * tq]

        @pl.when(qi < nq_lat)
        def _():
            attend(range(n_chunks))

        @pl.when(qi >= nq_lat)
        def _():
            attend(range(n_chunks - ctx_chunks, n_chunks))

        @pl.when(qi == nq - 1)
        def _():
            dk_acc[...] = dk_acc[...] * LN2

    qspec = pl.BlockSpec((g, tq, hd), lambda h, qi: (h, qi, 0))
    tspec = pl.BlockSpec((g * hd, tq), lambda h, qi: (h, qi))
    kspec = pl.BlockSpec((1, dm.R, hd), lambda h, qi: (h, 0, 0))
    sspec = pl.BlockSpec((g, tq, LANES), lambda h, qi: (h, qi, 0))
    ospec = pl.BlockSpec((hd, dm.R), lambda h, qi: (h, 0))
    return pl.pallas_call(
        body, name="flash_bwd", grid=(dm.NKV, nq),
        in_specs=[qspec, tspec, kspec, kspec, qspec, tspec, sspec, sspec],
        out_specs=(pl.BlockSpec((tq, g * hd), lambda h, qi: (qi, h)), ospec, ospec),
        out_shape=(jax.ShapeDtypeStruct((dm.R, dm.A), F32), jax.ShapeDtypeStruct((dm.KVW, dm.R), F32),
                   jax.ShapeDtypeStruct((dm.KVW, dm.R), F32)),
        compiler_params=_params(("parallel", "arbitrary")),
    )(qh, qt, kh, vh, doh, dot, lse, delta)


def _gate_b_specs(dm):
    ogb = 3 * dm.C + dm.A + 2 * dm.KVW
    half = dm.A // 2
    return [_rows(dm.TR, half, _col(ogb, half)), _rows(dm.TR, half, _col(ogb, half) + 1)]


def _attn_gate(dm, o, p):
    tr = dm.TR

    def body(o_ref, g0_ref, g1_ref, za_ref):
        gate = jnp.concatenate([g0_ref[...], g1_ref[...]], axis=1)
        za_ref[...] = (o_ref[...] * jax.nn.silu(gate)).astype(BF16)

    return pl.pallas_call(
        body, name="attn_gate", grid=(dm.R // tr,),
        in_specs=[_rows(tr, dm.A)] + _gate_b_specs(dm), out_specs=_rows(tr, dm.A),
        out_shape=jax.ShapeDtypeStruct((dm.R, dm.A), BF16), compiler_params=_params(("parallel",)),
    )(o, p, p)


def _attn_gate_bwd(dm, dza, o, p):
    tr, hd = dm.TR, dm.HD

    def body(dza_ref, o_ref, g0_ref, g1_ref, doh_ref, dot_ref, dl_ref, dgb_ref):
        gate = jnp.concatenate([g0_ref[...], g1_ref[...]], axis=1)
        dza, o = dza_ref[...], o_ref[...]
        do = dza * jax.nn.silu(gate)
        dgb_ref[...] = (dza * o * _dsilu(gate)).astype(BF16)
        dot_ref[...] = do.T.astype(BF16)
        prod = do * o
        for h in range(dm.NH):
            sl = slice(h * hd, (h + 1) * hd)
            doh_ref[h] = do[:, sl].astype(BF16)
            dl_ref[h] = jnp.broadcast_to(jnp.sum(prod[:, sl], axis=-1, keepdims=True), (tr, LANES))

    return pl.pallas_call(
        body, name="attn_gate_bwd", grid=(dm.R // tr,),
        in_specs=[_rows(tr, dm.A), _rows(tr, dm.A)] + _gate_b_specs(dm),
        out_specs=(pl.BlockSpec((dm.NH, tr, hd), lambda i: (0, i, 0)), pl.BlockSpec((dm.A, tr), lambda i: (0, i)),
                   pl.BlockSpec((dm.NH, tr, LANES), lambda i: (0, i, 0)), _rows(tr, dm.A)),
        out_shape=(jax.ShapeDtypeStruct((dm.NH, dm.R, hd), BF16), jax.ShapeDtypeStruct((dm.A, dm.R), BF16),
                   jax.ShapeDtypeStruct((dm.NH, dm.R, LANES), F32), jax.ShapeDtypeStruct((dm.R, dm.A), BF16)),
        compiler_params=_params(("parallel",)),
    )(dza, o, p, p)


def _halo_specs(dm, col):
    per = dm.TR // HALO
    last = dm.R // HALO - 1
    prev = pl.BlockSpec((HALO, dm.C), lambda i: (jnp.maximum(i * per - 1, 0), col))
    nxt = pl.BlockSpec((HALO, dm.C), lambda i: (jnp.minimum((i + 1) * per, last), col))
    return prev, nxt


def _seg_valid(dm, i):
    tr = dm.TR
    r = i * tr - HALO + lax.broadcasted_iota(jnp.int32, (tr + 2 * HALO, 1), 0)
    is_ctx = i * tr >= dm.N
    lo = jnp.where(is_ctx, dm.N, 0)
    hi = jnp.where(is_ctx, dm.R, dm.N)
    return jnp.logical_and(r >= lo, r < hi)


CONV_RB = 64


def _shifted_copies(src_ref, sl, sh_ref, tr):
    rows = tr + 2 * HALO - SUBLANES
    padded = src_ref[:, sl]
    for ph in range(SUBLANES):
        sh_ref[ph, 0:rows, :] = padded[ph:ph + rows]


def _tap_rows(sh_ref, off, n):
    ph = off % SUBLANES
    return sh_ref[ph, off - ph:off - ph + n, :]


def _conv_fwd(dm, p, conv_w_l, conv_b_l, ln_g_l, ln_b_l):
    tr, c = dm.TR, dm.C

    def body(a_ref, g_ref, gate_ref, ap_ref, an_ref, gp_ref, gn_ref, w_ref, b_ref, lg_ref, lb_ref,
             zc_ref, y1_ref, u_ref, sh_ref):
        i = pl.program_id(0)
        a = jnp.concatenate([ap_ref[...], a_ref[...], an_ref[...]], axis=0)
        g = jnp.concatenate([gp_ref[...], g_ref[...], gn_ref[...]], axis=0)
        u_ref[...] = jnp.where(_seg_valid(dm, i), a * jax.nn.sigmoid(g), 0.0)
        for cc in range(c // LANES):
            sl = slice(cc * LANES, (cc + 1) * LANES)
            _shifted_copies(u_ref, sl, sh_ref, tr)
            for r0 in range(0, tr, CONV_RB):
                acc = jnp.zeros((CONV_RB, LANES), F32)
                for j in range(CONV_KERNEL):
                    acc = acc + w_ref[j:j + 1, sl] * _tap_rows(sh_ref, r0 + HALO - CONV_PAD + j, CONV_RB)
                y1_ref[r0:r0 + CONV_RB, sl] = acc + b_ref[:, sl]
        zc_ref[...] = _convact_fn(y1_ref[...], gate_ref[...], lg_ref[...], lb_ref[...]).astype(BF16)

    ap, an = _halo_specs(dm, 0)
    gp, gn = _halo_specs(dm, 1)
    return pl.pallas_call(
        body, name="conv_fwd", grid=(dm.R // tr,),
        in_specs=[_rows(tr, c, 0), _rows(tr, c, 1), _rows(tr, c, 2), ap, an, gp, gn,
                  _whole((32, c)), _whole((1, c)), _whole((1, c)), _whole((1, c))],
        out_specs=(_rows(tr, c), _rows(tr, c)),
        out_shape=(jax.ShapeDtypeStruct((dm.R, c), BF16), jax.ShapeDtypeStruct((dm.R, c), F32)),
        scratch_shapes=[pltpu.VMEM((tr + 2 * HALO, c), F32), pltpu.VMEM((SUBLANES, tr + 2 * HALO, LANES), F32)],
        compiler_params=_params(("parallel",)),
    )(p, p, p, p, p, p, p, conv_w_l, conv_b_l, ln_g_l, ln_b_l)


def _conv_act_bwd(dm, dzc, y1, p, ln_g_l, ln_b_l):
    tr, c = dm.TR, dm.C

    def body(dz_ref, y1_ref, gate_ref, lg_ref, lb_ref, dy_ref, dgate_ref, dlg_ref, dlb_ref):
        i = pl.program_id(0)
        _, vjp = jax.vjp(_convact_fn, y1_ref[...], gate_ref[...], lg_ref[...], lb_ref[...])
        dy, dgate, dlg, dlb = vjp(dz_ref[...])
        dy_ref[...] = dy
        dgate_ref[...] = dgate.astype(BF16)

        @pl.when(i == 0)
        def _():
            dlg_ref[...] = jnp.zeros_like(dlg_ref)
            dlb_ref[...] = jnp.zeros_like(dlb_ref)

        dlg_ref[...] += dlg
        dlb_ref[...] += dlb

    return pl.pallas_call(
        body, name="conv_act_bwd", grid=(dm.R // tr,),
        in_specs=[_rows(tr, c), _rows(tr, c), _rows(tr, c, 2), _whole((1, c)), _whole((1, c))],
        out_specs=(_rows(tr, c), _rows(tr, c), _whole((1, c)), _whole((1, c))),
        out_shape=(jax.ShapeDtypeStruct((dm.R, c), F32), jax.ShapeDtypeStruct((dm.R, c), BF16),
                   jax.ShapeDtypeStruct((1, c), F32), jax.ShapeDtypeStruct((1, c), F32)),
        compiler_params=_params(("arbitrary",)),
    )(dzc, y1, p, ln_g_l, ln_b_l)


def _conv_bwd(dm, dy1, p, conv_w_l):
    tr, c = dm.TR, dm.C

    def body(dy_ref, dyp_ref, dyn_ref, a_ref, g_ref, ap_ref, an_ref, gp_ref, gn_ref, w_ref,
             dag_ref, dw_ref, db_ref, u_ref, dyp_scr, du_ref, dwp_ref, shu_ref, shd_ref):
        i = pl.program_id(0)
        valid = _seg_valid(dm, i)
        a = jnp.concatenate([ap_ref[...], a_ref[...], an_ref[...]], axis=0)
        g = jnp.concatenate([gp_ref[...], g_ref[...], gn_ref[...]], axis=0)
        u_ref[...] = jnp.where(valid, a * jax.nn.sigmoid(g), 0.0)
        dyp_scr[...] = jnp.where(valid, jnp.concatenate([dyp_ref[...], dy_ref[...], dyn_ref[...]], axis=0), 0.0)

        @pl.when(i == 0)
        def _():
            dwp_ref[...] = jnp.zeros_like(dwp_ref)
            db_ref[...] = jnp.zeros_like(db_ref)

        db_ref[...] += jnp.sum(dy_ref[...], axis=0, keepdims=True)
        for cc in range(c // LANES):
            sl = slice(cc * LANES, (cc + 1) * LANES)
            _shifted_copies(u_ref, sl, shu_ref, tr)
            _shifted_copies(dyp_scr, sl, shd_ref, tr)
            for r0 in range(0, tr, CONV_RB):
                dy_cur = dy_ref[r0:r0 + CONV_RB, sl]
                acc = jnp.zeros((CONV_RB, LANES), F32)
                for j in range(CONV_KERNEL):
                    acc = acc + w_ref[j:j + 1, sl] * _tap_rows(shd_ref, r0 + HALO + CONV_PAD - j, CONV_RB)
                    prod = dy_cur * _tap_rows(shu_ref, r0 + HALO - CONV_PAD + j, CONV_RB)
                    dwp_ref[j, :, sl] += jnp.sum(prod.reshape(CONV_RB // SUBLANES, SUBLANES, LANES), axis=0)
                du_ref[r0:r0 + CONV_RB, sl] = acc
        du = du_ref[...]
        sg = jax.nn.sigmoid(g_ref[...])
        dag_ref[:, :c] = (du * sg).astype(BF16)
        dag_ref[:, c:] = (du * a_ref[...] * sg * (1.0 - sg)).astype(BF16)

        @pl.when(i == dm.R // tr - 1)
        def _():
            dw_ref[...] = jnp.sum(dwp_ref[...], axis=1)

    ap, an = _halo_specs(dm, 0)
    gp, gn = _halo_specs(dm, 1)
    dyp, dyn = _halo_specs(dm, 0)
    return pl.pallas_call(
        body, name="conv_bwd", grid=(dm.R // tr,),
        in_specs=[_rows(tr, c), dyp, dyn, _rows(tr, c, 0), _rows(tr, c, 1), ap, an, gp, gn, _whole((32, c))],
        out_specs=(_rows(tr, 2 * c), _whole((32, c)), _whole((1, c))),
        out_shape=(jax.ShapeDtypeStruct((dm.R, 2 * c), BF16), jax.ShapeDtypeStruct((32, c), F32),
                   jax.ShapeDtypeStruct((1, c), F32)),
        scratch_shapes=[pltpu.VMEM((tr + 2 * HALO, c), F32), pltpu.VMEM((tr + 2 * HALO, c), F32), pltpu.VMEM((tr, c), F32),
                        pltpu.VMEM((32, SUBLANES, c), F32), pltpu.VMEM((SUBLANES, tr + 2 * HALO, LANES), F32),
                        pltpu.VMEM((SUBLANES, tr + 2 * HALO, LANES), F32)],
        compiler_params=_params(("arbitrary",)),
    )(dy1, dy1, dy1, p, p, p, p, p, p, conv_w_l)


def _gm_specs(dm):
    half = dm.D // 2
    first = _col(3 * dm.C + 2 * dm.A + 2 * dm.KVW, half)
    return [_rows(dm.TR, half, first + k) for k in range(4)]


def _gm_load(a0, a1, b0, b1):
    return jnp.concatenate([a0[...], a1[...]], axis=1), jnp.concatenate([b0[...], b1[...]], axis=1)


def _merge(dm, yc, ya, p):
    tr, d = dm.TR, dm.D

    def body(yc_ref, ya_ref, a0, a1, b0, b1, m_ref):
        ga, gb = _gm_load(a0, a1, b0, b1)
        m_ref[...] = (jax.nn.sigmoid(ga) * yc_ref[...] + jax.nn.sigmoid(gb) * ya_ref[...]).astype(BF16)

    return pl.pallas_call(
        body, name="merge", grid=(dm.R // tr,),
        in_specs=[_rows(tr, d), _rows(tr, d)] + _gm_specs(dm), out_specs=_rows(tr, d),
        out_shape=jax.ShapeDtypeStruct((dm.R, d), BF16), compiler_params=_params(("parallel",)),
    )(yc, ya, p, p, p, p)


def _merge_bwd(dm, dmerged, yc, ya, p):
    tr, d = dm.TR, dm.D

    def body(dm_ref, yc_ref, ya_ref, a0, a1, b0, b1, dyc_ref, dya_ref, dgm_ref):
        dmv = dm_ref[...]
        ga, gb = _gm_load(a0, a1, b0, b1)
        sa, sb = jax.nn.sigmoid(ga), jax.nn.sigmoid(gb)
        dyc_ref[...] = (dmv * sa).astype(BF16)
        dya_ref[...] = (dmv * sb).astype(BF16)
        dgm_ref[:, :d] = (dmv * yc_ref[...] * sa * (1.0 - sa)).astype(BF16)
        dgm_ref[:, d:] = (dmv * ya_ref[...] * sb * (1.0 - sb)).astype(BF16)

    return pl.pallas_call(
        body, name="merge_bwd", grid=(dm.R // tr,),
        in_specs=[_rows(tr, d), _rows(tr, d), _rows(tr, d)] + _gm_specs(dm),
        out_specs=(_rows(tr, d), _rows(tr, d), _rows(tr, 2 * d)),
        out_shape=(jax.ShapeDtypeStruct((dm.R, d), BF16), jax.ShapeDtypeStruct((dm.R, d), BF16),
                   jax.ShapeDtypeStruct((dm.R, 2 * d), BF16)),
        compiler_params=_params(("parallel",)),
    )(dmerged, yc, ya, p, p, p, p)


def _loss_head(dm, xc, target):
    tr, d = dm.TR, dm.D
    n_lat = dm.N // tr

    def body(x_ref, t_ref, dy_ref, l_ref):
        i = pl.program_id(0)

        @pl.when(i == 0)
        def _():
            l_ref[...] = jnp.zeros_like(l_ref)

        @pl.when(i < n_lat)
        def _():
            e = x_ref[...] - t_ref[...]
            dy_ref[...] = e * (1.0 / d)
            l_ref[...] += 0.5 * jnp.sum(jnp.mean(e * e, axis=-1, keepdims=True), axis=0, keepdims=True)

        @pl.when(i >= n_lat)
        def _():
            dy_ref[...] = jnp.zeros_like(dy_ref)

    return pl.pallas_call(
        body, name="loss_head", grid=(dm.R // tr,),
        in_specs=[_rows(tr, d), pl.BlockSpec((tr, d), lambda i: (jnp.minimum(i, n_lat - 1), 0))],
        out_specs=(_rows(tr, d), _whole((1, 1))),
        out_shape=(jax.ShapeDtypeStruct((dm.R, d), F32), jax.ShapeDtypeStruct((1, 1), F32)),
        compiler_params=_params(("arbitrary",)),
    )(xc, target)


def _adamw(w, g, m, v, name):
    rows, cols = w.shape
    fits = lambda t: t * cols * 4 <= ADAM_BLOCK_BYTES
    tr = rows if fits(rows) else _pick(rows, [t for t in (512, 256, 128, 64, 32, 16, 8) if fits(t)])

    def body(w_ref, g_ref, m_ref, v_ref, d_ref, mo_ref, vo_ref):
        d_ref[...], mo_ref[...], vo_ref[...] = _adam_update(w_ref[...], g_ref[...], m_ref[...], v_ref[...])

    spec = _rows(tr, cols)
    shp = jax.ShapeDtypeStruct(w.shape, F32)
    return pl.pallas_call(
        body, name=name, grid=(rows // tr,), in_specs=[spec] * 4, out_specs=(spec,) * 3, out_shape=(shp,) * 3,
        compiler_params=_params(("parallel",)),
    )(w, g, m, v)


def _adam_update(w, g, m, v):
    mn = ADAM_B1 * m + (1.0 - ADAM_B1) * g
    vn = ADAM_B2 * v + (1.0 - ADAM_B2) * jnp.square(g)
    m_hat = mn / (1.0 - ADAM_B1 ** ADAM_STEP)
    v_hat = vn / (1.0 - ADAM_B2 ** ADAM_STEP)
    return -ADAM_LR * (m_hat / (jnp.sqrt(v_hat) + ADAM_EPS) + ADAM_WD * w), mn, vn


def _rope_tables(dm):
    n = dm.N
    rows = n // GRID_W
    row = jnp.repeat(jnp.arange(rows, dtype=F32), GRID_W)
    col = jnp.tile(jnp.arange(GRID_W, dtype=F32), rows)
    half = dm.HD // 2
    inv_freq = ROPE_THETA ** (-jnp.arange(0, half, 2, dtype=F32) / half)
    ang = jnp.concatenate([row[:, None] * inv_freq, col[:, None] * inv_freq], axis=-1)
    cos, sin = jnp.cos(ang), jnp.sin(ang)
    cos = jnp.concatenate([cos, jnp.ones((dm.CTX, half), F32)], axis=0)
    sin = jnp.concatenate([sin, jnp.zeros((dm.CTX, half), F32)], axis=0)
    reps = LANES // dm.HD
    return jnp.tile(jnp.concatenate([cos, cos], axis=1), (1, reps)), jnp.tile(jnp.concatenate([-sin, sin], axis=1), (1, reps))


def _lane_vec(g, hd):
    return jnp.tile(g.reshape(1, hd), (1, LANES // hd))


def _layer_fwd(dm, xc, cv, wl, tabs):
    mod = _mod_fwd(cv, wl["w_mod"], wl["b_mod"])
    h = _prenorm(dm, xc, wl["g_pre"], mod)
    p = _matmul(h, wl["w_in"], "nn", "in_proj")
    qh, qt, kh, vh, vx = _qk_prep(dm, p, tabs["cos"], tabs["sins"], wl["gq"], wl["gk"], tabs["bd"])
    o, lse = _flash_fwd(dm, qh, kh, vx)
    za = _attn_gate(dm, o, p)
    zc, y1 = _conv_fwd(dm, p, wl["conv_w"], wl["conv_b"], wl["ln_g"], wl["ln_b"])
    yc = _matmul(zc, wl["w_conv_out"], "nn", "conv_out")
    ya = _matmul(za, wl["w_attn_out"], "nn", "attn_out")
    merged = _merge(dm, yc, ya, p)
    outp = _matmul(merged, wl["w_out"], "nn", "out_proj")
    x_new = _post(dm, xc, outp, wl["g_post"], mod)
    saved = dict(xc=xc, mod=mod, h=h, p=p, qh=qh, qt=qt, kh=kh, vh=vh, o=o, lse=lse, za=za, zc=zc, y1=y1, yc=yc, ya=ya,
                 merged=merged, outp=outp)
    return x_new, saved


def _layer_bwd(dm, dxn, cv, wl, tabs, s):
    p = s["p"]
    d_outp, dg_post, dgt = _post_bwd(dm, s["outp"], wl["g_post"], s["mod"], dxn)
    dmerged = _matmul(d_outp, wl["w_out"], "nt", "out_proj_dx")
    dw_out = _matmul(s["merged"], d_outp, "tn", "out_proj_dw")
    dyc, dya, dgm = _merge_bwd(dm, dmerged, s["yc"], s["ya"], p)
    dzc = _matmul(dyc, wl["w_conv_out"], "nt", "conv_out_dx")
    dw_conv_out = _matmul(s["zc"], dyc, "tn", "conv_out_dw")
    dza = _matmul(dya, wl["w_attn_out"], "nt", "attn_out_dx")
    dw_attn_out = _matmul(s["za"], dya, "tn", "attn_out_dw")
    doh, dot, delta, dgate_b = _attn_gate_bwd(dm, dza, s["o"], p)
    dq, dkt, dvt = _flash_bwd(dm, s["qh"], s["qt"], s["kh"], s["vh"], doh, dot, s["lse"], delta)
    dpq, dpkv, dgq, dgk = _qk_bwd(dm, p, dq, dkt, dvt, tabs["cos"], tabs["sins"], wl["gq"], wl["gk"], tabs["bd"])
    dy1, dgate_a, dln_g, dln_b = _conv_act_bwd(dm, dzc, s["y1"], p, wl["ln_g"], wl["ln_b"])
    dpag, dconv_w, dconv_b = _conv_bwd(dm, dy1, p, wl["conv_w"])
    dp = jnp.concatenate([dpag, dgate_a, dpq, dpkv, dgate_b, dgm], axis=1)
    dh = _matmul(dp, wl["w_in"], "nt", "in_proj_dx")
    dw_in = _matmul(s["h"], dp, "tn", "in_proj_dw")
    dxc, dg_pre, dshsc = _prenorm_bwd(dm, s["xc"], wl["g_pre"], s["mod"], dh, dxn)
    dmod = jnp.concatenate([dshsc, dgt], axis=1)
    dw_mod, db_mod, dcv = _mod_bwd(cv, wl["w_mod"], dmod)
    grads = dict(w_mod=dw_mod, b_mod=db_mod[0], g_pre=dg_pre[0], g_post=dg_post[0], w_in=dw_in,
                 conv_w=dconv_w[:CONV_KERNEL], conv_b=dconv_b[0], ln_g=dln_g[0], ln_b=dln_b[0],
                 w_conv_out=dw_conv_out, q_norm_g=dgq[0, :dm.HD], k_norm_g=dgk[0, :dm.HD],
                 w_attn_out=dw_attn_out, w_out=dw_out, c_ctx=dcv[1])
    return dxc, grads


def _local_step(dm, x, c, ctx, c_ctx, target, wf):
    depth = wf["w_in"].shape[0]
    cos, sins = _rope_tables(dm)
    lane = np.arange(LANES)
    bd = jnp.asarray((lane[:, None] // dm.HD == lane[None, :] // dm.HD), dtype=BF16)
    tabs = dict(cos=cos, sins=sins, bd=bd)
    cv = jnp.concatenate([c.reshape(1, dm.D), c_ctx.reshape(1, dm.D), jnp.zeros((6, dm.D), F32)], axis=0)
    xc = jnp.concatenate([x, ctx], axis=0)
    layers, saved = [], []
    for l in range(depth):
        wl = dict(
            w_mod=wf["w_mod"][l], b_mod=wf["b_mod"][l].reshape(1, -1), g_pre=wf["g_pre"][l].reshape(1, -1),
            g_post=wf["g_post"][l].reshape(1, -1), w_in=wf["w_in"][l],
            conv_w=jnp.concatenate([wf["conv_w"][l], jnp.zeros((1, dm.C), F32)], axis=0),
            conv_b=wf["conv_b"][l].reshape(1, -1), ln_g=wf["ln_g"][l].reshape(1, -1), ln_b=wf["ln_b"][l].reshape(1, -1),
            w_conv_out=wf["w_conv_out"][l], gq=_lane_vec(wf["q_norm_g"][l], dm.HD), gk=_lane_vec(wf["k_norm_g"][l], dm.HD),
            w_attn_out=wf["w_attn_out"][l], w_out=wf["w_out"][l])
        layers.append(wl)
        xc, s = _layer_fwd(dm, xc, cv, wl, tabs)
        saved.append(s)
    dxc, loss = _loss_head(dm, xc, target)
    grads = [None] * depth
    for l in reversed(range(depth)):
        dxc, grads[l] = _layer_bwd(dm, dxc, cv, layers[l], tabs, saved[l])
    big = ("w_mod", "w_in", "w_conv_out", "w_attn_out", "w_out")
    gw = {k: [g[k] for g in grads] if k in big else jnp.stack([g[k] for g in grads]) for k in grads[0] if k != "c_ctx"}
    gw["c_ctx"] = grads[0]["c_ctx"] + grads[1]["c_ctx"] if depth == 2 else sum(g["c_ctx"] for g in grads)
    return loss[0, 0], dxc[:dm.N], gw


def _dims(x, ctx, w_in_full_cols, conv_c, attn_w, hd):
    n, d = x.shape
    ctx_len = ctx.shape[0]
    c, a = conv_c, attn_w
    kvw = (w_in_full_cols - 3 * c - 2 * a - 2 * d) // 2
    r = n + ctx_len
    tr = _pick(np.gcd(n, ctx_len), (256, 128))
    return Dims(N=n, CTX=ctx_len, R=r, D=d, C=c, A=a, KVW=kvw, HD=hd, NH=a // hd, NKV=kvw // hd,
                G=a // kvw, W=w_in_full_cols, TR=tr)


_HBM = pl.BlockSpec(memory_space=pltpu.HBM)
_VMEM = pl.BlockSpec(memory_space=pltpu.VMEM)


def _place():
    x, y, c = lax.axis_index("x"), lax.axis_index("y"), lax.axis_index("c")
    chips = [(1 - x, y), (x, 1 - y), (1 - x, 1 - y)]
    return x, y, c, chips


def _row_pieces(rows, want):
    n = max(k for k in range(1, want + 1) if rows % (16 * k) == 0)
    return [pl.ds(i * (rows // n), rows // n) for i in range(n)]


def _allgather_chips(own, name):
    _, h, w = own.shape
    pieces = _row_pieces(h, 4)
    npc = len(pieces)

    def body(own_ref, out_ref, send_sems, recv_sems):
        x, y, c, chips = _place()
        s_me = 2 * x + y
        sibling = (x, y, 1 - c)

        def copy(k, src, dst, to):
            return pltpu.make_async_remote_copy(src_ref=src, dst_ref=dst, send_sem=send_sems.at[k], recv_sem=recv_sems.at[k],
                                                device_id=to, device_id_type=MESH)

        sends = []
        for i, rows in enumerate(pieces):
            for j, chip in enumerate(chips):
                sends.append(copy(i * 3 + j, own_ref.at[c, rows], out_ref.at[s_me, c, rows], (*chip, c)))
        own_slots = [(6 * npc + hf * npc + i, hf, rows) for hf in range(2) for i, rows in enumerate(pieces)]
        for k, hf, rows in own_slots:
            sends.append(copy(k, own_ref.at[hf, rows], out_ref.at[s_me, hf, rows], sibling))
        for cp in sends:
            cp.start()
        for i, rows in enumerate(pieces):
            for j, (cx, cy) in enumerate(chips):
                landed = out_ref.at[2 * cx + cy, c, rows]
                copy(i * 3 + j, landed, landed, (cx, cy, c)).wait_recv()
                fwd = copy(3 * npc + i * 3 + j, landed, landed, sibling)
                fwd.start()
                sends.append(fwd)
        for i, rows in enumerate(pieces):
            for j, (cx, cy) in enumerate(chips):
                theirs = out_ref.at[2 * cx + cy, 1 - c, rows]
                copy(3 * npc + i * 3 + j, theirs, theirs, sibling).wait_recv()
        for k, hf, rows in own_slots:
            mine = out_ref.at[s_me, hf, rows]
            copy(k, mine, mine, sibling).wait_recv()
        for cp in sends:
            cp.wait_send()

    return pl.pallas_call(
        body, name=name, in_specs=[_HBM], out_specs=_HBM,
        out_shape=jax.ShapeDtypeStruct((N_CHIPS, 2, h, w), own.dtype),
        scratch_shapes=[pltpu.SemaphoreType.DMA((8 * npc,)), pltpu.SemaphoreType.DMA((8 * npc,))],
    )(own)


def _pair_exchange(buf, name):
    _, m, w = buf.shape
    pieces = _row_pieces(m, 16)

    def body(buf_ref, out_ref, send_sems, recv_sems):
        x, y, c, _ = _place()
        cps = [pltpu.make_async_remote_copy(src_ref=buf_ref.at[1 - c, rows], dst_ref=out_ref.at[rows], send_sem=send_sems.at[i],
                                            recv_sem=recv_sems.at[i], device_id=(x, y, 1 - c), device_id_type=MESH)
               for i, rows in enumerate(pieces)]
        for cp in cps:
            cp.start()
        for cp in cps:
            cp.wait()

    return pl.pallas_call(
        body, name=name, in_specs=[_HBM], out_specs=_HBM, out_shape=jax.ShapeDtypeStruct((m, w), buf.dtype),
        scratch_shapes=[pltpu.SemaphoreType.DMA((len(pieces),)), pltpu.SemaphoreType.DMA((len(pieces),))],
    )(buf)


def _pair_swap(t, name):
    m, w = t.shape
    pieces = _row_pieces(m, 8)
    n = len(pieces)

    def body(t_ref, out_ref, send_sems, recv_sems):
        x, y, c, _ = _place()
        cps = [pltpu.make_async_remote_copy(src_ref=t_ref.at[rows], dst_ref=out_ref.at[rows], send_sem=send_sems.at[i],
                                            recv_sem=recv_sems.at[i], device_id=(x, y, 1 - c), device_id_type=MESH)
               for i, rows in enumerate(pieces)]
        for cp in cps:
            cp.start()
        for cp in cps:
            cp.wait()

    return pl.pallas_call(
        body, name=name, in_specs=[_HBM], out_specs=_HBM, out_shape=jax.ShapeDtypeStruct((m, w), t.dtype),
        scratch_shapes=[pltpu.SemaphoreType.DMA((n,)), pltpu.SemaphoreType.DMA((n,))],
    )(t)


def _scatter_chips(s, name):
    _, h, w = s.shape

    def body(s_ref, out_ref, send_sems, recv_sems):
        _, _, c, chips = _place()
        cps = [pltpu.make_async_remote_copy(src_ref=s_ref.at[2 * cx + cy], dst_ref=out_ref.at[j], send_sem=send_sems.at[j],
                                            recv_sem=recv_sems.at[j], device_id=(cx, cy, c), device_id_type=MESH)
               for j, (cx, cy) in enumerate(chips)]
        for cp in cps:
            cp.start()
        for cp in cps:
            cp.wait()

    return pl.pallas_call(
        body, name=name, in_specs=[_HBM], out_specs=_HBM, out_shape=jax.ShapeDtypeStruct((3, h, w), s.dtype),
        scratch_shapes=[pltpu.SemaphoreType.DMA((3,)), pltpu.SemaphoreType.DMA((3,))],
    )(s)


def _sum_pair(buf, recv, core, name):
    m, w = recv.shape
    tr = _pick(m, (512, 432, 256, 128, 64, 32, 16))

    def body(c_ref, b_ref, r_ref, o_ref, ob_ref):
        v = b_ref[0] + r_ref[...]
        o_ref[...] = v
        ob_ref[...] = v.astype(BF16)

    spec = pl.BlockSpec((tr, w), lambda i, c: (i, 0))
    return pl.pallas_call(
        body, name=name,
        grid_spec=pltpu.PrefetchScalarGridSpec(
            num_scalar_prefetch=1, grid=(m // tr,),
            in_specs=[pl.BlockSpec((1, tr, w), lambda i, c: (c[0], i, 0)), spec], out_specs=(spec, spec)),
        out_shape=(jax.ShapeDtypeStruct((m, w), F32), jax.ShapeDtypeStruct((m, w), BF16)),
        compiler_params=_params(("parallel",)),
    )(core, buf, recv)


def _sum_chips(own, recv, chip, name):
    _, h, w = own.shape
    tr = _pick(h, (512, 432, 256, 128, 64, 32, 16))

    def body(s_ref, a_ref, r_ref, o_ref):
        acc = a_ref[0]
        for j in range(3):
            acc = acc + r_ref[j].astype(F32)
        o_ref[...] = acc

    return pl.pallas_call(
        body, name=name,
        grid_spec=pltpu.PrefetchScalarGridSpec(
            num_scalar_prefetch=1, grid=(h // tr,),
            in_specs=[pl.BlockSpec((1, tr, w), lambda i, s: (s[0], i, 0)), pl.BlockSpec((3, tr, w), lambda i, s: (0, i, 0))],
            out_specs=pl.BlockSpec((tr, w), lambda i, s: (i, 0))),
        out_shape=jax.ShapeDtypeStruct((h, w), F32), compiler_params=_params(("parallel",)),
    )(chip, own, recv)


def _allreduce_small(own, name):
    m, w = own.shape

    def body(own_ref, sum_ref, all_ref, send_sems, recv_sems):
        x, y, c, _ = _place()
        me = 4 * x + 2 * y + c
        all_ref[me] = own_ref[...]
        sends = []
        for k in range(1, N_DEV):
            peer = (x ^ ((k >> 2) & 1), y ^ ((k >> 1) & 1), c ^ (k & 1))
            cp = pltpu.make_async_remote_copy(src_ref=own_ref, dst_ref=all_ref.at[me], send_sem=send_sems.at[k - 1],
                                              recv_sem=recv_sems.at[k - 1], device_id=peer, device_id_type=MESH)
            cp.start()
            sends.append(cp)
        for k in range(1, N_DEV):
            px, py, pc = x ^ ((k >> 2) & 1), y ^ ((k >> 1) & 1), c ^ (k & 1)
            slot = all_ref.at[4 * px + 2 * py + pc]
            pltpu.make_async_remote_copy(src_ref=slot, dst_ref=slot, send_sem=send_sems.at[k - 1], recv_sem=recv_sems.at[k - 1],
                                         device_id=(px, py, pc), device_id_type=MESH).wait_recv()
        for cp in sends:
            cp.wait_send()
        acc = all_ref[0]
        for d in range(1, N_DEV):
            acc = acc + all_ref[d]
        sum_ref[...] = acc

    return pl.pallas_call(
        body, name=name, in_specs=[_VMEM], out_specs=_VMEM, out_shape=jax.ShapeDtypeStruct((m, w), F32),
        scratch_shapes=[pltpu.VMEM((N_DEV, m, w), F32), pltpu.SemaphoreType.DMA((N_DEV - 1,)), pltpu.SemaphoreType.DMA((N_DEV - 1,))],
        compiler_params=pltpu.CompilerParams(vmem_limit_bytes=VMEM_LIMIT),
    )(own)


_COL_SHARDED = ("w_mod", "w_in")
_ROW_SHARDED = ("w_conv_out", "w_attn_out", "w_out")
_BIG = _COL_SHARDED + _ROW_SHARDED
_SMALL = ("c_ctx", "b_mod", "g_pre", "g_post", "conv_b", "ln_g", "ln_b", "q_norm_g", "k_norm_g")
FLAT_W = 1024


def _unflatten_shard(flat, like):
    out, off = {}, 0
    flat = flat.reshape(-1)
    for k in _BIG:
        n = int(np.prod(like[k].shape))
        out[k] = flat[off:off + n].reshape(like[k].shape)
        off += n
    return out


def _by_shard(k, layers):
    parts = []
    for full in layers:
        if k in _COL_SHARDED:
            d, cols = full.shape
            parts.append(full.reshape(d, N_CHIPS, cols // N_CHIPS).transpose(1, 0, 2).reshape(N_CHIPS, -1))
        else:
            parts.append(full.reshape(N_CHIPS, -1))
    return jnp.concatenate(parts, axis=1)


def _from_shards(k, flat4, like):
    l = like.shape[0]
    if k in _COL_SHARDED:
        d, cs = like.shape[1:]
        return flat4.reshape(N_CHIPS, l, d, cs).transpose(1, 2, 0, 3).reshape(l, d, N_CHIPS * cs)
    rs, d = like.shape[1:]
    return flat4.reshape(N_CHIPS, l, rs, d).transpose(1, 0, 2, 3).reshape(l, N_CHIPS * rs, d)


def _pad_rows(flat, mult=8):
    n = flat.shape[0]
    rows = -(-n // LANES)
    rows = -(-rows // mult) * mult
    return jnp.concatenate([flat, jnp.zeros((rows * LANES - n,), F32)]).reshape(rows, LANES)


def kernel(x, c, ctx, c_ctx, w_mod, b_mod, g_pre, g_post, w_in, conv_w, conv_b, ln_g, ln_b, w_conv_out, q_norm_g, k_norm_g, w_attn_out, w_out, loss_target, m_c_ctx, m_w_mod, m_b_mod, m_g_pre, m_g_post, m_w_in, m_conv_w, m_conv_b, m_ln_g, m_ln_b, m_w_conv_out, m_q_norm_g, m_k_norm_g, m_w_attn_out, m_w_out, v_c_ctx, v_w_mod, v_b_mod, v_g_pre, v_g_post, v_w_in, v_conv_w, v_conv_b, v_ln_g, v_ln_b, v_w_conv_out, v_q_norm_g, v_k_norm_g, v_w_attn_out, v_w_out):
    w = dict(c_ctx=c_ctx, w_mod=w_mod, b_mod=b_mod, g_pre=g_pre, g_post=g_post, w_in=w_in, conv_w=conv_w, conv_b=conv_b,
             ln_g=ln_g, ln_b=ln_b, w_conv_out=w_conv_out, q_norm_g=q_norm_g, k_norm_g=k_norm_g, w_attn_out=w_attn_out, w_out=w_out)
    m = dict(c_ctx=m_c_ctx, w_mod=m_w_mod, b_mod=m_b_mod, g_pre=m_g_pre, g_post=m_g_post, w_in=m_w_in, conv_w=m_conv_w,
             conv_b=m_conv_b, ln_g=m_ln_g, ln_b=m_ln_b, w_conv_out=m_w_conv_out, q_norm_g=m_q_norm_g, k_norm_g=m_k_norm_g,
             w_attn_out=m_w_attn_out, w_out=m_w_out)
    v = dict(c_ctx=v_c_ctx, w_mod=v_w_mod, b_mod=v_b_mod, g_pre=v_g_pre, g_post=v_g_post, w_in=v_w_in, conv_w=v_conv_w,
             conv_b=v_conv_b, ln_g=v_ln_g, ln_b=v_ln_b, w_conv_out=v_w_conv_out, q_norm_g=v_q_norm_g, k_norm_g=v_k_norm_g,
             w_attn_out=v_w_attn_out, w_out=v_w_out)
    order = ("c_ctx", "w_mod", "b_mod", "g_pre", "g_post", "w_in", "conv_w", "conv_b", "ln_g", "ln_b", "w_conv_out",
             "q_norm_g", "k_norm_g", "w_attn_out", "w_out")
    my_chip = 2 * lax.axis_index("x") + lax.axis_index("y")
    my_core = lax.axis_index("c")
    depth, _, conv_cs = conv_w.shape
    conv_c = conv_cs * N_CHIPS
    dm = _dims(x[0], ctx[0], w_in.shape[2] * N_CHIPS, conv_c, w_attn_out.shape[1] * N_CHIPS, q_norm_g.shape[1])

    own = jnp.concatenate([w[k].astype(BF16).reshape(-1) for k in _BIG])
    rows2 = own.shape[0] // FLAT_W
    own = own.reshape(2, rows2 // 2, FLAT_W)
    gathered = _allgather_chips(own, "gather_weights").reshape(N_CHIPS, -1)
    wf, off = {}, 0
    for k in _BIG:
        n = int(np.prod(w[k].shape))
        wf[k] = _from_shards(k, gathered[:, off:off + n], w[k])
        off += n
    cw = jnp.zeros((depth, CONV_KERNEL, N_CHIPS, conv_cs), F32)
    cw = lax.dynamic_update_slice(cw, conv_w[:, :, None, :], (0, 0, my_chip, 0))
    cw = jnp.where(my_core == 0, cw, 0.0).reshape(-1)
    wf["conv_w"] = _allreduce_small(_pad_rows(cw), "gather_conv_w").reshape(-1)[:cw.shape[0]].reshape(depth, CONV_KERNEL, conv_c)
    for k in _SMALL:
        wf[k] = w[k]

    loss_part, grad_x, gw = _local_step(dm, x[0], c, ctx[0], c_ctx, loss_target[0], wf)

    g4 = jnp.concatenate([_by_shard(k, gw[k]) for k in _BIG], axis=1)
    hrows = rows2 // 2
    g24 = g4.reshape(N_CHIPS, 2, hrows, FLAT_W).transpose(1, 0, 2, 3).reshape(2, N_CHIPS * hrows, FLAT_W)
    core_i = my_core.astype(jnp.int32).reshape(1)
    chip_i = my_chip.astype(jnp.int32).reshape(1)
    theirs = _pair_exchange(g24, "reduce_pair")
    chip_sum, chip_sum_bf = _sum_pair(g24, theirs, core_i, "reduce_pair_add")
    others = _scatter_chips(chip_sum_bf.reshape(N_CHIPS, hrows, FLAT_W), "reduce_chips")
    mine = _sum_chips(chip_sum.reshape(N_CHIPS, hrows, FLAT_W), others, chip_i, "reduce_chips_add")
    sibling_half = _pair_swap(mine, "reduce_share")

    small = jnp.concatenate([gw[k].reshape(-1) for k in _SMALL] + [gw["conv_w"].reshape(-1), loss_part.reshape(1)])
    small_sum = _allreduce_small(_pad_rows(small), "reduce_small").reshape(-1)
    loss = small_sum[small.shape[0] - 1]
    gs, off = {}, 0
    for k in _SMALL:
        n = int(np.prod(w[k].shape))
        gs[k] = small_sum[off:off + n].reshape(w[k].shape)
        off += n
    gcw = small_sum[off:off + depth * CONV_KERNEL * conv_c].reshape(depth, CONV_KERNEL, N_CHIPS, conv_cs)
    gs["conv_w"] = lax.dynamic_index_in_dim(gcw, my_chip, axis=2, keepdims=False)

    g_flat = jnp.where(my_core == 0, jnp.concatenate([mine, sibling_half]), jnp.concatenate([sibling_half, mine]))
    grads = _unflatten_shard(g_flat, w)
    delta, new_m, new_v = {}, {}, {}
    for k in _BIG:
        as2d = lambda t: t.reshape(-1, t.shape[-1])
        d_k, m_k, v_k = _adamw(as2d(w[k]), as2d(grads[k]), as2d(m[k]), as2d(v[k]), "adamw_" + k)
        delta[k], new_m[k], new_v[k] = d_k.reshape(w[k].shape), m_k.reshape(w[k].shape), v_k.reshape(w[k].shape)
    small_keys = _SMALL + ("conv_w",)
    join = lambda t: _pad_rows(jnp.concatenate([t[k].reshape(-1) for k in small_keys]))
    d_s, m_s, v_s = _adamw(join(w), join(gs), join(m), join(v), "adamw_small")
    off = 0
    for k in small_keys:
        n = int(np.prod(w[k].shape))
        grads[k] = gs[k]
        for dst, src in ((delta, d_s), (new_m, m_s), (new_v, v_s)):
            dst[k] = src.reshape(-1)[off:off + n].reshape(w[k].shape)
        off += n
    return (loss, grad_x[None], *[grads[k] for k in order], *[delta[k] for k in order],
            *[new_m[k] for k in order], *[new_v[k] for k in order])
```

```python
import functools
from typing import NamedTuple

import jax
import jax.numpy as jnp
import numpy as np
from jax import lax
from jax.experimental import pallas as pl
from jax.experimental.pallas import tpu as pltpu

F32 = jnp.float32
BF16 = jnp.bfloat16
EPS = 1e-6
GRID_W = 64
ROPE_THETA = 10000.0
CONV_KERNEL = 31
CONV_PAD = CONV_KERNEL // 2
HALO = 16
LANES = 128
SUBLANES = 8
ADAM_LR, ADAM_B1, ADAM_B2, ADAM_EPS, ADAM_WD, ADAM_STEP = 0.001, 0.9, 0.999, 1e-08, 0.01, 10
VMEM_LIMIT = 56 * 1024 * 1024
ADAM_BLOCK_BYTES = 1024 * 1024
LOG2E = 1.4426950408889634
LN2 = 0.6931471805599453
MESH = pl.DeviceIdType.MESH
N_CHIPS = 4
N_DEV = 8


class Dims(NamedTuple):
    N: int
    CTX: int
    R: int
    D: int
    C: int
    A: int
    KVW: int
    HD: int
    NH: int
    NKV: int
    G: int
    W: int
    TR: int


def _pick(n, prefs):
    for p in prefs:
        if n % p == 0:
            return p
    raise ValueError(f"no tile for {n} in {prefs}")


def _params(sem):
    return pltpu.CompilerParams(dimension_semantics=sem, vmem_limit_bytes=VMEM_LIMIT)


def _rows(tr, w, col=0):
    return pl.BlockSpec((tr, w), lambda i, col=col: (i, col))


def _whole(shape):
    return pl.BlockSpec(shape, lambda *_: (0,) * len(shape))


def _col(off, w):
    assert off % w == 0, (off, w)
    return off // w


_DN = {"nn": (((1,), (0,)), ((), ())), "nt": (((1,), (1,)), ((), ())), "tn": (((0,), (0,)), ((), ()))}


def _matmul(a, b, mode, name):
    if mode == "nn":
        (m, k), n = a.shape, b.shape[1]
    elif mode == "nt":
        (m, k), n = a.shape, b.shape[0]
    else:
        (k, m), n = a.shape, b.shape[1]
    tm = _pick(m, (1408, 1024, 768, 512, 256, 128, 8))
    tn = _pick(n, (1280, 1024, 768, 512, 256, 128))
    tk = _pick(k, (1536, 1408, 1280, 1024, 768, 512, 256, 128))
    nk = k // tk
    dn = _DN[mode]

    def body(a_ref, b_ref, o_ref):
        part = lax.dot_general(a_ref[...], b_ref[...], dn, preferred_element_type=F32)
        if nk == 1:
            o_ref[...] = part
            return
        kk = pl.program_id(2)

        @pl.when(kk == 0)
        def _():
            o_ref[...] = part

        @pl.when(kk > 0)
        def _():
            o_ref[...] += part

    a_spec = pl.BlockSpec((tk, tm), lambda i, j, kk: (kk, i)) if mode == "tn" else pl.BlockSpec((tm, tk), lambda i, j, kk: (i, kk))
    b_spec = pl.BlockSpec((tn, tk), lambda i, j, kk: (j, kk)) if mode == "nt" else pl.BlockSpec((tk, tn), lambda i, j, kk: (kk, j))
    return pl.pallas_call(
        body, name=name, grid=(m // tm, n // tn, nk),
        in_specs=[a_spec, b_spec], out_specs=pl.BlockSpec((tm, tn), lambda i, j, kk: (i, j)),
        out_shape=jax.ShapeDtypeStruct((m, n), F32),
        compiler_params=_params(("parallel", "parallel", "arbitrary")),
    )(a, b)


def _rms(x, g):
    return x * lax.rsqrt(jnp.mean(x * x, axis=-1, keepdims=True) + EPS) * g


def _prenorm_fn(x, g, sc, sh):
    return _rms(x, g) * (1.0 + sc) + sh


def _post_fn(outp, g, gt):
    return gt * _rms(outp, g)


def _convact_fn(y1, gate, ln_g, ln_b):
    mu = jnp.mean(y1, axis=-1, keepdims=True)
    var = jnp.mean(jnp.square(y1 - mu), axis=-1, keepdims=True)
    y = (y1 - mu) * lax.rsqrt(var + EPS) * ln_g + ln_b
    return jax.nn.silu(y) * jax.nn.silu(gate)


def _dsilu(x):
    s = jax.nn.sigmoid(x)
    return s * (1.0 + x * (1.0 - s))


def _seg_select(is_ctx, mod_ref, lo, w):
    return jnp.where(is_ctx, mod_ref[1:2, lo:lo + w], mod_ref[0:1, lo:lo + w])


def _seg_rows(is_ctx, v):
    row = lax.broadcasted_iota(jnp.int32, (8, v.shape[1]), 0)
    return jnp.where(row == is_ctx.astype(jnp.int32), v, 0.0)


def _mod_fwd(cv, w_mod_l, b_mod_l):
    def body(cv_ref, w_ref, b_ref, o_ref):
        s = jax.nn.silu(cv_ref[...]).astype(BF16)
        o_ref[...] = jnp.dot(s, w_ref[...], preferred_element_type=F32) + b_ref[...]

    return pl.pallas_call(
        body, name="mod_fwd", out_shape=jax.ShapeDtypeStruct((8, w_mod_l.shape[1]), F32),
        compiler_params=pltpu.CompilerParams(vmem_limit_bytes=VMEM_LIMIT),
    )(cv, w_mod_l, b_mod_l)


def _mod_bwd(cv, w_mod_l, dmod):
    d = cv.shape[1]

    def body(cv_ref, w_ref, dm_ref, dw_ref, db_ref, dcv_ref):
        cvv = cv_ref[...]
        s = jax.nn.silu(cvv).astype(BF16)
        dm = dm_ref[...]
        dmb = dm.astype(BF16)
        dw_ref[...] = lax.dot_general(s, dmb, _DN["tn"], preferred_element_type=F32)
        db_ref[...] = jnp.sum(dm, axis=0, keepdims=True)
        ds = lax.dot_general(dmb, w_ref[...], _DN["nt"], preferred_element_type=F32)
        dcv_ref[...] = ds * _dsilu(cvv)

    return pl.pallas_call(
        body, name="mod_bwd",
        out_shape=(jax.ShapeDtypeStruct(w_mod_l.shape, F32), jax.ShapeDtypeStruct((1, w_mod_l.shape[1]), F32),
                   jax.ShapeDtypeStruct((8, d), F32)),
        compiler_params=pltpu.CompilerParams(vmem_limit_bytes=VMEM_LIMIT),
    )(cv, w_mod_l, dmod)


def _prenorm(dm, xc, g_pre_l, mod):
    tr, d, n = dm.TR, dm.D, dm.N

    def body(x_ref, g_ref, mod_ref, h_ref):
        is_ctx = pl.program_id(0) * tr >= n
        sh = _seg_select(is_ctx, mod_ref, 0, d)
        sc = _seg_select(is_ctx, mod_ref, d, d)
        h_ref[...] = _prenorm_fn(x_ref[...], g_ref[...], sc, sh).astype(BF16)

    return pl.pallas_call(
        body, name="prenorm", grid=(dm.R // tr,),
        in_specs=[_rows(tr, d), _whole((1, d)), _whole((8, 3 * d))], out_specs=_rows(tr, d),
        out_shape=jax.ShapeDtypeStruct((dm.R, d), BF16), compiler_params=_params(("parallel",)),
    )(xc, g_pre_l, mod)


def _prenorm_bwd(dm, xc, g_pre_l, mod, dh, dxn):
    tr, d, n = dm.TR, dm.D, dm.N

    def body(x_ref, g_ref, mod_ref, dh_ref, dxn_ref, dx_ref, dg_ref, dss_ref):
        i = pl.program_id(0)
        is_ctx = i * tr >= n
        sh = _seg_select(is_ctx, mod_ref, 0, d)
        sc = _seg_select(is_ctx, mod_ref, d, d)
        _, vjp = jax.vjp(_prenorm_fn, x_ref[...], g_ref[...], sc, sh)
        dx, dg, dsc, dsh = vjp(dh_ref[...])
        dx_ref[...] = dxn_ref[...] + dx

        @pl.when(i == 0)
        def _():
            dg_ref[...] = jnp.zeros_like(dg_ref)
            dss_ref[...] = jnp.zeros_like(dss_ref)

        dg_ref[...] += dg
        dss_ref[...] += _seg_rows(is_ctx, jnp.concatenate([dsh, dsc], axis=1))

    return pl.pallas_call(
        body, name="prenorm_bwd", grid=(dm.R // tr,),
        in_specs=[_rows(tr, d), _whole((1, d)), _whole((8, 3 * d)), _rows(tr, d), _rows(tr, d)],
        out_specs=(_rows(tr, d), _whole((1, d)), _whole((8, 2 * d))),
        out_shape=(jax.ShapeDtypeStruct((dm.R, d), F32), jax.ShapeDtypeStruct((1, d), F32),
                   jax.ShapeDtypeStruct((8, 2 * d), F32)),
        compiler_params=_params(("arbitrary",)),
    )(xc, g_pre_l, mod, dh, dxn)


def _post(dm, xc, outp, g_post_l, mod):
    tr, d, n = dm.TR, dm.D, dm.N

    def body(x_ref, o_ref, g_ref, mod_ref, y_ref):
        is_ctx = pl.program_id(0) * tr >= n
        gt = _seg_select(is_ctx, mod_ref, 2 * d, d)
        y_ref[...] = x_ref[...] + _post_fn(o_ref[...], g_ref[...], gt)

    return pl.pallas_call(
        body, name="post", grid=(dm.R // tr,),
        in_specs=[_rows(tr, d), _rows(tr, d), _whole((1, d)), _whole((8, 3 * d))], out_specs=_rows(tr, d),
        out_shape=jax.ShapeDtypeStruct((dm.R, d), F32), compiler_params=_params(("parallel",)),
    )(xc, outp, g_post_l, mod)


def _post_bwd(dm, outp, g_post_l, mod, dxn):
    tr, d, n = dm.TR, dm.D, dm.N

    def body(o_ref, g_ref, mod_ref, dxn_ref, do_ref, dg_ref, dgt_ref):
        i = pl.program_id(0)
        is_ctx = i * tr >= n
        gt = _seg_select(is_ctx, mod_ref, 2 * d, d)
        _, vjp = jax.vjp(_post_fn, o_ref[...], g_ref[...], gt)
        do, dg, dgt = vjp(dxn_ref[...])
        do_ref[...] = do.astype(BF16)

        @pl.when(i == 0)
        def _():
            dg_ref[...] = jnp.zeros_like(dg_ref)
            dgt_ref[...] = jnp.zeros_like(dgt_ref)

        dg_ref[...] += dg
        dgt_ref[...] += _seg_rows(is_ctx, dgt)

    return pl.pallas_call(
        body, name="post_bwd", grid=(dm.R // tr,),
        in_specs=[_rows(tr, d), _whole((1, d)), _whole((8, 3 * d)), _rows(tr, d)],
        out_specs=(_rows(tr, d), _whole((1, d)), _whole((8, d))),
        out_shape=(jax.ShapeDtypeStruct((dm.R, d), BF16), jax.ShapeDtypeStruct((1, d), F32),
                   jax.ShapeDtypeStruct((8, d), F32)),
        compiler_params=_params(("arbitrary",)),
    )(outp, g_post_l, mod, dxn)


def _group_sum(x, bd):
    hi = x.astype(BF16)
    lo = (x - hi.astype(F32)).astype(BF16)
    return jnp.dot(hi, bd, preferred_element_type=F32) + jnp.dot(lo, bd, preferred_element_type=F32)


def _swap_halves(x):
    lane = lax.broadcasted_iota(jnp.int32, x.shape, 1)
    return jnp.where((lane & 32) == 0, pltpu.roll(x, LANES - 32, 1), pltpu.roll(x, 32, 1))


def _qk_prep(dm, p, cos, sins, gq, gk, bd):
    tr, hd = dm.TR, dm.HD
    oq, ok, ov = 3 * dm.C, 3 * dm.C + dm.A, 3 * dm.C + dm.A + dm.KVW
    scale = hd ** -0.5 * LOG2E

    def body(q_ref, k_ref, v_ref, cos_ref, sin_ref, gq_ref, gk_ref, bd_ref, qh_ref, qt_ref, kh_ref, vh_ref, vx_ref):
        cosv, sinv, bdv = cos_ref[...], sin_ref[...], bd_ref[...]

        def chunk(x, g):
            rstd = lax.rsqrt(_group_sum(x * x, bdv) * (1.0 / hd) + EPS)
            xn = x * rstd * g
            return xn * cosv + _swap_halves(xn) * sinv

        for c in range(dm.A // LANES):
            y = chunk(q_ref[:, c * LANES:(c + 1) * LANES], gq_ref[...]) * scale
            qh_ref[2 * c] = y[:, :hd].astype(BF16)
            qh_ref[2 * c + 1] = y[:, hd:].astype(BF16)
            qt_ref[c * LANES:(c + 1) * LANES, :] = y.T.astype(BF16)
        for c in range(dm.KVW // LANES):
            y = chunk(k_ref[:, c * LANES:(c + 1) * LANES], gk_ref[...])
            kh_ref[2 * c] = y[:, :hd].astype(BF16)
            kh_ref[2 * c + 1] = y[:, hd:].astype(BF16)
            v = v_ref[:, c * LANES:(c + 1) * LANES]
            vh_ref[2 * c] = v[:, :hd].astype(BF16)
            vh_ref[2 * c + 1] = v[:, hd:].astype(BF16)
            lane = lax.broadcasted_iota(jnp.int32, v.shape, 1)
            one_at_hd = (lane == hd).astype(F32)
            vx_ref[2 * c] = jnp.where(lane < hd, v, one_at_hd).astype(BF16)
            vx_ref[2 * c + 1] = jnp.where(lane < hd, pltpu.roll(v, hd, 1), one_at_hd).astype(BF16)

    hspec = lambda nh, w=hd: pl.BlockSpec((nh, tr, w), lambda i: (0, i, 0))
    return pl.pallas_call(
        body, name="qk_prep", grid=(dm.R // tr,),
        in_specs=[_rows(tr, dm.A, _col(oq, dm.A)), _rows(tr, dm.KVW, _col(ok, dm.KVW)), _rows(tr, dm.KVW, _col(ov, dm.KVW)),
                  _rows(tr, LANES), _rows(tr, LANES), _whole((1, LANES)), _whole((1, LANES)), _whole((LANES, LANES))],
        out_specs=(hspec(dm.NH), pl.BlockSpec((dm.A, tr), lambda i: (0, i)), hspec(dm.NKV), hspec(dm.NKV), hspec(dm.NKV, LANES)),
        out_shape=(jax.ShapeDtypeStruct((dm.NH, dm.R, hd), BF16), jax.ShapeDtypeStruct((dm.A, dm.R), BF16),
                   jax.ShapeDtypeStruct((dm.NKV, dm.R, hd), BF16), jax.ShapeDtypeStruct((dm.NKV, dm.R, hd), BF16),
                   jax.ShapeDtypeStruct((dm.NKV, dm.R, LANES), BF16)),
        compiler_params=_params(("parallel",)),
    )(p, p, p, cos, sins, gq, gk, bd)


def _qk_bwd(dm, p, dq, dkt, dvt, cos, sins, gq, gk, bd):
    tr, hd = dm.TR, dm.HD
    oq, ok = 3 * dm.C, 3 * dm.C + dm.A
    scale = hd ** -0.5

    def body(q_ref, k_ref, dq_ref, dkh_ref, dvh_ref, cos_ref, sin_ref, gq_ref, gk_ref, bd_ref,
             dpq_ref, dpkv_ref, dgq_ref, dgk_ref):
        i = pl.program_id(0)
        cosv, sinv, bdv = cos_ref[...], sin_ref[...], bd_ref[...]

        def chunk(x, g, dy):
            rstd = lax.rsqrt(_group_sum(x * x, bdv) * (1.0 / hd) + EPS)
            xhat = x * rstd
            dxn = dy * cosv + _swap_halves(dy * sinv)
            dxhat = dxn * g
            dx = rstd * (dxhat - xhat * (_group_sum(dxhat * xhat, bdv) * (1.0 / hd)))
            return dx, jnp.sum(dxn * xhat, axis=0, keepdims=True)

        @pl.when(i == 0)
        def _():
            dgq_ref[...] = jnp.zeros_like(dgq_ref)
            dgk_ref[...] = jnp.zeros_like(dgk_ref)

        dgq = jnp.zeros((1, LANES), F32)
        for c in range(dm.A // LANES):
            sl = slice(c * LANES, (c + 1) * LANES)
            dx, dg = chunk(q_ref[:, sl], gq_ref[...], dq_ref[:, sl] * scale)
            dpq_ref[:, sl] = dx.astype(BF16)
            dgq = dgq + dg
        dgq_ref[...] += dgq + pltpu.roll(dgq, hd, 1)
        dgk = jnp.zeros((1, LANES), F32)
        for c in range(dm.KVW // LANES):
            sl = slice(c * LANES, (c + 1) * LANES)
            dx, dg = chunk(k_ref[:, sl], gk_ref[...], dkh_ref[sl, :].T)
            dpkv_ref[:, sl] = dx.astype(BF16)
            dgk = dgk + dg
            dpkv_ref[:, dm.KVW + c * LANES:dm.KVW + (c + 1) * LANES] = dvh_ref[sl, :].T.astype(BF16)
        dgk_ref[...] += dgk + pltpu.roll(dgk, hd, 1)

    hspec = pl.BlockSpec((dm.KVW, tr), lambda i: (0, i))
    return pl.pallas_call(
        body, name="qk_bwd", grid=(dm.R // tr,),
        in_specs=[_rows(tr, dm.A, _col(oq, dm.A)), _rows(tr, dm.KVW, _col(ok, dm.KVW)), _rows(tr, dm.A), hspec, hspec,
                  _rows(tr, LANES), _rows(tr, LANES), _whole((1, LANES)), _whole((1, LANES)), _whole((LANES, LANES))],
        out_specs=(_rows(tr, dm.A), _rows(tr, 2 * dm.KVW), _whole((1, LANES)), _whole((1, LANES))),
        out_shape=(jax.ShapeDtypeStruct((dm.R, dm.A), BF16), jax.ShapeDtypeStruct((dm.R, 2 * dm.KVW), BF16),
                   jax.ShapeDtypeStruct((1, LANES), F32), jax.ShapeDtypeStruct((1, LANES), F32)),
        compiler_params=_params(("arbitrary",)),
    )(p, p, dq, dkt, dvt, cos, sins, gq, gk, bd)


def _flash_fwd(dm, qh, kh, vx):
    tq, ks = dm.TR, dm.TR
    rs = min(tq, LANES)
    g, hd = dm.G, dm.HD
    nq, nq_lat = dm.R // tq, dm.N // tq
    n_chunks, ctx_chunks = dm.R // ks, dm.CTX // ks

    def body(q_ref, k_ref, v_ref, o_ref, lse_ref):
        qi = pl.program_id(1)

        def attend(chunks):
            for gg in range(g):
                for r in range(tq // rs):
                    rows = slice(r * rs, (r + 1) * rs)
                    q = q_ref[gg, rows, :]
                    m_prev = jnp.full((rs, LANES), -1e30, F32)
                    acc = jnp.zeros((rs, LANES), F32)
                    for kc in chunks:
                        keys = slice(kc * ks, (kc + 1) * ks)
                        s = lax.dot_general(q, k_ref[0, keys, :], _DN["nt"], preferred_element_type=F32)
                        m_new = jnp.maximum(m_prev, jnp.max(s, axis=-1, keepdims=True))
                        p = jnp.exp2(s - jnp.tile(m_new, (1, ks // LANES)))
                        acc = acc * jnp.exp2(m_prev - m_new) + jnp.dot(p.astype(BF16), v_ref[0, keys, :],
                                                                        preferred_element_type=F32)
                        m_prev = m_new
                    l = acc[:, hd:hd + 1]
                    o_ref[rows, gg * hd:(gg + 1) * hd] = acc[:, :hd] / l
                    lse_ref[gg, rows, :] = m_prev + jnp.log2(l)

        @pl.when(qi < nq_lat)
        def _():
            attend(range(n_chunks))

        @pl.when(qi >= nq_lat)
        def _():
            attend(range(n_chunks - ctx_chunks, n_chunks))

    return pl.pallas_call(
        body, name="flash_fwd", grid=(dm.NKV, nq),
        in_specs=[pl.BlockSpec((g, tq, hd), lambda h, qi: (h, qi, 0)), pl.BlockSpec((1, dm.R, hd), lambda h, qi: (h, 0, 0)),
                  pl.BlockSpec((1, dm.R, LANES), lambda h, qi: (h, 0, 0))],
        out_specs=(pl.BlockSpec((tq, g * hd), lambda h, qi: (qi, h)), pl.BlockSpec((g, tq, LANES), lambda h, qi: (h, qi, 0))),
        out_shape=(jax.ShapeDtypeStruct((dm.R, dm.A), F32), jax.ShapeDtypeStruct((dm.NH, dm.R, LANES), F32)),
        compiler_params=_params(("parallel", "parallel")),
    )(qh, kh, vx)


def _flash_bwd(dm, qh, qt, kh, vh, doh, dot, lse, delta):
    tq, ks = dm.TR, dm.TR
    g, hd = dm.G, dm.HD
    nq, nq_lat = dm.R // tq, dm.N // tq
    n_chunks, ctx_chunks = dm.R // ks, dm.CTX // ks
    ng = FLASH_GROUPS if n_chunks % FLASH_GROUPS == 0 else 1
    gsz = n_chunks // ng
    rg = gsz * ks
    assert ctx_chunks <= gsz

    def body(q_ref, qt_ref, k_ref, v_ref, do_ref, dot_ref, lse_ref, dl_ref, dq_ref, dk_acc, dv_acc, dq_acc):
        qi = pl.program_id(1)

        @pl.when(qi == 0)
        def _():
            dk_acc[...] = jnp.zeros_like(dk_acc)
            dv_acc[...] = jnp.zeros_like(dv_acc)

        def attend(grp, chunks):
            q = q_ref[...].reshape(g * tq, hd)
            do = do_ref[...].reshape(g * tq, hd)
            q_t = jnp.concatenate([qt_ref[gg * hd:(gg + 1) * hd, :] for gg in range(g)], axis=1)
            do_t = jnp.concatenate([dot_ref[gg * hd:(gg + 1) * hd, :] for gg in range(g)], axis=1)
            wide = lambda ref: jnp.tile(ref[...].reshape(g * tq, LANES), (1, ks // LANES))
            dqs = jnp.zeros((g * tq, hd), F32)
            for kc in chunks:
                start = grp * rg + kc * ks
                rows = pl.ds(start if isinstance(start, int) else pl.multiple_of(start, ks), ks)
                cols = slice(kc * ks, (kc + 1) * ks)
                k, v = k_ref[0, rows, :], v_ref[0, rows, :]
                s = lax.dot_general(q, k, _DN["nt"], preferred_element_type=F32)
                p = jnp.exp2(s - wide(lse_ref))
                dp = lax.dot_general(do, v, _DN["nt"], preferred_element_type=F32)
                ds = (p * (dp - wide(dl_ref))).astype(BF16)
                dv_acc[0, grp, :, cols] += jnp.dot(do_t, p.astype(BF16), preferred_element_type=F32)
                dk_acc[0, grp, :, cols] += jnp.dot(q_t, ds, preferred_element_type=F32)
                dqs = dqs + jnp.dot(ds, k, preferred_element_type=F32)
            return dqs

        @pl.when(qi < nq_lat)
        def _():
            dq_acc[...] = jnp.zeros_like(dq_acc)

            def group(grp, carry):
                dq_acc[...] += attend(grp, range(gsz))
                return carry

            lax.fori_loop(0, ng, group, 0)
            for gg in range(g):
                dq_ref[:, gg * hd:(gg + 1) * hd] = dq_acc[gg * tq:(gg + 1) * tq, :]

        @pl.when(qi >= nq_lat)
        def _():
            dqs = attend(ng - 1, range(gsz - ctx_chunks, gsz))
            for gg in range(g):
                dq_ref[:, gg * hd:(gg + 1) * hd] = dqs[gg * tq:(gg + 1) * tq]

        @pl.when(qi == nq - 1)
        def _():
            dk_acc[...] = dk_acc[...] * LN2

    qspec = pl.BlockSpec((g, tq, hd), lambda h, qi: (h, qi, 0))
    tspec = pl.BlockSpec((g * hd, tq), lambda h, qi: (h, qi))
    kspec = pl.BlockSpec((1, dm.R, hd), lambda h, qi: (h, 0, 0))
    sspec = pl.BlockSpec((g, tq, LANES), lambda h, qi: (h, qi, 0))
    ospec = pl.BlockSpec((1, ng, hd, rg), lambda h, qi: (h, 0, 0, 0))
    oshape = jax.ShapeDtypeStruct((dm.NKV, ng, hd, rg), F32)
    dq, dk4, dv4 = pl.pallas_call(
        body, name="flash_bwd", grid=(dm.NKV, nq),
        in_specs=[qspec, tspec, kspec, kspec, qspec, tspec, sspec, sspec],
        out_specs=(pl.BlockSpec((tq, g * hd), lambda h, qi: (qi, h)), ospec, ospec),
        out_shape=(jax.ShapeDtypeStruct((dm.R, dm.A), F32), oshape, oshape),
        scratch_shapes=[pltpu.VMEM((g * tq, hd), F32)],
        compiler_params=_params(("parallel", "arbitrary")),
    )(qh, qt, kh, vh, doh, dot, lse, delta)
    flat = lambda t: t.transpose(0, 2, 1, 3).reshape(dm.KVW, dm.R)
    return dq, flat(dk4), flat(dv4)


def _gate_b_specs(dm):
    ogb = 3 * dm.C + dm.A + 2 * dm.KVW
    half = dm.A // 2
    return [_rows(dm.TR, half, _col(ogb, half)), _rows(dm.TR, half, _col(ogb, half) + 1)]


def _attn_gate(dm, o, p):
    tr = dm.TR

    def body(o_ref, g0_ref, g1_ref, za_ref):
        gate = jnp.concatenate([g0_ref[...], g1_ref[...]], axis=1)
        za_ref[...] = (o_ref[...] * jax.nn.silu(gate)).astype(BF16)

    return pl.pallas_call(
        body, name="attn_gate", grid=(dm.R // tr,),
        in_specs=[_rows(tr, dm.A)] + _gate_b_specs(dm), out_specs=_rows(tr, dm.A),
        out_shape=jax.ShapeDtypeStruct((dm.R, dm.A), BF16), compiler_params=_params(("parallel",)),
    )(o, p, p)


def _attn_gate_bwd(dm, dza, o, p):
    tr, hd = dm.TR, dm.HD

    def body(dza_ref, o_ref, g0_ref, g1_ref, doh_ref, dot_ref, dl_ref, dgb_ref):
        gate = jnp.concatenate([g0_ref[...], g1_ref[...]], axis=1)
        dza, o = dza_ref[...], o_ref[...]
        do = dza * jax.nn.silu(gate)
        dgb_ref[...] = (dza * o * _dsilu(gate)).astype(BF16)
        dot_ref[...] = do.T.astype(BF16)
        prod = do * o
        for h in range(dm.NH):
            sl = slice(h * hd, (h + 1) * hd)
            doh_ref[h] = do[:, sl].astype(BF16)
            dl_ref[h] = jnp.broadcast_to(jnp.sum(prod[:, sl], axis=-1, keepdims=True), (tr, LANES))

    return pl.pallas_call(
        body, name="attn_gate_bwd", grid=(dm.R // tr,),
        in_specs=[_rows(tr, dm.A), _rows(tr, dm.A)] + _gate_b_specs(dm),
        out_specs=(pl.BlockSpec((dm.NH, tr, hd), lambda i: (0, i, 0)), pl.BlockSpec((dm.A, tr), lambda i: (0, i)),
                   pl.BlockSpec((dm.NH, tr, LANES), lambda i: (0, i, 0)), _rows(tr, dm.A)),
        out_shape=(jax.ShapeDtypeStruct((dm.NH, dm.R, hd), BF16), jax.ShapeDtypeStruct((dm.A, dm.R), BF16),
                   jax.ShapeDtypeStruct((dm.NH, dm.R, LANES), F32), jax.ShapeDtypeStruct((dm.R, dm.A), BF16)),
        compiler_params=_params(("parallel",)),
    )(dza, o, p, p)


def _halo_specs(dm, col):
    per = dm.TR // HALO
    last = dm.R // HALO - 1
    prev = pl.BlockSpec((HALO, dm.C), lambda i: (jnp.maximum(i * per - 1, 0), col))
    nxt = pl.BlockSpec((HALO, dm.C), lambda i: (jnp.minimum((i + 1) * per, last), col))
    return prev, nxt


def _seg_valid(dm, i):
    tr = dm.TR
    r = i * tr - HALO + lax.broadcasted_iota(jnp.int32, (tr + 2 * HALO, 1), 0)
    is_ctx = i * tr >= dm.N
    lo = jnp.where(is_ctx, dm.N, 0)
    hi = jnp.where(is_ctx, dm.R, dm.N)
    return jnp.logical_and(r >= lo, r < hi)


FLASH_GROUPS = 3
CONV_RB = 64


def _shifted_copies(src_ref, sl, sh_ref, tr):
    rows = tr + 2 * HALO - SUBLANES
    padded = src_ref[:, sl]
    for ph in range(SUBLANES):
        sh_ref[ph, 0:rows, :] = padded[ph:ph + rows]


def _tap_rows(sh_ref, off, n):
    ph = off % SUBLANES
    return sh_ref[ph, off - ph:off - ph + n, :]


def _conv_fwd(dm, p, conv_w_l, conv_b_l, ln_g_l, ln_b_l):
    tr, c = dm.TR, dm.C

    def body(a_ref, g_ref, gate_ref, ap_ref, an_ref, gp_ref, gn_ref, w_ref, b_ref, lg_ref, lb_ref,
             zc_ref, y1_ref, u_ref, sh_ref):
        i = pl.program_id(0)
        a = jnp.concatenate([ap_ref[...], a_ref[...], an_ref[...]], axis=0)
        g = jnp.concatenate([gp_ref[...], g_ref[...], gn_ref[...]], axis=0)
        u_ref[...] = jnp.where(_seg_valid(dm, i), a * jax.nn.sigmoid(g), 0.0)
        for cc in range(c // LANES):
            sl = slice(cc * LANES, (cc + 1) * LANES)
            _shifted_copies(u_ref, sl, sh_ref, tr)
            for r0 in range(0, tr, CONV_RB):
                acc = jnp.zeros((CONV_RB, LANES), F32)
                for j in range(CONV_KERNEL):
                    acc = acc + w_ref[j:j + 1, sl] * _tap_rows(sh_ref, r0 + HALO - CONV_PAD + j, CONV_RB)
                y1_ref[r0:r0 + CONV_RB, sl] = acc + b_ref[:, sl]
        zc_ref[...] = _convact_fn(y1_ref[...], gate_ref[...], lg_ref[...], lb_ref[...]).astype(BF16)

    ap, an = _halo_specs(dm, 0)
    gp, gn = _halo_specs(dm, 1)
    return pl.pallas_call(
        body, name="conv_fwd", grid=(dm.R // tr,),
        in_specs=[_rows(tr, c, 0), _rows(tr, c, 1), _rows(tr, c, 2), ap, an, gp, gn,
                  _whole((32, c)), _whole((1, c)), _whole((1, c)), _whole((1, c))],
        out_specs=(_rows(tr, c), _rows(tr, c)),
        out_shape=(jax.ShapeDtypeStruct((dm.R, c), BF16), jax.ShapeDtypeStruct((dm.R, c), F32)),
        scratch_shapes=[pltpu.VMEM((tr + 2 * HALO, c), F32), pltpu.VMEM((SUBLANES, tr + 2 * HALO, LANES), F32)],
        compiler_params=_params(("parallel",)),
    )(p, p, p, p, p, p, p, conv_w_l, conv_b_l, ln_g_l, ln_b_l)


def _conv_act_bwd(dm, dzc, y1, p, ln_g_l, ln_b_l):
    tr, c = dm.TR, dm.C

    def body(dz_ref, y1_ref, gate_ref, lg_ref, lb_ref, dy_ref, dgate_ref, dlg_ref, dlb_ref):
        i = pl.program_id(0)
        _, vjp = jax.vjp(_convact_fn, y1_ref[...], gate_ref[...], lg_ref[...], lb_ref[...])
        dy, dgate, dlg, dlb = vjp(dz_ref[...])
        dy_ref[...] = dy
        dgate_ref[...] = dgate.astype(BF16)

        @pl.when(i == 0)
        def _():
            dlg_ref[...] = jnp.zeros_like(dlg_ref)
            dlb_ref[...] = jnp.zeros_like(dlb_ref)

        dlg_ref[...] += dlg
        dlb_ref[...] += dlb

    return pl.pallas_call(
        body, name="conv_act_bwd", grid=(dm.R // tr,),
        in_specs=[_rows(tr, c), _rows(tr, c), _rows(tr, c, 2), _whole((1, c)), _whole((1, c))],
        out_specs=(_rows(tr, c), _rows(tr, c), _whole((1, c)), _whole((1, c))),
        out_shape=(jax.ShapeDtypeStruct((dm.R, c), F32), jax.ShapeDtypeStruct((dm.R, c), BF16),
                   jax.ShapeDtypeStruct((1, c), F32), jax.ShapeDtypeStruct((1, c), F32)),
        compiler_params=_params(("arbitrary",)),
    )(dzc, y1, p, ln_g_l, ln_b_l)


def _conv_bwd(dm, dy1, p, conv_w_l):
    tr, c = dm.TR, dm.C

    def body(dy_ref, dyp_ref, dyn_ref, a_ref, g_ref, ap_ref, an_ref, gp_ref, gn_ref, w_ref,
             dag_ref, dw_ref, db_ref, u_ref, dyp_scr, du_ref, dwp_ref, shu_ref, shd_ref):
        i = pl.program_id(0)
        valid = _seg_valid(dm, i)
        a = jnp.concatenate([ap_ref[...], a_ref[...], an_ref[...]], axis=0)
        g = jnp.concatenate([gp_ref[...], g_ref[...], gn_ref[...]], axis=0)
        u_ref[...] = jnp.where(valid, a * jax.nn.sigmoid(g), 0.0)
        dyp_scr[...] = jnp.where(valid, jnp.concatenate([dyp_ref[...], dy_ref[...], dyn_ref[...]], axis=0), 0.0)

        @pl.when(i == 0)
        def _():
            dwp_ref[...] = jnp.zeros_like(dwp_ref)
            db_ref[...] = jnp.zeros_like(db_ref)

        db_ref[...] += jnp.sum(dy_ref[...], axis=0, keepdims=True)
        for cc in range(c // LANES):
            sl = slice(cc * LANES, (cc + 1) * LANES)
            _shifted_copies(u_ref, sl, shu_ref, tr)
            _shifted_copies(dyp_scr, sl, shd_ref, tr)
            for r0 in range(0, tr, CONV_RB):
                dy_cur = dy_ref[r0:r0 + CONV_RB, sl]
                acc = jnp.zeros((CONV_RB, LANES), F32)
                for j in range(CONV_KERNEL):
                    acc = acc + w_ref[j:j + 1, sl] * _tap_rows(shd_ref, r0 + HALO + CONV_PAD - j, CONV_RB)
                    prod = dy_cur * _tap_rows(shu_ref, r0 + HALO - CONV_PAD + j, CONV_RB)
                    dwp_ref[j, :, sl] += jnp.sum(prod.reshape(CONV_RB // SUBLANES, SUBLANES, LANES), axis=0)
                du_ref[r0:r0 + CONV_RB, sl] = acc
        du = du_ref[...]
        sg = jax.nn.sigmoid(g_ref[...])
        dag_ref[:, :c] = (du * sg).astype(BF16)
        dag_ref[:, c:] = (du * a_ref[...] * sg * (1.0 - sg)).astype(BF16)

        @pl.when(i == dm.R // tr - 1)
        def _():
            dw_ref[...] = jnp.sum(dwp_ref[...], axis=1)

    ap, an = _halo_specs(dm, 0)
    gp, gn = _halo_specs(dm, 1)
    dyp, dyn = _halo_specs(dm, 0)
    return pl.pallas_call(
        body, name="conv_bwd", grid=(dm.R // tr,),
        in_specs=[_rows(tr, c), dyp, dyn, _rows(tr, c, 0), _rows(tr, c, 1), ap, an, gp, gn, _whole((32, c))],
        out_specs=(_rows(tr, 2 * c), _whole((32, c)), _whole((1, c))),
        out_shape=(jax.ShapeDtypeStruct((dm.R, 2 * c), BF16), jax.ShapeDtypeStruct((32, c), F32),
                   jax.ShapeDtypeStruct((1, c), F32)),
        scratch_shapes=[pltpu.VMEM((tr + 2 * HALO, c), F32), pltpu.VMEM((tr + 2 * HALO, c), F32), pltpu.VMEM((tr, c), F32),
                        pltpu.VMEM((32, SUBLANES, c), F32), pltpu.VMEM((SUBLANES, tr + 2 * HALO, LANES), F32),
                        pltpu.VMEM((SUBLANES, tr + 2 * HALO, LANES), F32)],
        compiler_params=_params(("arbitrary",)),
    )(dy1, dy1, dy1, p, p, p, p, p, p, conv_w_l)


def _gm_specs(dm):
    half = dm.D // 2
    first = _col(3 * dm.C + 2 * dm.A + 2 * dm.KVW, half)
    return [_rows(dm.TR, half, first + k) for k in range(4)]


def _gm_load(a0, a1, b0, b1):
    return jnp.concatenate([a0[...], a1[...]], axis=1), jnp.concatenate([b0[...], b1[...]], axis=1)


def _merge(dm, yc, ya, p):
    tr, d = dm.TR, dm.D

    def body(yc_ref, ya_ref, a0, a1, b0, b1, m_ref):
        ga, gb = _gm_load(a0, a1, b0, b1)
        m_ref[...] = (jax.nn.sigmoid(ga) * yc_ref[...] + jax.nn.sigmoid(gb) * ya_ref[...]).astype(BF16)

    return pl.pallas_call(
        body, name="merge", grid=(dm.R // tr,),
        in_specs=[_rows(tr, d), _rows(tr, d)] + _gm_specs(dm), out_specs=_rows(tr, d),
        out_shape=jax.ShapeDtypeStruct((dm.R, d), BF16), compiler_params=_params(("parallel",)),
    )(yc, ya, p, p, p, p)


def _merge_bwd(dm, dmerged, yc, ya, p):
    tr, d = dm.TR, dm.D

    def body(dm_ref, yc_ref, ya_ref, a0, a1, b0, b1, dyc_ref, dya_ref, dgm_ref):
        dmv = dm_ref[...]
        ga, gb = _gm_load(a0, a1, b0, b1)
        sa, sb = jax.nn.sigmoid(ga), jax.nn.sigmoid(gb)
        dyc_ref[...] = (dmv * sa).astype(BF16)
        dya_ref[...] = (dmv * sb).astype(BF16)
        dgm_ref[:, :d] = (dmv * yc_ref[...] * sa * (1.0 - sa)).astype(BF16)
        dgm_ref[:, d:] = (dmv * ya_ref[...] * sb * (1.0 - sb)).astype(BF16)

    return pl.pallas_call(
        body, name="merge_bwd", grid=(dm.R // tr,),
        in_specs=[_rows(tr, d), _rows(tr, d), _rows(tr, d)] + _gm_specs(dm),
        out_specs=(_rows(tr, d), _rows(tr, d), _rows(tr, 2 * d)),
        out_shape=(jax.ShapeDtypeStruct((dm.R, d), BF16), jax.ShapeDtypeStruct((dm.R, d), BF16),
                   jax.ShapeDtypeStruct((dm.R, 2 * d), BF16)),
        compiler_params=_params(("parallel",)),
    )(dmerged, yc, ya, p, p, p, p)


def _loss_head(dm, xc, target):
    tr, d = dm.TR, dm.D
    n_lat = dm.N // tr

    def body(x_ref, t_ref, dy_ref, l_ref):
        i = pl.program_id(0)

        @pl.when(i == 0)
        def _():
            l_ref[...] = jnp.zeros_like(l_ref)

        @pl.when(i < n_lat)
        def _():
            e = x_ref[...] - t_ref[...]
            dy_ref[...] = e * (1.0 / d)
            l_ref[...] += 0.5 * jnp.sum(jnp.mean(e * e, axis=-1, keepdims=True), axis=0, keepdims=True)

        @pl.when(i >= n_lat)
        def _():
            dy_ref[...] = jnp.zeros_like(dy_ref)

    return pl.pallas_call(
        body, name="loss_head", grid=(dm.R // tr,),
        in_specs=[_rows(tr, d), pl.BlockSpec((tr, d), lambda i: (jnp.minimum(i, n_lat - 1), 0))],
        out_specs=(_rows(tr, d), _whole((1, 1))),
        out_shape=(jax.ShapeDtypeStruct((dm.R, d), F32), jax.ShapeDtypeStruct((1, 1), F32)),
        compiler_params=_params(("arbitrary",)),
    )(xc, target)


def _adamw(w, g, m, v, name):
    rows, cols = w.shape
    fits = lambda t: t * cols * 4 <= ADAM_BLOCK_BYTES
    tr = rows if fits(rows) else _pick(rows, [t for t in (512, 256, 128, 64, 32, 16, 8) if fits(t)])

    def body(w_ref, g_ref, m_ref, v_ref, d_ref, mo_ref, vo_ref):
        d_ref[...], mo_ref[...], vo_ref[...] = _adam_update(w_ref[...], g_ref[...], m_ref[...], v_ref[...])

    spec = _rows(tr, cols)
    shp = jax.ShapeDtypeStruct(w.shape, F32)
    return pl.pallas_call(
        body, name=name, grid=(rows // tr,), in_specs=[spec] * 4, out_specs=(spec,) * 3, out_shape=(shp,) * 3,
        compiler_params=_params(("parallel",)),
    )(w, g, m, v)


def _adam_update(w, g, m, v):
    mn = ADAM_B1 * m + (1.0 - ADAM_B1) * g
    vn = ADAM_B2 * v + (1.0 - ADAM_B2) * jnp.square(g)
    m_hat = mn / (1.0 - ADAM_B1 ** ADAM_STEP)
    v_hat = vn / (1.0 - ADAM_B2 ** ADAM_STEP)
    return -ADAM_LR * (m_hat / (jnp.sqrt(v_hat) + ADAM_EPS) + ADAM_WD * w), mn, vn


def _rope_tables(dm):
    n = dm.N
    rows = n // GRID_W
    row = jnp.repeat(jnp.arange(rows, dtype=F32), GRID_W)
    col = jnp.tile(jnp.arange(GRID_W, dtype=F32), rows)
    half = dm.HD // 2
    inv_freq = ROPE_THETA ** (-jnp.arange(0, half, 2, dtype=F32) / half)
    ang = jnp.concatenate([row[:, None] * inv_freq, col[:, None] * inv_freq], axis=-1)
    cos, sin = jnp.cos(ang), jnp.sin(ang)
    cos = jnp.concatenate([cos, jnp.ones((dm.CTX, half), F32)], axis=0)
    sin = jnp.concatenate([sin, jnp.zeros((dm.CTX, half), F32)], axis=0)
    reps = LANES // dm.HD
    return jnp.tile(jnp.concatenate([cos, cos], axis=1), (1, reps)), jnp.tile(jnp.concatenate([-sin, sin], axis=1), (1, reps))


def _lane_vec(g, hd):
    return jnp.tile(g.reshape(1, hd), (1, LANES // hd))


def _layer_fwd(dm, xc, cv, wl, tabs):
    mod = _mod_fwd(cv, wl["w_mod"], wl["b_mod"])
    h = _prenorm(dm, xc, wl["g_pre"], mod)
    p = _matmul(h, wl["w_in"], "nn", "in_proj")
    qh, qt, kh, vh, vx = _qk_prep(dm, p, tabs["cos"], tabs["sins"], wl["gq"], wl["gk"], tabs["bd"])
    o, lse = _flash_fwd(dm, qh, kh, vx)
    za = _attn_gate(dm, o, p)
    zc, y1 = _conv_fwd(dm, p, wl["conv_w"], wl["conv_b"], wl["ln_g"], wl["ln_b"])
    yc = _matmul(zc, wl["w_conv_out"], "nn", "conv_out")
    ya = _matmul(za, wl["w_attn_out"], "nn", "attn_out")
    merged = _merge(dm, yc, ya, p)
    outp = _matmul(merged, wl["w_out"], "nn", "out_proj")
    x_new = _post(dm, xc, outp, wl["g_post"], mod)
    saved = dict(xc=xc, mod=mod, h=h, p=p, qh=qh, qt=qt, kh=kh, vh=vh, o=o, lse=lse, za=za, zc=zc, y1=y1, yc=yc, ya=ya,
                 merged=merged, outp=outp)
    return x_new, saved


def _layer_bwd(dm, dxn, cv, wl, tabs, s):
    p = s["p"]
    d_outp, dg_post, dgt = _post_bwd(dm, s["outp"], wl["g_post"], s["mod"], dxn)
    dmerged = _matmul(d_outp, wl["w_out"], "nt", "out_proj_dx")
    dw_out = _matmul(s["merged"], d_outp, "tn", "out_proj_dw")
    dyc, dya, dgm = _merge_bwd(dm, dmerged, s["yc"], s["ya"], p)
    dzc = _matmul(dyc, wl["w_conv_out"], "nt", "conv_out_dx")
    dw_conv_out = _matmul(s["zc"], dyc, "tn", "conv_out_dw")
    dza = _matmul(dya, wl["w_attn_out"], "nt", "attn_out_dx")
    dw_attn_out = _matmul(s["za"], dya, "tn", "attn_out_dw")
    doh, dot, delta, dgate_b = _attn_gate_bwd(dm, dza, s["o"], p)
    dq, dkt, dvt = _flash_bwd(dm, s["qh"], s["qt"], s["kh"], s["vh"], doh, dot, s["lse"], delta)
    dpq, dpkv, dgq, dgk = _qk_bwd(dm, p, dq, dkt, dvt, tabs["cos"], tabs["sins"], wl["gq"], wl["gk"], tabs["bd"])
    dy1, dgate_a, dln_g, dln_b = _conv_act_bwd(dm, dzc, s["y1"], p, wl["ln_g"], wl["ln_b"])
    dpag, dconv_w, dconv_b = _conv_bwd(dm, dy1, p, wl["conv_w"])
    dp = jnp.concatenate([dpag, dgate_a, dpq, dpkv, dgate_b, dgm], axis=1)
    dh = _matmul(dp, wl["w_in"], "nt", "in_proj_dx")
    dw_in = _matmul(s["h"], dp, "tn", "in_proj_dw")
    dxc, dg_pre, dshsc = _prenorm_bwd(dm, s["xc"], wl["g_pre"], s["mod"], dh, dxn)
    dmod = jnp.concatenate([dshsc, dgt], axis=1)
    dw_mod, db_mod, dcv = _mod_bwd(cv, wl["w_mod"], dmod)
    grads = dict(w_mod=dw_mod, b_mod=db_mod[0], g_pre=dg_pre[0], g_post=dg_post[0], w_in=dw_in,
                 conv_w=dconv_w[:CONV_KERNEL], conv_b=dconv_b[0], ln_g=dln_g[0], ln_b=dln_b[0],
                 w_conv_out=dw_conv_out, q_norm_g=dgq[0, :dm.HD], k_norm_g=dgk[0, :dm.HD],
                 w_attn_out=dw_attn_out, w_out=dw_out, c_ctx=dcv[1])
    return dxc, grads


def _local_step(dm, x, c, ctx, c_ctx, target, wf):
    depth = wf["w_in"].shape[0]
    cos, sins = _rope_tables(dm)
    lane = np.arange(LANES)
    bd = jnp.asarray((lane[:, None] // dm.HD == lane[None, :] // dm.HD), dtype=BF16)
    tabs = dict(cos=cos, sins=sins, bd=bd)
    cv = jnp.concatenate([c.reshape(1, dm.D), c_ctx.reshape(1, dm.D), jnp.zeros((6, dm.D), F32)], axis=0)
    xc = jnp.concatenate([x, ctx], axis=0)
    layers, saved = [], []
    for l in range(depth):
        wl = dict(
            w_mod=wf["w_mod"][l], b_mod=wf["b_mod"][l].reshape(1, -1), g_pre=wf["g_pre"][l].reshape(1, -1),
            g_post=wf["g_post"][l].reshape(1, -1), w_in=wf["w_in"][l],
            conv_w=jnp.concatenate([wf["conv_w"][l], jnp.zeros((1, dm.C), F32)], axis=0),
            conv_b=wf["conv_b"][l].reshape(1, -1), ln_g=wf["ln_g"][l].reshape(1, -1), ln_b=wf["ln_b"][l].reshape(1, -1),
            w_conv_out=wf["w_conv_out"][l], gq=_lane_vec(wf["q_norm_g"][l], dm.HD), gk=_lane_vec(wf["k_norm_g"][l], dm.HD),
            w_attn_out=wf["w_attn_out"][l], w_out=wf["w_out"][l])
        layers.append(wl)
        xc, s = _layer_fwd(dm, xc, cv, wl, tabs)
        saved.append(s)
    dxc, loss = _loss_head(dm, xc, target)
    grads = [None] * depth
    for l in reversed(range(depth)):
        dxc, grads[l] = _layer_bwd(dm, dxc, cv, layers[l], tabs, saved[l])
    big = ("w_mod", "w_in", "w_conv_out", "w_attn_out", "w_out")
    gw = {k: [g[k] for g in grads] if k in big else jnp.stack([g[k] for g in grads]) for k in grads[0] if k != "c_ctx"}
    gw["c_ctx"] = grads[0]["c_ctx"] + grads[1]["c_ctx"] if depth == 2 else sum(g["c_ctx"] for g in grads)
    return loss[0, 0], dxc[:dm.N], gw


def _dims(x, ctx, w_in_full_cols, conv_c, attn_w, hd):
    n, d = x.shape
    ctx_len = ctx.shape[0]
    c, a = conv_c, attn_w
    kvw = (w_in_full_cols - 3 * c - 2 * a - 2 * d) // 2
    r = n + ctx_len
    tr = _pick(np.gcd(n, ctx_len), (256, 128))
    return Dims(N=n, CTX=ctx_len, R=r, D=d, C=c, A=a, KVW=kvw, HD=hd, NH=a // hd, NKV=kvw // hd,
                G=a // kvw, W=w_in_full_cols, TR=tr)


_HBM = pl.BlockSpec(memory_space=pltpu.HBM)
_VMEM = pl.BlockSpec(memory_space=pltpu.VMEM)


def _place():
    x, y, c = lax.axis_index("x"), lax.axis_index("y"), lax.axis_index("c")
    chips = [(1 - x, y), (x, 1 - y), (1 - x, 1 - y)]
    return x, y, c, chips


def _row_pieces(rows, want):
    n = max(k for k in range(1, want + 1) if rows % (16 * k) == 0)
    return [pl.ds(i * (rows // n), rows // n) for i in range(n)]


def _allgather_chips(own, name):
    _, h, w = own.shape
    pieces = _row_pieces(h, 4)
    npc = len(pieces)

    def body(own_ref, out_ref, send_sems, recv_sems):
        x, y, c, chips = _place()
        s_me = 2 * x + y
        sibling = (x, y, 1 - c)

        def copy(k, src, dst, to):
            return pltpu.make_async_remote_copy(src_ref=src, dst_ref=dst, send_sem=send_sems.at[k], recv_sem=recv_sems.at[k],
                                                device_id=to, device_id_type=MESH)

        sends = []
        for i, rows in enumerate(pieces):
            for j, chip in enumerate(chips):
                sends.append(copy(i * 3 + j, own_ref.at[c, rows], out_ref.at[s_me, c, rows], (*chip, c)))
        own_slots = [(6 * npc + hf * npc + i, hf, rows) for hf in range(2) for i, rows in enumerate(pieces)]
        for k, hf, rows in own_slots:
            sends.append(copy(k, own_ref.at[hf, rows], out_ref.at[s_me, hf, rows], sibling))
        for cp in sends:
            cp.start()
        for i, rows in enumerate(pieces):
            for j, (cx, cy) in enumerate(chips):
                landed = out_ref.at[2 * cx + cy, c, rows]
                copy(i * 3 + j, landed, landed, (cx, cy, c)).wait_recv()
                fwd = copy(3 * npc + i * 3 + j, landed, landed, sibling)
                fwd.start()
                sends.append(fwd)
        for i, rows in enumerate(pieces):
            for j, (cx, cy) in enumerate(chips):
                theirs = out_ref.at[2 * cx + cy, 1 - c, rows]
                copy(3 * npc + i * 3 + j, theirs, theirs, sibling).wait_recv()
        for k, hf, rows in own_slots:
            mine = out_ref.at[s_me, hf, rows]
            copy(k, mine, mine, sibling).wait_recv()
        for cp in sends:
            cp.wait_send()

    return pl.pallas_call(
        body, name=name, in_specs=[_HBM], out_specs=_HBM,
        out_shape=jax.ShapeDtypeStruct((N_CHIPS, 2, h, w), own.dtype),
        scratch_shapes=[pltpu.SemaphoreType.DMA((8 * npc,)), pltpu.SemaphoreType.DMA((8 * npc,))],
    )(own)


def _pair_exchange(buf, name):
    _, m, w = buf.shape
    pieces = _row_pieces(m, 16)

    def body(buf_ref, out_ref, send_sems, recv_sems):
        x, y, c, _ = _place()
        cps = [pltpu.make_async_remote_copy(src_ref=buf_ref.at[1 - c, rows], dst_ref=out_ref.at[rows], send_sem=send_sems.at[i],
                                            recv_sem=recv_sems.at[i], device_id=(x, y, 1 - c), device_id_type=MESH)
               for i, rows in enumerate(pieces)]
        for cp in cps:
            cp.start()
        for cp in cps:
            cp.wait()

    return pl.pallas_call(
        body, name=name, in_specs=[_HBM], out_specs=_HBM, out_shape=jax.ShapeDtypeStruct((m, w), buf.dtype),
        scratch_shapes=[pltpu.SemaphoreType.DMA((len(pieces),)), pltpu.SemaphoreType.DMA((len(pieces),))],
    )(buf)


def _pair_swap(t, name):
    m, w = t.shape
    pieces = _row_pieces(m, 8)
    n = len(pieces)

    def body(t_ref, out_ref, send_sems, recv_sems):
        x, y, c, _ = _place()
        cps = [pltpu.make_async_remote_copy(src_ref=t_ref.at[rows], dst_ref=out_ref.at[rows], send_sem=send_sems.at[i],
                                            recv_sem=recv_sems.at[i], device_id=(x, y, 1 - c), device_id_type=MESH)
               for i, rows in enumerate(pieces)]
        for cp in cps:
            cp.start()
        for cp in cps:
            cp.wait()

    return pl.pallas_call(
        body, name=name, in_specs=[_HBM], out_specs=_HBM, out_shape=jax.ShapeDtypeStruct((m, w), t.dtype),
        scratch_shapes=[pltpu.SemaphoreType.DMA((n,)), pltpu.SemaphoreType.DMA((n,))],
    )(t)


def _scatter_chips(s, name):
    _, h, w = s.shape

    def body(s_ref, out_ref, send_sems, recv_sems):
        _, _, c, chips = _place()
        cps = [pltpu.make_async_remote_copy(src_ref=s_ref.at[2 * cx + cy], dst_ref=out_ref.at[j], send_sem=send_sems.at[j],
                                            recv_sem=recv_sems.at[j], device_id=(cx, cy, c), device_id_type=MESH)
               for j, (cx, cy) in enumerate(chips)]
        for cp in cps:
            cp.start()
        for cp in cps:
            cp.wait()

    return pl.pallas_call(
        body, name=name, in_specs=[_HBM], out_specs=_HBM, out_shape=jax.ShapeDtypeStruct((3, h, w), s.dtype),
        scratch_shapes=[pltpu.SemaphoreType.DMA((3,)), pltpu.SemaphoreType.DMA((3,))],
    )(s)


def _sum_pair(buf, recv, core, name):
    m, w = recv.shape
    tr = _pick(m, (512, 432, 256, 128, 64, 32, 16))

    def body(c_ref, b_ref, r_ref, o_ref, ob_ref):
        v = b_ref[0] + r_ref[...]
        o_ref[...] = v
        ob_ref[...] = v.astype(BF16)

    spec = pl.BlockSpec((tr, w), lambda i, c: (i, 0))
    return pl.pallas_call(
        body, name=name,
        grid_spec=pltpu.PrefetchScalarGridSpec(
            num_scalar_prefetch=1, grid=(m // tr,),
            in_specs=[pl.BlockSpec((1, tr, w), lambda i, c: (c[0], i, 0)), spec], out_specs=(spec, spec)),
        out_shape=(jax.ShapeDtypeStruct((m, w), F32), jax.ShapeDtypeStruct((m, w), BF16)),
        compiler_params=_params(("parallel",)),
    )(core, buf, recv)


def _sum_chips(own, recv, chip, name):
    _, h, w = own.shape
    tr = _pick(h, (512, 432, 256, 128, 64, 32, 16))

    def body(s_ref, a_ref, r_ref, o_ref):
        acc = a_ref[0]
        for j in range(3):
            acc = acc + r_ref[j].astype(F32)
        o_ref[...] = acc

    return pl.pallas_call(
        body, name=name,
        grid_spec=pltpu.PrefetchScalarGridSpec(
            num_scalar_prefetch=1, grid=(h // tr,),
            in_specs=[pl.BlockSpec((1, tr, w), lambda i, s: (s[0], i, 0)), pl.BlockSpec((3, tr, w), lambda i, s: (0, i, 0))],
            out_specs=pl.BlockSpec((tr, w), lambda i, s: (i, 0))),
        out_shape=jax.ShapeDtypeStruct((h, w), F32), compiler_params=_params(("parallel",)),
    )(chip, own, recv)


def _allreduce_small(own, name):
    m, w = own.shape

    def body(own_ref, sum_ref, all_ref, send_sems, recv_sems):
        x, y, c, _ = _place()
        me = 4 * x + 2 * y + c
        all_ref[me] = own_ref[...]
        sends = []
        for k in range(1, N_DEV):
            peer = (x ^ ((k >> 2) & 1), y ^ ((k >> 1) & 1), c ^ (k & 1))
            cp = pltpu.make_async_remote_copy(src_ref=own_ref, dst_ref=all_ref.at[me], send_sem=send_sems.at[k - 1],
                                              recv_sem=recv_sems.at[k - 1], device_id=peer, device_id_type=MESH)
            cp.start()
            sends.append(cp)
        for k in range(1, N_DEV):
            px, py, pc = x ^ ((k >> 2) & 1), y ^ ((k >> 1) & 1), c ^ (k & 1)
            slot = all_ref.at[4 * px + 2 * py + pc]
            pltpu.make_async_remote_copy(src_ref=slot, dst_ref=slot, send_sem=send_sems.at[k - 1], recv_sem=recv_sems.at[k - 1],
                                         device_id=(px, py, pc), device_id_type=MESH).wait_recv()
        for cp in sends:
            cp.wait_send()
        acc = all_ref[0]
        for d in range(1, N_DEV):
            acc = acc + all_ref[d]
        sum_ref[...] = acc

    return pl.pallas_call(
        body, name=name, in_specs=[_VMEM], out_specs=_VMEM, out_shape=jax.ShapeDtypeStruct((m, w), F32),
        scratch_shapes=[pltpu.VMEM((N_DEV, m, w), F32), pltpu.SemaphoreType.DMA((N_DEV - 1,)), pltpu.SemaphoreType.DMA((N_DEV - 1,))],
        compiler_params=pltpu.CompilerParams(vmem_limit_bytes=VMEM_LIMIT),
    )(own)


_COL_SHARDED = ("w_mod", "w_in")
_ROW_SHARDED = ("w_conv_out", "w_attn_out", "w_out")
_BIG = _COL_SHARDED + _ROW_SHARDED
_SMALL = ("c_ctx", "b_mod", "g_pre", "g_post", "conv_b", "ln_g", "ln_b", "q_norm_g", "k_norm_g")
FLAT_W = 1024


def _unflatten_shard(flat, like):
    out, off = {}, 0
    flat = flat.reshape(-1)
    for k in _BIG:
        n = int(np.prod(like[k].shape))
        out[k] = flat[off:off + n].reshape(like[k].shape)
        off += n
    return out


def _by_shard(k, layers):
    parts = []
    for full in layers:
        if k in _COL_SHARDED:
            d, cols = full.shape
            parts.append(full.reshape(d, N_CHIPS, cols // N_CHIPS).transpose(1, 0, 2).reshape(N_CHIPS, -1))
        else:
            parts.append(full.reshape(N_CHIPS, -1))
    return jnp.concatenate(parts, axis=1)


def _from_shards(k, flat4, like):
    l = like.shape[0]
    if k in _COL_SHARDED:
        d, cs = like.shape[1:]
        return flat4.reshape(N_CHIPS, l, d, cs).transpose(1, 2, 0, 3).reshape(l, d, N_CHIPS * cs)
    rs, d = like.shape[1:]
    return flat4.reshape(N_CHIPS, l, rs, d).transpose(1, 0, 2, 3).reshape(l, N_CHIPS * rs, d)


def _pad_rows(flat, mult=8):
    n = flat.shape[0]
    rows = -(-n // LANES)
    rows = -(-rows // mult) * mult
    return jnp.concatenate([flat, jnp.zeros((rows * LANES - n,), F32)]).reshape(rows, LANES)


def kernel(x, c, ctx, c_ctx, w_mod, b_mod, g_pre, g_post, w_in, conv_w, conv_b, ln_g, ln_b, w_conv_out, q_norm_g, k_norm_g, w_attn_out, w_out, loss_target, m_c_ctx, m_w_mod, m_b_mod, m_g_pre, m_g_post, m_w_in, m_conv_w, m_conv_b, m_ln_g, m_ln_b, m_w_conv_out, m_q_norm_g, m_k_norm_g, m_w_attn_out, m_w_out, v_c_ctx, v_w_mod, v_b_mod, v_g_pre, v_g_post, v_w_in, v_conv_w, v_conv_b, v_ln_g, v_ln_b, v_w_conv_out, v_q_norm_g, v_k_norm_g, v_w_attn_out, v_w_out):
    w = dict(c_ctx=c_ctx, w_mod=w_mod, b_mod=b_mod, g_pre=g_pre, g_post=g_post, w_in=w_in, conv_w=conv_w, conv_b=conv_b,
             ln_g=ln_g, ln_b=ln_b, w_conv_out=w_conv_out, q_norm_g=q_norm_g, k_norm_g=k_norm_g, w_attn_out=w_attn_out, w_out=w_out)
    m = dict(c_ctx=m_c_ctx, w_mod=m_w_mod, b_mod=m_b_mod, g_pre=m_g_pre, g_post=m_g_post, w_in=m_w_in, conv_w=m_conv_w,
             conv_b=m_conv_b, ln_g=m_ln_g, ln_b=m_ln_b, w_conv_out=m_w_conv_out, q_norm_g=m_q_norm_g, k_norm_g=m_k_norm_g,
             w_attn_out=m_w_attn_out, w_out=m_w_out)
    v = dict(c_ctx=v_c_ctx, w_mod=v_w_mod, b_mod=v_b_mod, g_pre=v_g_pre, g_post=v_g_post, w_in=v_w_in, conv_w=v_conv_w,
             conv_b=v_conv_b, ln_g=v_ln_g, ln_b=v_ln_b, w_conv_out=v_w_conv_out, q_norm_g=v_q_norm_g, k_norm_g=v_k_norm_g,
             w_attn_out=v_w_attn_out, w_out=v_w_out)
    order = ("c_ctx", "w_mod", "b_mod", "g_pre", "g_post", "w_in", "conv_w", "conv_b", "ln_g", "ln_b", "w_conv_out",
             "q_norm_g", "k_norm_g", "w_attn_out", "w_out")
    my_chip = 2 * lax.axis_index("x") + lax.axis_index("y")
    my_core = lax.axis_index("c")
    depth, _, conv_cs = conv_w.shape
    conv_c = conv_cs * N_CHIPS
    dm = _dims(x[0], ctx[0], w_in.shape[2] * N_CHIPS, conv_c, w_attn_out.shape[1] * N_CHIPS, q_norm_g.shape[1])

    own = jnp.concatenate([w[k].astype(BF16).reshape(-1) for k in _BIG])
    rows2 = own.shape[0] // FLAT_W
    own = own.reshape(2, rows2 // 2, FLAT_W)
    gathered = _allgather_chips(own, "gather_weights").reshape(N_CHIPS, -1)
    wf, off = {}, 0
    for k in _BIG:
        n = int(np.prod(w[k].shape))
        wf[k] = _from_shards(k, gathered[:, off:off + n], w[k])
        off += n
    cw = jnp.zeros((depth, CONV_KERNEL, N_CHIPS, conv_cs), F32)
    cw = lax.dynamic_update_slice(cw, conv_w[:, :, None, :], (0, 0, my_chip, 0))
    cw = jnp.where(my_core == 0, cw, 0.0).reshape(-1)
    wf["conv_w"] = _allreduce_small(_pad_rows(cw), "gather_conv_w").reshape(-1)[:cw.shape[0]].reshape(depth, CONV_KERNEL, conv_c)
    for k in _SMALL:
        wf[k] = w[k]

    loss_part, grad_x, gw = _local_step(dm, x[0], c, ctx[0], c_ctx, loss_target[0], wf)

    g4 = jnp.concatenate([_by_shard(k, gw[k]) for k in _BIG], axis=1)
    hrows = rows2 // 2
    g24 = g4.reshape(N_CHIPS, 2, hrows, FLAT_W).transpose(1, 0, 2, 3).reshape(2, N_CHIPS * hrows, FLAT_W)
    core_i = my_core.astype(jnp.int32).reshape(1)
    chip_i = my_chip.astype(jnp.int32).reshape(1)
    theirs = _pair_exchange(g24, "reduce_pair")
    chip_sum, chip_sum_bf = _sum_pair(g24, theirs, core_i, "reduce_pair_add")
    others = _scatter_chips(chip_sum_bf.reshape(N_CHIPS, hrows, FLAT_W), "reduce_chips")
    mine = _sum_chips(chip_sum.reshape(N_CHIPS, hrows, FLAT_W), others, chip_i, "reduce_chips_add")
    sibling_half = _pair_swap(mine, "reduce_share")

    small = jnp.concatenate([gw[k].reshape(-1) for k in _SMALL] + [gw["conv_w"].reshape(-1), loss_part.reshape(1)])
    small_sum = _allreduce_small(_pad_rows(small), "reduce_small").reshape(-1)
    loss = small_sum[small.shape[0] - 1]
    gs, off = {}, 0
    for k in _SMALL:
        n = int(np.prod(w[k].shape))
        gs[k] = small_sum[off:off + n].reshape(w[k].shape)
        off += n
    gcw = small_sum[off:off + depth * CONV_KERNEL * conv_c].reshape(depth, CONV_KERNEL, N_CHIPS, conv_cs)
    gs["conv_w"] = lax.dynamic_index_in_dim(gcw, my_chip, axis=2, keepdims=False)

    g_flat = jnp.where(my_core == 0, jnp.concatenate([mine, sibling_half]), jnp.concatenate([sibling_half, mine]))
    grads = _unflatten_shard(g_flat, w)
    delta, new_m, new_v = {}, {}, {}
    for k in _BIG:
        as2d = lambda t: t.reshape(-1, t.shape[-1])
        d_k, m_k, v_k = _adamw(as2d(w[k]), as2d(grads[k]), as2d(m[k]), as2d(v[k]), "adamw_" + k)
        delta[k], new_m[k], new_v[k] = d_k.reshape(w[k].shape), m_k.reshape(w[k].shape), v_k.reshape(w[k].shape)
    small_keys = _SMALL + ("conv_w",)
    join = lambda t: _pad_rows(jnp.concatenate([t[k].reshape(-1) for k in small_keys]))
    d_s, m_s, v_s = _adamw(join(w), join(gs), join(m), join(v), "adamw_small")
    off = 0
    for k in small_keys:
        n = int(np.prod(w[k].shape))
        grads[k] = gs[k]
        for dst, src in ((delta, d_s), (new_m, m_s), (new_v, v_s)):
            dst[k] = src.reshape(-1)[off:off + n].reshape(w[k].shape)
        off += n
    return (loss, grad_x[None], *[grads[k] for k in order], *[delta[k] for k in order],
            *[new_m[k] for k in order], *[new_v[k] for k in order])
```

```python
import functools
from typing import NamedTuple

import jax
import jax.numpy as jnp
import numpy as np
from jax import lax
from jax.experimental import pallas as pl
from jax.experimental.pallas import tpu as pltpu

F32 = jnp.float32
BF16 = jnp.bfloat16
EPS = 1e-6
GRID_W = 64
ROPE_THETA = 10000.0
CONV_KERNEL = 31
CONV_PAD = CONV_KERNEL // 2
HALO = 16
LANES = 128
SUBLANES = 8
ADAM_LR, ADAM_B1, ADAM_B2, ADAM_EPS, ADAM_WD, ADAM_STEP = 0.001, 0.9, 0.999, 1e-08, 0.01, 10
VMEM_LIMIT = 56 * 1024 * 1024
ADAM_BLOCK_BYTES = 1024 * 1024
LOG2E = 1.4426950408889634
LN2 = 0.6931471805599453
MESH = pl.DeviceIdType.MESH
N_CHIPS = 4
N_DEV = 8


class Dims(NamedTuple):
    N: int
    CTX: int
    R: int
    D: int
    C: int
    A: int
    KVW: int
    HD: int
    NH: int
    NKV: int
    G: int
    W: int
    TR: int


def _pick(n, prefs):
    for p in prefs:
        if n % p == 0:
            return p
    raise ValueError(f"no tile for {n} in {prefs}")


def _params(sem):
    return pltpu.CompilerParams(dimension_semantics=sem, vmem_limit_bytes=VMEM_LIMIT)


def _rows(tr, w, col=0):
    return pl.BlockSpec((tr, w), lambda i, col=col: (i, col))


def _whole(shape):
    return pl.BlockSpec(shape, lambda *_: (0,) * len(shape))


def _col(off, w):
    assert off % w == 0, (off, w)
    return off // w


_DN = {"nn": (((1,), (0,)), ((), ())), "nt": (((1,), (1,)), ((), ())), "tn": (((0,), (0,)), ((), ()))}


def _matmul(a, b, mode, name, out_dtype=F32):
    if mode == "nn":
        (m, k), n = a.shape, b.shape[1]
    elif mode == "nt":
        (m, k), n = a.shape, b.shape[0]
    else:
        (k, m), n = a.shape, b.shape[1]
    tm = _pick(m, (1408, 1024, 768, 512, 256, 128, 8))
    tn = _pick(n, (1280, 1024, 768, 512, 256, 128))
    tk = _pick(k, (1536, 1408, 1280, 1024, 768, 512, 256, 128))
    nk = k // tk
    dn = _DN[mode]

    own_acc = out_dtype != F32 and nk > 1

    def body(a_ref, b_ref, o_ref, *scratch):
        part = lax.dot_general(a_ref[...], b_ref[...], dn, preferred_element_type=F32)
        if nk == 1:
            o_ref[...] = part.astype(out_dtype)
            return
        acc_ref = scratch[0] if own_acc else o_ref
        kk = pl.program_id(2)

        @pl.when(kk == 0)
        def _():
            acc_ref[...] = part

        @pl.when(kk > 0)
        def _():
            acc_ref[...] += part

        if own_acc:
            @pl.when(kk == nk - 1)
            def _():
                o_ref[...] = acc_ref[...].astype(out_dtype)

    a_spec = pl.BlockSpec((tk, tm), lambda i, j, kk: (kk, i)) if mode == "tn" else pl.BlockSpec((tm, tk), lambda i, j, kk: (i, kk))
    b_spec = pl.BlockSpec((tn, tk), lambda i, j, kk: (j, kk)) if mode == "nt" else pl.BlockSpec((tk, tn), lambda i, j, kk: (kk, j))
    return pl.pallas_call(
        body, name=name, grid=(m // tm, n // tn, nk),
        in_specs=[a_spec, b_spec], out_specs=pl.BlockSpec((tm, tn), lambda i, j, kk: (i, j)),
        out_shape=jax.ShapeDtypeStruct((m, n), out_dtype),
        scratch_shapes=[pltpu.VMEM((tm, tn), F32)] if own_acc else [],
        compiler_params=_params(("parallel", "parallel", "arbitrary")),
    )(a, b)


def _rms(x, g):
    return x * lax.rsqrt(jnp.mean(x * x, axis=-1, keepdims=True) + EPS) * g


def _prenorm_fn(x, g, sc, sh):
    return _rms(x, g) * (1.0 + sc) + sh


def _post_fn(outp, g, gt):
    return gt * _rms(outp, g)


def _convact_fn(y1, gate, ln_g, ln_b):
    mu = jnp.mean(y1, axis=-1, keepdims=True)
    var = jnp.mean(jnp.square(y1 - mu), axis=-1, keepdims=True)
    y = (y1 - mu) * lax.rsqrt(var + EPS) * ln_g + ln_b
    return jax.nn.silu(y) * jax.nn.silu(gate)


def _dsilu(x):
    s = jax.nn.sigmoid(x)
    return s * (1.0 + x * (1.0 - s))


def _seg_select(is_ctx, mod_ref, lo, w):
    return jnp.where(is_ctx, mod_ref[1:2, lo:lo + w], mod_ref[0:1, lo:lo + w])


def _seg_rows(is_ctx, v):
    row = lax.broadcasted_iota(jnp.int32, (8, v.shape[1]), 0)
    return jnp.where(row == is_ctx.astype(jnp.int32), v, 0.0)


def _mod_fwd(cv, w_mod_l, b_mod_l):
    def body(cv_ref, w_ref, b_ref, o_ref):
        s = jax.nn.silu(cv_ref[...]).astype(BF16)
        o_ref[...] = jnp.dot(s, w_ref[...], preferred_element_type=F32) + b_ref[...]

    return pl.pallas_call(
        body, name="mod_fwd", out_shape=jax.ShapeDtypeStruct((8, w_mod_l.shape[1]), F32),
        compiler_params=pltpu.CompilerParams(vmem_limit_bytes=VMEM_LIMIT),
    )(cv, w_mod_l, b_mod_l)


def _mod_bwd(cv, w_mod_l, dmod):
    d = cv.shape[1]

    def body(cv_ref, w_ref, dm_ref, dw_ref, db_ref, dcv_ref):
        cvv = cv_ref[...]
        s = jax.nn.silu(cvv).astype(BF16)
        dm = dm_ref[...]
        dmb = dm.astype(BF16)
        dw_ref[...] = lax.dot_general(s, dmb, _DN["tn"], preferred_element_type=F32)
        db_ref[...] = jnp.sum(dm, axis=0, keepdims=True)
        ds = lax.dot_general(dmb, w_ref[...], _DN["nt"], preferred_element_type=F32)
        dcv_ref[...] = ds * _dsilu(cvv)

    return pl.pallas_call(
        body, name="mod_bwd",
        out_shape=(jax.ShapeDtypeStruct(w_mod_l.shape, F32), jax.ShapeDtypeStruct((1, w_mod_l.shape[1]), F32),
                   jax.ShapeDtypeStruct((8, d), F32)),
        compiler_params=pltpu.CompilerParams(vmem_limit_bytes=VMEM_LIMIT),
    )(cv, w_mod_l, dmod)


def _prenorm(dm, xc, g_pre_l, mod):
    tr, d, n = dm.TR, dm.D, dm.N

    def body(x_ref, g_ref, mod_ref, h_ref):
        is_ctx = pl.program_id(0) * tr >= n
        sh = _seg_select(is_ctx, mod_ref, 0, d)
        sc = _seg_select(is_ctx, mod_ref, d, d)
        h_ref[...] = _prenorm_fn(x_ref[...], g_ref[...], sc, sh).astype(BF16)

    return pl.pallas_call(
        body, name="prenorm", grid=(dm.R // tr,),
        in_specs=[_rows(tr, d), _whole((1, d)), _whole((8, 3 * d))], out_specs=_rows(tr, d),
        out_shape=jax.ShapeDtypeStruct((dm.R, d), BF16), compiler_params=_params(("parallel",)),
    )(xc, g_pre_l, mod)


def _prenorm_bwd(dm, xc, g_pre_l, mod, dh, dxn):
    tr, d, n = dm.TR, dm.D, dm.N

    def body(x_ref, g_ref, mod_ref, dh_ref, dxn_ref, dx_ref, dg_ref, dss_ref):
        i = pl.program_id(0)
        is_ctx = i * tr >= n
        sh = _seg_select(is_ctx, mod_ref, 0, d)
        sc = _seg_select(is_ctx, mod_ref, d, d)
        _, vjp = jax.vjp(_prenorm_fn, x_ref[...], g_ref[...], sc, sh)
        dx, dg, dsc, dsh = vjp(dh_ref[...].astype(F32))
        dx_ref[...] = dxn_ref[...] + dx

        @pl.when(i == 0)
        def _():
            dg_ref[...] = jnp.zeros_like(dg_ref)
            dss_ref[...] = jnp.zeros_like(dss_ref)

        dg_ref[...] += dg
        dss_ref[...] += _seg_rows(is_ctx, jnp.concatenate([dsh, dsc], axis=1))

    return pl.pallas_call(
        body, name="prenorm_bwd", grid=(dm.R // tr,),
        in_specs=[_rows(tr, d), _whole((1, d)), _whole((8, 3 * d)), _rows(tr, d), _rows(tr, d)],
        out_specs=(_rows(tr, d), _whole((1, d)), _whole((8, 2 * d))),
        out_shape=(jax.ShapeDtypeStruct((dm.R, d), F32), jax.ShapeDtypeStruct((1, d), F32),
                   jax.ShapeDtypeStruct((8, 2 * d), F32)),
        compiler_params=_params(("arbitrary",)),
    )(xc, g_pre_l, mod, dh, dxn)


def _post(dm, xc, outp, g_post_l, mod):
    tr, d, n = dm.TR, dm.D, dm.N

    def body(x_ref, o_ref, g_ref, mod_ref, y_ref):
        is_ctx = pl.program_id(0) * tr >= n
        gt = _seg_select(is_ctx, mod_ref, 2 * d, d)
        y_ref[...] = x_ref[...] + _post_fn(o_ref[...], g_ref[...], gt)

    return pl.pallas_call(
        body, name="post", grid=(dm.R // tr,),
        in_specs=[_rows(tr, d), _rows(tr, d), _whole((1, d)), _whole((8, 3 * d))], out_specs=_rows(tr, d),
        out_shape=jax.ShapeDtypeStruct((dm.R, d), F32), compiler_params=_params(("parallel",)),
    )(xc, outp, g_post_l, mod)


def _post_bwd(dm, outp, g_post_l, mod, dxn):
    tr, d, n = dm.TR, dm.D, dm.N

    def body(o_ref, g_ref, mod_ref, dxn_ref, do_ref, dg_ref, dgt_ref):
        i = pl.program_id(0)
        is_ctx = i * tr >= n
        gt = _seg_select(is_ctx, mod_ref, 2 * d, d)
        _, vjp = jax.vjp(_post_fn, o_ref[...], g_ref[...], gt)
        do, dg, dgt = vjp(dxn_ref[...])
        do_ref[...] = do.astype(BF16)

        @pl.when(i == 0)
        def _():
            dg_ref[...] = jnp.zeros_like(dg_ref)
            dgt_ref[...] = jnp.zeros_like(dgt_ref)

        dg_ref[...] += dg
        dgt_ref[...] += _seg_rows(is_ctx, dgt)

    return pl.pallas_call(
        body, name="post_bwd", grid=(dm.R // tr,),
        in_specs=[_rows(tr, d), _whole((1, d)), _whole((8, 3 * d)), _rows(tr, d)],
        out_specs=(_rows(tr, d), _whole((1, d)), _whole((8, d))),
        out_shape=(jax.ShapeDtypeStruct((dm.R, d), BF16), jax.ShapeDtypeStruct((1, d), F32),
                   jax.ShapeDtypeStruct((8, d), F32)),
        compiler_params=_params(("arbitrary",)),
    )(outp, g_post_l, mod, dxn)


def _group_sum(x, bd):
    hi = x.astype(BF16)
    lo = (x - hi.astype(F32)).astype(BF16)
    return jnp.dot(hi, bd, preferred_element_type=F32) + jnp.dot(lo, bd, preferred_element_type=F32)


def _swap_halves(x):
    lane = lax.broadcasted_iota(jnp.int32, x.shape, 1)
    return jnp.where((lane & 32) == 0, pltpu.roll(x, LANES - 32, 1), pltpu.roll(x, 32, 1))


def _qk_prep(dm, p, cos, sins, gq, gk, bd):
    tr, hd = dm.TR, dm.HD
    oq, ok, ov = 3 * dm.C, 3 * dm.C + dm.A, 3 * dm.C + dm.A + dm.KVW
    scale = hd ** -0.5 * LOG2E

    def body(q_ref, k_ref, v_ref, cos_ref, sin_ref, gq_ref, gk_ref, bd_ref, qh_ref, qt_ref, kh_ref, vh_ref, vx_ref):
        cosv, sinv, bdv = cos_ref[...], sin_ref[...], bd_ref[...]

        def chunk(x, g):
            rstd = lax.rsqrt(_group_sum(x * x, bdv) * (1.0 / hd) + EPS)
            xn = x * rstd * g
            return xn * cosv + _swap_halves(xn) * sinv

        for c in range(dm.A // LANES):
            y = chunk(q_ref[:, c * LANES:(c + 1) * LANES].astype(F32), gq_ref[...]) * scale
            qh_ref[2 * c] = y[:, :hd].astype(BF16)
            qh_ref[2 * c + 1] = y[:, hd:].astype(BF16)
            qt_ref[c * LANES:(c + 1) * LANES, :] = y.T.astype(BF16)
        for c in range(dm.KVW // LANES):
            y = chunk(k_ref[:, c * LANES:(c + 1) * LANES].astype(F32), gk_ref[...])
            kh_ref[2 * c] = y[:, :hd].astype(BF16)
            kh_ref[2 * c + 1] = y[:, hd:].astype(BF16)
            v = v_ref[:, c * LANES:(c + 1) * LANES].astype(F32)
            vh_ref[2 * c] = v[:, :hd].astype(BF16)
            vh_ref[2 * c + 1] = v[:, hd:].astype(BF16)
            lane = lax.broadcasted_iota(jnp.int32, v.shape, 1)
            one_at_hd = (lane == hd).astype(F32)
            vx_ref[2 * c] = jnp.where(lane < hd, v, one_at_hd).astype(BF16)
            vx_ref[2 * c + 1] = jnp.where(lane < hd, pltpu.roll(v, hd, 1), one_at_hd).astype(BF16)

    hspec = lambda nh, w=hd: pl.BlockSpec((nh, tr, w), lambda i: (0, i, 0))
    return pl.pallas_call(
        body, name="qk_prep", grid=(dm.R // tr,),
        in_specs=[_rows(tr, dm.A, _col(oq, dm.A)), _rows(tr, dm.KVW, _col(ok, dm.KVW)), _rows(tr, dm.KVW, _col(ov, dm.KVW)),
                  _rows(tr, LANES), _rows(tr, LANES), _whole((1, LANES)), _whole((1, LANES)), _whole((LANES, LANES))],
        out_specs=(hspec(dm.NH), pl.BlockSpec((dm.A, tr), lambda i: (0, i)), hspec(dm.NKV), hspec(dm.NKV), hspec(dm.NKV, LANES)),
        out_shape=(jax.ShapeDtypeStruct((dm.NH, dm.R, hd), BF16), jax.ShapeDtypeStruct((dm.A, dm.R), BF16),
                   jax.ShapeDtypeStruct((dm.NKV, dm.R, hd), BF16), jax.ShapeDtypeStruct((dm.NKV, dm.R, hd), BF16),
                   jax.ShapeDtypeStruct((dm.NKV, dm.R, LANES), BF16)),
        compiler_params=_params(("parallel",)),
    )(p, p, p, cos, sins, gq, gk, bd)


def _qk_bwd(dm, p, dq, dkt, dvt, cos, sins, gq, gk, bd):
    tr, hd = dm.TR, dm.HD
    oq, ok = 3 * dm.C, 3 * dm.C + dm.A
    scale = hd ** -0.5

    def body(q_ref, k_ref, dq_ref, dkh_ref, dvh_ref, cos_ref, sin_ref, gq_ref, gk_ref, bd_ref,
             dpq_ref, dpkv_ref, dgq_ref, dgk_ref):
        i = pl.program_id(0)
        cosv, sinv, bdv = cos_ref[...], sin_ref[...], bd_ref[...]

        def chunk(x, g, dy):
            rstd = lax.rsqrt(_group_sum(x * x, bdv) * (1.0 / hd) + EPS)
            xhat = x * rstd
            dxn = dy * cosv + _swap_halves(dy * sinv)
            dxhat = dxn * g
            dx = rstd * (dxhat - xhat * (_group_sum(dxhat * xhat, bdv) * (1.0 / hd)))
            return dx, jnp.sum(dxn * xhat, axis=0, keepdims=True)

        @pl.when(i == 0)
        def _():
            dgq_ref[...] = jnp.zeros_like(dgq_ref)
            dgk_ref[...] = jnp.zeros_like(dgk_ref)

        dgq = jnp.zeros((1, LANES), F32)
        for c in range(dm.A // LANES):
            sl = slice(c * LANES, (c + 1) * LANES)
            dx, dg = chunk(q_ref[:, sl].astype(F32), gq_ref[...], dq_ref[:, sl] * scale)
            dpq_ref[:, sl] = dx.astype(BF16)
            dgq = dgq + dg
        dgq_ref[...] += dgq + pltpu.roll(dgq, hd, 1)
        dgk = jnp.zeros((1, LANES), F32)
        for c in range(dm.KVW // LANES):
            sl = slice(c * LANES, (c + 1) * LANES)
            dx, dg = chunk(k_ref[:, sl].astype(F32), gk_ref[...], dkh_ref[sl, :].T)
            dpkv_ref[:, sl] = dx.astype(BF16)
            dgk = dgk + dg
            dpkv_ref[:, dm.KVW + c * LANES:dm.KVW + (c + 1) * LANES] = dvh_ref[sl, :].T.astype(BF16)
        dgk_ref[...] += dgk + pltpu.roll(dgk, hd, 1)

    hspec = pl.BlockSpec((dm.KVW, tr), lambda i: (0, i))
    return pl.pallas_call(
        body, name="qk_bwd", grid=(dm.R // tr,),
        in_specs=[_rows(tr, dm.A, _col(oq, dm.A)), _rows(tr, dm.KVW, _col(ok, dm.KVW)), _rows(tr, dm.A), hspec, hspec,
                  _rows(tr, LANES), _rows(tr, LANES), _whole((1, LANES)), _whole((1, LANES)), _whole((LANES, LANES))],
        out_specs=(_rows(tr, dm.A), _rows(tr, 2 * dm.KVW), _whole((1, LANES)), _whole((1, LANES))),
        out_shape=(jax.ShapeDtypeStruct((dm.R, dm.A), BF16), jax.ShapeDtypeStruct((dm.R, 2 * dm.KVW), BF16),
                   jax.ShapeDtypeStruct((1, LANES), F32), jax.ShapeDtypeStruct((1, LANES), F32)),
        compiler_params=_params(("arbitrary",)),
    )(p, p, dq, dkt, dvt, cos, sins, gq, gk, bd)


def _flash_fwd(dm, qh, kh, vx):
    tq, ks = dm.TR, dm.TR
    rs = min(tq, LANES)
    g, hd = dm.G, dm.HD
    nq, nq_lat = dm.R // tq, dm.N // tq
    n_chunks, ctx_chunks = dm.R // ks, dm.CTX // ks

    def body(q_ref, k_ref, v_ref, o_ref, lse_ref):
        qi = pl.program_id(1)

        def attend(chunks):
            for gg in range(g):
                for r in range(tq // rs):
                    rows = slice(r * rs, (r + 1) * rs)
                    q = q_ref[gg, rows, :]
                    m_prev = jnp.full((rs, LANES), -1e30, F32)
                    acc = jnp.zeros((rs, LANES), F32)
                    for kc in chunks:
                        keys = slice(kc * ks, (kc + 1) * ks)
                        s = lax.dot_general(q, k_ref[0, keys, :], _DN["nt"], preferred_element_type=F32)
                        m_new = jnp.maximum(m_prev, jnp.max(s, axis=-1, keepdims=True))
                        p = jnp.exp2(s - jnp.tile(m_new, (1, ks // LANES)))
                        acc = acc * jnp.exp2(m_prev - m_new) + jnp.dot(p.astype(BF16), v_ref[0, keys, :],
                                                                        preferred_element_type=F32)
                        m_prev = m_new
                    l = acc[:, hd:hd + 1]
                    o_ref[rows, gg * hd:(gg + 1) * hd] = acc[:, :hd] / l
                    lse_ref[gg, rows, :] = m_prev + jnp.log2(l)

        @pl.when(qi < nq_lat)
        def _():
            attend(range(n_chunks))

        @pl.when(qi >= nq_lat)
        def _():
            attend(range(n_chunks - ctx_chunks, n_chunks))

    return pl.pallas_call(
        body, name="flash_fwd", grid=(dm.NKV, nq),
        in_specs=[pl.BlockSpec((g, tq, hd), lambda h, qi: (h, qi, 0)), pl.BlockSpec((1, dm.R, hd), lambda h, qi: (h, 0, 0)),
                  pl.BlockSpec((1, dm.R, LANES), lambda h, qi: (h, 0, 0))],
        out_specs=(pl.BlockSpec((tq, g * hd), lambda h, qi: (qi, h)), pl.BlockSpec((g, tq, LANES), lambda h, qi: (h, qi, 0))),
        out_shape=(jax.ShapeDtypeStruct((dm.R, dm.A), F32), jax.ShapeDtypeStruct((dm.NH, dm.R, LANES), F32)),
        compiler_params=_params(("parallel", "parallel")),
    )(qh, kh, vx)


def _flash_bwd(dm, qh, qt, kh, vh, doh, dot, lse, delta):
    tq, ks = dm.TR, dm.TR
    g, hd = dm.G, dm.HD
    nq, nq_lat = dm.R // tq, dm.N // tq
    n_chunks, ctx_chunks = dm.R // ks, dm.CTX // ks

    def body(q_ref, qt_ref, k_ref, v_ref, do_ref, dot_ref, lse_ref, dl_ref, dq_ref, dk_acc, dv_acc):
        qi = pl.program_id(1)

        @pl.when(qi == 0)
        def _():
            dk_acc[...] = jnp.zeros_like(dk_acc)
            dv_acc[...] = jnp.zeros_like(dv_acc)

        def attend(chunks):
            q = q_ref[...].reshape(g * tq, hd)
            do = do_ref[...].reshape(g * tq, hd)
            q_t = jnp.concatenate([qt_ref[gg * hd:(gg + 1) * hd, :] for gg in range(g)], axis=1)
            do_t = jnp.concatenate([dot_ref[gg * hd:(gg + 1) * hd, :] for gg in range(g)], axis=1)
            wide = lambda ref: jnp.tile(ref[...].reshape(g * tq, LANES), (1, ks // LANES))
            dqs = jnp.zeros((g * tq, hd), F32)
            for kc in chunks:
                keys = slice(kc * ks, (kc + 1) * ks)
                k, v = k_ref[0, keys, :], v_ref[0, keys, :]
                s = lax.dot_general(q, k, _DN["nt"], preferred_element_type=F32)
                p = jnp.exp2(s - wide(lse_ref))
                dp = lax.dot_general(do, v, _DN["nt"], preferred_element_type=F32)
                ds = (p * (dp - wide(dl_ref))).astype(BF16)
                dv_acc[:, keys] += jnp.dot(do_t, p.astype(BF16), preferred_element_type=F32)
                dk_acc[:, keys] += jnp.dot(q_t, ds, preferred_element_type=F32)
                dqs = dqs + jnp.dot(ds, k, preferred_element_type=F32)
            for gg in range(g):
                dq_ref[:, gg * hd:(gg + 1) * hd] = dqs[gg * tq:(gg + 1) * tq]

        @pl.when(qi < nq_lat)
        def _():
            attend(range(n_chunks))

        @pl.when(qi >= nq_lat)
        def _():
            attend(range(n_chunks - ctx_chunks, n_chunks))

        @pl.when(qi == nq - 1)
        def _():
            dk_acc[...] = dk_acc[...] * LN2

    qspec = pl.BlockSpec((g, tq, hd), lambda h, qi: (h, qi, 0))
    tspec = pl.BlockSpec((g * hd, tq), lambda h, qi: (h, qi))
    kspec = pl.BlockSpec((1, dm.R, hd), lambda h, qi: (h, 0, 0))
    sspec = pl.BlockSpec((g, tq, LANES), lambda h, qi: (h, qi, 0))
    ospec = pl.BlockSpec((hd, dm.R), lambda h, qi: (h, 0))
    return pl.pallas_call(
        body, name="flash_bwd", grid=(dm.NKV, nq),
        in_specs=[qspec, tspec, kspec, kspec, qspec, tspec, sspec, sspec],
        out_specs=(pl.BlockSpec((tq, g * hd), lambda h, qi: (qi, h)), ospec, ospec),
        out_shape=(jax.ShapeDtypeStruct((dm.R, dm.A), F32), jax.ShapeDtypeStruct((dm.KVW, dm.R), F32),
                   jax.ShapeDtypeStruct((dm.KVW, dm.R), F32)),
        compiler_params=_params(("parallel", "arbitrary")),
    )(qh, qt, kh, vh, doh, dot, lse, delta)


def _gate_b_specs(dm):
    ogb = 3 * dm.C + dm.A + 2 * dm.KVW
    half = dm.A // 2
    return [_rows(dm.TR, half, _col(ogb, half)), _rows(dm.TR, half, _col(ogb, half) + 1)]


def _attn_gate(dm, o, p):
    tr = dm.TR

    def body(o_ref, g0_ref, g1_ref, za_ref):
        gate = jnp.concatenate([g0_ref[...], g1_ref[...]], axis=1).astype(F32)
        za_ref[...] = (o_ref[...] * jax.nn.silu(gate)).astype(BF16)

    return pl.pallas_call(
        body, name="attn_gate", grid=(dm.R // tr,),
        in_specs=[_rows(tr, dm.A)] + _gate_b_specs(dm), out_specs=_rows(tr, dm.A),
        out_shape=jax.ShapeDtypeStruct((dm.R, dm.A), BF16), compiler_params=_params(("parallel",)),
    )(o, p, p)


def _attn_gate_bwd(dm, dza, o, p):
    tr, hd = dm.TR, dm.HD

    def body(dza_ref, o_ref, g0_ref, g1_ref, doh_ref, dot_ref, dl_ref, dgb_ref):
        gate = jnp.concatenate([g0_ref[...], g1_ref[...]], axis=1).astype(F32)
        dza, o = dza_ref[...].astype(F32), o_ref[...]
        do = dza * jax.nn.silu(gate)
        dgb_ref[...] = (dza * o * _dsilu(gate)).astype(BF16)
        dot_ref[...] = do.T.astype(BF16)
        prod = do * o
        for h in range(dm.NH):
            sl = slice(h * hd, (h + 1) * hd)
            doh_ref[h] = do[:, sl].astype(BF16)
            dl_ref[h] = jnp.broadcast_to(jnp.sum(prod[:, sl], axis=-1, keepdims=True), (tr, LANES))

    return pl.pallas_call(
        body, name="attn_gate_bwd", grid=(dm.R // tr,),
        in_specs=[_rows(tr, dm.A), _rows(tr, dm.A)] + _gate_b_specs(dm),
        out_specs=(pl.BlockSpec((dm.NH, tr, hd), lambda i: (0, i, 0)), pl.BlockSpec((dm.A, tr), lambda i: (0, i)),
                   pl.BlockSpec((dm.NH, tr, LANES), lambda i: (0, i, 0)), _rows(tr, dm.A)),
        out_shape=(jax.ShapeDtypeStruct((dm.NH, dm.R, hd), BF16), jax.ShapeDtypeStruct((dm.A, dm.R), BF16),
                   jax.ShapeDtypeStruct((dm.NH, dm.R, LANES), F32), jax.ShapeDtypeStruct((dm.R, dm.A), BF16)),
        compiler_params=_params(("parallel",)),
    )(dza, o, p, p)


def _halo_specs(dm, col):
    per = dm.TR // HALO
    last = dm.R // HALO - 1
    prev = pl.BlockSpec((HALO, dm.C), lambda i: (jnp.maximum(i * per - 1, 0), col))
    nxt = pl.BlockSpec((HALO, dm.C), lambda i: (jnp.minimum((i + 1) * per, last), col))
    return prev, nxt


def _seg_valid(dm, i):
    tr = dm.TR
    r = i * tr - HALO + lax.broadcasted_iota(jnp.int32, (tr + 2 * HALO, 1), 0)
    is_ctx = i * tr >= dm.N
    lo = jnp.where(is_ctx, dm.N, 0)
    hi = jnp.where(is_ctx, dm.R, dm.N)
    return jnp.logical_and(r >= lo, r < hi)


CONV_RB = 64


def _shifted_copies(src_ref, sl, sh_ref, tr):
    rows = tr + 2 * HALO - SUBLANES
    padded = src_ref[:, sl]
    for ph in range(SUBLANES):
        sh_ref[ph, 0:rows, :] = padded[ph:ph + rows]


def _tap_rows(sh_ref, off, n):
    ph = off % SUBLANES
    return sh_ref[ph, off - ph:off - ph + n, :]


def _conv_fwd(dm, p, conv_w_l, conv_b_l, ln_g_l, ln_b_l):
    tr, c = dm.TR, dm.C

    def body(a_ref, g_ref, gate_ref, ap_ref, an_ref, gp_ref, gn_ref, w_ref, b_ref, lg_ref, lb_ref,
             zc_ref, y1_ref, u_ref, sh_ref):
        i = pl.program_id(0)
        a = jnp.concatenate([ap_ref[...], a_ref[...], an_ref[...]], axis=0).astype(F32)
        g = jnp.concatenate([gp_ref[...], g_ref[...], gn_ref[...]], axis=0).astype(F32)
        u_ref[...] = jnp.where(_seg_valid(dm, i), a * jax.nn.sigmoid(g), 0.0)
        for cc in range(c // LANES):
            sl = slice(cc * LANES, (cc + 1) * LANES)
            _shifted_copies(u_ref, sl, sh_ref, tr)
            for r0 in range(0, tr, CONV_RB):
                acc = jnp.zeros((CONV_RB, LANES), F32)
                for j in range(CONV_KERNEL):
                    acc = acc + w_ref[j:j + 1, sl] * _tap_rows(sh_ref, r0 + HALO - CONV_PAD + j, CONV_RB)
                y1_ref[r0:r0 + CONV_RB, sl] = acc + b_ref[:, sl]
        zc_ref[...] = _convact_fn(y1_ref[...], gate_ref[...].astype(F32), lg_ref[...], lb_ref[...]).astype(BF16)

    ap, an = _halo_specs(dm, 0)
    gp, gn = _halo_specs(dm, 1)
    return pl.pallas_call(
        body, name="conv_fwd", grid=(dm.R // tr,),
        in_specs=[_rows(tr, c, 0), _rows(tr, c, 1), _rows(tr, c, 2), ap, an, gp, gn,
                  _whole((32, c)), _whole((1, c)), _whole((1, c)), _whole((1, c))],
        out_specs=(_rows(tr, c), _rows(tr, c)),
        out_shape=(jax.ShapeDtypeStruct((dm.R, c), BF16), jax.ShapeDtypeStruct((dm.R, c), F32)),
        scratch_shapes=[pltpu.VMEM((tr + 2 * HALO, c), F32), pltpu.VMEM((SUBLANES, tr + 2 * HALO, LANES), F32)],
        compiler_params=_params(("parallel",)),
    )(p, p, p, p, p, p, p, conv_w_l, conv_b_l, ln_g_l, ln_b_l)


def _conv_act_bwd(dm, dzc, y1, p, ln_g_l, ln_b_l):
    tr, c = dm.TR, dm.C

    def body(dz_ref, y1_ref, gate_ref, lg_ref, lb_ref, dy_ref, dgate_ref, dlg_ref, dlb_ref):
        i = pl.program_id(0)
        _, vjp = jax.vjp(_convact_fn, y1_ref[...], gate_ref[...].astype(F32), lg_ref[...], lb_ref[...])
        dy, dgate, dlg, dlb = vjp(dz_ref[...].astype(F32))
        dy_ref[...] = dy
        dgate_ref[...] = dgate.astype(BF16)

        @pl.when(i == 0)
        def _():
            dlg_ref[...] = jnp.zeros_like(dlg_ref)
            dlb_ref[...] = jnp.zeros_like(dlb_ref)

        dlg_ref[...] += dlg
        dlb_ref[...] += dlb

    return pl.pallas_call(
        body, name="conv_act_bwd", grid=(dm.R // tr,),
        in_specs=[_rows(tr, c), _rows(tr, c), _rows(tr, c, 2), _whole((1, c)), _whole((1, c))],
        out_specs=(_rows(tr, c), _rows(tr, c), _whole((1, c)), _whole((1, c))),
        out_shape=(jax.ShapeDtypeStruct((dm.R, c), F32), jax.ShapeDtypeStruct((dm.R, c), BF16),
                   jax.ShapeDtypeStruct((1, c), F32), jax.ShapeDtypeStruct((1, c), F32)),
        compiler_params=_params(("arbitrary",)),
    )(dzc, y1, p, ln_g_l, ln_b_l)


def _conv_bwd(dm, dy1, p, conv_w_l):
    tr, c = dm.TR, dm.C

    def body(dy_ref, dyp_ref, dyn_ref, a_ref, g_ref, ap_ref, an_ref, gp_ref, gn_ref, w_ref,
             dag_ref, dw_ref, db_ref, u_ref, dyp_scr, du_ref, dwp_ref, shu_ref, shd_ref):
        i = pl.program_id(0)
        valid = _seg_valid(dm, i)
        a = jnp.concatenate([ap_ref[...], a_ref[...], an_ref[...]], axis=0).astype(F32)
        g = jnp.concatenate([gp_ref[...], g_ref[...], gn_ref[...]], axis=0).astype(F32)
        u_ref[...] = jnp.where(valid, a * jax.nn.sigmoid(g), 0.0)
        dyp_scr[...] = jnp.where(valid, jnp.concatenate([dyp_ref[...], dy_ref[...], dyn_ref[...]], axis=0), 0.0)

        @pl.when(i == 0)
        def _():
            dwp_ref[...] = jnp.zeros_like(dwp_ref)
            db_ref[...] = jnp.zeros_like(db_ref)

        db_ref[...] += jnp.sum(dy_ref[...], axis=0, keepdims=True)
        for cc in range(c // LANES):
            sl = slice(cc * LANES, (cc + 1) * LANES)
            _shifted_copies(u_ref, sl, shu_ref, tr)
            _shifted_copies(dyp_scr, sl, shd_ref, tr)
            for r0 in range(0, tr, CONV_RB):
                dy_cur = dy_ref[r0:r0 + CONV_RB, sl]
                acc = jnp.zeros((CONV_RB, LANES), F32)
                for j in range(CONV_KERNEL):
                    acc = acc + w_ref[j:j + 1, sl] * _tap_rows(shd_ref, r0 + HALO + CONV_PAD - j, CONV_RB)
                    prod = dy_cur * _tap_rows(shu_ref, r0 + HALO - CONV_PAD + j, CONV_RB)
                    dwp_ref[j, :, sl] += jnp.sum(prod.reshape(CONV_RB // SUBLANES, SUBLANES, LANES), axis=0)
                du_ref[r0:r0 + CONV_RB, sl] = acc
        du = du_ref[...]
        sg = jax.nn.sigmoid(g_ref[...].astype(F32))
        dag_ref[:, :c] = (du * sg).astype(BF16)
        dag_ref[:, c:] = (du * a_ref[...].astype(F32) * sg * (1.0 - sg)).astype(BF16)

        @pl.when(i == dm.R // tr - 1)
        def _():
            dw_ref[...] = jnp.sum(dwp_ref[...], axis=1)

    ap, an = _halo_specs(dm, 0)
    gp, gn = _halo_specs(dm, 1)
    dyp, dyn = _halo_specs(dm, 0)
    return pl.pallas_call(
        body, name="conv_bwd", grid=(dm.R // tr,),
        in_specs=[_rows(tr, c), dyp, dyn, _rows(tr, c, 0), _rows(tr, c, 1), ap, an, gp, gn, _whole((32, c))],
        out_specs=(_rows(tr, 2 * c), _whole((32, c)), _whole((1, c))),
        out_shape=(jax.ShapeDtypeStruct((dm.R, 2 * c), BF16), jax.ShapeDtypeStruct((32, c), F32),
                   jax.ShapeDtypeStruct((1, c), F32)),
        scratch_shapes=[pltpu.VMEM((tr + 2 * HALO, c), F32), pltpu.VMEM((tr + 2 * HALO, c), F32), pltpu.VMEM((tr, c), F32),
                        pltpu.VMEM((32, SUBLANES, c), F32), pltpu.VMEM((SUBLANES, tr + 2 * HALO, LANES), F32),
                        pltpu.VMEM((SUBLANES, tr + 2 * HALO, LANES), F32)],
        compiler_params=_params(("arbitrary",)),
    )(dy1, dy1, dy1, p, p, p, p, p, p, conv_w_l)


def _gm_specs(dm):
    half = dm.D // 2
    first = _col(3 * dm.C + 2 * dm.A + 2 * dm.KVW, half)
    return [_rows(dm.TR, half, first + k) for k in range(4)]


def _gm_load(a0, a1, b0, b1):
    return (jnp.concatenate([a0[...], a1[...]], axis=1).astype(F32),
            jnp.concatenate([b0[...], b1[...]], axis=1).astype(F32))


def _merge(dm, yc, ya, p):
    tr, d = dm.TR, dm.D

    def body(yc_ref, ya_ref, a0, a1, b0, b1, m_ref):
        ga, gb = _gm_load(a0, a1, b0, b1)
        m_ref[...] = (jax.nn.sigmoid(ga) * yc_ref[...] + jax.nn.sigmoid(gb) * ya_ref[...]).astype(BF16)

    return pl.pallas_call(
        body, name="merge", grid=(dm.R // tr,),
        in_specs=[_rows(tr, d), _rows(tr, d)] + _gm_specs(dm), out_specs=_rows(tr, d),
        out_shape=jax.ShapeDtypeStruct((dm.R, d), BF16), compiler_params=_params(("parallel",)),
    )(yc, ya, p, p, p, p)


def _merge_bwd(dm, dmerged, yc, ya, p):
    tr, d = dm.TR, dm.D

    def body(dm_ref, yc_ref, ya_ref, a0, a1, b0, b1, dyc_ref, dya_ref, dgm_ref):
        dmv = dm_ref[...].astype(F32)
        ga, gb = _gm_load(a0, a1, b0, b1)
        sa, sb = jax.nn.sigmoid(ga), jax.nn.sigmoid(gb)
        dyc_ref[...] = (dmv * sa).astype(BF16)
        dya_ref[...] = (dmv * sb).astype(BF16)
        dgm_ref[:, :d] = (dmv * yc_ref[...] * sa * (1.0 - sa)).astype(BF16)
        dgm_ref[:, d:] = (dmv * ya_ref[...] * sb * (1.0 - sb)).astype(BF16)

    return pl.pallas_call(
        body, name="merge_bwd", grid=(dm.R // tr,),
        in_specs=[_rows(tr, d), _rows(tr, d), _rows(tr, d)] + _gm_specs(dm),
        out_specs=(_rows(tr, d), _rows(tr, d), _rows(tr, 2 * d)),
        out_shape=(jax.ShapeDtypeStruct((dm.R, d), BF16), jax.ShapeDtypeStruct((dm.R, d), BF16),
                   jax.ShapeDtypeStruct((dm.R, 2 * d), BF16)),
        compiler_params=_params(("parallel",)),
    )(dmerged, yc, ya, p, p, p, p)


def _loss_head(dm, xc, target):
    tr, d = dm.TR, dm.D
    n_lat = dm.N // tr

    def body(x_ref, t_ref, dy_ref, l_ref):
        i = pl.program_id(0)

        @pl.when(i == 0)
        def _():
            l_ref[...] = jnp.zeros_like(l_ref)

        @pl.when(i < n_lat)
        def _():
            e = x_ref[...] - t_ref[...]
            dy_ref[...] = e * (1.0 / d)
            l_ref[...] += 0.5 * jnp.sum(jnp.mean(e * e, axis=-1, keepdims=True), axis=0, keepdims=True)

        @pl.when(i >= n_lat)
        def _():
            dy_ref[...] = jnp.zeros_like(dy_ref)

    return pl.pallas_call(
        body, name="loss_head", grid=(dm.R // tr,),
        in_specs=[_rows(tr, d), pl.BlockSpec((tr, d), lambda i: (jnp.minimum(i, n_lat - 1), 0))],
        out_specs=(_rows(tr, d), _whole((1, 1))),
        out_shape=(jax.ShapeDtypeStruct((dm.R, d), F32), jax.ShapeDtypeStruct((1, 1), F32)),
        compiler_params=_params(("arbitrary",)),
    )(xc, target)


def _adamw(w, g, m, v, name):
    rows, cols = w.shape
    fits = lambda t: t * cols * 4 <= ADAM_BLOCK_BYTES
    tr = rows if fits(rows) else _pick(rows, [t for t in (512, 256, 128, 64, 32, 16, 8) if fits(t)])

    def body(w_ref, g_ref, m_ref, v_ref, d_ref, mo_ref, vo_ref):
        d_ref[...], mo_ref[...], vo_ref[...] = _adam_update(w_ref[...], g_ref[...], m_ref[...], v_ref[...])

    spec = _rows(tr, cols)
    shp = jax.ShapeDtypeStruct(w.shape, F32)
    return pl.pallas_call(
        body, name=name, grid=(rows // tr,), in_specs=[spec] * 4, out_specs=(spec,) * 3, out_shape=(shp,) * 3,
        compiler_params=_params(("parallel",)),
    )(w, g, m, v)


def _adam_update(w, g, m, v):
    mn = ADAM_B1 * m + (1.0 - ADAM_B1) * g
    vn = ADAM_B2 * v + (1.0 - ADAM_B2) * jnp.square(g)
    m_hat = mn / (1.0 - ADAM_B1 ** ADAM_STEP)
    v_hat = vn / (1.0 - ADAM_B2 ** ADAM_STEP)
    return -ADAM_LR * (m_hat / (jnp.sqrt(v_hat) + ADAM_EPS) + ADAM_WD * w), mn, vn


def _rope_tables(dm):
    n = dm.N
    rows = n // GRID_W
    row = jnp.repeat(jnp.arange(rows, dtype=F32), GRID_W)
    col = jnp.tile(jnp.arange(GRID_W, dtype=F32), rows)
    half = dm.HD // 2
    inv_freq = ROPE_THETA ** (-jnp.arange(0, half, 2, dtype=F32) / half)
    ang = jnp.concatenate([row[:, None] * inv_freq, col[:, None] * inv_freq], axis=-1)
    cos, sin = jnp.cos(ang), jnp.sin(ang)
    cos = jnp.concatenate([cos, jnp.ones((dm.CTX, half), F32)], axis=0)
    sin = jnp.concatenate([sin, jnp.zeros((dm.CTX, half), F32)], axis=0)
    reps = LANES // dm.HD
    return jnp.tile(jnp.concatenate([cos, cos], axis=1), (1, reps)), jnp.tile(jnp.concatenate([-sin, sin], axis=1), (1, reps))


def _lane_vec(g, hd):
    return jnp.tile(g.reshape(1, hd), (1, LANES // hd))


def _layer_fwd(dm, xc, cv, wl, tabs):
    mod = _mod_fwd(cv, wl["w_mod"], wl["b_mod"])
    h = _prenorm(dm, xc, wl["g_pre"], mod)
    p = _matmul(h, wl["w_in"], "nn", "in_proj", BF16)
    qh, qt, kh, vh, vx = _qk_prep(dm, p, tabs["cos"], tabs["sins"], wl["gq"], wl["gk"], tabs["bd"])
    o, lse = _flash_fwd(dm, qh, kh, vx)
    za = _attn_gate(dm, o, p)
    zc, y1 = _conv_fwd(dm, p, wl["conv_w"], wl["conv_b"], wl["ln_g"], wl["ln_b"])
    yc = _matmul(zc, wl["w_conv_out"], "nn", "conv_out")
    ya = _matmul(za, wl["w_attn_out"], "nn", "attn_out")
    merged = _merge(dm, yc, ya, p)
    outp = _matmul(merged, wl["w_out"], "nn", "out_proj")
    x_new = _post(dm, xc, outp, wl["g_post"], mod)
    saved = dict(xc=xc, mod=mod, h=h, p=p, qh=qh, qt=qt, kh=kh, vh=vh, o=o, lse=lse, za=za, zc=zc, y1=y1, yc=yc, ya=ya,
                 merged=merged, outp=outp)
    return x_new, saved


def _layer_bwd(dm, dxn, cv, wl, tabs, s):
    p = s["p"]
    d_outp, dg_post, dgt = _post_bwd(dm, s["outp"], wl["g_post"], s["mod"], dxn)
    dmerged = _matmul(d_outp, wl["w_out"], "nt", "out_proj_dx", BF16)
    dw_out = _matmul(s["merged"], d_outp, "tn", "out_proj_dw")
    dyc, dya, dgm = _merge_bwd(dm, dmerged, s["yc"], s["ya"], p)
    dzc = _matmul(dyc, wl["w_conv_out"], "nt", "conv_out_dx", BF16)
    dw_conv_out = _matmul(s["zc"], dyc, "tn", "conv_out_dw")
    dza = _matmul(dya, wl["w_attn_out"], "nt", "attn_out_dx", BF16)
    dw_attn_out = _matmul(s["za"], dya, "tn", "attn_out_dw")
    doh, dot, delta, dgate_b = _attn_gate_bwd(dm, dza, s["o"], p)
    dq, dkt, dvt = _flash_bwd(dm, s["qh"], s["qt"], s["kh"], s["vh"], doh, dot, s["lse"], delta)
    dpq, dpkv, dgq, dgk = _qk_bwd(dm, p, dq, dkt, dvt, tabs["cos"], tabs["sins"], wl["gq"], wl["gk"], tabs["bd"])
    dy1, dgate_a, dln_g, dln_b = _conv_act_bwd(dm, dzc, s["y1"], p, wl["ln_g"], wl["ln_b"])
    dpag, dconv_w, dconv_b = _conv_bwd(dm, dy1, p, wl["conv_w"])
    dp = jnp.concatenate([dpag, dgate_a, dpq, dpkv, dgate_b, dgm], axis=1)
    dh = _matmul(dp, wl["w_in"], "nt", "in_proj_dx", BF16)
    dw_in = _matmul(s["h"], dp, "tn", "in_proj_dw")
    dxc, dg_pre, dshsc = _prenorm_bwd(dm, s["xc"], wl["g_pre"], s["mod"], dh, dxn)
    dmod = jnp.concatenate([dshsc, dgt], axis=1)
    dw_mod, db_mod, dcv = _mod_bwd(cv, wl["w_mod"], dmod)
    grads = dict(w_mod=dw_mod, b_mod=db_mod[0], g_pre=dg_pre[0], g_post=dg_post[0], w_in=dw_in,
                 conv_w=dconv_w[:CONV_KERNEL], conv_b=dconv_b[0], ln_g=dln_g[0], ln_b=dln_b[0],
                 w_conv_out=dw_conv_out, q_norm_g=dgq[0, :dm.HD], k_norm_g=dgk[0, :dm.HD],
                 w_attn_out=dw_attn_out, w_out=dw_out, c_ctx=dcv[1])
    return dxc, grads


def _local_step(dm, x, c, ctx, c_ctx, target, wf):
    depth = wf["w_in"].shape[0]
    cos, sins = _rope_tables(dm)
    lane = np.arange(LANES)
    bd = jnp.asarray((lane[:, None] // dm.HD == lane[None, :] // dm.HD), dtype=BF16)
    tabs = dict(cos=cos, sins=sins, bd=bd)
    cv = jnp.concatenate([c.reshape(1, dm.D), c_ctx.reshape(1, dm.D), jnp.zeros((6, dm.D), F32)], axis=0)
    xc = jnp.concatenate([x, ctx], axis=0)
    layers, saved = [], []
    for l in range(depth):
        wl = dict(
            w_mod=wf["w_mod"][l], b_mod=wf["b_mod"][l].reshape(1, -1), g_pre=wf["g_pre"][l].reshape(1, -1),
            g_post=wf["g_post"][l].reshape(1, -1), w_in=wf["w_in"][l],
            conv_w=jnp.concatenate([wf["conv_w"][l], jnp.zeros((1, dm.C), F32)], axis=0),
            conv_b=wf["conv_b"][l].reshape(1, -1), ln_g=wf["ln_g"][l].reshape(1, -1), ln_b=wf["ln_b"][l].reshape(1, -1),
            w_conv_out=wf["w_conv_out"][l], gq=_lane_vec(wf["q_norm_g"][l], dm.HD), gk=_lane_vec(wf["k_norm_g"][l], dm.HD),
            w_attn_out=wf["w_attn_out"][l], w_out=wf["w_out"][l])
        layers.append(wl)
        xc, s = _layer_fwd(dm, xc, cv, wl, tabs)
        saved.append(s)
    dxc, loss = _loss_head(dm, xc, target)
    grads = [None] * depth
    for l in reversed(range(depth)):
        dxc, grads[l] = _layer_bwd(dm, dxc, cv, layers[l], tabs, saved[l])
    big = ("w_mod", "w_in", "w_conv_out", "w_attn_out", "w_out")
    gw = {k: [g[k] for g in grads] if k in big else jnp.stack([g[k] for g in grads]) for k in grads[0] if k != "c_ctx"}
    gw["c_ctx"] = grads[0]["c_ctx"] + grads[1]["c_ctx"] if depth == 2 else sum(g["c_ctx"] for g in grads)
    return loss[0, 0], dxc[:dm.N], gw


def _dims(x, ctx, w_in_full_cols, conv_c, attn_w, hd):
    n, d = x.shape
    ctx_len = ctx.shape[0]
    c, a = conv_c, attn_w
    kvw = (w_in_full_cols - 3 * c - 2 * a - 2 * d) // 2
    r = n + ctx_len
    tr = _pick(np.gcd(n, ctx_len), (256, 128))
    return Dims(N=n, CTX=ctx_len, R=r, D=d, C=c, A=a, KVW=kvw, HD=hd, NH=a // hd, NKV=kvw // hd,
                G=a // kvw, W=w_in_full_cols, TR=tr)


_HBM = pl.BlockSpec(memory_space=pltpu.HBM)
_VMEM = pl.BlockSpec(memory_space=pltpu.VMEM)


def _place():
    x, y, c = lax.axis_index("x"), lax.axis_index("y"), lax.axis_index("c")
    chips = [(1 - x, y), (x, 1 - y), (1 - x, 1 - y)]
    return x, y, c, chips


def _row_pieces(rows, want):
    n = max(k for k in range(1, want + 1) if rows % (16 * k) == 0)
    return [pl.ds(i * (rows // n), rows // n) for i in range(n)]


def _allgather_chips(own, name):
    _, h, w = own.shape
    pieces = _row_pieces(h, 4)
    npc = len(pieces)

    def body(own_ref, out_ref, send_sems, recv_sems):
        x, y, c, chips = _place()
        s_me = 2 * x + y
        sibling = (x, y, 1 - c)

        def copy(k, src, dst, to):
            return pltpu.make_async_remote_copy(src_ref=src, dst_ref=dst, send_sem=send_sems.at[k], recv_sem=recv_sems.at[k],
                                                device_id=to, device_id_type=MESH)

        sends = []
        for i, rows in enumerate(pieces):
            for j, chip in enumerate(chips):
                sends.append(copy(i * 3 + j, own_ref.at[c, rows], out_ref.at[s_me, c, rows], (*chip, c)))
        own_slots = [(6 * npc + hf * npc + i, hf, rows) for hf in range(2) for i, rows in enumerate(pieces)]
        for k, hf, rows in own_slots:
            sends.append(copy(k, own_ref.at[hf, rows], out_ref.at[s_me, hf, rows], sibling))
        for cp in sends:
            cp.start()
        for i, rows in enumerate(pieces):
            for j, (cx, cy) in enumerate(chips):
                landed = out_ref.at[2 * cx + cy, c, rows]
                copy(i * 3 + j, landed, landed, (cx, cy, c)).wait_recv()
                fwd = copy(3 * npc + i * 3 + j, landed, landed, sibling)
                fwd.start()
                sends.append(fwd)
        for i, rows in enumerate(pieces):
            for j, (cx, cy) in enumerate(chips):
                theirs = out_ref.at[2 * cx + cy, 1 - c, rows]
                copy(3 * npc + i * 3 + j, theirs, theirs, sibling).wait_recv()
        for k, hf, rows in own_slots:
            mine = out_ref.at[s_me, hf, rows]
            copy(k, mine, mine, sibling).wait_recv()
        for cp in sends:
            cp.wait_send()

    return pl.pallas_call(
        body, name=name, in_specs=[_HBM], out_specs=_HBM,
        out_shape=jax.ShapeDtypeStruct((N_CHIPS, 2, h, w), own.dtype),
        scratch_shapes=[pltpu.SemaphoreType.DMA((8 * npc,)), pltpu.SemaphoreType.DMA((8 * npc,))],
    )(own)


def _pair_exchange(buf, name):
    _, m, w = buf.shape
    pieces = _row_pieces(m, 16)

    def body(buf_ref, out_ref, send_sems, recv_sems):
        x, y, c, _ = _place()
        cps = [pltpu.make_async_remote_copy(src_ref=buf_ref.at[1 - c, rows], dst_ref=out_ref.at[rows], send_sem=send_sems.at[i],
                                            recv_sem=recv_sems.at[i], device_id=(x, y, 1 - c), device_id_type=MESH)
               for i, rows in enumerate(pieces)]
        for cp in cps:
            cp.start()
        for cp in cps:
            cp.wait()

    return pl.pallas_call(
        body, name=name, in_specs=[_HBM], out_specs=_HBM, out_shape=jax.ShapeDtypeStruct((m, w), buf.dtype),
        scratch_shapes=[pltpu.SemaphoreType.DMA((len(pieces),)), pltpu.SemaphoreType.DMA((len(pieces),))],
    )(buf)


def _pair_swap(t, name):
    m, w = t.shape
    pieces = _row_pieces(m, 8)
    n = len(pieces)

    def body(t_ref, out_ref, send_sems, recv_sems):
        x, y, c, _ = _place()
        cps = [pltpu.make_async_remote_copy(src_ref=t_ref.at[rows], dst_ref=out_ref.at[rows], send_sem=send_sems.at[i],
                                            recv_sem=recv_sems.at[i], device_id=(x, y, 1 - c), device_id_type=MESH)
               for i, rows in enumerate(pieces)]
        for cp in cps:
            cp.start()
        for cp in cps:
            cp.wait()

    return pl.pallas_call(
        body, name=name, in_specs=[_HBM], out_specs=_HBM, out_shape=jax.ShapeDtypeStruct((m, w), t.dtype),
        scratch_shapes=[pltpu.SemaphoreType.DMA((n,)), pltpu.SemaphoreType.DMA((n,))],
    )(t)


def _scatter_chips(s, name):
    _, h, w = s.shape

    def body(s_ref, out_ref, send_sems, recv_sems):
        _, _, c, chips = _place()
        cps = [pltpu.make_async_remote_copy(src_ref=s_ref.at[2 * cx + cy], dst_ref=out_ref.at[j], send_sem=send_sems.at[j],
                                            recv_sem=recv_sems.at[j], device_id=(cx, cy, c), device_id_type=MESH)
               for j, (cx, cy) in enumerate(chips)]
        for cp in cps:
            cp.start()
        for cp in cps:
            cp.wait()

    return pl.pallas_call(
        body, name=name, in_specs=[_HBM], out_specs=_HBM, out_shape=jax.ShapeDtypeStruct((3, h, w), s.dtype),
        scratch_shapes=[pltpu.SemaphoreType.DMA((3,)), pltpu.SemaphoreType.DMA((3,))],
    )(s)


def _sum_pair(buf, recv, core, name):
    m, w = recv.shape
    tr = _pick(m, (512, 432, 256, 128, 64, 32, 16))

    def body(c_ref, b_ref, r_ref, o_ref, ob_ref):
        v = b_ref[0] + r_ref[...]
        o_ref[...] = v
        ob_ref[...] = v.astype(BF16)

    spec = pl.BlockSpec((tr, w), lambda i, c: (i, 0))
    return pl.pallas_call(
        body, name=name,
        grid_spec=pltpu.PrefetchScalarGridSpec(
            num_scalar_prefetch=1, grid=(m // tr,),
            in_specs=[pl.BlockSpec((1, tr, w), lambda i, c: (c[0], i, 0)), spec], out_specs=(spec, spec)),
        out_shape=(jax.ShapeDtypeStruct((m, w), F32), jax.ShapeDtypeStruct((m, w), BF16)),
        compiler_params=_params(("parallel",)),
    )(core, buf, recv)


def _sum_chips(own, recv, chip, name):
    _, h, w = own.shape
    tr = _pick(h, (512, 432, 256, 128, 64, 32, 16))

    def body(s_ref, a_ref, r_ref, o_ref):
        acc = a_ref[0]
        for j in range(3):
            acc = acc + r_ref[j].astype(F32)
        o_ref[...] = acc

    return pl.pallas_call(
        body, name=name,
        grid_spec=pltpu.PrefetchScalarGridSpec(
            num_scalar_prefetch=1, grid=(h // tr,),
            in_specs=[pl.BlockSpec((1, tr, w), lambda i, s: (s[0], i, 0)), pl.BlockSpec((3, tr, w), lambda i, s: (0, i, 0))],
            out_specs=pl.BlockSpec((tr, w), lambda i, s: (i, 0))),
        out_shape=jax.ShapeDtypeStruct((h, w), F32), compiler_params=_params(("parallel",)),
    )(chip, own, recv)


def _allreduce_small(own, name):
    m, w = own.shape

    def body(own_ref, sum_ref, all_ref, send_sems, recv_sems):
        x, y, c, _ = _place()
        me = 4 * x + 2 * y + c
        all_ref[me] = own_ref[...]
        sends = []
        for k in range(1, N_DEV):
            peer = (x ^ ((k >> 2) & 1), y ^ ((k >> 1) & 1), c ^ (k & 1))
            cp = pltpu.make_async_remote_copy(src_ref=own_ref, dst_ref=all_ref.at[me], send_sem=send_sems.at[k - 1],
                                              recv_sem=recv_sems.at[k - 1], device_id=peer, device_id_type=MESH)
            cp.start()
            sends.append(cp)
        for k in range(1, N_DEV):
            px, py, pc = x ^ ((k >> 2) & 1), y ^ ((k >> 1) & 1), c ^ (k & 1)
            slot = all_ref.at[4 * px + 2 * py + pc]
            pltpu.make_async_remote_copy(src_ref=slot, dst_ref=slot, send_sem=send_sems.at[k - 1], recv_sem=recv_sems.at[k - 1],
                                         device_id=(px, py, pc), device_id_type=MESH).wait_recv()
        for cp in sends:
            cp.wait_send()
        acc = all_ref[0]
        for d in range(1, N_DEV):
            acc = acc + all_ref[d]
        sum_ref[...] = acc

    return pl.pallas_call(
        body, name=name, in_specs=[_VMEM], out_specs=_VMEM, out_shape=jax.ShapeDtypeStruct((m, w), F32),
        scratch_shapes=[pltpu.VMEM((N_DEV, m, w), F32), pltpu.SemaphoreType.DMA((N_DEV - 1,)), pltpu.SemaphoreType.DMA((N_DEV - 1,))],
        compiler_params=pltpu.CompilerParams(vmem_limit_bytes=VMEM_LIMIT),
    )(own)


_COL_SHARDED = ("w_mod", "w_in")
_ROW_SHARDED = ("w_conv_out", "w_attn_out", "w_out")
_BIG = _COL_SHARDED + _ROW_SHARDED
_SMALL = ("c_ctx", "b_mod", "g_pre", "g_post", "conv_b", "ln_g", "ln_b", "q_norm_g", "k_norm_g")
FLAT_W = 1024


def _unflatten_shard(flat, like):
    out, off = {}, 0
    flat = flat.reshape(-1)
    for k in _BIG:
        n = int(np.prod(like[k].shape))
        out[k] = flat[off:off + n].reshape(like[k].shape)
        off += n
    return out


def _by_shard(k, layers):
    parts = []
    for full in layers:
        if k in _COL_SHARDED:
            d, cols = full.shape
            parts.append(full.reshape(d, N_CHIPS, cols // N_CHIPS).transpose(1, 0, 2).reshape(N_CHIPS, -1))
        else:
            parts.append(full.reshape(N_CHIPS, -1))
    return jnp.concatenate(parts, axis=1)


def _from_shards(k, flat4, like):
    l = like.shape[0]
    if k in _COL_SHARDED:
        d, cs = like.shape[1:]
        return flat4.reshape(N_CHIPS, l, d, cs).transpose(1, 2, 0, 3).reshape(l, d, N_CHIPS * cs)
    rs, d = like.shape[1:]
    return flat4.reshape(N_CHIPS, l, rs, d).transpose(1, 0, 2, 3).reshape(l, N_CHIPS * rs, d)


def _pad_rows(flat, mult=8):
    n = flat.shape[0]
    rows = -(-n // LANES)
    rows = -(-rows // mult) * mult
    return jnp.concatenate([flat, jnp.zeros((rows * LANES - n,), F32)]).reshape(rows, LANES)


def kernel(x, c, ctx, c_ctx, w_mod, b_mod, g_pre, g_post, w_in, conv_w, conv_b, ln_g, ln_b, w_conv_out, q_norm_g, k_norm_g, w_attn_out, w_out, loss_target, m_c_ctx, m_w_mod, m_b_mod, m_g_pre, m_g_post, m_w_in, m_conv_w, m_conv_b, m_ln_g, m_ln_b, m_w_conv_out, m_q_norm_g, m_k_norm_g, m_w_attn_out, m_w_out, v_c_ctx, v_w_mod, v_b_mod, v_g_pre, v_g_post, v_w_in, v_conv_w, v_conv_b, v_ln_g, v_ln_b, v_w_conv_out, v_q_norm_g, v_k_norm_g, v_w_attn_out, v_w_out):
    w = dict(c_ctx=c_ctx, w_mod=w_mod, b_mod=b_mod, g_pre=g_pre, g_post=g_post, w_in=w_in, conv_w=conv_w, conv_b=conv_b,
             ln_g=ln_g, ln_b=ln_b, w_conv_out=w_conv_out, q_norm_g=q_norm_g, k_norm_g=k_norm_g, w_attn_out=w_attn_out, w_out=w_out)
    m = dict(c_ctx=m_c_ctx, w_mod=m_w_mod, b_mod=m_b_mod, g_pre=m_g_pre, g_post=m_g_post, w_in=m_w_in, conv_w=m_conv_w,
             conv_b=m_conv_b, ln_g=m_ln_g, ln_b=m_ln_b, w_conv_out=m_w_conv_out, q_norm_g=m_q_norm_g, k_norm_g=m_k_norm_g,
             w_attn_out=m_w_attn_out, w_out=m_w_out)
    v = dict(c_ctx=v_c_ctx, w_mod=v_w_mod, b_mod=v_b_mod, g_pre=v_g_pre, g_post=v_g_post, w_in=v_w_in, conv_w=v_conv_w,
             conv_b=v_conv_b, ln_g=v_ln_g, ln_b=v_ln_b, w_conv_out=v_w_conv_out, q_norm_g=v_q_norm_g, k_norm_g=v_k_norm_g,
             w_attn_out=v_w_attn_out, w_out=v_w_out)
    order = ("c_ctx", "w_mod", "b_mod", "g_pre", "g_post", "w_in", "conv_w", "conv_b", "ln_g", "ln_b", "w_conv_out",
             "q_norm_g", "k_norm_g", "w_attn_out", "w_out")
    my_chip = 2 * lax.axis_index("x") + lax.axis_index("y")
    my_core = lax.axis_index("c")
    depth, _, conv_cs = conv_w.shape
    conv_c = conv_cs * N_CHIPS
    dm = _dims(x[0], ctx[0], w_in.shape[2] * N_CHIPS, conv_c, w_attn_out.shape[1] * N_CHIPS, q_norm_g.shape[1])

    own = jnp.concatenate([w[k].astype(BF16).reshape(-1) for k in _BIG])
    rows2 = own.shape[0] // FLAT_W
    own = own.reshape(2, rows2 // 2, FLAT_W)
    gathered = _allgather_chips(own, "gather_weights").reshape(N_CHIPS, -1)
    wf, off = {}, 0
    for k in _BIG:
        n = int(np.prod(w[k].shape))
        wf[k] = _from_shards(k, gathered[:, off:off + n], w[k])
        off += n
    cw = jnp.zeros((depth, CONV_KERNEL, N_CHIPS, conv_cs), F32)
    cw = lax.dynamic_update_slice(cw, conv_w[:, :, None, :], (0, 0, my_chip, 0))
    cw = jnp.where(my_core == 0, cw, 0.0).reshape(-1)
    wf["conv_w"] = _allreduce_small(_pad_rows(cw), "gather_conv_w").reshape(-1)[:cw.shape[0]].reshape(depth, CONV_KERNEL, conv_c)
    for k in _SMALL:
        wf[k] = w[k]

    loss_part, grad_x, gw = _local_step(dm, x[0], c, ctx[0], c_ctx, loss_target[0], wf)

    g4 = jnp.concatenate([_by_shard(k, gw[k]) for k in _BIG], axis=1)
    hrows = rows2 // 2
    g24 = g4.reshape(N_CHIPS, 2, hrows, FLAT_W).transpose(1, 0, 2, 3).reshape(2, N_CHIPS * hrows, FLAT_W)
    core_i = my_core.astype(jnp.int32).reshape(1)
    chip_i = my_chip.astype(jnp.int32).reshape(1)
    theirs = _pair_exchange(g24, "reduce_pair")
    chip_sum, chip_sum_bf = _sum_pair(g24, theirs, core_i, "reduce_pair_add")
    others = _scatter_chips(chip_sum_bf.reshape(N_CHIPS, hrows, FLAT_W), "reduce_chips")
    mine = _sum_chips(chip_sum.reshape(N_CHIPS, hrows, FLAT_W), others, chip_i, "reduce_chips_add")
    sibling_half = _pair_swap(mine, "reduce_share")

    small = jnp.concatenate([gw[k].reshape(-1) for k in _SMALL] + [gw["conv_w"].reshape(-1), loss_part.reshape(1)])
    small_sum = _allreduce_small(_pad_rows(small), "reduce_small").reshape(-1)
    loss = small_sum[small.shape[0] - 1]
    gs, off = {}, 0
    for k in _SMALL:
        n = int(np.prod(w[k].shape))
        gs[k] = small_sum[off:off + n].reshape(w[k].shape)
        off += n
    gcw = small_sum[off:off + depth * CONV_KERNEL * conv_c].reshape(depth, CONV_KERNEL, N_CHIPS, conv_cs)
    gs["conv_w"] = lax.dynamic_index_in_dim(gcw, my_chip, axis=2, keepdims=False)

    g_flat = jnp.where(my_core == 0, jnp.concatenate([mine, sibling_half]), jnp.concatenate([sibling_half, mine]))
    grads = _unflatten_shard(g_flat, w)
    delta, new_m, new_v = {}, {}, {}
    for k in _BIG:
        as2d = lambda t: t.reshape(-1, t.shape[-1])
        d_k, m_k, v_k = _adamw(as2d(w[k]), as2d(grads[k]), as2d(m[k]), as2d(v[k]), "adamw_" + k)
        delta[k], new_m[k], new_v[k] = d_k.reshape(w[k].shape), m_k.reshape(w[k].shape), v_k.reshape(w[k].shape)
    small_keys = _SMALL + ("conv_w",)
    join = lambda t: _pad_rows(jnp.concatenate([t[k].reshape(-1) for k in small_keys]))
    d_s, m_s, v_s = _adamw(join(w), join(gs), join(m), join(v), "adamw_small")
    off = 0
    for k in small_keys:
        n = int(np.prod(w[k].shape))
        grads[k] = gs[k]
        for dst, src in ((delta, d_s), (new_m, m_s), (new_v, v_s)):
            dst[k] = src.reshape(-1)[off:off + n].reshape(w[k].shape)
        off += n
    return (loss, grad_x[None], *[grads[k] for k in order], *[delta[k] for k in order],
            *[new_m[k] for k in order], *[new_v[k] for k in order])
```

```python
import functools
from typing import NamedTuple

import jax
import jax.numpy as jnp
import numpy as np
from jax import lax
from jax.experimental import pallas as pl
from jax.experimental.pallas import tpu as pltpu

F32 = jnp.float32
BF16 = jnp.bfloat16
EPS = 1e-6
GRID_W = 64
ROPE_THETA = 10000.0
CONV_KERNEL = 31
CONV_PAD = CONV_KERNEL // 2
HALO = 16
LANES = 128
SUBLANES = 8
ADAM_LR, ADAM_B1, ADAM_B2, ADAM_EPS, ADAM_WD, ADAM_STEP = 0.001, 0.9, 0.999, 1e-08, 0.01, 10
VMEM_LIMIT = 56 * 1024 * 1024
ADAM_BLOCK_BYTES = 1024 * 1024
LOG2E = 1.4426950408889634
LN2 = 0.6931471805599453
MESH = pl.DeviceIdType.MESH
N_CHIPS = 4
N_DEV = 8


class Dims(NamedTuple):
    N: int
    CTX: int
    R: int
    D: int
    C: int
    A: int
    KVW: int
    HD: int
    NH: int
    NKV: int
    G: int
    W: int
    TR: int


def _pick(n, prefs):
    for p in prefs:
        if n % p == 0:
            return p
    raise ValueError(f"no tile for {n} in {prefs}")


def _params(sem):
    return pltpu.CompilerParams(dimension_semantics=sem, vmem_limit_bytes=VMEM_LIMIT)


def _rows(tr, w, col=0):
    return pl.BlockSpec((tr, w), lambda i, col=col: (i, col))


def _whole(shape):
    return pl.BlockSpec(shape, lambda *_: (0,) * len(shape))


def _col(off, w):
    assert off % w == 0, (off, w)
    return off // w


_DN = {"nn": (((1,), (0,)), ((), ())), "nt": (((1,), (1,)), ((), ())), "tn": (((0,), (0,)), ((), ()))}


def _matmul(a, b, mode, name, out_dtype=F32):
    if mode == "nn":
        (m, k), n = a.shape, b.shape[1]
    elif mode == "nt":
        (m, k), n = a.shape, b.shape[0]
    else:
        (k, m), n = a.shape, b.shape[1]
    tm = _pick(m, (1408, 1024, 768, 512, 256, 128, 8))
    tn = _pick(n, (1280, 1024, 768, 512, 256, 128))
    tk = _pick(k, (1536, 1408, 1280, 1024, 768, 512, 256, 128))
    nk = k // tk
    dn = _DN[mode]

    own_acc = out_dtype != F32 and nk > 1

    def body(a_ref, b_ref, o_ref, *scratch):
        part = lax.dot_general(a_ref[...], b_ref[...], dn, preferred_element_type=F32)
        if nk == 1:
            o_ref[...] = part.astype(out_dtype)
            return
        acc_ref = scratch[0] if own_acc else o_ref
        kk = pl.program_id(2)

        @pl.when(kk == 0)
        def _():
            acc_ref[...] = part

        @pl.when(kk > 0)
        def _():
            acc_ref[...] += part

        if own_acc:
            @pl.when(kk == nk - 1)
            def _():
                o_ref[...] = acc_ref[...].astype(out_dtype)

    a_spec = pl.BlockSpec((tk, tm), lambda i, j, kk: (kk, i)) if mode == "tn" else pl.BlockSpec((tm, tk), lambda i, j, kk: (i, kk))
    b_spec = pl.BlockSpec((tn, tk), lambda i, j, kk: (j, kk)) if mode == "nt" else pl.BlockSpec((tk, tn), lambda i, j, kk: (kk, j))
    return pl.pallas_call(
        body, name=name, grid=(m // tm, n // tn, nk),
        in_specs=[a_spec, b_spec], out_specs=pl.BlockSpec((tm, tn), lambda i, j, kk: (i, j)),
        out_shape=jax.ShapeDtypeStruct((m, n), out_dtype),
        scratch_shapes=[pltpu.VMEM((tm, tn), F32)] if own_acc else [],
        compiler_params=_params(("parallel", "parallel", "arbitrary")),
    )(a, b)


def _rms(x, g):
    return x * lax.rsqrt(jnp.mean(x * x, axis=-1, keepdims=True) + EPS) * g


def _prenorm_fn(x, g, sc, sh):
    return _rms(x, g) * (1.0 + sc) + sh


def _post_fn(outp, g, gt):
    return gt * _rms(outp, g)


def _convact_fn(y1, gate, ln_g, ln_b):
    mu = jnp.mean(y1, axis=-1, keepdims=True)
    var = jnp.mean(jnp.square(y1 - mu), axis=-1, keepdims=True)
    y = (y1 - mu) * lax.rsqrt(var + EPS) * ln_g + ln_b
    return jax.nn.silu(y) * jax.nn.silu(gate)


def _dsilu(x):
    s = jax.nn.sigmoid(x)
    return s * (1.0 + x * (1.0 - s))


def _seg_select(is_ctx, mod_ref, lo, w):
    return jnp.where(is_ctx, mod_ref[1:2, lo:lo + w], mod_ref[0:1, lo:lo + w])


def _seg_rows(is_ctx, v):
    row = lax.broadcasted_iota(jnp.int32, (8, v.shape[1]), 0)
    return jnp.where(row == is_ctx.astype(jnp.int32), v, 0.0)


def _mod_fwd(cv, w_mod_l, b_mod_l):
    def body(cv_ref, w_ref, b_ref, o_ref):
        s = jax.nn.silu(cv_ref[...]).astype(BF16)
        o_ref[...] = jnp.dot(s, w_ref[...], preferred_element_type=F32) + b_ref[...]

    return pl.pallas_call(
        body, name="mod_fwd", out_shape=jax.ShapeDtypeStruct((8, w_mod_l.shape[1]), F32),
        compiler_params=pltpu.CompilerParams(vmem_limit_bytes=VMEM_LIMIT),
    )(cv, w_mod_l, b_mod_l)


def _mod_bwd(cv, w_mod_l, dmod):
    d = cv.shape[1]

    def body(cv_ref, w_ref, dm_ref, dw_ref, db_ref, dcv_ref):
        cvv = cv_ref[...]
        s = jax.nn.silu(cvv).astype(BF16)
        dm = dm_ref[...]
        dmb = dm.astype(BF16)
        dw_ref[...] = lax.dot_general(s, dmb, _DN["tn"], preferred_element_type=F32)
        db_ref[...] = jnp.sum(dm, axis=0, keepdims=True)
        ds = lax.dot_general(dmb, w_ref[...], _DN["nt"], preferred_element_type=F32)
        dcv_ref[...] = ds * _dsilu(cvv)

    return pl.pallas_call(
        body, name="mod_bwd",
        out_shape=(jax.ShapeDtypeStruct(w_mod_l.shape, F32), jax.ShapeDtypeStruct((1, w_mod_l.shape[1]), F32),
                   jax.ShapeDtypeStruct((8, d), F32)),
        compiler_params=pltpu.CompilerParams(vmem_limit_bytes=VMEM_LIMIT),
    )(cv, w_mod_l, dmod)


def _prenorm(dm, xc, g_pre_l, mod):
    tr, d, n = dm.TR, dm.D, dm.N

    def body(x_ref, g_ref, mod_ref, h_ref):
        is_ctx = pl.program_id(0) * tr >= n
        sh = _seg_select(is_ctx, mod_ref, 0, d)
        sc = _seg_select(is_ctx, mod_ref, d, d)
        h_ref[...] = _prenorm_fn(x_ref[...], g_ref[...], sc, sh).astype(BF16)

    return pl.pallas_call(
        body, name="prenorm", grid=(dm.R // tr,),
        in_specs=[_rows(tr, d), _whole((1, d)), _whole((8, 3 * d))], out_specs=_rows(tr, d),
        out_shape=jax.ShapeDtypeStruct((dm.R, d), BF16), compiler_params=_params(("parallel",)),
    )(xc, g_pre_l, mod)


def _prenorm_bwd(dm, xc, g_pre_l, mod, dh, dxn):
    tr, d, n = dm.TR, dm.D, dm.N

    def body(x_ref, g_ref, mod_ref, dh_ref, dxn_ref, dx_ref, dg_ref, dss_ref):
        i = pl.program_id(0)
        is_ctx = i * tr >= n
        sh = _seg_select(is_ctx, mod_ref, 0, d)
        sc = _seg_select(is_ctx, mod_ref, d, d)
        _, vjp = jax.vjp(_prenorm_fn, x_ref[...], g_ref[...], sc, sh)
        dx, dg, dsc, dsh = vjp(dh_ref[...].astype(F32))
        dx_ref[...] = dxn_ref[...] + dx

        @pl.when(i == 0)
        def _():
            dg_ref[...] = jnp.zeros_like(dg_ref)
            dss_ref[...] = jnp.zeros_like(dss_ref)

        dg_ref[...] += dg
        dss_ref[...] += _seg_rows(is_ctx, jnp.concatenate([dsh, dsc], axis=1))

    return pl.pallas_call(
        body, name="prenorm_bwd", grid=(dm.R // tr,),
        in_specs=[_rows(tr, d), _whole((1, d)), _whole((8, 3 * d)), _rows(tr, d), _rows(tr, d)],
        out_specs=(_rows(tr, d), _whole((1, d)), _whole((8, 2 * d))),
        out_shape=(jax.ShapeDtypeStruct((dm.R, d), F32), jax.ShapeDtypeStruct((1, d), F32),
                   jax.ShapeDtypeStruct((8, 2 * d), F32)),
        compiler_params=_params(("arbitrary",)),
    )(xc, g_pre_l, mod, dh, dxn)


def _post(dm, xc, outp, g_post_l, mod):
    tr, d, n = dm.TR, dm.D, dm.N

    def body(x_ref, o_ref, g_ref, mod_ref, y_ref):
        is_ctx = pl.program_id(0) * tr >= n
        gt = _seg_select(is_ctx, mod_ref, 2 * d, d)
        y_ref[...] = x_ref[...] + _post_fn(o_ref[...], g_ref[...], gt)

    return pl.pallas_call(
        body, name="post", grid=(dm.R // tr,),
        in_specs=[_rows(tr, d), _rows(tr, d), _whole((1, d)), _whole((8, 3 * d))], out_specs=_rows(tr, d),
        out_shape=jax.ShapeDtypeStruct((dm.R, d), F32), compiler_params=_params(("parallel",)),
    )(xc, outp, g_post_l, mod)


def _post_bwd(dm, outp, g_post_l, mod, dxn):
    tr, d, n = dm.TR, dm.D, dm.N

    def body(o_ref, g_ref, mod_ref, dxn_ref, do_ref, dg_ref, dgt_ref):
        i = pl.program_id(0)
        is_ctx = i * tr >= n
        gt = _seg_select(is_ctx, mod_ref, 2 * d, d)
        _, vjp = jax.vjp(_post_fn, o_ref[...], g_ref[...], gt)
        do, dg, dgt = vjp(dxn_ref[...])
        do_ref[...] = do.astype(BF16)

        @pl.when(i == 0)
        def _():
            dg_ref[...] = jnp.zeros_like(dg_ref)
            dgt_ref[...] = jnp.zeros_like(dgt_ref)

        dg_ref[...] += dg
        dgt_ref[...] += _seg_rows(is_ctx, dgt)

    return pl.pallas_call(
        body, name="post_bwd", grid=(dm.R // tr,),
        in_specs=[_rows(tr, d), _whole((1, d)), _whole((8, 3 * d)), _rows(tr, d)],
        out_specs=(_rows(tr, d), _whole((1, d)), _whole((8, d))),
        out_shape=(jax.ShapeDtypeStruct((dm.R, d), BF16), jax.ShapeDtypeStruct((1, d), F32),
                   jax.ShapeDtypeStruct((8, d), F32)),
        compiler_params=_params(("arbitrary",)),
    )(outp, g_post_l, mod, dxn)


def _group_sum(x, bd):
    hi = x.astype(BF16)
    lo = (x - hi.astype(F32)).astype(BF16)
    return jnp.dot(hi, bd, preferred_element_type=F32) + jnp.dot(lo, bd, preferred_element_type=F32)


def _swap_halves(x):
    lane = lax.broadcasted_iota(jnp.int32, x.shape, 1)
    return jnp.where((lane & 32) == 0, pltpu.roll(x, LANES - 32, 1), pltpu.roll(x, 32, 1))


def _qk_prep(dm, p, cos, sins, gq, gk, bd):
    tr, hd = dm.TR, dm.HD
    oq, ok, ov = 3 * dm.C, 3 * dm.C + dm.A, 3 * dm.C + dm.A + dm.KVW
    scale = hd ** -0.5 * LOG2E

    def body(q_ref, k_ref, v_ref, cos_ref, sin_ref, gq_ref, gk_ref, bd_ref, qh_ref, qt_ref, kh_ref, kt_ref, vh_ref, vx_ref):
        cosv, sinv, bdv = cos_ref[...], sin_ref[...], bd_ref[...]

        def chunk(x, g):
            rstd = lax.rsqrt(_group_sum(x * x, bdv) * (1.0 / hd) + EPS)
            xn = x * rstd * g
            return xn * cosv + _swap_halves(xn) * sinv

        for c in range(dm.A // LANES):
            y = chunk(q_ref[:, c * LANES:(c + 1) * LANES].astype(F32), gq_ref[...]) * scale
            qh_ref[2 * c] = y[:, :hd].astype(BF16)
            qh_ref[2 * c + 1] = y[:, hd:].astype(BF16)
            qt_ref[c * LANES:(c + 1) * LANES, :] = y.T.astype(BF16)
        for c in range(dm.KVW // LANES):
            y = chunk(k_ref[:, c * LANES:(c + 1) * LANES].astype(F32), gk_ref[...])
            kh_ref[2 * c] = y[:, :hd].astype(BF16)
            kh_ref[2 * c + 1] = y[:, hd:].astype(BF16)
            kt_ref[c * LANES:(c + 1) * LANES, :] = y.T.astype(BF16)
            v = v_ref[:, c * LANES:(c + 1) * LANES].astype(F32)
            vh_ref[2 * c] = v[:, :hd].astype(BF16)
            vh_ref[2 * c + 1] = v[:, hd:].astype(BF16)
            lane = lax.broadcasted_iota(jnp.int32, v.shape, 1)
            one_at_hd = (lane == hd).astype(F32)
            vx_ref[2 * c] = jnp.where(lane < hd, v, one_at_hd).astype(BF16)
            vx_ref[2 * c + 1] = jnp.where(lane < hd, pltpu.roll(v, hd, 1), one_at_hd).astype(BF16)

    hspec = lambda nh, w=hd: pl.BlockSpec((nh, tr, w), lambda i: (0, i, 0))
    return pl.pallas_call(
        body, name="qk_prep", grid=(dm.R // tr,),
        in_specs=[_rows(tr, dm.A, _col(oq, dm.A)), _rows(tr, dm.KVW, _col(ok, dm.KVW)), _rows(tr, dm.KVW, _col(ov, dm.KVW)),
                  _rows(tr, LANES), _rows(tr, LANES), _whole((1, LANES)), _whole((1, LANES)), _whole((LANES, LANES))],
        out_specs=(hspec(dm.NH), pl.BlockSpec((dm.A, tr), lambda i: (0, i)), hspec(dm.NKV),
                   pl.BlockSpec((dm.KVW, tr), lambda i: (0, i)), hspec(dm.NKV), hspec(dm.NKV, LANES)),
        out_shape=(jax.ShapeDtypeStruct((dm.NH, dm.R, hd), BF16), jax.ShapeDtypeStruct((dm.A, dm.R), BF16),
                   jax.ShapeDtypeStruct((dm.NKV, dm.R, hd), BF16), jax.ShapeDtypeStruct((dm.KVW, dm.R), BF16),
                   jax.ShapeDtypeStruct((dm.NKV, dm.R, hd), BF16), jax.ShapeDtypeStruct((dm.NKV, dm.R, LANES), BF16)),
        compiler_params=_params(("parallel",)),
    )(p, p, p, cos, sins, gq, gk, bd)


def _qk_bwd(dm, p, dq, dkt, dvt, cos, sins, gq, gk, bd):
    tr, hd = dm.TR, dm.HD
    oq, ok = 3 * dm.C, 3 * dm.C + dm.A
    scale = hd ** -0.5

    def body(q_ref, k_ref, dq_ref, dkh_ref, dvh_ref, cos_ref, sin_ref, gq_ref, gk_ref, bd_ref,
             dpq_ref, dpkv_ref, dgq_ref, dgk_ref):
        i = pl.program_id(0)
        cosv, sinv, bdv = cos_ref[...], sin_ref[...], bd_ref[...]

        def chunk(x, g, dy):
            rstd = lax.rsqrt(_group_sum(x * x, bdv) * (1.0 / hd) + EPS)
            xhat = x * rstd
            dxn = dy * cosv + _swap_halves(dy * sinv)
            dxhat = dxn * g
            dx = rstd * (dxhat - xhat * (_group_sum(dxhat * xhat, bdv) * (1.0 / hd)))
            return dx, jnp.sum(dxn * xhat, axis=0, keepdims=True)

        @pl.when(i == 0)
        def _():
            dgq_ref[...] = jnp.zeros_like(dgq_ref)
            dgk_ref[...] = jnp.zeros_like(dgk_ref)

        dgq = jnp.zeros((1, LANES), F32)
        for c in range(dm.A // LANES):
            sl = slice(c * LANES, (c + 1) * LANES)
            dx, dg = chunk(q_ref[:, sl].astype(F32), gq_ref[...], dq_ref[:, sl] * scale)
            dpq_ref[:, sl] = dx.astype(BF16)
            dgq = dgq + dg
        dgq_ref[...] += dgq + pltpu.roll(dgq, hd, 1)
        dgk = jnp.zeros((1, LANES), F32)
        for c in range(dm.KVW // LANES):
            sl = slice(c * LANES, (c + 1) * LANES)
            dx, dg = chunk(k_ref[:, sl].astype(F32), gk_ref[...], dkh_ref[sl, :].T)
            dpkv_ref[:, sl] = dx.astype(BF16)
            dgk = dgk + dg
            dpkv_ref[:, dm.KVW + c * LANES:dm.KVW + (c + 1) * LANES] = dvh_ref[sl, :].T.astype(BF16)
        dgk_ref[...] += dgk + pltpu.roll(dgk, hd, 1)

    hspec = pl.BlockSpec((dm.KVW, tr), lambda i: (0, i))
    return pl.pallas_call(
        body, name="qk_bwd", grid=(dm.R // tr,),
        in_specs=[_rows(tr, dm.A, _col(oq, dm.A)), _rows(tr, dm.KVW, _col(ok, dm.KVW)), _rows(tr, dm.A), hspec, hspec,
                  _rows(tr, LANES), _rows(tr, LANES), _whole((1, LANES)), _whole((1, LANES)), _whole((LANES, LANES))],
        out_specs=(_rows(tr, dm.A), _rows(tr, 2 * dm.KVW), _whole((1, LANES)), _whole((1, LANES))),
        out_shape=(jax.ShapeDtypeStruct((dm.R, dm.A), BF16), jax.ShapeDtypeStruct((dm.R, 2 * dm.KVW), BF16),
                   jax.ShapeDtypeStruct((1, LANES), F32), jax.ShapeDtypeStruct((1, LANES), F32)),
        compiler_params=_params(("arbitrary",)),
    )(p, p, dq, dkt, dvt, cos, sins, gq, gk, bd)


def _flash_fwd(dm, qh, kh, vx):
    tq, ks = dm.TR, dm.TR
    rs = min(tq, LANES)
    g, hd = dm.G, dm.HD
    nq, nq_lat = dm.R // tq, dm.N // tq
    n_chunks, ctx_chunks = dm.R // ks, dm.CTX // ks

    def body(q_ref, k_ref, v_ref, o_ref, lse_ref):
        qi = pl.program_id(1)

        def attend(chunks):
            for gg in range(g):
                for r in range(tq // rs):
                    rows = slice(r * rs, (r + 1) * rs)
                    q = q_ref[gg, rows, :]
                    m_prev = jnp.full((rs, LANES), -1e30, F32)
                    acc = jnp.zeros((rs, LANES), F32)
                    for kc in chunks:
                        keys = slice(kc * ks, (kc + 1) * ks)
                        s = lax.dot_general(q, k_ref[0, keys, :], _DN["nt"], preferred_element_type=F32)
                        m_new = jnp.maximum(m_prev, jnp.max(s, axis=-1, keepdims=True))
                        p = jnp.exp2(s - jnp.tile(m_new, (1, ks // LANES)))
                        acc = acc * jnp.exp2(m_prev - m_new) + jnp.dot(p.astype(BF16), v_ref[0, keys, :],
                                                                        preferred_element_type=F32)
                        m_prev = m_new
                    l = acc[:, hd:hd + 1]
                    o_ref[rows, gg * hd:(gg + 1) * hd] = acc[:, :hd] / l
                    lse_ref[gg, rows, :] = m_prev + jnp.log2(l)

        @pl.when(qi < nq_lat)
        def _():
            attend(range(n_chunks))

        @pl.when(qi >= nq_lat)
        def _():
            attend(range(n_chunks - ctx_chunks, n_chunks))

    return pl.pallas_call(
        body, name="flash_fwd", grid=(dm.NKV, nq),
        in_specs=[pl.BlockSpec((g, tq, hd), lambda h, qi: (h, qi, 0)), pl.BlockSpec((1, dm.R, hd), lambda h, qi: (h, 0, 0)),
                  pl.BlockSpec((1, dm.R, LANES), lambda h, qi: (h, 0, 0))],
        out_specs=(pl.BlockSpec((tq, g * hd), lambda h, qi: (qi, h)), pl.BlockSpec((g, tq, LANES), lambda h, qi: (h, qi, 0))),
        out_shape=(jax.ShapeDtypeStruct((dm.R, dm.A), F32), jax.ShapeDtypeStruct((dm.NH, dm.R, LANES), F32)),
        compiler_params=_params(("parallel", "parallel")),
    )(qh, kh, vx)


def _flash_bwd(dm, qh, qt, kh, kt, vh, doh, dot, lse, delta):
    tq, ks = dm.TR, dm.TR
    g, hd = dm.G, dm.HD
    nq, nq_lat = dm.R // tq, dm.N // tq
    n_chunks, ctx_chunks = dm.R // ks, dm.CTX // ks

    def body(q_ref, qt_ref, k_ref, kt_ref, v_ref, do_ref, dot_ref, lse_ref, dl_ref, dq_ref, dk_acc, dv_acc):
        qi = pl.program_id(1)

        @pl.when(qi == 0)
        def _():
            dk_acc[...] = jnp.zeros_like(dk_acc)
            dv_acc[...] = jnp.zeros_like(dv_acc)

        def attend(chunks):
            q = q_ref[...].reshape(g * tq, hd)
            do = do_ref[...].reshape(g * tq, hd)
            q_t = jnp.concatenate([qt_ref[gg * hd:(gg + 1) * hd, :] for gg in range(g)], axis=1)
            do_t = jnp.concatenate([dot_ref[gg * hd:(gg + 1) * hd, :] for gg in range(g)], axis=1)
            wide = lambda ref: jnp.tile(ref[...].reshape(g * tq, LANES), (1, ks // LANES))
            dq_t = jnp.zeros((hd, g * tq), F32)
            for kc in chunks:
                keys = slice(kc * ks, (kc + 1) * ks)
                k, v = k_ref[0, keys, :], v_ref[0, keys, :]
                s = lax.dot_general(q, k, _DN["nt"], preferred_element_type=F32)
                p = jnp.exp2(s - wide(lse_ref))
                dp = lax.dot_general(do, v, _DN["nt"], preferred_element_type=F32)
                ds = p * (dp - wide(dl_ref))
                dv_acc[:, keys] += jnp.dot(do_t, p.astype(BF16), preferred_element_type=F32)
                dk_acc[:, keys] += jnp.dot(q_t, ds.astype(BF16), preferred_element_type=F32)
                dq_t = dq_t + jnp.dot(kt_ref[:, keys], ds.T.astype(BF16), preferred_element_type=F32)
            for gg in range(g):
                dq_ref[:, gg * hd:(gg + 1) * hd] = dq_t[:, gg * tq:(gg + 1) * tq].T

        @pl.when(qi < nq_lat)
        def _():
            attend(range(n_chunks))

        @pl.when(qi >= nq_lat)
        def _():
            attend(range(n_chunks - ctx_chunks, n_chunks))

        @pl.when(qi == nq - 1)
        def _():
            dk_acc[...] = dk_acc[...] * LN2

    qspec = pl.BlockSpec((g, tq, hd), lambda h, qi: (h, qi, 0))
    tspec = pl.BlockSpec((g * hd, tq), lambda h, qi: (h, qi))
    kspec = pl.BlockSpec((1, dm.R, hd), lambda h, qi: (h, 0, 0))
    sspec = pl.BlockSpec((g, tq, LANES), lambda h, qi: (h, qi, 0))
    ospec = pl.BlockSpec((hd, dm.R), lambda h, qi: (h, 0))
    return pl.pallas_call(
        body, name="flash_bwd", grid=(dm.NKV, nq),
        in_specs=[qspec, tspec, kspec, ospec, kspec, qspec, tspec, sspec, sspec],
        out_specs=(pl.BlockSpec((tq, g * hd), lambda h, qi: (qi, h)), ospec, ospec),
        out_shape=(jax.ShapeDtypeStruct((dm.R, dm.A), F32), jax.ShapeDtypeStruct((dm.KVW, dm.R), F32),
                   jax.ShapeDtypeStruct((dm.KVW, dm.R), F32)),
        compiler_params=_params(("parallel", "arbitrary")),
    )(qh, qt, kh, kt, vh, doh, dot, lse, delta)


def _gate_b_specs(dm):
    ogb = 3 * dm.C + dm.A + 2 * dm.KVW
    half = dm.A // 2
    return [_rows(dm.TR, half, _col(ogb, half)), _rows(dm.TR, half, _col(ogb, half) + 1)]


def _attn_gate(dm, o, p):
    tr = dm.TR

    def body(o_ref, g0_ref, g1_ref, za_ref):
        gate = jnp.concatenate([g0_ref[...], g1_ref[...]], axis=1).astype(F32)
        za_ref[...] = (o_ref[...] * jax.nn.silu(gate)).astype(BF16)

    return pl.pallas_call(
        body, name="attn_gate", grid=(dm.R // tr,),
        in_specs=[_rows(tr, dm.A)] + _gate_b_specs(dm), out_specs=_rows(tr, dm.A),
        out_shape=jax.ShapeDtypeStruct((dm.R, dm.A), BF16), compiler_params=_params(("parallel",)),
    )(o, p, p)


def _attn_gate_bwd(dm, dza, o, p):
    tr, hd = dm.TR, dm.HD

    def body(dza_ref, o_ref, g0_ref, g1_ref, doh_ref, dot_ref, dl_ref, dgb_ref):
        gate = jnp.concatenate([g0_ref[...], g1_ref[...]], axis=1).astype(F32)
        dza, o = dza_ref[...].astype(F32), o_ref[...]
        do = dza * jax.nn.silu(gate)
        dgb_ref[...] = (dza * o * _dsilu(gate)).astype(BF16)
        dot_ref[...] = do.T.astype(BF16)
        prod = do * o
        for h in range(dm.NH):
            sl = slice(h * hd, (h + 1) * hd)
            doh_ref[h] = do[:, sl].astype(BF16)
            dl_ref[h] = jnp.broadcast_to(jnp.sum(prod[:, sl], axis=-1, keepdims=True), (tr, LANES))

    return pl.pallas_call(
        body, name="attn_gate_bwd", grid=(dm.R // tr,),
        in_specs=[_rows(tr, dm.A), _rows(tr, dm.A)] + _gate_b_specs(dm),
        out_specs=(pl.BlockSpec((dm.NH, tr, hd), lambda i: (0, i, 0)), pl.BlockSpec((dm.A, tr), lambda i: (0, i)),
                   pl.BlockSpec((dm.NH, tr, LANES), lambda i: (0, i, 0)), _rows(tr, dm.A)),
        out_shape=(jax.ShapeDtypeStruct((dm.NH, dm.R, hd), BF16), jax.ShapeDtypeStruct((dm.A, dm.R), BF16),
                   jax.ShapeDtypeStruct((dm.NH, dm.R, LANES), F32), jax.ShapeDtypeStruct((dm.R, dm.A), BF16)),
        compiler_params=_params(("parallel",)),
    )(dza, o, p, p)


def _halo_specs(dm, col):
    per = dm.TR // HALO
    last = dm.R // HALO - 1
    prev = pl.BlockSpec((HALO, dm.C), lambda i: (jnp.maximum(i * per - 1, 0), col))
    nxt = pl.BlockSpec((HALO, dm.C), lambda i: (jnp.minimum((i + 1) * per, last), col))
    return prev, nxt


def _seg_valid(dm, i):
    tr = dm.TR
    r = i * tr - HALO + lax.broadcasted_iota(jnp.int32, (tr + 2 * HALO, 1), 0)
    is_ctx = i * tr >= dm.N
    lo = jnp.where(is_ctx, dm.N, 0)
    hi = jnp.where(is_ctx, dm.R, dm.N)
    return jnp.logical_and(r >= lo, r < hi)


CONV_RB = 64


def _shifted_copies(src_ref, sl, sh_ref, tr):
    rows = tr + 2 * HALO - SUBLANES
    padded = src_ref[:, sl]
    for ph in range(SUBLANES):
        sh_ref[ph, 0:rows, :] = padded[ph:ph + rows]


def _tap_rows(sh_ref, off, n):
    ph = off % SUBLANES
    return sh_ref[ph, off - ph:off - ph + n, :]


def _conv_fwd(dm, p, conv_w_l, conv_b_l, ln_g_l, ln_b_l):
    tr, c = dm.TR, dm.C

    def body(a_ref, g_ref, gate_ref, ap_ref, an_ref, gp_ref, gn_ref, w_ref, b_ref, lg_ref, lb_ref,
             zc_ref, y1_ref, u_ref, sh_ref):
        i = pl.program_id(0)
        a = jnp.concatenate([ap_ref[...], a_ref[...], an_ref[...]], axis=0).astype(F32)
        g = jnp.concatenate([gp_ref[...], g_ref[...], gn_ref[...]], axis=0).astype(F32)
        u_ref[...] = jnp.where(_seg_valid(dm, i), a * jax.nn.sigmoid(g), 0.0)
        for cc in range(c // LANES):
            sl = slice(cc * LANES, (cc + 1) * LANES)
            _shifted_copies(u_ref, sl, sh_ref, tr)
            for r0 in range(0, tr, CONV_RB):
                acc = jnp.zeros((CONV_RB, LANES), F32)
                for j in range(CONV_KERNEL):
                    acc = acc + w_ref[j:j + 1, sl] * _tap_rows(sh_ref, r0 + HALO - CONV_PAD + j, CONV_RB)
                y1_ref[r0:r0 + CONV_RB, sl] = acc + b_ref[:, sl]
        zc_ref[...] = _convact_fn(y1_ref[...], gate_ref[...].astype(F32), lg_ref[...], lb_ref[...]).astype(BF16)

    ap, an = _halo_specs(dm, 0)
    gp, gn = _halo_specs(dm, 1)
    return pl.pallas_call(
        body, name="conv_fwd", grid=(dm.R // tr,),
        in_specs=[_rows(tr, c, 0), _rows(tr, c, 1), _rows(tr, c, 2), ap, an, gp, gn,
                  _whole((32, c)), _whole((1, c)), _whole((1, c)), _whole((1, c))],
        out_specs=(_rows(tr, c), _rows(tr, c)),
        out_shape=(jax.ShapeDtypeStruct((dm.R, c), BF16), jax.ShapeDtypeStruct((dm.R, c), F32)),
        scratch_shapes=[pltpu.VMEM((tr + 2 * HALO, c), F32), pltpu.VMEM((SUBLANES, tr + 2 * HALO, LANES), F32)],
        compiler_params=_params(("parallel",)),
    )(p, p, p, p, p, p, p, conv_w_l, conv_b_l, ln_g_l, ln_b_l)


def _conv_act_bwd(dm, dzc, y1, p, ln_g_l, ln_b_l):
    tr, c = dm.TR, dm.C

    def body(dz_ref, y1_ref, gate_ref, lg_ref, lb_ref, dy_ref, dgate_ref, dlg_ref, dlb_ref):
        i = pl.program_id(0)
        _, vjp = jax.vjp(_convact_fn, y1_ref[...], gate_ref[...].astype(F32), lg_ref[...], lb_ref[...])
        dy, dgate, dlg, dlb = vjp(dz_ref[...].astype(F32))
        dy_ref[...] = dy
        dgate_ref[...] = dgate.astype(BF16)

        @pl.when(i == 0)
        def _():
            dlg_ref[...] = jnp.zeros_like(dlg_ref)
            dlb_ref[...] = jnp.zeros_like(dlb_ref)

        dlg_ref[...] += dlg
        dlb_ref[...] += dlb

    return pl.pallas_call(
        body, name="conv_act_bwd", grid=(dm.R // tr,),
        in_specs=[_rows(tr, c), _rows(tr, c), _rows(tr, c, 2), _whole((1, c)), _whole((1, c))],
        out_specs=(_rows(tr, c), _rows(tr, c), _whole((1, c)), _whole((1, c))),
        out_shape=(jax.ShapeDtypeStruct((dm.R, c), F32), jax.ShapeDtypeStruct((dm.R, c), BF16),
                   jax.ShapeDtypeStruct((1, c), F32), jax.ShapeDtypeStruct((1, c), F32)),
        compiler_params=_params(("arbitrary",)),
    )(dzc, y1, p, ln_g_l, ln_b_l)


def _conv_bwd(dm, dy1, p, conv_w_l):
    tr, c = dm.TR, dm.C

    def body(dy_ref, dyp_ref, dyn_ref, a_ref, g_ref, ap_ref, an_ref, gp_ref, gn_ref, w_ref,
             dag_ref, dw_ref, db_ref, u_ref, dyp_scr, du_ref, dwp_ref, shu_ref, shd_ref):
        i = pl.program_id(0)
        valid = _seg_valid(dm, i)
        a = jnp.concatenate([ap_ref[...], a_ref[...], an_ref[...]], axis=0).astype(F32)
        g = jnp.concatenate([gp_ref[...], g_ref[...], gn_ref[...]], axis=0).astype(F32)
        u_ref[...] = jnp.where(valid, a * jax.nn.sigmoid(g), 0.0)
        dyp_scr[...] = jnp.where(valid, jnp.concatenate([dyp_ref[...], dy_ref[...], dyn_ref[...]], axis=0), 0.0)

        @pl.when(i == 0)
        def _():
            dwp_ref[...] = jnp.zeros_like(dwp_ref)
            db_ref[...] = jnp.zeros_like(db_ref)

        db_ref[...] += jnp.sum(dy_ref[...], axis=0, keepdims=True)
        for cc in range(c // LANES):
            sl = slice(cc * LANES, (cc + 1) * LANES)
            _shifted_copies(u_ref, sl, shu_ref, tr)
            _shifted_copies(dyp_scr, sl, shd_ref, tr)
            for r0 in range(0, tr, CONV_RB):
                dy_cur = dy_ref[r0:r0 + CONV_RB, sl]
                acc = jnp.zeros((CONV_RB, LANES), F32)
                for j in range(CONV_KERNEL):
                    acc = acc + w_ref[j:j + 1, sl] * _tap_rows(shd_ref, r0 + HALO + CONV_PAD - j, CONV_RB)
                    prod = dy_cur * _tap_rows(shu_ref, r0 + HALO - CONV_PAD + j, CONV_RB)
                    dwp_ref[j, :, sl] += jnp.sum(prod.reshape(CONV_RB // SUBLANES, SUBLANES, LANES), axis=0)
                du_ref[r0:r0 + CONV_RB, sl] = acc
        du = du_ref[...]
        sg = jax.nn.sigmoid(g_ref[...].astype(F32))
        dag_ref[:, :c] = (du * sg).astype(BF16)
        dag_ref[:, c:] = (du * a_ref[...].astype(F32) * sg * (1.0 - sg)).astype(BF16)

        @pl.when(i == dm.R // tr - 1)
        def _():
            dw_ref[...] = jnp.sum(dwp_ref[...], axis=1)

    ap, an = _halo_specs(dm, 0)
    gp, gn = _halo_specs(dm, 1)
    dyp, dyn = _halo_specs(dm, 0)
    return pl.pallas_call(
        body, name="conv_bwd", grid=(dm.R // tr,),
        in_specs=[_rows(tr, c), dyp, dyn, _rows(tr, c, 0), _rows(tr, c, 1), ap, an, gp, gn, _whole((32, c))],
        out_specs=(_rows(tr, 2 * c), _whole((32, c)), _whole((1, c))),
        out_shape=(jax.ShapeDtypeStruct((dm.R, 2 * c), BF16), jax.ShapeDtypeStruct((32, c), F32),
                   jax.ShapeDtypeStruct((1, c), F32)),
        scratch_shapes=[pltpu.VMEM((tr + 2 * HALO, c), F32), pltpu.VMEM((tr + 2 * HALO, c), F32), pltpu.VMEM((tr, c), F32),
                        pltpu.VMEM((32, SUBLANES, c), F32), pltpu.VMEM((SUBLANES, tr + 2 * HALO, LANES), F32),
                        pltpu.VMEM((SUBLANES, tr + 2 * HALO, LANES), F32)],
        compiler_params=_params(("arbitrary",)),
    )(dy1, dy1, dy1, p, p, p, p, p, p, conv_w_l)


def _gm_specs(dm):
    half = dm.D // 2
    first = _col(3 * dm.C + 2 * dm.A + 2 * dm.KVW, half)
    return [_rows(dm.TR, half, first + k) for k in range(4)]


def _gm_load(a0, a1, b0, b1):
    return (jnp.concatenate([a0[...], a1[...]], axis=1).astype(F32),
            jnp.concatenate([b0[...], b1[...]], axis=1).astype(F32))


def _merge(dm, yc, ya, p):
    tr, d = dm.TR, dm.D

    def body(yc_ref, ya_ref, a0, a1, b0, b1, m_ref):
        ga, gb = _gm_load(a0, a1, b0, b1)
        m_ref[...] = (jax.nn.sigmoid(ga) * yc_ref[...] + jax.nn.sigmoid(gb) * ya_ref[...]).astype(BF16)

    return pl.pallas_call(
        body, name="merge", grid=(dm.R // tr,),
        in_specs=[_rows(tr, d), _rows(tr, d)] + _gm_specs(dm), out_specs=_rows(tr, d),
        out_shape=jax.ShapeDtypeStruct((dm.R, d), BF16), compiler_params=_params(("parallel",)),
    )(yc, ya, p, p, p, p)


def _merge_bwd(dm, dmerged, yc, ya, p):
    tr, d = dm.TR, dm.D

    def body(dm_ref, yc_ref, ya_ref, a0, a1, b0, b1, dyc_ref, dya_ref, dgm_ref):
        dmv = dm_ref[...].astype(F32)
        ga, gb = _gm_load(a0, a1, b0, b1)
        sa, sb = jax.nn.sigmoid(ga), jax.nn.sigmoid(gb)
        dyc_ref[...] = (dmv * sa).astype(BF16)
        dya_ref[...] = (dmv * sb).astype(BF16)
        dgm_ref[:, :d] = (dmv * yc_ref[...] * sa * (1.0 - sa)).astype(BF16)
        dgm_ref[:, d:] = (dmv * ya_ref[...] * sb * (1.0 - sb)).astype(BF16)

    return pl.pallas_call(
        body, name="merge_bwd", grid=(dm.R // tr,),
        in_specs=[_rows(tr, d), _rows(tr, d), _rows(tr, d)] + _gm_specs(dm),
        out_specs=(_rows(tr, d), _rows(tr, d), _rows(tr, 2 * d)),
        out_shape=(jax.ShapeDtypeStruct((dm.R, d), BF16), jax.ShapeDtypeStruct((dm.R, d), BF16),
                   jax.ShapeDtypeStruct((dm.R, 2 * d), BF16)),
        compiler_params=_params(("parallel",)),
    )(dmerged, yc, ya, p, p, p, p)


def _loss_head(dm, xc, target):
    tr, d = dm.TR, dm.D
    n_lat = dm.N // tr

    def body(x_ref, t_ref, dy_ref, l_ref):
        i = pl.program_id(0)

        @pl.when(i == 0)
        def _():
            l_ref[...] = jnp.zeros_like(l_ref)

        @pl.when(i < n_lat)
        def _():
            e = x_ref[...] - t_ref[...]
            dy_ref[...] = e * (1.0 / d)
            l_ref[...] += 0.5 * jnp.sum(jnp.mean(e * e, axis=-1, keepdims=True), axis=0, keepdims=True)

        @pl.when(i >= n_lat)
        def _():
            dy_ref[...] = jnp.zeros_like(dy_ref)

    return pl.pallas_call(
        body, name="loss_head", grid=(dm.R // tr,),
        in_specs=[_rows(tr, d), pl.BlockSpec((tr, d), lambda i: (jnp.minimum(i, n_lat - 1), 0))],
        out_specs=(_rows(tr, d), _whole((1, 1))),
        out_shape=(jax.ShapeDtypeStruct((dm.R, d), F32), jax.ShapeDtypeStruct((1, 1), F32)),
        compiler_params=_params(("arbitrary",)),
    )(xc, target)


def _adamw(w, g, m, v, name):
    rows, cols = w.shape
    fits = lambda t: t * cols * 4 <= ADAM_BLOCK_BYTES
    tr = rows if fits(rows) else _pick(rows, [t for t in (512, 256, 128, 64, 32, 16, 8) if fits(t)])

    def body(w_ref, g_ref, m_ref, v_ref, d_ref, mo_ref, vo_ref):
        d_ref[...], mo_ref[...], vo_ref[...] = _adam_update(w_ref[...], g_ref[...], m_ref[...], v_ref[...])

    spec = _rows(tr, cols)
    shp = jax.ShapeDtypeStruct(w.shape, F32)
    return pl.pallas_call(
        body, name=name, grid=(rows // tr,), in_specs=[spec] * 4, out_specs=(spec,) * 3, out_shape=(shp,) * 3,
        compiler_params=_params(("parallel",)),
    )(w, g, m, v)


def _adam_update(w, g, m, v):
    mn = ADAM_B1 * m + (1.0 - ADAM_B1) * g
    vn = ADAM_B2 * v + (1.0 - ADAM_B2) * jnp.square(g)
    m_hat = mn / (1.0 - ADAM_B1 ** ADAM_STEP)
    v_hat = vn / (1.0 - ADAM_B2 ** ADAM_STEP)
    return -ADAM_LR * (m_hat / (jnp.sqrt(v_hat) + ADAM_EPS) + ADAM_WD * w), mn, vn


def _rope_tables(dm):
    n = dm.N
    rows = n // GRID_W
    row = jnp.repeat(jnp.arange(rows, dtype=F32), GRID_W)
    col = jnp.tile(jnp.arange(GRID_W, dtype=F32), rows)
    half = dm.HD // 2
    inv_freq = ROPE_THETA ** (-jnp.arange(0, half, 2, dtype=F32) / half)
    ang = jnp.concatenate([row[:, None] * inv_freq, col[:, None] * inv_freq], axis=-1)
    cos, sin = jnp.cos(ang), jnp.sin(ang)
    cos = jnp.concatenate([cos, jnp.ones((dm.CTX, half), F32)], axis=0)
    sin = jnp.concatenate([sin, jnp.zeros((dm.CTX, half), F32)], axis=0)
    reps = LANES // dm.HD
    return jnp.tile(jnp.concatenate([cos, cos], axis=1), (1, reps)), jnp.tile(jnp.concatenate([-sin, sin], axis=1), (1, reps))


def _lane_vec(g, hd):
    return jnp.tile(g.reshape(1, hd), (1, LANES // hd))


def _layer_fwd(dm, xc, cv, wl, tabs):
    mod = _mod_fwd(cv, wl["w_mod"], wl["b_mod"])
    h = _prenorm(dm, xc, wl["g_pre"], mod)
    p = _matmul(h, wl["w_in"], "nn", "in_proj", BF16)
    qh, qt, kh, kt, vh, vx = _qk_prep(dm, p, tabs["cos"], tabs["sins"], wl["gq"], wl["gk"], tabs["bd"])
    o, lse = _flash_fwd(dm, qh, kh, vx)
    za = _attn_gate(dm, o, p)
    zc, y1 = _conv_fwd(dm, p, wl["conv_w"], wl["conv_b"], wl["ln_g"], wl["ln_b"])
    yc = _matmul(zc, wl["w_conv_out"], "nn", "conv_out")
    ya = _matmul(za, wl["w_attn_out"], "nn", "attn_out")
    merged = _merge(dm, yc, ya, p)
    outp = _matmul(merged, wl["w_out"], "nn", "out_proj")
    x_new = _post(dm, xc, outp, wl["g_post"], mod)
    saved = dict(xc=xc, mod=mod, h=h, p=p, qh=qh, qt=qt, kh=kh, kt=kt, vh=vh, o=o, lse=lse, za=za, zc=zc, y1=y1, yc=yc, ya=ya,
                 merged=merged, outp=outp)
    return x_new, saved


def _layer_bwd(dm, dxn, cv, wl, tabs, s):
    p = s["p"]
    d_outp, dg_post, dgt = _post_bwd(dm, s["outp"], wl["g_post"], s["mod"], dxn)
    dmerged = _matmul(d_outp, wl["w_out"], "nt", "out_proj_dx", BF16)
    dw_out = _matmul(s["merged"], d_outp, "tn", "out_proj_dw")
    dyc, dya, dgm = _merge_bwd(dm, dmerged, s["yc"], s["ya"], p)
    dzc = _matmul(dyc, wl["w_conv_out"], "nt", "conv_out_dx", BF16)
    dw_conv_out = _matmul(s["zc"], dyc, "tn", "conv_out_dw")
    dza = _matmul(dya, wl["w_attn_out"], "nt", "attn_out_dx", BF16)
    dw_attn_out = _matmul(s["za"], dya, "tn", "attn_out_dw")
    doh, dot, delta, dgate_b = _attn_gate_bwd(dm, dza, s["o"], p)
    dq, dkt, dvt = _flash_bwd(dm, s["qh"], s["qt"], s["kh"], s["kt"], s["vh"], doh, dot, s["lse"], delta)
    dpq, dpkv, dgq, dgk = _qk_bwd(dm, p, dq, dkt, dvt, tabs["cos"], tabs["sins"], wl["gq"], wl["gk"], tabs["bd"])
    dy1, dgate_a, dln_g, dln_b = _conv_act_bwd(dm, dzc, s["y1"], p, wl["ln_g"], wl["ln_b"])
    dpag, dconv_w, dconv_b = _conv_bwd(dm, dy1, p, wl["conv_w"])
    dp = jnp.concatenate([dpag, dgate_a, dpq, dpkv, dgate_b, dgm], axis=1)
    dh = _matmul(dp, wl["w_in"], "nt", "in_proj_dx", BF16)
    dw_in = _matmul(s["h"], dp, "tn", "in_proj_dw")
    dxc, dg_pre, dshsc = _prenorm_bwd(dm, s["xc"], wl["g_pre"], s["mod"], dh, dxn)
    dmod = jnp.concatenate([dshsc, dgt], axis=1)
    dw_mod, db_mod, dcv = _mod_bwd(cv, wl["w_mod"], dmod)
    grads = dict(w_mod=dw_mod, b_mod=db_mod[0], g_pre=dg_pre[0], g_post=dg_post[0], w_in=dw_in,
                 conv_w=dconv_w[:CONV_KERNEL], conv_b=dconv_b[0], ln_g=dln_g[0], ln_b=dln_b[0],
                 w_conv_out=dw_conv_out, q_norm_g=dgq[0, :dm.HD], k_norm_g=dgk[0, :dm.HD],
                 w_attn_out=dw_attn_out, w_out=dw_out, c_ctx=dcv[1])
    return dxc, grads


def _local_step(dm, x, c, ctx, c_ctx, target, wf):
    depth = wf["w_in"].shape[0]
    cos, sins = _rope_tables(dm)
    lane = np.arange(LANES)
    bd = jnp.asarray((lane[:, None] // dm.HD == lane[None, :] // dm.HD), dtype=BF16)
    tabs = dict(cos=cos, sins=sins, bd=bd)
    cv = jnp.concatenate([c.reshape(1, dm.D), c_ctx.reshape(1, dm.D), jnp.zeros((6, dm.D), F32)], axis=0)
    xc = jnp.concatenate([x, ctx], axis=0)
    layers, saved = [], []
    for l in range(depth):
        wl = dict(
            w_mod=wf["w_mod"][l], b_mod=wf["b_mod"][l].reshape(1, -1), g_pre=wf["g_pre"][l].reshape(1, -1),
            g_post=wf["g_post"][l].reshape(1, -1), w_in=wf["w_in"][l],
            conv_w=jnp.concatenate([wf["conv_w"][l], jnp.zeros((1, dm.C), F32)], axis=0),
            conv_b=wf["conv_b"][l].reshape(1, -1), ln_g=wf["ln_g"][l].reshape(1, -1), ln_b=wf["ln_b"][l].reshape(1, -1),
            w_conv_out=wf["w_conv_out"][l], gq=_lane_vec(wf["q_norm_g"][l], dm.HD), gk=_lane_vec(wf["k_norm_g"][l], dm.HD),
            w_attn_out=wf["w_attn_out"][l], w_out=wf["w_out"][l])
        layers.append(wl)
        xc, s = _layer_fwd(dm, xc, cv, wl, tabs)
        saved.append(s)
    dxc, loss = _loss_head(dm, xc, target)
    grads = [None] * depth
    for l in reversed(range(depth)):
        dxc, grads[l] = _layer_bwd(dm, dxc, cv, layers[l], tabs, saved[l])
    big = ("w_mod", "w_in", "w_conv_out", "w_attn_out", "w_out")
    gw = {k: [g[k] for g in grads] if k in big else jnp.stack([g[k] for g in grads]) for k in grads[0] if k != "c_ctx"}
    gw["c_ctx"] = grads[0]["c_ctx"] + grads[1]["c_ctx"] if depth == 2 else sum(g["c_ctx"] for g in grads)
    return loss[0, 0], dxc[:dm.N], gw


def _dims(x, ctx, w_in_full_cols, conv_c, attn_w, hd):
    n, d = x.shape
    ctx_len = ctx.shape[0]
    c, a = conv_c, attn_w
    kvw = (w_in_full_cols - 3 * c - 2 * a - 2 * d) // 2
    r = n + ctx_len
    tr = _pick(np.gcd(n, ctx_len), (256, 128))
    return Dims(N=n, CTX=ctx_len, R=r, D=d, C=c, A=a, KVW=kvw, HD=hd, NH=a // hd, NKV=kvw // hd,
                G=a // kvw, W=w_in_full_cols, TR=tr)


_HBM = pl.BlockSpec(memory_space=pltpu.HBM)
_VMEM = pl.BlockSpec(memory_space=pltpu.VMEM)


def _place():
    x, y, c = lax.axis_index("x"), lax.axis_index("y"), lax.axis_index("c")
    chips = [(1 - x, y), (x, 1 - y), (1 - x, 1 - y)]
    return x, y, c, chips


def _row_pieces(rows, want):
    n = max(k for k in range(1, want + 1) if rows % (16 * k) == 0)
    return [pl.ds(i * (rows // n), rows // n) for i in range(n)]


def _allgather_chips(own, name):
    _, h, w = own.shape
    pieces = _row_pieces(h, 4)
    npc = len(pieces)

    def body(own_ref, out_ref, send_sems, recv_sems):
        x, y, c, chips = _place()
        s_me = 2 * x + y
        sibling = (x, y, 1 - c)

        def copy(k, src, dst, to):
            return pltpu.make_async_remote_copy(src_ref=src, dst_ref=dst, send_sem=send_sems.at[k], recv_sem=recv_sems.at[k],
                                                device_id=to, device_id_type=MESH)

        sends = []
        for i, rows in enumerate(pieces):
            for j, chip in enumerate(chips):
                sends.append(copy(i * 3 + j, own_ref.at[c, rows], out_ref.at[s_me, c, rows], (*chip, c)))
        own_slots = [(6 * npc + hf * npc + i, hf, rows) for hf in range(2) for i, rows in enumerate(pieces)]
        for k, hf, rows in own_slots:
            sends.append(copy(k, own_ref.at[hf, rows], out_ref.at[s_me, hf, rows], sibling))
        for cp in sends:
            cp.start()
        for i, rows in enumerate(pieces):
            for j, (cx, cy) in enumerate(chips):
                landed = out_ref.at[2 * cx + cy, c, rows]
                copy(i * 3 + j, landed, landed, (cx, cy, c)).wait_recv()
                fwd = copy(3 * npc + i * 3 + j, landed, landed, sibling)
                fwd.start()
                sends.append(fwd)
        for i, rows in enumerate(pieces):
            for j, (cx, cy) in enumerate(chips):
                theirs = out_ref.at[2 * cx + cy, 1 - c, rows]
                copy(3 * npc + i * 3 + j, theirs, theirs, sibling).wait_recv()
        for k, hf, rows in own_slots:
            mine = out_ref.at[s_me, hf, rows]
            copy(k, mine, mine, sibling).wait_recv()
        for cp in sends:
            cp.wait_send()

    return pl.pallas_call(
        body, name=name, in_specs=[_HBM], out_specs=_HBM,
        out_shape=jax.ShapeDtypeStruct((N_CHIPS, 2, h, w), own.dtype),
        scratch_shapes=[pltpu.SemaphoreType.DMA((8 * npc,)), pltpu.SemaphoreType.DMA((8 * npc,))],
    )(own)


def _pair_exchange(buf, name):
    _, m, w = buf.shape
    pieces = _row_pieces(m, 16)

    def body(buf_ref, out_ref, send_sems, recv_sems):
        x, y, c, _ = _place()
        cps = [pltpu.make_async_remote_copy(src_ref=buf_ref.at[1 - c, rows], dst_ref=out_ref.at[rows], send_sem=send_sems.at[i],
                                            recv_sem=recv_sems.at[i], device_id=(x, y, 1 - c), device_id_type=MESH)
               for i, rows in enumerate(pieces)]
        for cp in cps:
            cp.start()
        for cp in cps:
            cp.wait()

    return pl.pallas_call(
        body, name=name, in_specs=[_HBM], out_specs=_HBM, out_shape=jax.ShapeDtypeStruct((m, w), buf.dtype),
        scratch_shapes=[pltpu.SemaphoreType.DMA((len(pieces),)), pltpu.SemaphoreType.DMA((len(pieces),))],
    )(buf)


def _pair_swap(t, name):
    m, w = t.shape
    pieces = _row_pieces(m, 8)
    n = len(pieces)

    def body(t_ref, out_ref, send_sems, recv_sems):
        x, y, c, _ = _place()
        cps = [pltpu.make_async_remote_copy(src_ref=t_ref.at[rows], dst_ref=out_ref.at[rows], send_sem=send_sems.at[i],
                                            recv_sem=recv_sems.at[i], device_id=(x, y, 1 - c), device_id_type=MESH)
               for i, rows in enumerate(pieces)]
        for cp in cps:
            cp.start()
        for cp in cps:
            cp.wait()

    return pl.pallas_call(
        body, name=name, in_specs=[_HBM], out_specs=_HBM, out_shape=jax.ShapeDtypeStruct((m, w), t.dtype),
        scratch_shapes=[pltpu.SemaphoreType.DMA((n,)), pltpu.SemaphoreType.DMA((n,))],
    )(t)


def _scatter_chips(s, name):
    _, h, w = s.shape

    def body(s_ref, out_ref, send_sems, recv_sems):
        _, _, c, chips = _place()
        cps = [pltpu.make_async_remote_copy(src_ref=s_ref.at[2 * cx + cy], dst_ref=out_ref.at[j], send_sem=send_sems.at[j],
                                            recv_sem=recv_sems.at[j], device_id=(cx, cy, c), device_id_type=MESH)
               for j, (cx, cy) in enumerate(chips)]
        for cp in cps:
            cp.start()
        for cp in cps:
            cp.wait()

    return pl.pallas_call(
        body, name=name, in_specs=[_HBM], out_specs=_HBM, out_shape=jax.ShapeDtypeStruct((3, h, w), s.dtype),
        scratch_shapes=[pltpu.SemaphoreType.DMA((3,)), pltpu.SemaphoreType.DMA((3,))],
    )(s)


def _sum_pair(buf, recv, core, name):
    m, w = recv.shape
    tr = _pick(m, (512, 432, 256, 128, 64, 32, 16))

    def body(c_ref, b_ref, r_ref, o_ref, ob_ref):
        v = b_ref[0] + r_ref[...]
        o_ref[...] = v
        ob_ref[...] = v.astype(BF16)

    spec = pl.BlockSpec((tr, w), lambda i, c: (i, 0))
    return pl.pallas_call(
        body, name=name,
        grid_spec=pltpu.PrefetchScalarGridSpec(
            num_scalar_prefetch=1, grid=(m // tr,),
            in_specs=[pl.BlockSpec((1, tr, w), lambda i, c: (c[0], i, 0)), spec], out_specs=(spec, spec)),
        out_shape=(jax.ShapeDtypeStruct((m, w), F32), jax.ShapeDtypeStruct((m, w), BF16)),
        compiler_params=_params(("parallel",)),
    )(core, buf, recv)


def _sum_chips(own, recv, chip, name):
    _, h, w = own.shape
    tr = _pick(h, (512, 432, 256, 128, 64, 32, 16))

    def body(s_ref, a_ref, r_ref, o_ref):
        acc = a_ref[0]
        for j in range(3):
            acc = acc + r_ref[j].astype(F32)
        o_ref[...] = acc

    return pl.pallas_call(
        body, name=name,
        grid_spec=pltpu.PrefetchScalarGridSpec(
            num_scalar_prefetch=1, grid=(h // tr,),
            in_specs=[pl.BlockSpec((1, tr, w), lambda i, s: (s[0], i, 0)), pl.BlockSpec((3, tr, w), lambda i, s: (0, i, 0))],
            out_specs=pl.BlockSpec((tr, w), lambda i, s: (i, 0))),
        out_shape=jax.ShapeDtypeStruct((h, w), F32), compiler_params=_params(("parallel",)),
    )(chip, own, recv)


def _allreduce_small(own, name):
    m, w = own.shape

    def body(own_ref, sum_ref, all_ref, send_sems, recv_sems):
        x, y, c, _ = _place()
        me = 4 * x + 2 * y + c
        all_ref[me] = own_ref[...]
        sends = []
        for k in range(1, N_DEV):
            peer = (x ^ ((k >> 2) & 1), y ^ ((k >> 1) & 1), c ^ (k & 1))
            cp = pltpu.make_async_remote_copy(src_ref=own_ref, dst_ref=all_ref.at[me], send_sem=send_sems.at[k - 1],
                                              recv_sem=recv_sems.at[k - 1], device_id=peer, device_id_type=MESH)
            cp.start()
            sends.append(cp)
        for k in range(1, N_DEV):
            px, py, pc = x ^ ((k >> 2) & 1), y ^ ((k >> 1) & 1), c ^ (k & 1)
            slot = all_ref.at[4 * px + 2 * py + pc]
            pltpu.make_async_remote_copy(src_ref=slot, dst_ref=slot, send_sem=send_sems.at[k - 1], recv_sem=recv_sems.at[k - 1],
                                         device_id=(px, py, pc), device_id_type=MESH).wait_recv()
        for cp in sends:
            cp.wait_send()
        acc = all_ref[0]
        for d in range(1, N_DEV):
            acc = acc + all_ref[d]
        sum_ref[...] = acc

    return pl.pallas_call(
        body, name=name, in_specs=[_VMEM], out_specs=_VMEM, out_shape=jax.ShapeDtypeStruct((m, w), F32),
        scratch_shapes=[pltpu.VMEM((N_DEV, m, w), F32), pltpu.SemaphoreType.DMA((N_DEV - 1,)), pltpu.SemaphoreType.DMA((N_DEV - 1,))],
        compiler_params=pltpu.CompilerParams(vmem_limit_bytes=VMEM_LIMIT),
    )(own)


_COL_SHARDED = ("w_mod", "w_in")
_ROW_SHARDED = ("w_conv_out", "w_attn_out", "w_out")
_BIG = _COL_SHARDED + _ROW_SHARDED
_SMALL = ("c_ctx", "b_mod", "g_pre", "g_post", "conv_b", "ln_g", "ln_b", "q_norm_g", "k_norm_g")
FLAT_W = 1024


def _unflatten_shard(flat, like):
    out, off = {}, 0
    flat = flat.reshape(-1)
    for k in _BIG:
        n = int(np.prod(like[k].shape))
        out[k] = flat[off:off + n].reshape(like[k].shape)
        off += n
    return out


def _by_shard(k, layers):
    parts = []
    for full in layers:
        if k in _COL_SHARDED:
            d, cols = full.shape
            parts.append(full.reshape(d, N_CHIPS, cols // N_CHIPS).transpose(1, 0, 2).reshape(N_CHIPS, -1))
        else:
            parts.append(full.reshape(N_CHIPS, -1))
    return jnp.concatenate(parts, axis=1)


def _from_shards(k, flat4, like):
    l = like.shape[0]
    if k in _COL_SHARDED:
        d, cs = like.shape[1:]
        return flat4.reshape(N_CHIPS, l, d, cs).transpose(1, 2, 0, 3).reshape(l, d, N_CHIPS * cs)
    rs, d = like.shape[1:]
    return flat4.reshape(N_CHIPS, l, rs, d).transpose(1, 0, 2, 3).reshape(l, N_CHIPS * rs, d)


def _pad_rows(flat, mult=8):
    n = flat.shape[0]
    rows = -(-n // LANES)
    rows = -(-rows // mult) * mult
    return jnp.concatenate([flat, jnp.zeros((rows * LANES - n,), F32)]).reshape(rows, LANES)


def kernel(x, c, ctx, c_ctx, w_mod, b_mod, g_pre, g_post, w_in, conv_w, conv_b, ln_g, ln_b, w_conv_out, q_norm_g, k_norm_g, w_attn_out, w_out, loss_target, m_c_ctx, m_w_mod, m_b_mod, m_g_pre, m_g_post, m_w_in, m_conv_w, m_conv_b, m_ln_g, m_ln_b, m_w_conv_out, m_q_norm_g, m_k_norm_g, m_w_attn_out, m_w_out, v_c_ctx, v_w_mod, v_b_mod, v_g_pre, v_g_post, v_w_in, v_conv_w, v_conv_b, v_ln_g, v_ln_b, v_w_conv_out, v_q_norm_g, v_k_norm_g, v_w_attn_out, v_w_out):
    w = dict(c_ctx=c_ctx, w_mod=w_mod, b_mod=b_mod, g_pre=g_pre, g_post=g_post, w_in=w_in, conv_w=conv_w, conv_b=conv_b,
             ln_g=ln_g, ln_b=ln_b, w_conv_out=w_conv_out, q_norm_g=q_norm_g, k_norm_g=k_norm_g, w_attn_out=w_attn_out, w_out=w_out)
    m = dict(c_ctx=m_c_ctx, w_mod=m_w_mod, b_mod=m_b_mod, g_pre=m_g_pre, g_post=m_g_post, w_in=m_w_in, conv_w=m_conv_w,
             conv_b=m_conv_b, ln_g=m_ln_g, ln_b=m_ln_b, w_conv_out=m_w_conv_out, q_norm_g=m_q_norm_g, k_norm_g=m_k_norm_g,
             w_attn_out=m_w_attn_out, w_out=m_w_out)
    v = dict(c_ctx=v_c_ctx, w_mod=v_w_mod, b_mod=v_b_mod, g_pre=v_g_pre, g_post=v_g_post, w_in=v_w_in, conv_w=v_conv_w,
             conv_b=v_conv_b, ln_g=v_ln_g, ln_b=v_ln_b, w_conv_out=v_w_conv_out, q_norm_g=v_q_norm_g, k_norm_g=v_k_norm_g,
             w_attn_out=v_w_attn_out, w_out=v_w_out)
    order = ("c_ctx", "w_mod", "b_mod", "g_pre", "g_post", "w_in", "conv_w", "conv_b", "ln_g", "ln_b", "w_conv_out",
             "q_norm_g", "k_norm_g", "w_attn_out", "w_out")
    my_chip = 2 * lax.axis_index("x") + lax.axis_index("y")
    my_core = lax.axis_index("c")
    depth, _, conv_cs = conv_w.shape
    conv_c = conv_cs * N_CHIPS
    dm = _dims(x[0], ctx[0], w_in.shape[2] * N_CHIPS, conv_c, w_attn_out.shape[1] * N_CHIPS, q_norm_g.shape[1])

    own = jnp.concatenate([w[k].astype(BF16).reshape(-1) for k in _BIG])
    rows2 = own.shape[0] // FLAT_W
    own = own.reshape(2, rows2 // 2, FLAT_W)
    gathered = _allgather_chips(own, "gather_weights").reshape(N_CHIPS, -1)
    wf, off = {}, 0
    for k in _BIG:
        n = int(np.prod(w[k].shape))
        wf[k] = _from_shards(k, gathered[:, off:off + n], w[k])
        off += n
    cw = jnp.zeros((depth, CONV_KERNEL, N_CHIPS, conv_cs), F32)
    cw = lax.dynamic_update_slice(cw, conv_w[:, :, None, :], (0, 0, my_chip, 0))
    cw = jnp.where(my_core == 0, cw, 0.0).reshape(-1)
    wf["conv_w"] = _allreduce_small(_pad_rows(cw), "gather_conv_w").reshape(-1)[:cw.shape[0]].reshape(depth, CONV_KERNEL, conv_c)
    for k in _SMALL:
        wf[k] = w[k]

    loss_part, grad_x, gw = _local_step(dm, x[0], c, ctx[0], c_ctx, loss_target[0], wf)

    g4 = jnp.concatenate([_by_shard(k, gw[k]) for k in _BIG], axis=1)
    hrows = rows2 // 2
    g24 = g4.reshape(N_CHIPS, 2, hrows, FLAT_W).transpose(1, 0, 2, 3).reshape(2, N_CHIPS * hrows, FLAT_W)
    core_i = my_core.astype(jnp.int32).reshape(1)
    chip_i = my_chip.astype(jnp.int32).reshape(1)
    theirs = _pair_exchange(g24, "reduce_pair")
    chip_sum, chip_sum_bf = _sum_pair(g24, theirs, core_i, "reduce_pair_add")
    others = _scatter_chips(chip_sum_bf.reshape(N_CHIPS, hrows, FLAT_W), "reduce_chips")
    mine = _sum_chips(chip_sum.reshape(N_CHIPS, hrows, FLAT_W), others, chip_i, "reduce_chips_add")
    sibling_half = _pair_swap(mine, "reduce_share")

    small = jnp.concatenate([gw[k].reshape(-1) for k in _SMALL] + [gw["conv_w"].reshape(-1), loss_part.reshape(1)])
    small_sum = _allreduce_small(_pad_rows(small), "reduce_small").reshape(-1)
    loss = small_sum[small.shape[0] - 1]
    gs, off = {}, 0
    for k in _SMALL:
        n = int(np.prod(w[k].shape))
        gs[k] = small_sum[off:off + n].reshape(w[k].shape)
        off += n
    gcw = small_sum[off:off + depth * CONV_KERNEL * conv_c].reshape(depth, CONV_KERNEL, N_CHIPS, conv_cs)
    gs["conv_w"] = lax.dynamic_index_in_dim(gcw, my_chip, axis=2, keepdims=False)

    g_flat = jnp.where(my_core == 0, jnp.concatenate([mine, sibling_half]), jnp.concatenate([sibling_half, mine]))
    grads = _unflatten_shard(g_flat, w)
    delta, new_m, new_v = {}, {}, {}
    for k in _BIG:
        as2d = lambda t: t.reshape(-1, t.shape[-1])
        d_k, m_k, v_k = _adamw(as2d(w[k]), as2d(grads[k]), as2d(m[k]), as2d(v[k]), "adamw_" + k)
        delta[k], new_m[k], new_v[k] = d_k.reshape(w[k].shape), m_k.reshape(w[k].shape), v_k.reshape(w[k].shape)
    small_keys = _SMALL + ("conv_w",)
    join = lambda t: _pad_rows(jnp.concatenate([t[k].reshape(-1) for k in small_keys]))
    d_s, m_s, v_s = _adamw(join(w), join(gs), join(m), join(v), "adamw_small")
    off = 0
    for k in small_keys:
        n = int(np.prod(w[k].shape))
        grads[k] = gs[k]
        for dst, src in ((delta, d_s), (new_m, m_s), (new_v, v_s)):
            dst[k] = src.reshape(-1)[off:off + n].reshape(w[k].shape)
        off += n
    return (loss, grad_x[None], *[grads[k] for k in order], *[delta[k] for k in order],
            *[new_m[k] for k in order], *[new_v[k] for k in order])
```

```python
from typing import NamedTuple

import jax
import jax.numpy as jnp
import numpy as np
from jax import lax
from jax.experimental import pallas as pl
from jax.experimental.pallas import tpu as pltpu

F32 = jnp.float32
BF16 = jnp.bfloat16
EPS = 1e-6
GRID_W = 64
ROPE_THETA = 10000.0
CONV_KERNEL = 31
CONV_PAD = CONV_KERNEL // 2
HALO = 16
LANES = 128
SUBLANES = 8
ADAM_LR, ADAM_B1, ADAM_B2, ADAM_EPS, ADAM_WD, ADAM_STEP = 0.001, 0.9, 0.999, 1e-08, 0.01, 10
VMEM_LIMIT = 56 * 1024 * 1024
ADAM_BLOCK_BYTES = 1024 * 1024
LOG2E = 1.4426950408889634
LN2 = 0.6931471805599453
MESH = pl.DeviceIdType.MESH
N_CHIPS = 4
N_DEV = 8


class Dims(NamedTuple):
    N: int
    CTX: int
    R: int
    D: int
    C: int
    A: int
    KVW: int
    HD: int
    NH: int
    NKV: int
    G: int
    W: int
    TR: int


def _pick(n, prefs):
    for p in prefs:
        if n % p == 0:
            return p
    raise ValueError(f"no tile for {n} in {prefs}")


def _params(sem):
    return pltpu.CompilerParams(dimension_semantics=sem, vmem_limit_bytes=VMEM_LIMIT)


def _rows(tr, w, col=0):
    return pl.BlockSpec((tr, w), lambda i, col=col: (i, col))


def _whole(shape):
    return pl.BlockSpec(shape, lambda *_: (0,) * len(shape))


def _col(off, w):
    assert off % w == 0, (off, w)
    return off // w


_DN = {"nn": (((1,), (0,)), ((), ())), "nt": (((1,), (1,)), ((), ())), "tn": (((0,), (0,)), ((), ()))}


def _matmul(a, b, mode, name, out_dtype=F32):
    if mode == "nn":
        (m, k), n = a.shape, b.shape[1]
    elif mode == "nt":
        (m, k), n = a.shape, b.shape[0]
    else:
        (k, m), n = a.shape, b.shape[1]
    tm = _pick(m, (1408, 1024, 768, 512, 256, 128, 8))
    tn = _pick(n, (1280, 1024, 768, 512, 256, 128))
    tk = _pick(k, (1536, 1408, 1280, 1024, 768, 512, 256, 128))
    nk = k // tk
    dn = _DN[mode]

    own_acc = out_dtype != F32 and nk > 1

    def body(a_ref, b_ref, o_ref, *scratch):
        part = lax.dot_general(a_ref[...], b_ref[...], dn, preferred_element_type=F32)
        if nk == 1:
            o_ref[...] = part.astype(out_dtype)
            return
        acc_ref = scratch[0] if own_acc else o_ref
        kk = pl.program_id(2)

        @pl.when(kk == 0)
        def _():
            acc_ref[...] = part

        @pl.when(kk > 0)
        def _():
            acc_ref[...] += part

        if own_acc:
            @pl.when(kk == nk - 1)
            def _():
                o_ref[...] = acc_ref[...].astype(out_dtype)

    a_spec = pl.BlockSpec((tk, tm), lambda i, j, kk: (kk, i)) if mode == "tn" else pl.BlockSpec((tm, tk), lambda i, j, kk: (i, kk))
    b_spec = pl.BlockSpec((tn, tk), lambda i, j, kk: (j, kk)) if mode == "nt" else pl.BlockSpec((tk, tn), lambda i, j, kk: (kk, j))
    return pl.pallas_call(
        body, name=name, grid=(m // tm, n // tn, nk),
        in_specs=[a_spec, b_spec], out_specs=pl.BlockSpec((tm, tn), lambda i, j, kk: (i, j)),
        out_shape=jax.ShapeDtypeStruct((m, n), out_dtype),
        scratch_shapes=[pltpu.VMEM((tm, tn), F32)] if own_acc else [],
        compiler_params=_params(("parallel", "parallel", "arbitrary")),
    )(a, b)


def _rms(x, g):
    return x * lax.rsqrt(jnp.mean(x * x, axis=-1, keepdims=True) + EPS) * g


def _prenorm_fn(x, g, sc, sh):
    return _rms(x, g) * (1.0 + sc) + sh


def _post_fn(outp, g, gt):
    return gt * _rms(outp, g)


def _convact_fn(y1, gate, ln_g, ln_b):
    mu = jnp.mean(y1, axis=-1, keepdims=True)
    var = jnp.mean(jnp.square(y1 - mu), axis=-1, keepdims=True)
    y = (y1 - mu) * lax.rsqrt(var + EPS) * ln_g + ln_b
    return jax.nn.silu(y) * jax.nn.silu(gate)


def _dsilu(x):
    s = jax.nn.sigmoid(x)
    return s * (1.0 + x * (1.0 - s))


def _seg_select(is_ctx, mod_ref, lo, w):
    return jnp.where(is_ctx, mod_ref[1:2, lo:lo + w], mod_ref[0:1, lo:lo + w])


def _seg_rows(is_ctx, v):
    row = lax.broadcasted_iota(jnp.int32, (8, v.shape[1]), 0)
    return jnp.where(row == is_ctx.astype(jnp.int32), v, 0.0)


def _mod_fwd(cv, w_mod_l, b_mod_l):
    def body(cv_ref, w_ref, b_ref, o_ref):
        s = jax.nn.silu(cv_ref[...]).astype(BF16)
        o_ref[...] = jnp.dot(s, w_ref[...], preferred_element_type=F32) + b_ref[...]

    return pl.pallas_call(
        body, name="mod_fwd", out_shape=jax.ShapeDtypeStruct((8, w_mod_l.shape[1]), F32),
        compiler_params=pltpu.CompilerParams(vmem_limit_bytes=VMEM_LIMIT),
    )(cv, w_mod_l, b_mod_l)


def _mod_bwd(cv, w_mod_l, dmod):
    d = cv.shape[1]

    def body(cv_ref, w_ref, dm_ref, dw_ref, db_ref, dcv_ref):
        cvv = cv_ref[...]
        s = jax.nn.silu(cvv).astype(BF16)
        dm = dm_ref[...]
        dmb = dm.astype(BF16)
        dw_ref[...] = lax.dot_general(s, dmb, _DN["tn"], preferred_element_type=F32)
        db_ref[...] = jnp.sum(dm, axis=0, keepdims=True)
        ds = lax.dot_general(dmb, w_ref[...], _DN["nt"], preferred_element_type=F32)
        dcv_ref[...] = ds * _dsilu(cvv)

    return pl.pallas_call(
        body, name="mod_bwd",
        out_shape=(jax.ShapeDtypeStruct(w_mod_l.shape, F32), jax.ShapeDtypeStruct((1, w_mod_l.shape[1]), F32),
                   jax.ShapeDtypeStruct((8, d), F32)),
        compiler_params=pltpu.CompilerParams(vmem_limit_bytes=VMEM_LIMIT),
    )(cv, w_mod_l, dmod)


def _prenorm(dm, xc, g_pre_l, mod):
    tr, d, n = dm.TR, dm.D, dm.N

    def body(x_ref, g_ref, mod_ref, h_ref):
        is_ctx = pl.program_id(0) * tr >= n
        sh = _seg_select(is_ctx, mod_ref, 0, d)
        sc = _seg_select(is_ctx, mod_ref, d, d)
        h_ref[...] = _prenorm_fn(x_ref[...], g_ref[...], sc, sh).astype(BF16)

    return pl.pallas_call(
        body, name="prenorm", grid=(dm.R // tr,),
        in_specs=[_rows(tr, d), _whole((1, d)), _whole((8, 3 * d))], out_specs=_rows(tr, d),
        out_shape=jax.ShapeDtypeStruct((dm.R, d), BF16), compiler_params=_params(("parallel",)),
    )(xc, g_pre_l, mod)


def _prenorm_bwd(dm, xc, g_pre_l, mod, dh, dxn):
    tr, d, n = dm.TR, dm.D, dm.N

    def body(x_ref, g_ref, mod_ref, dh_ref, dxn_ref, dx_ref, dg_ref, dss_ref):
        i = pl.program_id(0)
        is_ctx = i * tr >= n
        sh = _seg_select(is_ctx, mod_ref, 0, d)
        sc = _seg_select(is_ctx, mod_ref, d, d)
        _, vjp = jax.vjp(_prenorm_fn, x_ref[...], g_ref[...], sc, sh)
        dx, dg, dsc, dsh = vjp(dh_ref[...].astype(F32))
        dx_ref[...] = dxn_ref[...] + dx

        @pl.when(i == 0)
        def _():
            dg_ref[...] = jnp.zeros_like(dg_ref)
            dss_ref[...] = jnp.zeros_like(dss_ref)

        dg_ref[...] += dg
        dss_ref[...] += _seg_rows(is_ctx, jnp.concatenate([dsh, dsc], axis=1))

    return pl.pallas_call(
        body, name="prenorm_bwd", grid=(dm.R // tr,),
        in_specs=[_rows(tr, d), _whole((1, d)), _whole((8, 3 * d)), _rows(tr, d), _rows(tr, d)],
        out_specs=(_rows(tr, d), _whole((1, d)), _whole((8, 2 * d))),
        out_shape=(jax.ShapeDtypeStruct((dm.R, d), F32), jax.ShapeDtypeStruct((1, d), F32),
                   jax.ShapeDtypeStruct((8, 2 * d), F32)),
        compiler_params=_params(("arbitrary",)),
    )(xc, g_pre_l, mod, dh, dxn)


def _post(dm, xc, outp, g_post_l, mod):
    tr, d, n = dm.TR, dm.D, dm.N

    def body(x_ref, o_ref, g_ref, mod_ref, y_ref):
        is_ctx = pl.program_id(0) * tr >= n
        gt = _seg_select(is_ctx, mod_ref, 2 * d, d)
        y_ref[...] = x_ref[...] + _post_fn(o_ref[...].astype(F32), g_ref[...], gt)

    return pl.pallas_call(
        body, name="post", grid=(dm.R // tr,),
        in_specs=[_rows(tr, d), _rows(tr, d), _whole((1, d)), _whole((8, 3 * d))], out_specs=_rows(tr, d),
        out_shape=jax.ShapeDtypeStruct((dm.R, d), F32), compiler_params=_params(("parallel",)),
    )(xc, outp, g_post_l, mod)


def _post_bwd(dm, outp, g_post_l, mod, dxn):
    tr, d, n = dm.TR, dm.D, dm.N

    def body(o_ref, g_ref, mod_ref, dxn_ref, do_ref, dg_ref, dgt_ref):
        i = pl.program_id(0)
        is_ctx = i * tr >= n
        gt = _seg_select(is_ctx, mod_ref, 2 * d, d)
        _, vjp = jax.vjp(_post_fn, o_ref[...].astype(F32), g_ref[...], gt)
        do, dg, dgt = vjp(dxn_ref[...])
        do_ref[...] = do.astype(BF16)

        @pl.when(i == 0)
        def _():
            dg_ref[...] = jnp.zeros_like(dg_ref)
            dgt_ref[...] = jnp.zeros_like(dgt_ref)

        dg_ref[...] += dg
        dgt_ref[...] += _seg_rows(is_ctx, dgt)

    return pl.pallas_call(
        body, name="post_bwd", grid=(dm.R // tr,),
        in_specs=[_rows(tr, d), _whole((1, d)), _whole((8, 3 * d)), _rows(tr, d)],
        out_specs=(_rows(tr, d), _whole((1, d)), _whole((8, d))),
        out_shape=(jax.ShapeDtypeStruct((dm.R, d), BF16), jax.ShapeDtypeStruct((1, d), F32),
                   jax.ShapeDtypeStruct((8, d), F32)),
        compiler_params=_params(("arbitrary",)),
    )(outp, g_post_l, mod, dxn)


def _group_sum(x, bd):
    hi = x.astype(BF16)
    lo = (x - hi.astype(F32)).astype(BF16)
    return jnp.dot(hi, bd, preferred_element_type=F32) + jnp.dot(lo, bd, preferred_element_type=F32)


def _swap_halves(x):
    lane = lax.broadcasted_iota(jnp.int32, x.shape, 1)
    return jnp.where((lane & 32) == 0, pltpu.roll(x, LANES - 32, 1), pltpu.roll(x, 32, 1))


def _qk_prep(dm, p, cos, sins, gq, gk, bd):
    tr, hd = dm.TR, dm.HD
    oq, ok, ov = 3 * dm.C, 3 * dm.C + dm.A, 3 * dm.C + dm.A + dm.KVW
    scale = hd ** -0.5 * LOG2E

    def body(q_ref, k_ref, v_ref, cos_ref, sin_ref, gq_ref, gk_ref, bd_ref, qh_ref, qt_ref, kh_ref, kt_ref, vh_ref, vx_ref):
        cosv, sinv, bdv = cos_ref[...], sin_ref[...], bd_ref[...]

        def chunk(x, g):
            rstd = lax.rsqrt(_group_sum(x * x, bdv) * (1.0 / hd) + EPS)
            xn = x * rstd * g
            return xn * cosv + _swap_halves(xn) * sinv

        for c in range(dm.A // LANES):
            y = chunk(q_ref[:, c * LANES:(c + 1) * LANES].astype(F32), gq_ref[...]) * scale
            qh_ref[2 * c] = y[:, :hd].astype(BF16)
            qh_ref[2 * c + 1] = y[:, hd:].astype(BF16)
            qt_ref[c * LANES:(c + 1) * LANES, :] = y.T.astype(BF16)
        for c in range(dm.KVW // LANES):
            y = chunk(k_ref[:, c * LANES:(c + 1) * LANES].astype(F32), gk_ref[...])
            kh_ref[2 * c] = y[:, :hd].astype(BF16)
            kh_ref[2 * c + 1] = y[:, hd:].astype(BF16)
            kt_ref[c * LANES:(c + 1) * LANES, :] = y.T.astype(BF16)
            v = v_ref[:, c * LANES:(c + 1) * LANES].astype(F32)
            vh_ref[2 * c] = v[:, :hd].astype(BF16)
            vh_ref[2 * c + 1] = v[:, hd:].astype(BF16)
            lane = lax.broadcasted_iota(jnp.int32, v.shape, 1)
            one_at_hd = (lane == hd).astype(F32)
            vx_ref[2 * c] = jnp.where(lane < hd, v, one_at_hd).astype(BF16)
            vx_ref[2 * c + 1] = jnp.where(lane < hd, pltpu.roll(v, hd, 1), one_at_hd).astype(BF16)

    hspec = lambda nh, w=hd: pl.BlockSpec((nh, tr, w), lambda i: (0, i, 0))
    return pl.pallas_call(
        body, name="qk_prep", grid=(dm.R // tr,),
        in_specs=[_rows(tr, dm.A, _col(oq, dm.A)), _rows(tr, dm.KVW, _col(ok, dm.KVW)), _rows(tr, dm.KVW, _col(ov, dm.KVW)),
                  _rows(tr, LANES), _rows(tr, LANES), _whole((1, LANES)), _whole((1, LANES)), _whole((LANES, LANES))],
        out_specs=(hspec(dm.NH), pl.BlockSpec((dm.A, tr), lambda i: (0, i)), hspec(dm.NKV),
                   pl.BlockSpec((dm.KVW, tr), lambda i: (0, i)), hspec(dm.NKV), hspec(dm.NKV, LANES)),
        out_shape=(jax.ShapeDtypeStruct((dm.NH, dm.R, hd), BF16), jax.ShapeDtypeStruct((dm.A, dm.R), BF16),
                   jax.ShapeDtypeStruct((dm.NKV, dm.R, hd), BF16), jax.ShapeDtypeStruct((dm.KVW, dm.R), BF16),
                   jax.ShapeDtypeStruct((dm.NKV, dm.R, hd), BF16), jax.ShapeDtypeStruct((dm.NKV, dm.R, LANES), BF16)),
        compiler_params=_params(("parallel",)),
    )(p, p, p, cos, sins, gq, gk, bd)


def _qk_bwd(dm, p, dq, dkt, dvt, cos, sins, gq, gk, bd):
    tr, hd = dm.TR, dm.HD
    oq, ok = 3 * dm.C, 3 * dm.C + dm.A
    scale = hd ** -0.5

    def body(q_ref, k_ref, dq_ref, dkh_ref, dvh_ref, cos_ref, sin_ref, gq_ref, gk_ref, bd_ref,
             dpq_ref, dpkv_ref, dgq_ref, dgk_ref):
        i = pl.program_id(0)
        cosv, sinv, bdv = cos_ref[...], sin_ref[...], bd_ref[...]

        def chunk(x, g, dy):
            rstd = lax.rsqrt(_group_sum(x * x, bdv) * (1.0 / hd) + EPS)
            xhat = x * rstd
            dxn = dy * cosv + _swap_halves(dy * sinv)
            dxhat = dxn * g
            dx = rstd * (dxhat - xhat * (_group_sum(dxhat * xhat, bdv) * (1.0 / hd)))
            return dx, jnp.sum(dxn * xhat, axis=0, keepdims=True)

        @pl.when(i == 0)
        def _():
            dgq_ref[...] = jnp.zeros_like(dgq_ref)
            dgk_ref[...] = jnp.zeros_like(dgk_ref)

        dgq = jnp.zeros((1, LANES), F32)
        for c in range(dm.A // LANES):
            sl = slice(c * LANES, (c + 1) * LANES)
            dx, dg = chunk(q_ref[:, sl].astype(F32), gq_ref[...], dq_ref[:, sl] * scale)
            dpq_ref[:, sl] = dx.astype(BF16)
            dgq = dgq + dg
        dgq_ref[...] += dgq + pltpu.roll(dgq, hd, 1)
        dgk = jnp.zeros((1, LANES), F32)
        for c in range(dm.KVW // LANES):
            sl = slice(c * LANES, (c + 1) * LANES)
            dx, dg = chunk(k_ref[:, sl].astype(F32), gk_ref[...], dkh_ref[sl, :].T)
            dpkv_ref[:, sl] = dx.astype(BF16)
            dgk = dgk + dg
            dpkv_ref[:, dm.KVW + c * LANES:dm.KVW + (c + 1) * LANES] = dvh_ref[sl, :].T.astype(BF16)
        dgk_ref[...] += dgk + pltpu.roll(dgk, hd, 1)

    hspec = pl.BlockSpec((dm.KVW, tr), lambda i: (0, i))
    return pl.pallas_call(
        body, name="qk_bwd", grid=(dm.R // tr,),
        in_specs=[_rows(tr, dm.A, _col(oq, dm.A)), _rows(tr, dm.KVW, _col(ok, dm.KVW)), _rows(tr, dm.A), hspec, hspec,
                  _rows(tr, LANES), _rows(tr, LANES), _whole((1, LANES)), _whole((1, LANES)), _whole((LANES, LANES))],
        out_specs=(_rows(tr, dm.A), _rows(tr, 2 * dm.KVW), _whole((1, LANES)), _whole((1, LANES))),
        out_shape=(jax.ShapeDtypeStruct((dm.R, dm.A), BF16), jax.ShapeDtypeStruct((dm.R, 2 * dm.KVW), BF16),
                   jax.ShapeDtypeStruct((1, LANES), F32), jax.ShapeDtypeStruct((1, LANES), F32)),
        compiler_params=_params(("arbitrary",)),
    )(p, p, dq, dkt, dvt, cos, sins, gq, gk, bd)


def _flash_fwd(dm, qh, kh, vx):
    tq, ks = dm.TR, dm.TR
    rs = min(tq, LANES)
    g, hd = dm.G, dm.HD
    nq, nq_lat = dm.R // tq, dm.N // tq
    n_chunks, ctx_chunks = dm.R // ks, dm.CTX // ks

    def body(q_ref, k_ref, v_ref, o_ref, lse_ref):
        qi = pl.program_id(1)

        def attend(chunks):
            for gg in range(g):
                for r in range(tq // rs):
                    rows = slice(r * rs, (r + 1) * rs)
                    q = q_ref[gg, rows, :]
                    m_prev = jnp.full((rs, LANES), -1e30, F32)
                    acc = jnp.zeros((rs, LANES), F32)
                    for kc in chunks:
                        keys = slice(kc * ks, (kc + 1) * ks)
                        s = lax.dot_general(q, k_ref[0, keys, :], _DN["nt"], preferred_element_type=F32)
                        m_new = jnp.maximum(m_prev, jnp.max(s, axis=-1, keepdims=True))
                        p = jnp.exp2(s - jnp.tile(m_new, (1, ks // LANES)))
                        acc = acc * jnp.exp2(m_prev - m_new) + jnp.dot(p.astype(BF16), v_ref[0, keys, :],
                                                                        preferred_element_type=F32)
                        m_prev = m_new
                    l = acc[:, hd:hd + 1]
                    o_ref[rows, gg * hd:(gg + 1) * hd] = acc[:, :hd] / l
                    lse_ref[gg, rows, :] = m_prev + jnp.log2(l)

        @pl.when(qi < nq_lat)
        def _():
            attend(range(n_chunks))

        @pl.when(qi >= nq_lat)
        def _():
            attend(range(n_chunks - ctx_chunks, n_chunks))

    return pl.pallas_call(
        body, name="flash_fwd", grid=(dm.NKV, nq),
        in_specs=[pl.BlockSpec((g, tq, hd), lambda h, qi: (h, qi, 0)), pl.BlockSpec((1, dm.R, hd), lambda h, qi: (h, 0, 0)),
                  pl.BlockSpec((1, dm.R, LANES), lambda h, qi: (h, 0, 0))],
        out_specs=(pl.BlockSpec((tq, g * hd), lambda h, qi: (qi, h)), pl.BlockSpec((g, tq, LANES), lambda h, qi: (h, qi, 0))),
        out_shape=(jax.ShapeDtypeStruct((dm.R, dm.A), F32), jax.ShapeDtypeStruct((dm.NH, dm.R, LANES), F32)),
        compiler_params=_params(("parallel", "parallel")),
    )(qh, kh, vx)


def _flash_bwd(dm, qh, qt, kh, kt, vh, doh, dot, lse, delta):
    tq, ks = dm.TR, dm.TR
    g, hd = dm.G, dm.HD
    nq, nq_lat = dm.R // tq, dm.N // tq
    n_chunks, ctx_chunks = dm.R // ks, dm.CTX // ks

    def body(q_ref, qt_ref, k_ref, kt_ref, v_ref, do_ref, dot_ref, lse_ref, dl_ref, dq_ref, dk_acc, dv_acc):
        qi = pl.program_id(1)

        @pl.when(qi == 0)
        def _():
            dk_acc[...] = jnp.zeros_like(dk_acc)
            dv_acc[...] = jnp.zeros_like(dv_acc)

        def attend(chunks):
            q = q_ref[...].reshape(g * tq, hd)
            do = do_ref[...].reshape(g * tq, hd)
            q_t = jnp.concatenate([qt_ref[gg * hd:(gg + 1) * hd, :] for gg in range(g)], axis=1)
            do_t = jnp.concatenate([dot_ref[gg * hd:(gg + 1) * hd, :] for gg in range(g)], axis=1)
            wide = lambda ref: jnp.tile(ref[...].reshape(g * tq, LANES), (1, ks // LANES))
            dq_t = jnp.zeros((hd, g * tq), F32)
            for kc in chunks:
                keys = slice(kc * ks, (kc + 1) * ks)
                k, v = k_ref[0, keys, :], v_ref[0, keys, :]
                s = lax.dot_general(q, k, _DN["nt"], preferred_element_type=F32)
                p = jnp.exp2(s - wide(lse_ref))
                dp = lax.dot_general(do, v, _DN["nt"], preferred_element_type=F32)
                ds = p * (dp - wide(dl_ref))
                dv_acc[:, keys] += jnp.dot(do_t, p.astype(BF16), preferred_element_type=F32)
                dk_acc[:, keys] += jnp.dot(q_t, ds.astype(BF16), preferred_element_type=F32)
                dq_t = dq_t + jnp.dot(kt_ref[:, keys], ds.T.astype(BF16), preferred_element_type=F32)
            for gg in range(g):
                dq_ref[:, gg * hd:(gg + 1) * hd] = dq_t[:, gg * tq:(gg + 1) * tq].T

        @pl.when(qi < nq_lat)
        def _():
            attend(range(n_chunks))

        @pl.when(qi >= nq_lat)
        def _():
            attend(range(n_chunks - ctx_chunks, n_chunks))

        @pl.when(qi == nq - 1)
        def _():
            dk_acc[...] = dk_acc[...] * LN2

    qspec = pl.BlockSpec((g, tq, hd), lambda h, qi: (h, qi, 0))
    tspec = pl.BlockSpec((g * hd, tq), lambda h, qi: (h, qi))
    kspec = pl.BlockSpec((1, dm.R, hd), lambda h, qi: (h, 0, 0))
    sspec = pl.BlockSpec((g, tq, LANES), lambda h, qi: (h, qi, 0))
    ospec = pl.BlockSpec((hd, dm.R), lambda h, qi: (h, 0))
    return pl.pallas_call(
        body, name="flash_bwd", grid=(dm.NKV, nq),
        in_specs=[qspec, tspec, kspec, ospec, kspec, qspec, tspec, sspec, sspec],
        out_specs=(pl.BlockSpec((tq, g * hd), lambda h, qi: (qi, h)), ospec, ospec),
        out_shape=(jax.ShapeDtypeStruct((dm.R, dm.A), F32), jax.ShapeDtypeStruct((dm.KVW, dm.R), F32),
                   jax.ShapeDtypeStruct((dm.KVW, dm.R), F32)),
        compiler_params=_params(("parallel", "arbitrary")),
    )(qh, qt, kh, kt, vh, doh, dot, lse, delta)


def _gate_b_specs(dm):
    ogb = 3 * dm.C + dm.A + 2 * dm.KVW
    half = dm.A // 2
    return [_rows(dm.TR, half, _col(ogb, half)), _rows(dm.TR, half, _col(ogb, half) + 1)]


def _attn_gate(dm, o, p):
    tr = dm.TR

    def body(o_ref, g0_ref, g1_ref, za_ref):
        gate = jnp.concatenate([g0_ref[...], g1_ref[...]], axis=1).astype(F32)
        za_ref[...] = (o_ref[...] * jax.nn.silu(gate)).astype(BF16)

    return pl.pallas_call(
        body, name="attn_gate", grid=(dm.R // tr,),
        in_specs=[_rows(tr, dm.A)] + _gate_b_specs(dm), out_specs=_rows(tr, dm.A),
        out_shape=jax.ShapeDtypeStruct((dm.R, dm.A), BF16), compiler_params=_params(("parallel",)),
    )(o, p, p)


def _attn_gate_bwd(dm, dza, o, p):
    tr, hd = dm.TR, dm.HD

    def body(dza_ref, o_ref, g0_ref, g1_ref, doh_ref, dot_ref, dl_ref, dgb_ref):
        gate = jnp.concatenate([g0_ref[...], g1_ref[...]], axis=1).astype(F32)
        dza, o = dza_ref[...].astype(F32), o_ref[...]
        do = dza * jax.nn.silu(gate)
        dgb_ref[...] = (dza * o * _dsilu(gate)).astype(BF16)
        dot_ref[...] = do.T.astype(BF16)
        prod = do * o
        for h in range(dm.NH):
            sl = slice(h * hd, (h + 1) * hd)
            doh_ref[h] = do[:, sl].astype(BF16)
            dl_ref[h] = jnp.broadcast_to(jnp.sum(prod[:, sl], axis=-1, keepdims=True), (tr, LANES))

    return pl.pallas_call(
        body, name="attn_gate_bwd", grid=(dm.R // tr,),
        in_specs=[_rows(tr, dm.A), _rows(tr, dm.A)] + _gate_b_specs(dm),
        out_specs=(pl.BlockSpec((dm.NH, tr, hd), lambda i: (0, i, 0)), pl.BlockSpec((dm.A, tr), lambda i: (0, i)),
                   pl.BlockSpec((dm.NH, tr, LANES), lambda i: (0, i, 0)), _rows(tr, dm.A)),
        out_shape=(jax.ShapeDtypeStruct((dm.NH, dm.R, hd), BF16), jax.ShapeDtypeStruct((dm.A, dm.R), BF16),
                   jax.ShapeDtypeStruct((dm.NH, dm.R, LANES), F32), jax.ShapeDtypeStruct((dm.R, dm.A), BF16)),
        compiler_params=_params(("parallel",)),
    )(dza, o, p, p)


def _halo_specs(dm, col):
    per = dm.TR // HALO
    last = dm.R // HALO - 1
    prev = pl.BlockSpec((HALO, dm.C), lambda i: (jnp.maximum(i * per - 1, 0), col))
    nxt = pl.BlockSpec((HALO, dm.C), lambda i: (jnp.minimum((i + 1) * per, last), col))
    return prev, nxt


def _seg_valid(dm, i):
    tr = dm.TR
    r = i * tr - HALO + lax.broadcasted_iota(jnp.int32, (tr + 2 * HALO, 1), 0)
    is_ctx = i * tr >= dm.N
    lo = jnp.where(is_ctx, dm.N, 0)
    hi = jnp.where(is_ctx, dm.R, dm.N)
    return jnp.logical_and(r >= lo, r < hi)


CONV_RB = 64


def _shifted_copies(src_ref, sl, sh_ref, tr):
    rows = tr + 2 * HALO - SUBLANES
    padded = src_ref[:, sl]
    for ph in range(SUBLANES):
        sh_ref[ph, 0:rows, :] = padded[ph:ph + rows]


def _tap_rows(sh_ref, off, n):
    ph = off % SUBLANES
    return sh_ref[ph, off - ph:off - ph + n, :]


def _conv_fwd(dm, p, conv_w_l, conv_b_l, ln_g_l, ln_b_l):
    tr, c = dm.TR, dm.C

    def body(a_ref, g_ref, gate_ref, ap_ref, an_ref, gp_ref, gn_ref, w_ref, b_ref, lg_ref, lb_ref,
             zc_ref, y1_ref, u_ref, sh_ref):
        i = pl.program_id(0)
        a = jnp.concatenate([ap_ref[...], a_ref[...], an_ref[...]], axis=0).astype(F32)
        g = jnp.concatenate([gp_ref[...], g_ref[...], gn_ref[...]], axis=0).astype(F32)
        u_ref[...] = jnp.where(_seg_valid(dm, i), a * jax.nn.sigmoid(g), 0.0)
        for cc in range(c // LANES):
            sl = slice(cc * LANES, (cc + 1) * LANES)
            _shifted_copies(u_ref, sl, sh_ref, tr)
            for r0 in range(0, tr, CONV_RB):
                acc = jnp.zeros((CONV_RB, LANES), F32)
                for j in range(CONV_KERNEL):
                    acc = acc + w_ref[j:j + 1, sl] * _tap_rows(sh_ref, r0 + HALO - CONV_PAD + j, CONV_RB)
                y1_ref[r0:r0 + CONV_RB, sl] = acc + b_ref[:, sl]
        zc_ref[...] = _convact_fn(y1_ref[...], gate_ref[...].astype(F32), lg_ref[...], lb_ref[...]).astype(BF16)

    ap, an = _halo_specs(dm, 0)
    gp, gn = _halo_specs(dm, 1)
    return pl.pallas_call(
        body, name="conv_fwd", grid=(dm.R // tr,),
        in_specs=[_rows(tr, c, 0), _rows(tr, c, 1), _rows(tr, c, 2), ap, an, gp, gn,
                  _whole((32, c)), _whole((1, c)), _whole((1, c)), _whole((1, c))],
        out_specs=(_rows(tr, c), _rows(tr, c)),
        out_shape=(jax.ShapeDtypeStruct((dm.R, c), BF16), jax.ShapeDtypeStruct((dm.R, c), F32)),
        scratch_shapes=[pltpu.VMEM((tr + 2 * HALO, c), F32), pltpu.VMEM((SUBLANES, tr + 2 * HALO, LANES), F32)],
        compiler_params=_params(("parallel",)),
    )(p, p, p, p, p, p, p, conv_w_l, conv_b_l, ln_g_l, ln_b_l)


def _conv_act_bwd(dm, dzc, y1, p, ln_g_l, ln_b_l):
    tr, c = dm.TR, dm.C

    def body(dz_ref, y1_ref, gate_ref, lg_ref, lb_ref, dy_ref, dgate_ref, dlg_ref, dlb_ref):
        i = pl.program_id(0)
        _, vjp = jax.vjp(_convact_fn, y1_ref[...], gate_ref[...].astype(F32), lg_ref[...], lb_ref[...])
        dy, dgate, dlg, dlb = vjp(dz_ref[...].astype(F32))
        dy_ref[...] = dy
        dgate_ref[...] = dgate.astype(BF16)

        @pl.when(i == 0)
        def _():
            dlg_ref[...] = jnp.zeros_like(dlg_ref)
            dlb_ref[...] = jnp.zeros_like(dlb_ref)

        dlg_ref[...] += dlg
        dlb_ref[...] += dlb

    return pl.pallas_call(
        body, name="conv_act_bwd", grid=(dm.R // tr,),
        in_specs=[_rows(tr, c), _rows(tr, c), _rows(tr, c, 2), _whole((1, c)), _whole((1, c))],
        out_specs=(_rows(tr, c), _rows(tr, c), _whole((1, c)), _whole((1, c))),
        out_shape=(jax.ShapeDtypeStruct((dm.R, c), F32), jax.ShapeDtypeStruct((dm.R, c), BF16),
                   jax.ShapeDtypeStruct((1, c), F32), jax.ShapeDtypeStruct((1, c), F32)),
        compiler_params=_params(("arbitrary",)),
    )(dzc, y1, p, ln_g_l, ln_b_l)


def _conv_bwd(dm, dy1, p, conv_w_l):
    tr, c = dm.TR, dm.C

    def body(dy_ref, dyp_ref, dyn_ref, a_ref, g_ref, ap_ref, an_ref, gp_ref, gn_ref, w_ref,
             dag_ref, dw_ref, db_ref, u_ref, dyp_scr, du_ref, dwp_ref, shu_ref, shd_ref):
        i = pl.program_id(0)
        valid = _seg_valid(dm, i)
        a = jnp.concatenate([ap_ref[...], a_ref[...], an_ref[...]], axis=0).astype(F32)
        g = jnp.concatenate([gp_ref[...], g_ref[...], gn_ref[...]], axis=0).astype(F32)
        u_ref[...] = jnp.where(valid, a * jax.nn.sigmoid(g), 0.0)
        dyp_scr[...] = jnp.where(valid, jnp.concatenate([dyp_ref[...], dy_ref[...], dyn_ref[...]], axis=0), 0.0)

        @pl.when(i == 0)
        def _():
            dwp_ref[...] = jnp.zeros_like(dwp_ref)
            db_ref[...] = jnp.zeros_like(db_ref)

        db_ref[...] += jnp.sum(dy_ref[...], axis=0, keepdims=True)
        for cc in range(c // LANES):
            sl = slice(cc * LANES, (cc + 1) * LANES)
            _shifted_copies(u_ref, sl, shu_ref, tr)
            _shifted_copies(dyp_scr, sl, shd_ref, tr)
            for r0 in range(0, tr, CONV_RB):
                dy_cur = dy_ref[r0:r0 + CONV_RB, sl]
                acc = jnp.zeros((CONV_RB, LANES), F32)
                for j in range(CONV_KERNEL):
                    acc = acc + w_ref[j:j + 1, sl] * _tap_rows(shd_ref, r0 + HALO + CONV_PAD - j, CONV_RB)
                    prod = dy_cur * _tap_rows(shu_ref, r0 + HALO - CONV_PAD + j, CONV_RB)
                    dwp_ref[j, :, sl] += jnp.sum(prod.reshape(CONV_RB // SUBLANES, SUBLANES, LANES), axis=0)
                du_ref[r0:r0 + CONV_RB, sl] = acc
        du = du_ref[...]
        sg = jax.nn.sigmoid(g_ref[...].astype(F32))
        dag_ref[:, :c] = (du * sg).astype(BF16)
        dag_ref[:, c:] = (du * a_ref[...].astype(F32) * sg * (1.0 - sg)).astype(BF16)

        @pl.when(i == dm.R // tr - 1)
        def _():
            dw_ref[...] = jnp.sum(dwp_ref[...], axis=1)

    ap, an = _halo_specs(dm, 0)
    gp, gn = _halo_specs(dm, 1)
    dyp, dyn = _halo_specs(dm, 0)
    return pl.pallas_call(
        body, name="conv_bwd", grid=(dm.R // tr,),
        in_specs=[_rows(tr, c), dyp, dyn, _rows(tr, c, 0), _rows(tr, c, 1), ap, an, gp, gn, _whole((32, c))],
        out_specs=(_rows(tr, 2 * c), _whole((32, c)), _whole((1, c))),
        out_shape=(jax.ShapeDtypeStruct((dm.R, 2 * c), BF16), jax.ShapeDtypeStruct((32, c), F32),
                   jax.ShapeDtypeStruct((1, c), F32)),
        scratch_shapes=[pltpu.VMEM((tr + 2 * HALO, c), F32), pltpu.VMEM((tr + 2 * HALO, c), F32), pltpu.VMEM((tr, c), F32),
                        pltpu.VMEM((32, SUBLANES, c), F32), pltpu.VMEM((SUBLANES, tr + 2 * HALO, LANES), F32),
                        pltpu.VMEM((SUBLANES, tr + 2 * HALO, LANES), F32)],
        compiler_params=_params(("arbitrary",)),
    )(dy1, dy1, dy1, p, p, p, p, p, p, conv_w_l)


def _gm_specs(dm):
    half = dm.D // 2
    first = _col(3 * dm.C + 2 * dm.A + 2 * dm.KVW, half)
    return [_rows(dm.TR, half, first + k) for k in range(4)]


def _gm_load(a0, a1, b0, b1):
    return (jnp.concatenate([a0[...], a1[...]], axis=1).astype(F32),
            jnp.concatenate([b0[...], b1[...]], axis=1).astype(F32))


def _merge(dm, yc, ya, p):
    tr, d = dm.TR, dm.D

    def body(yc_ref, ya_ref, a0, a1, b0, b1, m_ref):
        ga, gb = _gm_load(a0, a1, b0, b1)
        yc, ya = yc_ref[...].astype(F32), ya_ref[...].astype(F32)
        m_ref[...] = (jax.nn.sigmoid(ga) * yc + jax.nn.sigmoid(gb) * ya).astype(BF16)

    return pl.pallas_call(
        body, name="merge", grid=(dm.R // tr,),
        in_specs=[_rows(tr, d), _rows(tr, d)] + _gm_specs(dm), out_specs=_rows(tr, d),
        out_shape=jax.ShapeDtypeStruct((dm.R, d), BF16), compiler_params=_params(("parallel",)),
    )(yc, ya, p, p, p, p)


def _merge_bwd(dm, dmerged, yc, ya, p):
    tr, d = dm.TR, dm.D

    def body(dm_ref, yc_ref, ya_ref, a0, a1, b0, b1, dyc_ref, dya_ref, dgm_ref):
        dmv = dm_ref[...].astype(F32)
        ga, gb = _gm_load(a0, a1, b0, b1)
        sa, sb = jax.nn.sigmoid(ga), jax.nn.sigmoid(gb)
        dyc_ref[...] = (dmv * sa).astype(BF16)
        dya_ref[...] = (dmv * sb).astype(BF16)
        dgm_ref[:, :d] = (dmv * yc_ref[...].astype(F32) * sa * (1.0 - sa)).astype(BF16)
        dgm_ref[:, d:] = (dmv * ya_ref[...].astype(F32) * sb * (1.0 - sb)).astype(BF16)

    return pl.pallas_call(
        body, name="merge_bwd", grid=(dm.R // tr,),
        in_specs=[_rows(tr, d), _rows(tr, d), _rows(tr, d)] + _gm_specs(dm),
        out_specs=(_rows(tr, d), _rows(tr, d), _rows(tr, 2 * d)),
        out_shape=(jax.ShapeDtypeStruct((dm.R, d), BF16), jax.ShapeDtypeStruct((dm.R, d), BF16),
                   jax.ShapeDtypeStruct((dm.R, 2 * d), BF16)),
        compiler_params=_params(("parallel",)),
    )(dmerged, yc, ya, p, p, p, p)


def _loss_head(dm, xc, target):
    tr, d = dm.TR, dm.D
    n_lat = dm.N // tr

    def body(x_ref, t_ref, dy_ref, l_ref):
        i = pl.program_id(0)

        @pl.when(i == 0)
        def _():
            l_ref[...] = jnp.zeros_like(l_ref)

        @pl.when(i < n_lat)
        def _():
            e = x_ref[...] - t_ref[...]
            dy_ref[...] = e * (1.0 / d)
            l_ref[...] += 0.5 * jnp.sum(jnp.mean(e * e, axis=-1, keepdims=True), axis=0, keepdims=True)

        @pl.when(i >= n_lat)
        def _():
            dy_ref[...] = jnp.zeros_like(dy_ref)

    return pl.pallas_call(
        body, name="loss_head", grid=(dm.R // tr,),
        in_specs=[_rows(tr, d), pl.BlockSpec((tr, d), lambda i: (jnp.minimum(i, n_lat - 1), 0))],
        out_specs=(_rows(tr, d), _whole((1, 1))),
        out_shape=(jax.ShapeDtypeStruct((dm.R, d), F32), jax.ShapeDtypeStruct((1, 1), F32)),
        compiler_params=_params(("arbitrary",)),
    )(xc, target)


def _adamw(w, g, m, v, name):
    rows, cols = w.shape
    fits = lambda t: t * cols * 4 <= ADAM_BLOCK_BYTES
    tr = rows if fits(rows) else _pick(rows, [t for t in (512, 256, 128, 64, 32, 16, 8) if fits(t)])

    def body(w_ref, g_ref, m_ref, v_ref, d_ref, mo_ref, vo_ref):
        d_ref[...], mo_ref[...], vo_ref[...] = _adam_update(w_ref[...], g_ref[...], m_ref[...], v_ref[...])

    spec = _rows(tr, cols)
    shp = jax.ShapeDtypeStruct(w.shape, F32)
    return pl.pallas_call(
        body, name=name, grid=(rows // tr,), in_specs=[spec] * 4, out_specs=(spec,) * 3, out_shape=(shp,) * 3,
        compiler_params=_params(("parallel",)),
    )(w, g, m, v)


def _adam_update(w, g, m, v):
    mn = ADAM_B1 * m + (1.0 - ADAM_B1) * g
    vn = ADAM_B2 * v + (1.0 - ADAM_B2) * jnp.square(g)
    m_hat = mn / (1.0 - ADAM_B1 ** ADAM_STEP)
    v_hat = vn / (1.0 - ADAM_B2 ** ADAM_STEP)
    return -ADAM_LR * (m_hat / (jnp.sqrt(v_hat) + ADAM_EPS) + ADAM_WD * w), mn, vn


def _rope_tables(dm):
    n = dm.N
    rows = n // GRID_W
    row = jnp.repeat(jnp.arange(rows, dtype=F32), GRID_W)
    col = jnp.tile(jnp.arange(GRID_W, dtype=F32), rows)
    half = dm.HD // 2
    inv_freq = ROPE_THETA ** (-jnp.arange(0, half, 2, dtype=F32) / half)
    ang = jnp.concatenate([row[:, None] * inv_freq, col[:, None] * inv_freq], axis=-1)
    cos, sin = jnp.cos(ang), jnp.sin(ang)
    cos = jnp.concatenate([cos, jnp.ones((dm.CTX, half), F32)], axis=0)
    sin = jnp.concatenate([sin, jnp.zeros((dm.CTX, half), F32)], axis=0)
    reps = LANES // dm.HD
    return jnp.tile(jnp.concatenate([cos, cos], axis=1), (1, reps)), jnp.tile(jnp.concatenate([-sin, sin], axis=1), (1, reps))


def _lane_vec(g, hd):
    return jnp.tile(g.reshape(1, hd), (1, LANES // hd))


def _layer_fwd(dm, xc, cv, wl, tabs):
    mod = _mod_fwd(cv, wl["w_mod"], wl["b_mod"])
    h = _prenorm(dm, xc, wl["g_pre"], mod)
    p = _matmul(h, wl["w_in"], "nn", "in_proj", BF16)
    qh, qt, kh, kt, vh, vx = _qk_prep(dm, p, tabs["cos"], tabs["sins"], wl["gq"], wl["gk"], tabs["bd"])
    o, lse = _flash_fwd(dm, qh, kh, vx)
    za = _attn_gate(dm, o, p)
    zc, y1 = _conv_fwd(dm, p, wl["conv_w"], wl["conv_b"], wl["ln_g"], wl["ln_b"])
    yc = _matmul(zc, wl["w_conv_out"], "nn", "conv_out", BF16)
    ya = _matmul(za, wl["w_attn_out"], "nn", "attn_out", BF16)
    merged = _merge(dm, yc, ya, p)
    outp = _matmul(merged, wl["w_out"], "nn", "out_proj", BF16)
    x_new = _post(dm, xc, outp, wl["g_post"], mod)
    saved = dict(xc=xc, mod=mod, h=h, p=p, qh=qh, qt=qt, kh=kh, kt=kt, vh=vh, o=o, lse=lse, za=za, zc=zc, y1=y1, yc=yc, ya=ya,
                 merged=merged, outp=outp)
    return x_new, saved


def _layer_bwd(dm, dxn, cv, wl, tabs, s):
    p = s["p"]
    d_outp, dg_post, dgt = _post_bwd(dm, s["outp"], wl["g_post"], s["mod"], dxn)
    dmerged = _matmul(d_outp, wl["w_out"], "nt", "out_proj_dx", BF16)
    dw_out = _matmul(s["merged"], d_outp, "tn", "out_proj_dw")
    dyc, dya, dgm = _merge_bwd(dm, dmerged, s["yc"], s["ya"], p)
    dzc = _matmul(dyc, wl["w_conv_out"], "nt", "conv_out_dx", BF16)
    dw_conv_out = _matmul(s["zc"], dyc, "tn", "conv_out_dw")
    dza = _matmul(dya, wl["w_attn_out"], "nt", "attn_out_dx", BF16)
    dw_attn_out = _matmul(s["za"], dya, "tn", "attn_out_dw")
    doh, dot, delta, dgate_b = _attn_gate_bwd(dm, dza, s["o"], p)
    dq, dkt, dvt = _flash_bwd(dm, s["qh"], s["qt"], s["kh"], s["kt"], s["vh"], doh, dot, s["lse"], delta)
    dpq, dpkv, dgq, dgk = _qk_bwd(dm, p, dq, dkt, dvt, tabs["cos"], tabs["sins"], wl["gq"], wl["gk"], tabs["bd"])
    dy1, dgate_a, dln_g, dln_b = _conv_act_bwd(dm, dzc, s["y1"], p, wl["ln_g"], wl["ln_b"])
    dpag, dconv_w, dconv_b = _conv_bwd(dm, dy1, p, wl["conv_w"])
    dp = jnp.concatenate([dpag, dgate_a, dpq, dpkv, dgate_b, dgm], axis=1)
    dh = _matmul(dp, wl["w_in"], "nt", "in_proj_dx", BF16)
    dw_in = _matmul(s["h"], dp, "tn", "in_proj_dw")
    dxc, dg_pre, dshsc = _prenorm_bwd(dm, s["xc"], wl["g_pre"], s["mod"], dh, dxn)
    dmod = jnp.concatenate([dshsc, dgt], axis=1)
    dw_mod, db_mod, dcv = _mod_bwd(cv, wl["w_mod"], dmod)
    grads = dict(w_mod=dw_mod, b_mod=db_mod[0], g_pre=dg_pre[0], g_post=dg_post[0], w_in=dw_in,
                 conv_w=dconv_w[:CONV_KERNEL], conv_b=dconv_b[0], ln_g=dln_g[0], ln_b=dln_b[0],
                 w_conv_out=dw_conv_out, q_norm_g=dgq[0, :dm.HD], k_norm_g=dgk[0, :dm.HD],
                 w_attn_out=dw_attn_out, w_out=dw_out, c_ctx=dcv[1])
    return dxc, grads


def _local_step(dm, x, c, ctx, c_ctx, target, wf):
    depth = wf["w_in"].shape[0]
    cos, sins = _rope_tables(dm)
    lane = np.arange(LANES)
    bd = jnp.asarray((lane[:, None] // dm.HD == lane[None, :] // dm.HD), dtype=BF16)
    tabs = dict(cos=cos, sins=sins, bd=bd)
    cv = jnp.concatenate([c.reshape(1, dm.D), c_ctx.reshape(1, dm.D), jnp.zeros((6, dm.D), F32)], axis=0)
    xc = jnp.concatenate([x, ctx], axis=0)
    layers, saved = [], []
    for l in range(depth):
        wl = dict(
            w_mod=wf["w_mod"][l], b_mod=wf["b_mod"][l].reshape(1, -1), g_pre=wf["g_pre"][l].reshape(1, -1),
            g_post=wf["g_post"][l].reshape(1, -1), w_in=wf["w_in"][l],
            conv_w=jnp.concatenate([wf["conv_w"][l], jnp.zeros((1, dm.C), F32)], axis=0),
            conv_b=wf["conv_b"][l].reshape(1, -1), ln_g=wf["ln_g"][l].reshape(1, -1), ln_b=wf["ln_b"][l].reshape(1, -1),
            w_conv_out=wf["w_conv_out"][l], gq=_lane_vec(wf["q_norm_g"][l], dm.HD), gk=_lane_vec(wf["k_norm_g"][l], dm.HD),
            w_attn_out=wf["w_attn_out"][l], w_out=wf["w_out"][l])
        layers.append(wl)
        xc, s = _layer_fwd(dm, xc, cv, wl, tabs)
        saved.append(s)
    dxc, loss = _loss_head(dm, xc, target)
    grads = [None] * depth
    for l in reversed(range(depth)):
        dxc, grads[l] = _layer_bwd(dm, dxc, cv, layers[l], tabs, saved[l])
    big = ("w_mod", "w_in", "w_conv_out", "w_attn_out", "w_out")
    gw = {k: [g[k] for g in grads] if k in big else jnp.stack([g[k] for g in grads]) for k in grads[0] if k != "c_ctx"}
    gw["c_ctx"] = grads[0]["c_ctx"] + grads[1]["c_ctx"] if depth == 2 else sum(g["c_ctx"] for g in grads)
    return loss[0, 0], dxc[:dm.N], gw


def _dims(x, ctx, w_in_full_cols, conv_c, attn_w, hd):
    n, d = x.shape
    ctx_len = ctx.shape[0]
    c, a = conv_c, attn_w
    kvw = (w_in_full_cols - 3 * c - 2 * a - 2 * d) // 2
    r = n + ctx_len
    tr = _pick(np.gcd(n, ctx_len), (256, 128))
    return Dims(N=n, CTX=ctx_len, R=r, D=d, C=c, A=a, KVW=kvw, HD=hd, NH=a // hd, NKV=kvw // hd,
                G=a // kvw, W=w_in_full_cols, TR=tr)


_HBM = pl.BlockSpec(memory_space=pltpu.HBM)
_VMEM = pl.BlockSpec(memory_space=pltpu.VMEM)


def _place():
    x, y, c = lax.axis_index("x"), lax.axis_index("y"), lax.axis_index("c")
    chips = [(1 - x, y), (x, 1 - y), (1 - x, 1 - y)]
    return x, y, c, chips


def _row_pieces(rows, want):
    n = max(k for k in range(1, want + 1) if rows % (16 * k) == 0)
    return [pl.ds(i * (rows // n), rows // n) for i in range(n)]


def _allgather_chips(own, name):
    _, h, w = own.shape
    pieces = _row_pieces(h, 4)
    npc = len(pieces)

    def body(own_ref, out_ref, send_sems, recv_sems):
        x, y, c, chips = _place()
        s_me = 2 * x + y
        sibling = (x, y, 1 - c)

        def copy(k, src, dst, to):
            return pltpu.make_async_remote_copy(src_ref=src, dst_ref=dst, send_sem=send_sems.at[k], recv_sem=recv_sems.at[k],
                                                device_id=to, device_id_type=MESH)

        sends = []
        for i, rows in enumerate(pieces):
            for j, chip in enumerate(chips):
                sends.append(copy(i * 3 + j, own_ref.at[c, rows], out_ref.at[s_me, c, rows], (*chip, c)))
        own_slots = [(6 * npc + hf * npc + i, hf, rows) for hf in range(2) for i, rows in enumerate(pieces)]
        for k, hf, rows in own_slots:
            sends.append(copy(k, own_ref.at[hf, rows], out_ref.at[s_me, hf, rows], sibling))
        for cp in sends:
            cp.start()
        for i, rows in enumerate(pieces):
            for j, (cx, cy) in enumerate(chips):
                landed = out_ref.at[2 * cx + cy, c, rows]
                copy(i * 3 + j, landed, landed, (cx, cy, c)).wait_recv()
                fwd = copy(3 * npc + i * 3 + j, landed, landed, sibling)
                fwd.start()
                sends.append(fwd)
        for i, rows in enumerate(pieces):
            for j, (cx, cy) in enumerate(chips):
                theirs = out_ref.at[2 * cx + cy, 1 - c, rows]
                copy(3 * npc + i * 3 + j, theirs, theirs, sibling).wait_recv()
        for k, hf, rows in own_slots:
            mine = out_ref.at[s_me, hf, rows]
            copy(k, mine, mine, sibling).wait_recv()
        for cp in sends:
            cp.wait_send()

    return pl.pallas_call(
        body, name=name, in_specs=[_HBM], out_specs=_HBM,
        out_shape=jax.ShapeDtypeStruct((N_CHIPS, 2, h, w), own.dtype),
        scratch_shapes=[pltpu.SemaphoreType.DMA((8 * npc,)), pltpu.SemaphoreType.DMA((8 * npc,))],
    )(own)


def _pair_exchange(buf, name):
    _, m, w = buf.shape
    pieces = _row_pieces(m, 16)

    def body(buf_ref, out_ref, send_sems, recv_sems):
        x, y, c, _ = _place()
        cps = [pltpu.make_async_remote_copy(src_ref=buf_ref.at[1 - c, rows], dst_ref=out_ref.at[rows], send_sem=send_sems.at[i],
                                            recv_sem=recv_sems.at[i], device_id=(x, y, 1 - c), device_id_type=MESH)
               for i, rows in enumerate(pieces)]
        for cp in cps:
            cp.start()
        for cp in cps:
            cp.wait()

    return pl.pallas_call(
        body, name=name, in_specs=[_HBM], out_specs=_HBM, out_shape=jax.ShapeDtypeStruct((m, w), buf.dtype),
        scratch_shapes=[pltpu.SemaphoreType.DMA((len(pieces),)), pltpu.SemaphoreType.DMA((len(pieces),))],
    )(buf)


def _pair_swap(t, name):
    m, w = t.shape
    pieces = _row_pieces(m, 8)
    n = len(pieces)

    def body(t_ref, out_ref, send_sems, recv_sems):
        x, y, c, _ = _place()
        cps = [pltpu.make_async_remote_copy(src_ref=t_ref.at[rows], dst_ref=out_ref.at[rows], send_sem=send_sems.at[i],
                                            recv_sem=recv_sems.at[i], device_id=(x, y, 1 - c), device_id_type=MESH)
               for i, rows in enumerate(pieces)]
        for cp in cps:
            cp.start()
        for cp in cps:
            cp.wait()

    return pl.pallas_call(
        body, name=name, in_specs=[_HBM], out_specs=_HBM, out_shape=jax.ShapeDtypeStruct((m, w), t.dtype),
        scratch_shapes=[pltpu.SemaphoreType.DMA((n,)), pltpu.SemaphoreType.DMA((n,))],
    )(t)


def _scatter_chips(s, name):
    _, h, w = s.shape

    def body(s_ref, out_ref, send_sems, recv_sems):
        _, _, c, chips = _place()
        cps = [pltpu.make_async_remote_copy(src_ref=s_ref.at[2 * cx + cy], dst_ref=out_ref.at[j], send_sem=send_sems.at[j],
                                            recv_sem=recv_sems.at[j], device_id=(cx, cy, c), device_id_type=MESH)
               for j, (cx, cy) in enumerate(chips)]
        for cp in cps:
            cp.start()
        for cp in cps:
            cp.wait()

    return pl.pallas_call(
        body, name=name, in_specs=[_HBM], out_specs=_HBM, out_shape=jax.ShapeDtypeStruct((3, h, w), s.dtype),
        scratch_shapes=[pltpu.SemaphoreType.DMA((3,)), pltpu.SemaphoreType.DMA((3,))],
    )(s)


def _sum_pair(buf, recv, core, name):
    m, w = recv.shape
    tr = _pick(m, (512, 432, 256, 128, 64, 32, 16))

    def body(c_ref, b_ref, r_ref, o_ref, ob_ref):
        v = b_ref[0] + r_ref[...]
        o_ref[...] = v
        ob_ref[...] = v.astype(BF16)

    spec = pl.BlockSpec((tr, w), lambda i, c: (i, 0))
    return pl.pallas_call(
        body, name=name,
        grid_spec=pltpu.PrefetchScalarGridSpec(
            num_scalar_prefetch=1, grid=(m // tr,),
            in_specs=[pl.BlockSpec((1, tr, w), lambda i, c: (c[0], i, 0)), spec], out_specs=(spec, spec)),
        out_shape=(jax.ShapeDtypeStruct((m, w), F32), jax.ShapeDtypeStruct((m, w), BF16)),
        compiler_params=_params(("parallel",)),
    )(core, buf, recv)


def _sum_chips(own, recv, chip, name):
    _, h, w = own.shape
    tr = _pick(h, (512, 432, 256, 128, 64, 32, 16))

    def body(s_ref, a_ref, r_ref, o_ref):
        acc = a_ref[0]
        for j in range(3):
            acc = acc + r_ref[j].astype(F32)
        o_ref[...] = acc

    return pl.pallas_call(
        body, name=name,
        grid_spec=pltpu.PrefetchScalarGridSpec(
            num_scalar_prefetch=1, grid=(h // tr,),
            in_specs=[pl.BlockSpec((1, tr, w), lambda i, s: (s[0], i, 0)), pl.BlockSpec((3, tr, w), lambda i, s: (0, i, 0))],
            out_specs=pl.BlockSpec((tr, w), lambda i, s: (i, 0))),
        out_shape=jax.ShapeDtypeStruct((h, w), F32), compiler_params=_params(("parallel",)),
    )(chip, own, recv)


def _allreduce_small(own, name):
    m, w = own.shape

    def body(own_ref, sum_ref, all_ref, send_sems, recv_sems):
        x, y, c, _ = _place()
        me = 4 * x + 2 * y + c
        all_ref[me] = own_ref[...]
        sends = []
        for k in range(1, N_DEV):
            peer = (x ^ ((k >> 2) & 1), y ^ ((k >> 1) & 1), c ^ (k & 1))
            cp = pltpu.make_async_remote_copy(src_ref=own_ref, dst_ref=all_ref.at[me], send_sem=send_sems.at[k - 1],
                                              recv_sem=recv_sems.at[k - 1], device_id=peer, device_id_type=MESH)
            cp.start()
            sends.append(cp)
        for k in range(1, N_DEV):
            px, py, pc = x ^ ((k >> 2) & 1), y ^ ((k >> 1) & 1), c ^ (k & 1)
            slot = all_ref.at[4 * px + 2 * py + pc]
            pltpu.make_async_remote_copy(src_ref=slot, dst_ref=slot, send_sem=send_sems.at[k - 1], recv_sem=recv_sems.at[k - 1],
                                         device_id=(px, py, pc), device_id_type=MESH).wait_recv()
        for cp in sends:
            cp.wait_send()
        acc = all_ref[0]
        for d in range(1, N_DEV):
            acc = acc + all_ref[d]
        sum_ref[...] = acc

    return pl.pallas_call(
        body, name=name, in_specs=[_VMEM], out_specs=_VMEM, out_shape=jax.ShapeDtypeStruct((m, w), F32),
        scratch_shapes=[pltpu.VMEM((N_DEV, m, w), F32), pltpu.SemaphoreType.DMA((N_DEV - 1,)), pltpu.SemaphoreType.DMA((N_DEV - 1,))],
        compiler_params=pltpu.CompilerParams(vmem_limit_bytes=VMEM_LIMIT),
    )(own)


_COL_SHARDED = ("w_mod", "w_in")
_ROW_SHARDED = ("w_conv_out", "w_attn_out", "w_out")
_BIG = _COL_SHARDED + _ROW_SHARDED
_SMALL = ("c_ctx", "b_mod", "g_pre", "g_post", "conv_b", "ln_g", "ln_b", "q_norm_g", "k_norm_g")
FLAT_W = 1024


def _unflatten_shard(flat, like):
    out, off = {}, 0
    flat = flat.reshape(-1)
    for k in _BIG:
        n = int(np.prod(like[k].shape))
        out[k] = flat[off:off + n].reshape(like[k].shape)
        off += n
    return out


def _by_shard(k, layers):
    parts = []
    for full in layers:
        if k in _COL_SHARDED:
            d, cols = full.shape
            parts.append(full.reshape(d, N_CHIPS, cols // N_CHIPS).transpose(1, 0, 2).reshape(N_CHIPS, -1))
        else:
            parts.append(full.reshape(N_CHIPS, -1))
    return jnp.concatenate(parts, axis=1)


def _from_shards(k, flat4, like):
    l = like.shape[0]
    if k in _COL_SHARDED:
        d, cs = like.shape[1:]
        return flat4.reshape(N_CHIPS, l, d, cs).transpose(1, 2, 0, 3).reshape(l, d, N_CHIPS * cs)
    rs, d = like.shape[1:]
    return flat4.reshape(N_CHIPS, l, rs, d).transpose(1, 0, 2, 3).reshape(l, N_CHIPS * rs, d)


def _pad_rows(flat, mult=8):
    n = flat.shape[0]
    rows = -(-n // LANES)
    rows = -(-rows // mult) * mult
    return jnp.concatenate([flat, jnp.zeros((rows * LANES - n,), F32)]).reshape(rows, LANES)


def kernel(x, c, ctx, c_ctx, w_mod, b_mod, g_pre, g_post, w_in, conv_w, conv_b, ln_g, ln_b, w_conv_out, q_norm_g, k_norm_g, w_attn_out, w_out, loss_target, m_c_ctx, m_w_mod, m_b_mod, m_g_pre, m_g_post, m_w_in, m_conv_w, m_conv_b, m_ln_g, m_ln_b, m_w_conv_out, m_q_norm_g, m_k_norm_g, m_w_attn_out, m_w_out, v_c_ctx, v_w_mod, v_b_mod, v_g_pre, v_g_post, v_w_in, v_conv_w, v_conv_b, v_ln_g, v_ln_b, v_w_conv_out, v_q_norm_g, v_k_norm_g, v_w_attn_out, v_w_out):
    w = dict(c_ctx=c_ctx, w_mod=w_mod, b_mod=b_mod, g_pre=g_pre, g_post=g_post, w_in=w_in, conv_w=conv_w, conv_b=conv_b,
             ln_g=ln_g, ln_b=ln_b, w_conv_out=w_conv_out, q_norm_g=q_norm_g, k_norm_g=k_norm_g, w_attn_out=w_attn_out, w_out=w_out)
    m = dict(c_ctx=m_c_ctx, w_mod=m_w_mod, b_mod=m_b_mod, g_pre=m_g_pre, g_post=m_g_post, w_in=m_w_in, conv_w=m_conv_w,
             conv_b=m_conv_b, ln_g=m_ln_g, ln_b=m_ln_b, w_conv_out=m_w_conv_out, q_norm_g=m_q_norm_g, k_norm_g=m_k_norm_g,
             w_attn_out=m_w_attn_out, w_out=m_w_out)
    v = dict(c_ctx=v_c_ctx, w_mod=v_w_mod, b_mod=v_b_mod, g_pre=v_g_pre, g_post=v_g_post, w_in=v_w_in, conv_w=v_conv_w,
             conv_b=v_conv_b, ln_g=v_ln_g, ln_b=v_ln_b, w_conv_out=v_w_conv_out, q_norm_g=v_q_norm_g, k_norm_g=v_k_norm_g,
             w_attn_out=v_w_attn_out, w_out=v_w_out)
    order = ("c_ctx", "w_mod", "b_mod", "g_pre", "g_post", "w_in", "conv_w", "conv_b", "ln_g", "ln_b", "w_conv_out",
             "q_norm_g", "k_norm_g", "w_attn_out", "w_out")
    my_chip = 2 * lax.axis_index("x") + lax.axis_index("y")
    my_core = lax.axis_index("c")
    depth, _, conv_cs = conv_w.shape
    conv_c = conv_cs * N_CHIPS
    dm = _dims(x[0], ctx[0], w_in.shape[2] * N_CHIPS, conv_c, w_attn_out.shape[1] * N_CHIPS, q_norm_g.shape[1])

    own = jnp.concatenate([w[k].astype(BF16).reshape(-1) for k in _BIG])
    rows2 = own.shape[0] // FLAT_W
    own = own.reshape(2, rows2 // 2, FLAT_W)
    gathered = _allgather_chips(own, "gather_weights").reshape(N_CHIPS, -1)
    wf, off = {}, 0
    for k in _BIG:
        n = int(np.prod(w[k].shape))
        wf[k] = _from_shards(k, gathered[:, off:off + n], w[k])
        off += n
    cw = jnp.zeros((depth, CONV_KERNEL, N_CHIPS, conv_cs), F32)
    cw = lax.dynamic_update_slice(cw, conv_w[:, :, None, :], (0, 0, my_chip, 0))
    cw = jnp.where(my_core == 0, cw, 0.0).reshape(-1)
    wf["conv_w"] = _allreduce_small(_pad_rows(cw), "gather_conv_w").reshape(-1)[:cw.shape[0]].reshape(depth, CONV_KERNEL, conv_c)
    for k in _SMALL:
        wf[k] = w[k]

    loss_part, grad_x, gw = _local_step(dm, x[0], c, ctx[0], c_ctx, loss_target[0], wf)

    g4 = jnp.concatenate([_by_shard(k, gw[k]) for k in _BIG], axis=1)
    hrows = rows2 // 2
    g24 = g4.reshape(N_CHIPS, 2, hrows, FLAT_W).transpose(1, 0, 2, 3).reshape(2, N_CHIPS * hrows, FLAT_W)
    core_i = my_core.astype(jnp.int32).reshape(1)
    chip_i = my_chip.astype(jnp.int32).reshape(1)
    theirs = _pair_exchange(g24, "reduce_pair")
    chip_sum, chip_sum_bf = _sum_pair(g24, theirs, core_i, "reduce_pair_add")
    others = _scatter_chips(chip_sum_bf.reshape(N_CHIPS, hrows, FLAT_W), "reduce_chips")
    mine = _sum_chips(chip_sum.reshape(N_CHIPS, hrows, FLAT_W), others, chip_i, "reduce_chips_add")
    sibling_half = _pair_swap(mine, "reduce_share")

    small = jnp.concatenate([gw[k].reshape(-1) for k in _SMALL] + [gw["conv_w"].reshape(-1), loss_part.reshape(1)])
    small_sum = _allreduce_small(_pad_rows(small), "reduce_small").reshape(-1)
    loss = small_sum[small.shape[0] - 1]
    gs, off = {}, 0
    for k in _SMALL:
        n = int(np.prod(w[k].shape))
        gs[k] = small_sum[off:off + n].reshape(w[k].shape)
        off += n
    gcw = small_sum[off:off + depth * CONV_KERNEL * conv_c].reshape(depth, CONV_KERNEL, N_CHIPS, conv_cs)
    gs["conv_w"] = lax.dynamic_index_in_dim(gcw, my_chip, axis=2, keepdims=False)

    g_flat = jnp.where(my_core == 0, jnp.concatenate([mine, sibling_half]), jnp.concatenate([sibling_half, mine]))
    grads = _unflatten_shard(g_flat, w)
    delta, new_m, new_v = {}, {}, {}
    for k in _BIG:
        as2d = lambda t: t.reshape(-1, t.shape[-1])
        d_k, m_k, v_k = _adamw(as2d(w[k]), as2d(grads[k]), as2d(m[k]), as2d(v[k]), "adamw_" + k)
        delta[k], new_m[k], new_v[k] = d_k.reshape(w[k].shape), m_k.reshape(w[k].shape), v_k.reshape(w[k].shape)
    small_keys = _SMALL + ("conv_w",)
    join = lambda t: _pad_rows(jnp.concatenate([t[k].reshape(-1) for k in small_keys]))
    d_s, m_s, v_s = _adamw(join(w), join(gs), join(m), join(v), "adamw_small")
    off = 0
    for k in small_keys:
        n = int(np.prod(w[k].shape))
        grads[k] = gs[k]
        for dst, src in ((delta, d_s), (new_m, m_s), (new_v, v_s)):
            dst[k] = src.reshape(-1)[off:off + n].reshape(w[k].shape)
        off += n
    return (loss, grad_x[None], *[grads[k] for k in order], *[delta[k] for k in order],
            *[new_m[k] for k in order], *[new_v[k] for k in order])
```

```python
from typing import NamedTuple

import jax
import jax.numpy as jnp
import numpy as np
from jax import lax
from jax.experimental import pallas as pl
from jax.experimental.pallas import tpu as pltpu

F32 = jnp.float32
BF16 = jnp.bfloat16
EPS = 1e-6
GRID_W = 64
ROPE_THETA = 10000.0
CONV_KERNEL = 31
CONV_PAD = CONV_KERNEL // 2
HALO = 16
LANES = 128
SUBLANES = 8
ADAM_LR, ADAM_B1, ADAM_B2, ADAM_EPS, ADAM_WD, ADAM_STEP = 0.001, 0.9, 0.999, 1e-08, 0.01, 10
VMEM_LIMIT = 56 * 1024 * 1024
ADAM_BLOCK_BYTES = 1024 * 1024
LOG2E = 1.4426950408889634
LN2 = 0.6931471805599453
MESH = pl.DeviceIdType.MESH
N_CHIPS = 4
N_DEV = 8


class Dims(NamedTuple):
    N: int
    CTX: int
    R: int
    D: int
    C: int
    A: int
    KVW: int
    HD: int
    NH: int
    NKV: int
    G: int
    W: int
    TR: int


def _pick(n, prefs):
    for p in prefs:
        if n % p == 0:
            return p
    raise ValueError(f"no tile for {n} in {prefs}")


def _params(sem):
    return pltpu.CompilerParams(dimension_semantics=sem, vmem_limit_bytes=VMEM_LIMIT)


def _rows(tr, w, col=0):
    return pl.BlockSpec((tr, w), lambda i, col=col: (i, col))


def _whole(shape):
    return pl.BlockSpec(shape, lambda *_: (0,) * len(shape))


def _col(off, w):
    assert off % w == 0, (off, w)
    return off // w


_DN = {"nn": (((1,), (0,)), ((), ())), "nt": (((1,), (1,)), ((), ())), "tn": (((0,), (0,)), ((), ()))}


def _matmul(a, b, mode, name, out_dtype=F32):
    if mode == "nn":
        (m, k), n = a.shape, b.shape[1]
    elif mode == "nt":
        (m, k), n = a.shape, b.shape[0]
    else:
        (k, m), n = a.shape, b.shape[1]
    tm = _pick(m, (1408, 1024, 768, 512, 256, 128, 8))
    tn = _pick(n, (1280, 1024, 768, 512, 256, 128))
    tk = _pick(k, (1536, 1408, 1280, 1024, 768, 512, 256, 128))
    nk = k // tk
    dn = _DN[mode]

    own_acc = out_dtype != F32 and nk > 1

    def body(a_ref, b_ref, o_ref, *scratch):
        part = lax.dot_general(a_ref[...], b_ref[...], dn, preferred_element_type=F32)
        if nk == 1:
            o_ref[...] = part.astype(out_dtype)
            return
        acc_ref = scratch[0] if own_acc else o_ref
        kk = pl.program_id(2)

        @pl.when(kk == 0)
        def _():
            acc_ref[...] = part

        @pl.when(kk > 0)
        def _():
            acc_ref[...] += part

        if own_acc:
            @pl.when(kk == nk - 1)
            def _():
                o_ref[...] = acc_ref[...].astype(out_dtype)

    a_spec = pl.BlockSpec((tk, tm), lambda i, j, kk: (kk, i)) if mode == "tn" else pl.BlockSpec((tm, tk), lambda i, j, kk: (i, kk))
    b_spec = pl.BlockSpec((tn, tk), lambda i, j, kk: (j, kk)) if mode == "nt" else pl.BlockSpec((tk, tn), lambda i, j, kk: (kk, j))
    return pl.pallas_call(
        body, name=name, grid=(m // tm, n // tn, nk),
        in_specs=[a_spec, b_spec], out_specs=pl.BlockSpec((tm, tn), lambda i, j, kk: (i, j)),
        out_shape=jax.ShapeDtypeStruct((m, n), out_dtype),
        scratch_shapes=[pltpu.VMEM((tm, tn), F32)] if own_acc else [],
        compiler_params=_params(("parallel", "parallel", "arbitrary")),
    )(a, b)


def _rms(x, g):
    return x * lax.rsqrt(jnp.mean(x * x, axis=-1, keepdims=True) + EPS) * g


def _prenorm_fn(x, g, sc, sh):
    return _rms(x, g) * (1.0 + sc) + sh


def _post_fn(outp, g, gt):
    return gt * _rms(outp, g)


def _convact_fn(y1, gate, ln_g, ln_b):
    mu = jnp.mean(y1, axis=-1, keepdims=True)
    var = jnp.mean(jnp.square(y1 - mu), axis=-1, keepdims=True)
    y = (y1 - mu) * lax.rsqrt(var + EPS) * ln_g + ln_b
    return jax.nn.silu(y) * jax.nn.silu(gate)


def _dsilu(x):
    s = jax.nn.sigmoid(x)
    return s * (1.0 + x * (1.0 - s))


def _seg_select(is_ctx, mod_ref, lo, w):
    return jnp.where(is_ctx, mod_ref[1:2, lo:lo + w], mod_ref[0:1, lo:lo + w])


def _seg_rows(is_ctx, v):
    row = lax.broadcasted_iota(jnp.int32, (8, v.shape[1]), 0)
    return jnp.where(row == is_ctx.astype(jnp.int32), v, 0.0)


def _mod_fwd(cv, w_mod_l, b_mod_l):
    def body(cv_ref, w_ref, b_ref, o_ref):
        s = jax.nn.silu(cv_ref[...]).astype(BF16)
        o_ref[...] = jnp.dot(s, w_ref[...], preferred_element_type=F32) + b_ref[...]

    return pl.pallas_call(
        body, name="mod_fwd", out_shape=jax.ShapeDtypeStruct((8, w_mod_l.shape[1]), F32),
        compiler_params=pltpu.CompilerParams(vmem_limit_bytes=VMEM_LIMIT),
    )(cv, w_mod_l, b_mod_l)


def _mod_bwd(cv, w_mod_l, dmod):
    d = cv.shape[1]

    def body(cv_ref, w_ref, dm_ref, dw_ref, db_ref, dcv_ref):
        cvv = cv_ref[...]
        s = jax.nn.silu(cvv).astype(BF16)
        dm = dm_ref[...]
        dmb = dm.astype(BF16)
        dw_ref[...] = lax.dot_general(s, dmb, _DN["tn"], preferred_element_type=F32)
        db_ref[...] = jnp.sum(dm, axis=0, keepdims=True)
        ds = lax.dot_general(dmb, w_ref[...], _DN["nt"], preferred_element_type=F32)
        dcv_ref[...] = ds * _dsilu(cvv)

    return pl.pallas_call(
        body, name="mod_bwd",
        out_shape=(jax.ShapeDtypeStruct(w_mod_l.shape, F32), jax.ShapeDtypeStruct((1, w_mod_l.shape[1]), F32),
                   jax.ShapeDtypeStruct((8, d), F32)),
        compiler_params=pltpu.CompilerParams(vmem_limit_bytes=VMEM_LIMIT),
    )(cv, w_mod_l, dmod)


def _prenorm(dm, xc, g_pre_l, mod):
    tr, d, n = dm.TR, dm.D, dm.N

    def body(x_ref, g_ref, mod_ref, h_ref):
        is_ctx = pl.program_id(0) * tr >= n
        sh = _seg_select(is_ctx, mod_ref, 0, d)
        sc = _seg_select(is_ctx, mod_ref, d, d)
        h_ref[...] = _prenorm_fn(x_ref[...], g_ref[...], sc, sh).astype(BF16)

    return pl.pallas_call(
        body, name="prenorm", grid=(dm.R // tr,),
        in_specs=[_rows(tr, d), _whole((1, d)), _whole((8, 3 * d))], out_specs=_rows(tr, d),
        out_shape=jax.ShapeDtypeStruct((dm.R, d), BF16), compiler_params=_params(("parallel",)),
    )(xc, g_pre_l, mod)


def _prenorm_bwd(dm, xc, g_pre_l, mod, dh, dxn):
    tr, d, n = dm.TR, dm.D, dm.N

    def body(x_ref, g_ref, mod_ref, dh_ref, dxn_ref, dx_ref, dg_ref, dss_ref):
        i = pl.program_id(0)
        is_ctx = i * tr >= n
        sh = _seg_select(is_ctx, mod_ref, 0, d)
        sc = _seg_select(is_ctx, mod_ref, d, d)
        _, vjp = jax.vjp(_prenorm_fn, x_ref[...], g_ref[...], sc, sh)
        dx, dg, dsc, dsh = vjp(dh_ref[...].astype(F32))
        dx_ref[...] = dxn_ref[...] + dx

        @pl.when(i == 0)
        def _():
            dg_ref[...] = jnp.zeros_like(dg_ref)
            dss_ref[...] = jnp.zeros_like(dss_ref)

        dg_ref[...] += dg
        dss_ref[...] += _seg_rows(is_ctx, jnp.concatenate([dsh, dsc], axis=1))

    return pl.pallas_call(
        body, name="prenorm_bwd", grid=(dm.R // tr,),
        in_specs=[_rows(tr, d), _whole((1, d)), _whole((8, 3 * d)), _rows(tr, d), _rows(tr, d)],
        out_specs=(_rows(tr, d), _whole((1, d)), _whole((8, 2 * d))),
        out_shape=(jax.ShapeDtypeStruct((dm.R, d), F32), jax.ShapeDtypeStruct((1, d), F32),
                   jax.ShapeDtypeStruct((8, 2 * d), F32)),
        compiler_params=_params(("arbitrary",)),
    )(xc, g_pre_l, mod, dh, dxn)


def _post(dm, xc, outp, g_post_l, mod):
    tr, d, n = dm.TR, dm.D, dm.N

    def body(x_ref, o_ref, g_ref, mod_ref, y_ref):
        is_ctx = pl.program_id(0) * tr >= n
        gt = _seg_select(is_ctx, mod_ref, 2 * d, d)
        y_ref[...] = x_ref[...] + _post_fn(o_ref[...].astype(F32), g_ref[...], gt)

    return pl.pallas_call(
        body, name="post", grid=(dm.R // tr,),
        in_specs=[_rows(tr, d), _rows(tr, d), _whole((1, d)), _whole((8, 3 * d))], out_specs=_rows(tr, d),
        out_shape=jax.ShapeDtypeStruct((dm.R, d), F32), compiler_params=_params(("parallel",)),
    )(xc, outp, g_post_l, mod)


def _post_bwd(dm, outp, g_post_l, mod, dxn):
    tr, d, n = dm.TR, dm.D, dm.N

    def body(o_ref, g_ref, mod_ref, dxn_ref, do_ref, dg_ref, dgt_ref):
        i = pl.program_id(0)
        is_ctx = i * tr >= n
        gt = _seg_select(is_ctx, mod_ref, 2 * d, d)
        _, vjp = jax.vjp(_post_fn, o_ref[...].astype(F32), g_ref[...], gt)
        do, dg, dgt = vjp(dxn_ref[...])
        do_ref[...] = do.astype(BF16)

        @pl.when(i == 0)
        def _():
            dg_ref[...] = jnp.zeros_like(dg_ref)
            dgt_ref[...] = jnp.zeros_like(dgt_ref)

        dg_ref[...] += dg
        dgt_ref[...] += _seg_rows(is_ctx, dgt)

    return pl.pallas_call(
        body, name="post_bwd", grid=(dm.R // tr,),
        in_specs=[_rows(tr, d), _whole((1, d)), _whole((8, 3 * d)), _rows(tr, d)],
        out_specs=(_rows(tr, d), _whole((1, d)), _whole((8, d))),
        out_shape=(jax.ShapeDtypeStruct((dm.R, d), BF16), jax.ShapeDtypeStruct((1, d), F32),
                   jax.ShapeDtypeStruct((8, d), F32)),
        compiler_params=_params(("arbitrary",)),
    )(outp, g_post_l, mod, dxn)


def _group_sum(x, bd):
    hi = x.astype(BF16)
    lo = (x - hi.astype(F32)).astype(BF16)
    return jnp.dot(hi, bd, preferred_element_type=F32) + jnp.dot(lo, bd, preferred_element_type=F32)


def _swap_halves(x):
    lane = lax.broadcasted_iota(jnp.int32, x.shape, 1)
    return jnp.where((lane & 32) == 0, pltpu.roll(x, LANES - 32, 1), pltpu.roll(x, 32, 1))


def _qk_prep(dm, p, cos, sins, gq, gk, bd):
    tr, hd = dm.TR, dm.HD
    oq, ok, ov = 3 * dm.C, 3 * dm.C + dm.A, 3 * dm.C + dm.A + dm.KVW
    scale = hd ** -0.5 * LOG2E

    def body(q_ref, k_ref, v_ref, cos_ref, sin_ref, gq_ref, gk_ref, bd_ref, qh_ref, qt_ref, kh_ref, kt_ref, vh_ref, vx_ref):
        cosv, sinv, bdv = cos_ref[...], sin_ref[...], bd_ref[...]

        def chunk(x, g):
            rstd = lax.rsqrt(_group_sum(x * x, bdv) * (1.0 / hd) + EPS)
            xn = x * rstd * g
            return xn * cosv + _swap_halves(xn) * sinv

        for c in range(dm.A // LANES):
            y = chunk(q_ref[:, c * LANES:(c + 1) * LANES].astype(F32), gq_ref[...]) * scale
            qh_ref[2 * c] = y[:, :hd].astype(BF16)
            qh_ref[2 * c + 1] = y[:, hd:].astype(BF16)
            qt_ref[c * LANES:(c + 1) * LANES, :] = y.T.astype(BF16)
        for c in range(dm.KVW // LANES):
            y = chunk(k_ref[:, c * LANES:(c + 1) * LANES].astype(F32), gk_ref[...])
            kh_ref[2 * c] = y[:, :hd].astype(BF16)
            kh_ref[2 * c + 1] = y[:, hd:].astype(BF16)
            kt_ref[c * LANES:(c + 1) * LANES, :] = y.T.astype(BF16)
            v = v_ref[:, c * LANES:(c + 1) * LANES].astype(F32)
            vh_ref[2 * c] = v[:, :hd].astype(BF16)
            vh_ref[2 * c + 1] = v[:, hd:].astype(BF16)
            lane = lax.broadcasted_iota(jnp.int32, v.shape, 1)
            one_at_hd = (lane == hd).astype(F32)
            vx_ref[2 * c] = jnp.where(lane < hd, v, one_at_hd).astype(BF16)
            vx_ref[2 * c + 1] = jnp.where(lane < hd, pltpu.roll(v, hd, 1), one_at_hd).astype(BF16)

    hspec = lambda nh, w=hd: pl.BlockSpec((nh, tr, w), lambda i: (0, i, 0))
    return pl.pallas_call(
        body, name="qk_prep", grid=(dm.R // tr,),
        in_specs=[_rows(tr, dm.A, _col(oq, dm.A)), _rows(tr, dm.KVW, _col(ok, dm.KVW)), _rows(tr, dm.KVW, _col(ov, dm.KVW)),
                  _rows(tr, LANES), _rows(tr, LANES), _whole((1, LANES)), _whole((1, LANES)), _whole((LANES, LANES))],
        out_specs=(hspec(dm.NH), pl.BlockSpec((dm.A, tr), lambda i: (0, i)), hspec(dm.NKV),
                   pl.BlockSpec((dm.KVW, tr), lambda i: (0, i)), hspec(dm.NKV), hspec(dm.NKV, LANES)),
        out_shape=(jax.ShapeDtypeStruct((dm.NH, dm.R, hd), BF16), jax.ShapeDtypeStruct((dm.A, dm.R), BF16),
                   jax.ShapeDtypeStruct((dm.NKV, dm.R, hd), BF16), jax.ShapeDtypeStruct((dm.KVW, dm.R), BF16),
                   jax.ShapeDtypeStruct((dm.NKV, dm.R, hd), BF16), jax.ShapeDtypeStruct((dm.NKV, dm.R, LANES), BF16)),
        compiler_params=_params(("parallel",)),
    )(p, p, p, cos, sins, gq, gk, bd)


def _qk_bwd(dm, p, dq, dkt, dvt, cos, sins, gq, gk, bd):
    tr, hd = dm.TR, dm.HD
    oq, ok = 3 * dm.C, 3 * dm.C + dm.A
    scale = hd ** -0.5

    def body(q_ref, k_ref, dq_ref, dkh_ref, dvh_ref, cos_ref, sin_ref, gq_ref, gk_ref, bd_ref,
             dpq_ref, dpkv_ref, dgq_ref, dgk_ref):
        i = pl.program_id(0)
        cosv, sinv, bdv = cos_ref[...], sin_ref[...], bd_ref[...]

        def chunk(x, g, dy):
            rstd = lax.rsqrt(_group_sum(x * x, bdv) * (1.0 / hd) + EPS)
            xhat = x * rstd
            dxn = dy * cosv + _swap_halves(dy * sinv)
            dxhat = dxn * g
            dx = rstd * (dxhat - xhat * (_group_sum(dxhat * xhat, bdv) * (1.0 / hd)))
            return dx, jnp.sum(dxn * xhat, axis=0, keepdims=True)

        @pl.when(i == 0)
        def _():
            dgq_ref[...] = jnp.zeros_like(dgq_ref)
            dgk_ref[...] = jnp.zeros_like(dgk_ref)

        dgq = jnp.zeros((1, LANES), F32)
        for c in range(dm.A // LANES):
            sl = slice(c * LANES, (c + 1) * LANES)
            dx, dg = chunk(q_ref[:, sl].astype(F32), gq_ref[...], dq_ref[:, sl] * scale)
            dpq_ref[:, sl] = dx.astype(BF16)
            dgq = dgq + dg
        dgq_ref[...] += dgq + pltpu.roll(dgq, hd, 1)
        dgk = jnp.zeros((1, LANES), F32)
        for c in range(dm.KVW // LANES):
            sl = slice(c * LANES, (c + 1) * LANES)
            dx, dg = chunk(k_ref[:, sl].astype(F32), gk_ref[...], dkh_ref[sl, :].T)
            dpkv_ref[:, sl] = dx.astype(BF16)
            dgk = dgk + dg
            dpkv_ref[:, dm.KVW + c * LANES:dm.KVW + (c + 1) * LANES] = dvh_ref[sl, :].T.astype(BF16)
        dgk_ref[...] += dgk + pltpu.roll(dgk, hd, 1)

    hspec = pl.BlockSpec((dm.KVW, tr), lambda i: (0, i))
    return pl.pallas_call(
        body, name="qk_bwd", grid=(dm.R // tr,),
        in_specs=[_rows(tr, dm.A, _col(oq, dm.A)), _rows(tr, dm.KVW, _col(ok, dm.KVW)), _rows(tr, dm.A), hspec, hspec,
                  _rows(tr, LANES), _rows(tr, LANES), _whole((1, LANES)), _whole((1, LANES)), _whole((LANES, LANES))],
        out_specs=(_rows(tr, dm.A), _rows(tr, 2 * dm.KVW), _whole((1, LANES)), _whole((1, LANES))),
        out_shape=(jax.ShapeDtypeStruct((dm.R, dm.A), BF16), jax.ShapeDtypeStruct((dm.R, 2 * dm.KVW), BF16),
                   jax.ShapeDtypeStruct((1, LANES), F32), jax.ShapeDtypeStruct((1, LANES), F32)),
        compiler_params=_params(("arbitrary",)),
    )(p, p, dq, dkt, dvt, cos, sins, gq, gk, bd)


def _flash_fwd(dm, qh, kh, vx):
    tq, ks = dm.TR, dm.TR
    rs = min(tq, LANES)
    g, hd = dm.G, dm.HD
    nq, nq_lat = dm.R // tq, dm.N // tq
    n_chunks, ctx_chunks = dm.R // ks, dm.CTX // ks

    def body(q_ref, k_ref, v_ref, o_ref, lse_ref):
        qi = pl.program_id(1)

        def attend(chunks):
            for gg in range(g):
                for r in range(tq // rs):
                    rows = slice(r * rs, (r + 1) * rs)
                    q = q_ref[gg, rows, :]
                    m_prev = jnp.full((rs, LANES), -1e30, F32)
                    acc = jnp.zeros((rs, LANES), F32)
                    for kc in chunks:
                        keys = slice(kc * ks, (kc + 1) * ks)
                        s = lax.dot_general(q, k_ref[0, keys, :], _DN["nt"], preferred_element_type=F32)
                        m_new = jnp.maximum(m_prev, jnp.max(s, axis=-1, keepdims=True))
                        p = jnp.exp2(s - jnp.tile(m_new, (1, ks // LANES)))
                        acc = acc * jnp.exp2(m_prev - m_new) + jnp.dot(p.astype(BF16), v_ref[0, keys, :],
                                                                        preferred_element_type=F32)
                        m_prev = m_new
                    l = acc[:, hd:hd + 1]
                    o_ref[rows, gg * hd:(gg + 1) * hd] = (acc[:, :hd] / l).astype(BF16)
                    lse_ref[gg, rows, :] = m_prev + jnp.log2(l)

        @pl.when(qi < nq_lat)
        def _():
            attend(range(n_chunks))

        @pl.when(qi >= nq_lat)
        def _():
            attend(range(n_chunks - ctx_chunks, n_chunks))

    return pl.pallas_call(
        body, name="flash_fwd", grid=(dm.NKV, nq),
        in_specs=[pl.BlockSpec((g, tq, hd), lambda h, qi: (h, qi, 0)), pl.BlockSpec((1, dm.R, hd), lambda h, qi: (h, 0, 0)),
                  pl.BlockSpec((1, dm.R, LANES), lambda h, qi: (h, 0, 0))],
        out_specs=(pl.BlockSpec((tq, g * hd), lambda h, qi: (qi, h)), pl.BlockSpec((g, tq, LANES), lambda h, qi: (h, qi, 0))),
        out_shape=(jax.ShapeDtypeStruct((dm.R, dm.A), BF16), jax.ShapeDtypeStruct((dm.NH, dm.R, LANES), F32)),
        compiler_params=_params(("parallel", "parallel")),
    )(qh, kh, vx)


def _flash_bwd(dm, qh, qt, kh, kt, vh, doh, dot, lse, delta):
    tq, ks = dm.TR, dm.TR
    g, hd = dm.G, dm.HD
    nq, nq_lat = dm.R // tq, dm.N // tq
    n_chunks, ctx_chunks = dm.R // ks, dm.CTX // ks

    def body(q_ref, qt_ref, k_ref, kt_ref, v_ref, do_ref, dot_ref, lse_ref, dl_ref, dq_ref, dk_acc, dv_acc):
        qi = pl.program_id(1)

        @pl.when(qi == 0)
        def _():
            dk_acc[...] = jnp.zeros_like(dk_acc)
            dv_acc[...] = jnp.zeros_like(dv_acc)

        def attend(chunks):
            q = q_ref[...].reshape(g * tq, hd)
            do = do_ref[...].reshape(g * tq, hd)
            q_t = jnp.concatenate([qt_ref[gg * hd:(gg + 1) * hd, :] for gg in range(g)], axis=1)
            do_t = jnp.concatenate([dot_ref[gg * hd:(gg + 1) * hd, :] for gg in range(g)], axis=1)
            wide = lambda ref: jnp.tile(ref[...].reshape(g * tq, LANES), (1, ks // LANES))
            dq_t = jnp.zeros((hd, g * tq), F32)
            for kc in chunks:
                keys = slice(kc * ks, (kc + 1) * ks)
                k, v = k_ref[0, keys, :], v_ref[0, keys, :]
                s = lax.dot_general(q, k, _DN["nt"], preferred_element_type=F32)
                p = jnp.exp2(s - wide(lse_ref))
                dp = lax.dot_general(do, v, _DN["nt"], preferred_element_type=F32)
                ds = p * (dp - wide(dl_ref))
                dv_acc[:, keys] += jnp.dot(do_t, p.astype(BF16), preferred_element_type=F32)
                dk_acc[:, keys] += jnp.dot(q_t, ds.astype(BF16), preferred_element_type=F32)
                dq_t = dq_t + jnp.dot(kt_ref[:, keys], ds.T.astype(BF16), preferred_element_type=F32)
            for gg in range(g):
                dq_ref[:, gg * hd:(gg + 1) * hd] = dq_t[:, gg * tq:(gg + 1) * tq].T

        @pl.when(qi < nq_lat)
        def _():
            attend(range(n_chunks))

        @pl.when(qi >= nq_lat)
        def _():
            attend(range(n_chunks - ctx_chunks, n_chunks))

        @pl.when(qi == nq - 1)
        def _():
            dk_acc[...] = dk_acc[...] * LN2

    qspec = pl.BlockSpec((g, tq, hd), lambda h, qi: (h, qi, 0))
    tspec = pl.BlockSpec((g * hd, tq), lambda h, qi: (h, qi))
    kspec = pl.BlockSpec((1, dm.R, hd), lambda h, qi: (h, 0, 0))
    sspec = pl.BlockSpec((g, tq, LANES), lambda h, qi: (h, qi, 0))
    ospec = pl.BlockSpec((hd, dm.R), lambda h, qi: (h, 0))
    return pl.pallas_call(
        body, name="flash_bwd", grid=(dm.NKV, nq),
        in_specs=[qspec, tspec, kspec, ospec, kspec, qspec, tspec, sspec, sspec],
        out_specs=(pl.BlockSpec((tq, g * hd), lambda h, qi: (qi, h)), ospec, ospec),
        out_shape=(jax.ShapeDtypeStruct((dm.R, dm.A), F32), jax.ShapeDtypeStruct((dm.KVW, dm.R), F32),
                   jax.ShapeDtypeStruct((dm.KVW, dm.R), F32)),
        compiler_params=_params(("parallel", "arbitrary")),
    )(qh, qt, kh, kt, vh, doh, dot, lse, delta)


def _gate_b_specs(dm):
    ogb = 3 * dm.C + dm.A + 2 * dm.KVW
    half = dm.A // 2
    return [_rows(dm.TR, half, _col(ogb, half)), _rows(dm.TR, half, _col(ogb, half) + 1)]


def _attn_gate(dm, o, p):
    tr = dm.TR

    def body(o_ref, g0_ref, g1_ref, za_ref):
        gate = jnp.concatenate([g0_ref[...], g1_ref[...]], axis=1).astype(F32)
        za_ref[...] = (o_ref[...].astype(F32) * jax.nn.silu(gate)).astype(BF16)

    return pl.pallas_call(
        body, name="attn_gate", grid=(dm.R // tr,),
        in_specs=[_rows(tr, dm.A)] + _gate_b_specs(dm), out_specs=_rows(tr, dm.A),
        out_shape=jax.ShapeDtypeStruct((dm.R, dm.A), BF16), compiler_params=_params(("parallel",)),
    )(o, p, p)


def _attn_gate_bwd(dm, dza, o, p):
    tr, hd = dm.TR, dm.HD

    def body(dza_ref, o_ref, g0_ref, g1_ref, doh_ref, dot_ref, dl_ref, dgb_ref):
        gate = jnp.concatenate([g0_ref[...], g1_ref[...]], axis=1).astype(F32)
        dza, o = dza_ref[...].astype(F32), o_ref[...].astype(F32)
        do = dza * jax.nn.silu(gate)
        dgb_ref[...] = (dza * o * _dsilu(gate)).astype(BF16)
        dot_ref[...] = do.T.astype(BF16)
        prod = do * o
        for h in range(dm.NH):
            sl = slice(h * hd, (h + 1) * hd)
            doh_ref[h] = do[:, sl].astype(BF16)
            dl_ref[h] = jnp.broadcast_to(jnp.sum(prod[:, sl], axis=-1, keepdims=True), (tr, LANES))

    return pl.pallas_call(
        body, name="attn_gate_bwd", grid=(dm.R // tr,),
        in_specs=[_rows(tr, dm.A), _rows(tr, dm.A)] + _gate_b_specs(dm),
        out_specs=(pl.BlockSpec((dm.NH, tr, hd), lambda i: (0, i, 0)), pl.BlockSpec((dm.A, tr), lambda i: (0, i)),
                   pl.BlockSpec((dm.NH, tr, LANES), lambda i: (0, i, 0)), _rows(tr, dm.A)),
        out_shape=(jax.ShapeDtypeStruct((dm.NH, dm.R, hd), BF16), jax.ShapeDtypeStruct((dm.A, dm.R), BF16),
                   jax.ShapeDtypeStruct((dm.NH, dm.R, LANES), F32), jax.ShapeDtypeStruct((dm.R, dm.A), BF16)),
        compiler_params=_params(("parallel",)),
    )(dza, o, p, p)


def _halo_specs(dm, col):
    per = dm.TR // HALO
    last = dm.R // HALO - 1
    prev = pl.BlockSpec((HALO, dm.C), lambda i: (jnp.maximum(i * per - 1, 0), col))
    nxt = pl.BlockSpec((HALO, dm.C), lambda i: (jnp.minimum((i + 1) * per, last), col))
    return prev, nxt


def _seg_valid(dm, i):
    tr = dm.TR
    r = i * tr - HALO + lax.broadcasted_iota(jnp.int32, (tr + 2 * HALO, 1), 0)
    is_ctx = i * tr >= dm.N
    lo = jnp.where(is_ctx, dm.N, 0)
    hi = jnp.where(is_ctx, dm.R, dm.N)
    return jnp.logical_and(r >= lo, r < hi)


CONV_RB = 64


def _shifted_copies(src_ref, sl, sh_ref, tr):
    rows = tr + 2 * HALO - SUBLANES
    padded = src_ref[:, sl]
    for ph in range(SUBLANES):
        sh_ref[ph, 0:rows, :] = padded[ph:ph + rows]


def _tap_rows(sh_ref, off, n):
    ph = off % SUBLANES
    return sh_ref[ph, off - ph:off - ph + n, :]


def _conv_fwd(dm, p, conv_w_l, conv_b_l, ln_g_l, ln_b_l):
    tr, c = dm.TR, dm.C

    def body(a_ref, g_ref, gate_ref, ap_ref, an_ref, gp_ref, gn_ref, w_ref, b_ref, lg_ref, lb_ref,
             zc_ref, y1_ref, u_ref, sh_ref):
        i = pl.program_id(0)
        a = jnp.concatenate([ap_ref[...], a_ref[...], an_ref[...]], axis=0).astype(F32)
        g = jnp.concatenate([gp_ref[...], g_ref[...], gn_ref[...]], axis=0).astype(F32)
        u_ref[...] = jnp.where(_seg_valid(dm, i), a * jax.nn.sigmoid(g), 0.0)
        for cc in range(c // LANES):
            sl = slice(cc * LANES, (cc + 1) * LANES)
            _shifted_copies(u_ref, sl, sh_ref, tr)
            for r0 in range(0, tr, CONV_RB):
                acc = jnp.zeros((CONV_RB, LANES), F32)
                for j in range(CONV_KERNEL):
                    acc = acc + w_ref[j:j + 1, sl] * _tap_rows(sh_ref, r0 + HALO - CONV_PAD + j, CONV_RB)
                y1_ref[r0:r0 + CONV_RB, sl] = acc + b_ref[:, sl]
        zc_ref[...] = _convact_fn(y1_ref[...], gate_ref[...].astype(F32), lg_ref[...], lb_ref[...]).astype(BF16)

    ap, an = _halo_specs(dm, 0)
    gp, gn = _halo_specs(dm, 1)
    return pl.pallas_call(
        body, name="conv_fwd", grid=(dm.R // tr,),
        in_specs=[_rows(tr, c, 0), _rows(tr, c, 1), _rows(tr, c, 2), ap, an, gp, gn,
                  _whole((32, c)), _whole((1, c)), _whole((1, c)), _whole((1, c))],
        out_specs=(_rows(tr, c), _rows(tr, c)),
        out_shape=(jax.ShapeDtypeStruct((dm.R, c), BF16), jax.ShapeDtypeStruct((dm.R, c), F32)),
        scratch_shapes=[pltpu.VMEM((tr + 2 * HALO, c), F32), pltpu.VMEM((SUBLANES, tr + 2 * HALO, LANES), F32)],
        compiler_params=_params(("parallel",)),
    )(p, p, p, p, p, p, p, conv_w_l, conv_b_l, ln_g_l, ln_b_l)


def _conv_act_bwd(dm, dzc, y1, p, ln_g_l, ln_b_l):
    tr, c = dm.TR, dm.C

    def body(dz_ref, y1_ref, gate_ref, lg_ref, lb_ref, dy_ref, dgate_ref, dlg_ref, dlb_ref):
        i = pl.program_id(0)
        _, vjp = jax.vjp(_convact_fn, y1_ref[...], gate_ref[...].astype(F32), lg_ref[...], lb_ref[...])
        dy, dgate, dlg, dlb = vjp(dz_ref[...].astype(F32))
        dy_ref[...] = dy
        dgate_ref[...] = dgate.astype(BF16)

        @pl.when(i == 0)
        def _():
            dlg_ref[...] = jnp.zeros_like(dlg_ref)
            dlb_ref[...] = jnp.zeros_like(dlb_ref)

        dlg_ref[...] += dlg
        dlb_ref[...] += dlb

    return pl.pallas_call(
        body, name="conv_act_bwd", grid=(dm.R // tr,),
        in_specs=[_rows(tr, c), _rows(tr, c), _rows(tr, c, 2), _whole((1, c)), _whole((1, c))],
        out_specs=(_rows(tr, c), _rows(tr, c), _whole((1, c)), _whole((1, c))),
        out_shape=(jax.ShapeDtypeStruct((dm.R, c), F32), jax.ShapeDtypeStruct((dm.R, c), BF16),
                   jax.ShapeDtypeStruct((1, c), F32), jax.ShapeDtypeStruct((1, c), F32)),
        compiler_params=_params(("arbitrary",)),
    )(dzc, y1, p, ln_g_l, ln_b_l)


def _conv_bwd(dm, dy1, p, conv_w_l):
    tr, c = dm.TR, dm.C

    def body(dy_ref, dyp_ref, dyn_ref, a_ref, g_ref, ap_ref, an_ref, gp_ref, gn_ref, w_ref,
             dag_ref, dw_ref, db_ref, u_ref, dyp_scr, du_ref, dwp_ref, shu_ref, shd_ref):
        i = pl.program_id(0)
        valid = _seg_valid(dm, i)
        a = jnp.concatenate([ap_ref[...], a_ref[...], an_ref[...]], axis=0).astype(F32)
        g = jnp.concatenate([gp_ref[...], g_ref[...], gn_ref[...]], axis=0).astype(F32)
        u_ref[...] = jnp.where(valid, a * jax.nn.sigmoid(g), 0.0)
        dyp_scr[...] = jnp.where(valid, jnp.concatenate([dyp_ref[...], dy_ref[...], dyn_ref[...]], axis=0), 0.0)

        @pl.when(i == 0)
        def _():
            dwp_ref[...] = jnp.zeros_like(dwp_ref)
            db_ref[...] = jnp.zeros_like(db_ref)

        db_ref[...] += jnp.sum(dy_ref[...], axis=0, keepdims=True)
        for cc in range(c // LANES):
            sl = slice(cc * LANES, (cc + 1) * LANES)
            _shifted_copies(u_ref, sl, shu_ref, tr)
            _shifted_copies(dyp_scr, sl, shd_ref, tr)
            for r0 in range(0, tr, CONV_RB):
                dy_cur = dy_ref[r0:r0 + CONV_RB, sl]
                acc = jnp.zeros((CONV_RB, LANES), F32)
                for j in range(CONV_KERNEL):
                    acc = acc + w_ref[j:j + 1, sl] * _tap_rows(shd_ref, r0 + HALO + CONV_PAD - j, CONV_RB)
                    prod = dy_cur * _tap_rows(shu_ref, r0 + HALO - CONV_PAD + j, CONV_RB)
                    dwp_ref[j, :, sl] += jnp.sum(prod.reshape(CONV_RB // SUBLANES, SUBLANES, LANES), axis=0)
                du_ref[r0:r0 + CONV_RB, sl] = acc
        du = du_ref[...]
        sg = jax.nn.sigmoid(g_ref[...].astype(F32))
        dag_ref[:, :c] = (du * sg).astype(BF16)
        dag_ref[:, c:] = (du * a_ref[...].astype(F32) * sg * (1.0 - sg)).astype(BF16)

        @pl.when(i == dm.R // tr - 1)
        def _():
            dw_ref[...] = jnp.sum(dwp_ref[...], axis=1)

    ap, an = _halo_specs(dm, 0)
    gp, gn = _halo_specs(dm, 1)
    dyp, dyn = _halo_specs(dm, 0)
    return pl.pallas_call(
        body, name="conv_bwd", grid=(dm.R // tr,),
        in_specs=[_rows(tr, c), dyp, dyn, _rows(tr, c, 0), _rows(tr, c, 1), ap, an, gp, gn, _whole((32, c))],
        out_specs=(_rows(tr, 2 * c), _whole((32, c)), _whole((1, c))),
        out_shape=(jax.ShapeDtypeStruct((dm.R, 2 * c), BF16), jax.ShapeDtypeStruct((32, c), F32),
                   jax.ShapeDtypeStruct((1, c), F32)),
        scratch_shapes=[pltpu.VMEM((tr + 2 * HALO, c), F32), pltpu.VMEM((tr + 2 * HALO, c), F32), pltpu.VMEM((tr, c), F32),
                        pltpu.VMEM((32, SUBLANES, c), F32), pltpu.VMEM((SUBLANES, tr + 2 * HALO, LANES), F32),
                        pltpu.VMEM((SUBLANES, tr + 2 * HALO, LANES), F32)],
        compiler_params=_params(("arbitrary",)),
    )(dy1, dy1, dy1, p, p, p, p, p, p, conv_w_l)


def _gm_specs(dm):
    half = dm.D // 2
    first = _col(3 * dm.C + 2 * dm.A + 2 * dm.KVW, half)
    return [_rows(dm.TR, half, first + k) for k in range(4)]


def _gm_load(a0, a1, b0, b1):
    return (jnp.concatenate([a0[...], a1[...]], axis=1).astype(F32),
            jnp.concatenate([b0[...], b1[...]], axis=1).astype(F32))


def _merge(dm, yc, ya, p):
    tr, d = dm.TR, dm.D

    def body(yc_ref, ya_ref, a0, a1, b0, b1, m_ref):
        ga, gb = _gm_load(a0, a1, b0, b1)
        yc, ya = yc_ref[...].astype(F32), ya_ref[...].astype(F32)
        m_ref[...] = (jax.nn.sigmoid(ga) * yc + jax.nn.sigmoid(gb) * ya).astype(BF16)

    return pl.pallas_call(
        body, name="merge", grid=(dm.R // tr,),
        in_specs=[_rows(tr, d), _rows(tr, d)] + _gm_specs(dm), out_specs=_rows(tr, d),
        out_shape=jax.ShapeDtypeStruct((dm.R, d), BF16), compiler_params=_params(("parallel",)),
    )(yc, ya, p, p, p, p)


def _merge_bwd(dm, dmerged, yc, ya, p):
    tr, d = dm.TR, dm.D

    def body(dm_ref, yc_ref, ya_ref, a0, a1, b0, b1, dyc_ref, dya_ref, dgm_ref):
        dmv = dm_ref[...].astype(F32)
        ga, gb = _gm_load(a0, a1, b0, b1)
        sa, sb = jax.nn.sigmoid(ga), jax.nn.sigmoid(gb)
        dyc_ref[...] = (dmv * sa).astype(BF16)
        dya_ref[...] = (dmv * sb).astype(BF16)
        dgm_ref[:, :d] = (dmv * yc_ref[...].astype(F32) * sa * (1.0 - sa)).astype(BF16)
        dgm_ref[:, d:] = (dmv * ya_ref[...].astype(F32) * sb * (1.0 - sb)).astype(BF16)

    return pl.pallas_call(
        body, name="merge_bwd", grid=(dm.R // tr,),
        in_specs=[_rows(tr, d), _rows(tr, d), _rows(tr, d)] + _gm_specs(dm),
        out_specs=(_rows(tr, d), _rows(tr, d), _rows(tr, 2 * d)),
        out_shape=(jax.ShapeDtypeStruct((dm.R, d), BF16), jax.ShapeDtypeStruct((dm.R, d), BF16),
                   jax.ShapeDtypeStruct((dm.R, 2 * d), BF16)),
        compiler_params=_params(("parallel",)),
    )(dmerged, yc, ya, p, p, p, p)


def _loss_head(dm, xc, target):
    tr, d = dm.TR, dm.D
    n_lat = dm.N // tr

    def body(x_ref, t_ref, dy_ref, l_ref):
        i = pl.program_id(0)

        @pl.when(i == 0)
        def _():
            l_ref[...] = jnp.zeros_like(l_ref)

        @pl.when(i < n_lat)
        def _():
            e = x_ref[...] - t_ref[...]
            dy_ref[...] = e * (1.0 / d)
            l_ref[...] += 0.5 * jnp.sum(jnp.mean(e * e, axis=-1, keepdims=True), axis=0, keepdims=True)

        @pl.when(i >= n_lat)
        def _():
            dy_ref[...] = jnp.zeros_like(dy_ref)

    return pl.pallas_call(
        body, name="loss_head", grid=(dm.R // tr,),
        in_specs=[_rows(tr, d), pl.BlockSpec((tr, d), lambda i: (jnp.minimum(i, n_lat - 1), 0))],
        out_specs=(_rows(tr, d), _whole((1, 1))),
        out_shape=(jax.ShapeDtypeStruct((dm.R, d), F32), jax.ShapeDtypeStruct((1, 1), F32)),
        compiler_params=_params(("arbitrary",)),
    )(xc, target)


def _adamw(w, g, m, v, name):
    rows, cols = w.shape
    fits = lambda t: t * cols * 4 <= ADAM_BLOCK_BYTES
    tr = rows if fits(rows) else _pick(rows, [t for t in (512, 256, 128, 64, 32, 16, 8) if fits(t)])

    def body(w_ref, g_ref, m_ref, v_ref, d_ref, mo_ref, vo_ref):
        d_ref[...], mo_ref[...], vo_ref[...] = _adam_update(w_ref[...], g_ref[...], m_ref[...], v_ref[...])

    spec = _rows(tr, cols)
    shp = jax.ShapeDtypeStruct(w.shape, F32)
    return pl.pallas_call(
        body, name=name, grid=(rows // tr,), in_specs=[spec] * 4, out_specs=(spec,) * 3, out_shape=(shp,) * 3,
        compiler_params=_params(("parallel",)),
    )(w, g, m, v)


def _adam_update(w, g, m, v):
    mn = ADAM_B1 * m + (1.0 - ADAM_B1) * g
    vn = ADAM_B2 * v + (1.0 - ADAM_B2) * jnp.square(g)
    m_hat = mn / (1.0 - ADAM_B1 ** ADAM_STEP)
    v_hat = vn / (1.0 - ADAM_B2 ** ADAM_STEP)
    return -ADAM_LR * (m_hat / (jnp.sqrt(v_hat) + ADAM_EPS) + ADAM_WD * w), mn, vn


def _rope_tables(dm):
    n = dm.N
    rows = n // GRID_W
    row = jnp.repeat(jnp.arange(rows, dtype=F32), GRID_W)
    col = jnp.tile(jnp.arange(GRID_W, dtype=F32), rows)
    half = dm.HD // 2
    inv_freq = ROPE_THETA ** (-jnp.arange(0, half, 2, dtype=F32) / half)
    ang = jnp.concatenate([row[:, None] * inv_freq, col[:, None] * inv_freq], axis=-1)
    cos, sin = jnp.cos(ang), jnp.sin(ang)
    cos = jnp.concatenate([cos, jnp.ones((dm.CTX, half), F32)], axis=0)
    sin = jnp.concatenate([sin, jnp.zeros((dm.CTX, half), F32)], axis=0)
    reps = LANES // dm.HD
    return jnp.tile(jnp.concatenate([cos, cos], axis=1), (1, reps)), jnp.tile(jnp.concatenate([-sin, sin], axis=1), (1, reps))


def _lane_vec(g, hd):
    return jnp.tile(g.reshape(1, hd), (1, LANES // hd))


def _layer_fwd(dm, xc, cv, wl, tabs):
    mod = _mod_fwd(cv, wl["w_mod"], wl["b_mod"])
    h = _prenorm(dm, xc, wl["g_pre"], mod)
    p = _matmul(h, wl["w_in"], "nn", "in_proj", BF16)
    qh, qt, kh, kt, vh, vx = _qk_prep(dm, p, tabs["cos"], tabs["sins"], wl["gq"], wl["gk"], tabs["bd"])
    o, lse = _flash_fwd(dm, qh, kh, vx)
    za = _attn_gate(dm, o, p)
    zc, y1 = _conv_fwd(dm, p, wl["conv_w"], wl["conv_b"], wl["ln_g"], wl["ln_b"])
    yc = _matmul(zc, wl["w_conv_out"], "nn", "conv_out", BF16)
    ya = _matmul(za, wl["w_attn_out"], "nn", "attn_out", BF16)
    merged = _merge(dm, yc, ya, p)
    outp = _matmul(merged, wl["w_out"], "nn", "out_proj", BF16)
    x_new = _post(dm, xc, outp, wl["g_post"], mod)
    saved = dict(xc=xc, mod=mod, h=h, p=p, qh=qh, qt=qt, kh=kh, kt=kt, vh=vh, o=o, lse=lse, za=za, zc=zc, y1=y1, yc=yc, ya=ya,
                 merged=merged, outp=outp)
    return x_new, saved


def _layer_bwd(dm, dxn, cv, wl, tabs, s):
    p = s["p"]
    d_outp, dg_post, dgt = _post_bwd(dm, s["outp"], wl["g_post"], s["mod"], dxn)
    dmerged = _matmul(d_outp, wl["w_out"], "nt", "out_proj_dx", BF16)
    dw_out = _matmul(s["merged"], d_outp, "tn", "out_proj_dw")
    dyc, dya, dgm = _merge_bwd(dm, dmerged, s["yc"], s["ya"], p)
    dzc = _matmul(dyc, wl["w_conv_out"], "nt", "conv_out_dx", BF16)
    dw_conv_out = _matmul(s["zc"], dyc, "tn", "conv_out_dw")
    dza = _matmul(dya, wl["w_attn_out"], "nt", "attn_out_dx", BF16)
    dw_attn_out = _matmul(s["za"], dya, "tn", "attn_out_dw")
    doh, dot, delta, dgate_b = _attn_gate_bwd(dm, dza, s["o"], p)
    dq, dkt, dvt = _flash_bwd(dm, s["qh"], s["qt"], s["kh"], s["kt"], s["vh"], doh, dot, s["lse"], delta)
    dpq, dpkv, dgq, dgk = _qk_bwd(dm, p, dq, dkt, dvt, tabs["cos"], tabs["sins"], wl["gq"], wl["gk"], tabs["bd"])
    dy1, dgate_a, dln_g, dln_b = _conv_act_bwd(dm, dzc, s["y1"], p, wl["ln_g"], wl["ln_b"])
    dpag, dconv_w, dconv_b = _conv_bwd(dm, dy1, p, wl["conv_w"])
    dp = jnp.concatenate([dpag, dgate_a, dpq, dpkv, dgate_b, dgm], axis=1)
    dh = _matmul(dp, wl["w_in"], "nt", "in_proj_dx", BF16)
    dw_in = _matmul(s["h"], dp, "tn", "in_proj_dw")
    dxc, dg_pre, dshsc = _prenorm_bwd(dm, s["xc"], wl["g_pre"], s["mod"], dh, dxn)
    dmod = jnp.concatenate([dshsc, dgt], axis=1)
    dw_mod, db_mod, dcv = _mod_bwd(cv, wl["w_mod"], dmod)
    grads = dict(w_mod=dw_mod, b_mod=db_mod[0], g_pre=dg_pre[0], g_post=dg_post[0], w_in=dw_in,
                 conv_w=dconv_w[:CONV_KERNEL], conv_b=dconv_b[0], ln_g=dln_g[0], ln_b=dln_b[0],
                 w_conv_out=dw_conv_out, q_norm_g=dgq[0, :dm.HD], k_norm_g=dgk[0, :dm.HD],
                 w_attn_out=dw_attn_out, w_out=dw_out, c_ctx=dcv[1])
    return dxc, grads


def _local_step(dm, x, c, ctx, c_ctx, target, wf):
    depth = wf["w_in"].shape[0]
    cos, sins = _rope_tables(dm)
    lane = np.arange(LANES)
    bd = jnp.asarray((lane[:, None] // dm.HD == lane[None, :] // dm.HD), dtype=BF16)
    tabs = dict(cos=cos, sins=sins, bd=bd)
    cv = jnp.concatenate([c.reshape(1, dm.D), c_ctx.reshape(1, dm.D), jnp.zeros((6, dm.D), F32)], axis=0)
    xc = jnp.concatenate([x, ctx], axis=0)
    layers, saved = [], []
    for l in range(depth):
        wl = dict(
            w_mod=wf["w_mod"][l], b_mod=wf["b_mod"][l].reshape(1, -1), g_pre=wf["g_pre"][l].reshape(1, -1),
            g_post=wf["g_post"][l].reshape(1, -1), w_in=wf["w_in"][l],
            conv_w=jnp.concatenate([wf["conv_w"][l], jnp.zeros((1, dm.C), F32)], axis=0),
            conv_b=wf["conv_b"][l].reshape(1, -1), ln_g=wf["ln_g"][l].reshape(1, -1), ln_b=wf["ln_b"][l].reshape(1, -1),
            w_conv_out=wf["w_conv_out"][l], gq=_lane_vec(wf["q_norm_g"][l], dm.HD), gk=_lane_vec(wf["k_norm_g"][l], dm.HD),
            w_attn_out=wf["w_attn_out"][l], w_out=wf["w_out"][l])
        layers.append(wl)
        xc, s = _layer_fwd(dm, xc, cv, wl, tabs)
        saved.append(s)
    dxc, loss = _loss_head(dm, xc, target)
    grads = [None] * depth
    for l in reversed(range(depth)):
        dxc, grads[l] = _layer_bwd(dm, dxc, cv, layers[l], tabs, saved[l])
    big = ("w_mod", "w_in", "w_conv_out", "w_attn_out", "w_out")
    gw = {k: [g[k] for g in grads] if k in big else jnp.stack([g[k] for g in grads]) for k in grads[0] if k != "c_ctx"}
    gw["c_ctx"] = grads[0]["c_ctx"] + grads[1]["c_ctx"] if depth == 2 else sum(g["c_ctx"] for g in grads)
    return loss[0, 0], dxc[:dm.N], gw


def _dims(x, ctx, w_in_full_cols, conv_c, attn_w, hd):
    n, d = x.shape
    ctx_len = ctx.shape[0]
    c, a = conv_c, attn_w
    kvw = (w_in_full_cols - 3 * c - 2 * a - 2 * d) // 2
    r = n + ctx_len
    tr = _pick(np.gcd(n, ctx_len), (256, 128))
    return Dims(N=n, CTX=ctx_len, R=r, D=d, C=c, A=a, KVW=kvw, HD=hd, NH=a // hd, NKV=kvw // hd,
                G=a // kvw, W=w_in_full_cols, TR=tr)


_HBM = pl.BlockSpec(memory_space=pltpu.HBM)
_VMEM = pl.BlockSpec(memory_space=pltpu.VMEM)


def _place():
    x, y, c = lax.axis_index("x"), lax.axis_index("y"), lax.axis_index("c")
    chips = [(1 - x, y), (x, 1 - y), (1 - x, 1 - y)]
    return x, y, c, chips


def _row_pieces(rows, want):
    n = max(k for k in range(1, want + 1) if rows % (16 * k) == 0)
    return [pl.ds(i * (rows // n), rows // n) for i in range(n)]


def _allgather_chips(own, name):
    _, h, w = own.shape
    pieces = _row_pieces(h, 4)
    npc = len(pieces)

    def body(own_ref, out_ref, send_sems, recv_sems):
        x, y, c, chips = _place()
        s_me = 2 * x + y
        sibling = (x, y, 1 - c)

        def copy(k, src, dst, to):
            return pltpu.make_async_remote_copy(src_ref=src, dst_ref=dst, send_sem=send_sems.at[k], recv_sem=recv_sems.at[k],
                                                device_id=to, device_id_type=MESH)

        sends = []
        for i, rows in enumerate(pieces):
            for j, chip in enumerate(chips):
                sends.append(copy(i * 3 + j, own_ref.at[c, rows], out_ref.at[s_me, c, rows], (*chip, c)))
        own_slots = [(6 * npc + hf * npc + i, hf, rows) for hf in range(2) for i, rows in enumerate(pieces)]
        for k, hf, rows in own_slots:
            sends.append(copy(k, own_ref.at[hf, rows], out_ref.at[s_me, hf, rows], sibling))
        for cp in sends:
            cp.start()
        for i, rows in enumerate(pieces):
            for j, (cx, cy) in enumerate(chips):
                landed = out_ref.at[2 * cx + cy, c, rows]
                copy(i * 3 + j, landed, landed, (cx, cy, c)).wait_recv()
                fwd = copy(3 * npc + i * 3 + j, landed, landed, sibling)
                fwd.start()
                sends.append(fwd)
        for i, rows in enumerate(pieces):
            for j, (cx, cy) in enumerate(chips):
                theirs = out_ref.at[2 * cx + cy, 1 - c, rows]
                copy(3 * npc + i * 3 + j, theirs, theirs, sibling).wait_recv()
        for k, hf, rows in own_slots:
            mine = out_ref.at[s_me, hf, rows]
            copy(k, mine, mine, sibling).wait_recv()
        for cp in sends:
            cp.wait_send()

    return pl.pallas_call(
        body, name=name, in_specs=[_HBM], out_specs=_HBM,
        out_shape=jax.ShapeDtypeStruct((N_CHIPS, 2, h, w), own.dtype),
        scratch_shapes=[pltpu.SemaphoreType.DMA((8 * npc,)), pltpu.SemaphoreType.DMA((8 * npc,))],
    )(own)


def _pair_exchange(buf, name):
    _, m, w = buf.shape
    pieces = _row_pieces(m, 16)

    def body(buf_ref, out_ref, send_sems, recv_sems):
        x, y, c, _ = _place()
        cps = [pltpu.make_async_remote_copy(src_ref=buf_ref.at[1 - c, rows], dst_ref=out_ref.at[rows], send_sem=send_sems.at[i],
                                            recv_sem=recv_sems.at[i], device_id=(x, y, 1 - c), device_id_type=MESH)
               for i, rows in enumerate(pieces)]
        for cp in cps:
            cp.start()
        for cp in cps:
            cp.wait()

    return pl.pallas_call(
        body, name=name, in_specs=[_HBM], out_specs=_HBM, out_shape=jax.ShapeDtypeStruct((m, w), buf.dtype),
        scratch_shapes=[pltpu.SemaphoreType.DMA((len(pieces),)), pltpu.SemaphoreType.DMA((len(pieces),))],
    )(buf)


def _pair_swap(t, name):
    m, w = t.shape
    pieces = _row_pieces(m, 8)
    n = len(pieces)

    def body(t_ref, out_ref, send_sems, recv_sems):
        x, y, c, _ = _place()
        cps = [pltpu.make_async_remote_copy(src_ref=t_ref.at[rows], dst_ref=out_ref.at[rows], send_sem=send_sems.at[i],
                                            recv_sem=recv_sems.at[i], device_id=(x, y, 1 - c), device_id_type=MESH)
               for i, rows in enumerate(pieces)]
        for cp in cps:
            cp.start()
        for cp in cps:
            cp.wait()

    return pl.pallas_call(
        body, name=name, in_specs=[_HBM], out_specs=_HBM, out_shape=jax.ShapeDtypeStruct((m, w), t.dtype),
        scratch_shapes=[pltpu.SemaphoreType.DMA((n,)), pltpu.SemaphoreType.DMA((n,))],
    )(t)


def _scatter_chips(s, name):
    _, h, w = s.shape

    def body(s_ref, out_ref, send_sems, recv_sems):
        _, _, c, chips = _place()
        cps = [pltpu.make_async_remote_copy(src_ref=s_ref.at[2 * cx + cy], dst_ref=out_ref.at[j], send_sem=send_sems.at[j],
                                            recv_sem=recv_sems.at[j], device_id=(cx, cy, c), device_id_type=MESH)
               for j, (cx, cy) in enumerate(chips)]
        for cp in cps:
            cp.start()
        for cp in cps:
            cp.wait()

    return pl.pallas_call(
        body, name=name, in_specs=[_HBM], out_specs=_HBM, out_shape=jax.ShapeDtypeStruct((3, h, w), s.dtype),
        scratch_shapes=[pltpu.SemaphoreType.DMA((3,)), pltpu.SemaphoreType.DMA((3,))],
    )(s)


def _sum_pair(buf, recv, core, name):
    m, w = recv.shape
    tr = _pick(m, (512, 432, 256, 128, 64, 32, 16))

    def body(c_ref, b_ref, r_ref, o_ref, ob_ref):
        v = b_ref[0] + r_ref[...]
        o_ref[...] = v
        ob_ref[...] = v.astype(BF16)

    spec = pl.BlockSpec((tr, w), lambda i, c: (i, 0))
    return pl.pallas_call(
        body, name=name,
        grid_spec=pltpu.PrefetchScalarGridSpec(
            num_scalar_prefetch=1, grid=(m // tr,),
            in_specs=[pl.BlockSpec((1, tr, w), lambda i, c: (c[0], i, 0)), spec], out_specs=(spec, spec)),
        out_shape=(jax.ShapeDtypeStruct((m, w), F32), jax.ShapeDtypeStruct((m, w), BF16)),
        compiler_params=_params(("parallel",)),
    )(core, buf, recv)


def _sum_chips(own, recv, chip, name):
    _, h, w = own.shape
    tr = _pick(h, (512, 432, 256, 128, 64, 32, 16))

    def body(s_ref, a_ref, r_ref, o_ref):
        acc = a_ref[0]
        for j in range(3):
            acc = acc + r_ref[j].astype(F32)
        o_ref[...] = acc

    return pl.pallas_call(
        body, name=name,
        grid_spec=pltpu.PrefetchScalarGridSpec(
            num_scalar_prefetch=1, grid=(h // tr,),
            in_specs=[pl.BlockSpec((1, tr, w), lambda i, s: (s[0], i, 0)), pl.BlockSpec((3, tr, w), lambda i, s: (0, i, 0))],
            out_specs=pl.BlockSpec((tr, w), lambda i, s: (i, 0))),
        out_shape=jax.ShapeDtypeStruct((h, w), F32), compiler_params=_params(("parallel",)),
    )(chip, own, recv)


def _allreduce_small(own, name):
    m, w = own.shape

    def body(own_ref, sum_ref, all_ref, send_sems, recv_sems):
        x, y, c, _ = _place()
        me = 4 * x + 2 * y + c
        all_ref[me] = own_ref[...]
        sends = []
        for k in range(1, N_DEV):
            peer = (x ^ ((k >> 2) & 1), y ^ ((k >> 1) & 1), c ^ (k & 1))
            cp = pltpu.make_async_remote_copy(src_ref=own_ref, dst_ref=all_ref.at[me], send_sem=send_sems.at[k - 1],
                                              recv_sem=recv_sems.at[k - 1], device_id=peer, device_id_type=MESH)
            cp.start()
            sends.append(cp)
        for k in range(1, N_DEV):
            px, py, pc = x ^ ((k >> 2) & 1), y ^ ((k >> 1) & 1), c ^ (k & 1)
            slot = all_ref.at[4 * px + 2 * py + pc]
            pltpu.make_async_remote_copy(src_ref=slot, dst_ref=slot, send_sem=send_sems.at[k - 1], recv_sem=recv_sems.at[k - 1],
                                         device_id=(px, py, pc), device_id_type=MESH).wait_recv()
        for cp in sends:
            cp.wait_send()
        acc = all_ref[0]
        for d in range(1, N_DEV):
            acc = acc + all_ref[d]
        sum_ref[...] = acc

    return pl.pallas_call(
        body, name=name, in_specs=[_VMEM], out_specs=_VMEM, out_shape=jax.ShapeDtypeStruct((m, w), F32),
        scratch_shapes=[pltpu.VMEM((N_DEV, m, w), F32), pltpu.SemaphoreType.DMA((N_DEV - 1,)), pltpu.SemaphoreType.DMA((N_DEV - 1,))],
        compiler_params=pltpu.CompilerParams(vmem_limit_bytes=VMEM_LIMIT),
    )(own)


_COL_SHARDED = ("w_mod", "w_in")
_ROW_SHARDED = ("w_conv_out", "w_attn_out", "w_out")
_BIG = _COL_SHARDED + _ROW_SHARDED
_SMALL = ("c_ctx", "b_mod", "g_pre", "g_post", "conv_b", "ln_g", "ln_b", "q_norm_g", "k_norm_g")
FLAT_W = 1024


def _unflatten_shard(flat, like):
    out, off = {}, 0
    flat = flat.reshape(-1)
    for k in _BIG:
        n = int(np.prod(like[k].shape))
        out[k] = flat[off:off + n].reshape(like[k].shape)
        off += n
    return out


def _by_shard(k, layers):
    parts = []
    for full in layers:
        if k in _COL_SHARDED:
            d, cols = full.shape
            parts.append(full.reshape(d, N_CHIPS, cols // N_CHIPS).transpose(1, 0, 2).reshape(N_CHIPS, -1))
        else:
            parts.append(full.reshape(N_CHIPS, -1))
    return jnp.concatenate(parts, axis=1)


def _from_shards(k, flat4, like):
    l = like.shape[0]
    if k in _COL_SHARDED:
        d, cs = like.shape[1:]
        return flat4.reshape(N_CHIPS, l, d, cs).transpose(1, 2, 0, 3).reshape(l, d, N_CHIPS * cs)
    rs, d = like.shape[1:]
    return flat4.reshape(N_CHIPS, l, rs, d).transpose(1, 0, 2, 3).reshape(l, N_CHIPS * rs, d)


def _pad_rows(flat, mult=8):
    n = flat.shape[0]
    rows = -(-n // LANES)
    rows = -(-rows // mult) * mult
    return jnp.concatenate([flat, jnp.zeros((rows * LANES - n,), F32)]).reshape(rows, LANES)


def kernel(x, c, ctx, c_ctx, w_mod, b_mod, g_pre, g_post, w_in, conv_w, conv_b, ln_g, ln_b, w_conv_out, q_norm_g, k_norm_g, w_attn_out, w_out, loss_target, m_c_ctx, m_w_mod, m_b_mod, m_g_pre, m_g_post, m_w_in, m_conv_w, m_conv_b, m_ln_g, m_ln_b, m_w_conv_out, m_q_norm_g, m_k_norm_g, m_w_attn_out, m_w_out, v_c_ctx, v_w_mod, v_b_mod, v_g_pre, v_g_post, v_w_in, v_conv_w, v_conv_b, v_ln_g, v_ln_b, v_w_conv_out, v_q_norm_g, v_k_norm_g, v_w_attn_out, v_w_out):
    w = dict(c_ctx=c_ctx, w_mod=w_mod, b_mod=b_mod, g_pre=g_pre, g_post=g_post, w_in=w_in, conv_w=conv_w, conv_b=conv_b,
             ln_g=ln_g, ln_b=ln_b, w_conv_out=w_conv_out, q_norm_g=q_norm_g, k_norm_g=k_norm_g, w_attn_out=w_attn_out, w_out=w_out)
    m = dict(c_ctx=m_c_ctx, w_mod=m_w_mod, b_mod=m_b_mod, g_pre=m_g_pre, g_post=m_g_post, w_in=m_w_in, conv_w=m_conv_w,
             conv_b=m_conv_b, ln_g=m_ln_g, ln_b=m_ln_b, w_conv_out=m_w_conv_out, q_norm_g=m_q_norm_g, k_norm_g=m_k_norm_g,
             w_attn_out=m_w_attn_out, w_out=m_w_out)
    v = dict(c_ctx=v_c_ctx, w_mod=v_w_mod, b_mod=v_b_mod, g_pre=v_g_pre, g_post=v_g_post, w_in=v_w_in, conv_w=v_conv_w,
             conv_b=v_conv_b, ln_g=v_ln_g, ln_b=v_ln_b, w_conv_out=v_w_conv_out, q_norm_g=v_q_norm_g, k_norm_g=v_k_norm_g,
             w_attn_out=v_w_attn_out, w_out=v_w_out)
    order = ("c_ctx", "w_mod", "b_mod", "g_pre", "g_post", "w_in", "conv_w", "conv_b", "ln_g", "ln_b", "w_conv_out",
             "q_norm_g", "k_norm_g", "w_attn_out", "w_out")
    my_chip = 2 * lax.axis_index("x") + lax.axis_index("y")
    my_core = lax.axis_index("c")
    depth, _, conv_cs = conv_w.shape
    conv_c = conv_cs * N_CHIPS
    dm = _dims(x[0], ctx[0], w_in.shape[2] * N_CHIPS, conv_c, w_attn_out.shape[1] * N_CHIPS, q_norm_g.shape[1])

    own = jnp.concatenate([w[k].astype(BF16).reshape(-1) for k in _BIG])
    rows2 = own.shape[0] // FLAT_W
    own = own.reshape(2, rows2 // 2, FLAT_W)
    gathered = _allgather_chips(own, "gather_weights").reshape(N_CHIPS, -1)
    wf, off = {}, 0
    for k in _BIG:
        n = int(np.prod(w[k].shape))
        wf[k] = _from_shards(k, gathered[:, off:off + n], w[k])
        off += n
    cw = jnp.zeros((depth, CONV_KERNEL, N_CHIPS, conv_cs), F32)
    cw = lax.dynamic_update_slice(cw, conv_w[:, :, None, :], (0, 0, my_chip, 0))
    cw = jnp.where(my_core == 0, cw, 0.0).reshape(-1)
    wf["conv_w"] = _allreduce_small(_pad_rows(cw), "gather_conv_w").reshape(-1)[:cw.shape[0]].reshape(depth, CONV_KERNEL, conv_c)
    for k in _SMALL:
        wf[k] = w[k]

    loss_part, grad_x, gw = _local_step(dm, x[0], c, ctx[0], c_ctx, loss_target[0], wf)

    g4 = jnp.concatenate([_by_shard(k, gw[k]) for k in _BIG], axis=1)
    hrows = rows2 // 2
    g24 = g4.reshape(N_CHIPS, 2, hrows, FLAT_W).transpose(1, 0, 2, 3).reshape(2, N_CHIPS * hrows, FLAT_W)
    core_i = my_core.astype(jnp.int32).reshape(1)
    chip_i = my_chip.astype(jnp.int32).reshape(1)
    theirs = _pair_exchange(g24, "reduce_pair")
    chip_sum, chip_sum_bf = _sum_pair(g24, theirs, core_i, "reduce_pair_add")
    others = _scatter_chips(chip_sum_bf.reshape(N_CHIPS, hrows, FLAT_W), "reduce_chips")
    mine = _sum_chips(chip_sum.reshape(N_CHIPS, hrows, FLAT_W), others, chip_i, "reduce_chips_add")
    sibling_half = _pair_swap(mine, "reduce_share")

    small = jnp.concatenate([gw[k].reshape(-1) for k in _SMALL] + [gw["conv_w"].reshape(-1), loss_part.reshape(1)])
    small_sum = _allreduce_small(_pad_rows(small), "reduce_small").reshape(-1)
    loss = small_sum[small.shape[0] - 1]
    gs, off = {}, 0
    for k in _SMALL:
        n = int(np.prod(w[k].shape))
        gs[k] = small_sum[off:off + n].reshape(w[k].shape)
        off += n
    gcw = small_sum[off:off + depth * CONV_KERNEL * conv_c].reshape(depth, CONV_KERNEL, N_CHIPS, conv_cs)
    gs["conv_w"] = lax.dynamic_index_in_dim(gcw, my_chip, axis=2, keepdims=False)

    g_flat = jnp.where(my_core == 0, jnp.concatenate([mine, sibling_half]), jnp.concatenate([sibling_half, mine]))
    grads = _unflatten_shard(g_flat, w)
    delta, new_m, new_v = {}, {}, {}
    for k in _BIG:
        as2d = lambda t: t.reshape(-1, t.shape[-1])
        d_k, m_k, v_k = _adamw(as2d(w[k]), as2d(grads[k]), as2d(m[k]), as2d(v[k]), "adamw_" + k)
        delta[k], new_m[k], new_v[k] = d_k.reshape(w[k].shape), m_k.reshape(w[k].shape), v_k.reshape(w[k].shape)
    small_keys = _SMALL + ("conv_w",)
    join = lambda t: _pad_rows(jnp.concatenate([t[k].reshape(-1) for k in small_keys]))
    d_s, m_s, v_s = _adamw(join(w), join(gs), join(m), join(v), "adamw_small")
    off = 0
    for k in small_keys:
        n = int(np.prod(w[k].shape))
        grads[k] = gs[k]
        for dst, src in ((delta, d_s), (new_m, m_s), (new_v, v_s)):
            dst[k] = src.reshape(-1)[off:off + n].reshape(w[k].shape)
        off += n
    return (loss, grad_x[None], *[grads[k] for k in order], *[delta[k] for k in order],
            *[new_m[k] for k in order], *[new_v[k] for k in order])
```

```python
from typing import NamedTuple

import jax
import jax.numpy as jnp
import numpy as np
from jax import lax
from jax.experimental import pallas as pl
from jax.experimental.pallas import tpu as pltpu

F32 = jnp.float32
BF16 = jnp.bfloat16
EPS = 1e-6
GRID_W = 64
ROPE_THETA = 10000.0
CONV_KERNEL = 31
CONV_PAD = CONV_KERNEL // 2
HALO = 16
LANES = 128
SUBLANES = 8
ADAM_LR, ADAM_B1, ADAM_B2, ADAM_EPS, ADAM_WD, ADAM_STEP = 0.001, 0.9, 0.999, 1e-08, 0.01, 10
VMEM_LIMIT = 56 * 1024 * 1024
ADAM_BLOCK_BYTES = 1024 * 1024
LOG2E = 1.4426950408889634
LN2 = 0.6931471805599453
MESH = pl.DeviceIdType.MESH
N_CHIPS = 4
N_DEV = 8


class Dims(NamedTuple):
    N: int
    CTX: int
    R: int
    D: int
    C: int
    A: int
    KVW: int
    HD: int
    NH: int
    NKV: int
    G: int
    W: int
    TR: int


def _pick(n, prefs):
    for p in prefs:
        if n % p == 0:
            return p
    raise ValueError(f"no tile for {n} in {prefs}")


def _params(sem):
    return pltpu.CompilerParams(dimension_semantics=sem, vmem_limit_bytes=VMEM_LIMIT)


def _rows(tr, w, col=0):
    return pl.BlockSpec((tr, w), lambda i, col=col: (i, col))


def _whole(shape):
    return pl.BlockSpec(shape, lambda *_: (0,) * len(shape))


def _col(off, w):
    assert off % w == 0, (off, w)
    return off // w


_DN = {"nn": (((1,), (0,)), ((), ())), "nt": (((1,), (1,)), ((), ())), "tn": (((0,), (0,)), ((), ()))}


def _matmul(a, b, mode, name, out_dtype=F32):
    if mode == "nn":
        (m, k), n = a.shape, b.shape[1]
    elif mode == "nt":
        (m, k), n = a.shape, b.shape[0]
    else:
        (k, m), n = a.shape, b.shape[1]
    tm = _pick(m, (1408, 1024, 768, 512, 256, 128, 8))
    tn = _pick(n, (1280, 1024, 768, 512, 256, 128))
    tk = _pick(k, (2816, 1536, 1408, 1280, 1024, 768, 512, 256, 128))
    nk = k // tk
    dn = _DN[mode]

    own_acc = out_dtype != F32 and nk > 1

    def body(a_ref, b_ref, o_ref, *scratch):
        part = lax.dot_general(a_ref[...], b_ref[...], dn, preferred_element_type=F32)
        if nk == 1:
            o_ref[...] = part.astype(out_dtype)
            return
        acc_ref = scratch[0] if own_acc else o_ref
        kk = pl.program_id(2)

        @pl.when(kk == 0)
        def _():
            acc_ref[...] = part

        @pl.when(kk > 0)
        def _():
            acc_ref[...] += part

        if own_acc:
            @pl.when(kk == nk - 1)
            def _():
                o_ref[...] = acc_ref[...].astype(out_dtype)

    a_spec = pl.BlockSpec((tk, tm), lambda i, j, kk: (kk, i)) if mode == "tn" else pl.BlockSpec((tm, tk), lambda i, j, kk: (i, kk))
    b_spec = pl.BlockSpec((tn, tk), lambda i, j, kk: (j, kk)) if mode == "nt" else pl.BlockSpec((tk, tn), lambda i, j, kk: (kk, j))
    return pl.pallas_call(
        body, name=name, grid=(m // tm, n // tn, nk),
        in_specs=[a_spec, b_spec], out_specs=pl.BlockSpec((tm, tn), lambda i, j, kk: (i, j)),
        out_shape=jax.ShapeDtypeStruct((m, n), out_dtype),
        scratch_shapes=[pltpu.VMEM((tm, tn), F32)] if own_acc else [],
        compiler_params=_params(("parallel", "parallel", "arbitrary")),
    )(a, b)


def _rms(x, g):
    return x * lax.rsqrt(jnp.mean(x * x, axis=-1, keepdims=True) + EPS) * g


def _prenorm_fn(x, g, sc, sh):
    return _rms(x, g) * (1.0 + sc) + sh


def _post_fn(outp, g, gt):
    return gt * _rms(outp, g)


def _convact_fn(y1, gate, ln_g, ln_b):
    mu = jnp.mean(y1, axis=-1, keepdims=True)
    var = jnp.mean(jnp.square(y1 - mu), axis=-1, keepdims=True)
    y = (y1 - mu) * lax.rsqrt(var + EPS) * ln_g + ln_b
    return jax.nn.silu(y) * jax.nn.silu(gate)


def _dsilu(x):
    s = jax.nn.sigmoid(x)
    return s * (1.0 + x * (1.0 - s))


def _seg_select(is_ctx, mod_ref, lo, w):
    return jnp.where(is_ctx, mod_ref[1:2, lo:lo + w], mod_ref[0:1, lo:lo + w])


def _seg_rows(is_ctx, v):
    row = lax.broadcasted_iota(jnp.int32, (8, v.shape[1]), 0)
    return jnp.where(row == is_ctx.astype(jnp.int32), v, 0.0)


def _mod_fwd(cv, w_mod_l, b_mod_l):
    def body(cv_ref, w_ref, b_ref, o_ref):
        s = jax.nn.silu(cv_ref[...]).astype(BF16)
        o_ref[...] = jnp.dot(s, w_ref[...], preferred_element_type=F32) + b_ref[...]

    return pl.pallas_call(
        body, name="mod_fwd", out_shape=jax.ShapeDtypeStruct((8, w_mod_l.shape[1]), F32),
        compiler_params=pltpu.CompilerParams(vmem_limit_bytes=VMEM_LIMIT),
    )(cv, w_mod_l, b_mod_l)


def _mod_bwd(cv, w_mod_l, dmod):
    d = cv.shape[1]

    def body(cv_ref, w_ref, dm_ref, dw_ref, db_ref, dcv_ref):
        cvv = cv_ref[...]
        s = jax.nn.silu(cvv).astype(BF16)
        dm = dm_ref[...]
        dmb = dm.astype(BF16)
        dw_ref[...] = lax.dot_general(s, dmb, _DN["tn"], preferred_element_type=F32)
        db_ref[...] = jnp.sum(dm, axis=0, keepdims=True)
        ds = lax.dot_general(dmb, w_ref[...], _DN["nt"], preferred_element_type=F32)
        dcv_ref[...] = ds * _dsilu(cvv)

    return pl.pallas_call(
        body, name="mod_bwd",
        out_shape=(jax.ShapeDtypeStruct(w_mod_l.shape, F32), jax.ShapeDtypeStruct((1, w_mod_l.shape[1]), F32),
                   jax.ShapeDtypeStruct((8, d), F32)),
        compiler_params=pltpu.CompilerParams(vmem_limit_bytes=VMEM_LIMIT),
    )(cv, w_mod_l, dmod)


def _prenorm(dm, xc, g_pre_l, mod):
    tr, d, n = dm.TR, dm.D, dm.N

    def body(x_ref, g_ref, mod_ref, h_ref):
        is_ctx = pl.program_id(0) * tr >= n
        sh = _seg_select(is_ctx, mod_ref, 0, d)
        sc = _seg_select(is_ctx, mod_ref, d, d)
        h_ref[...] = _prenorm_fn(x_ref[...], g_ref[...], sc, sh).astype(BF16)

    return pl.pallas_call(
        body, name="prenorm", grid=(dm.R // tr,),
        in_specs=[_rows(tr, d), _whole((1, d)), _whole((8, 3 * d))], out_specs=_rows(tr, d),
        out_shape=jax.ShapeDtypeStruct((dm.R, d), BF16), compiler_params=_params(("parallel",)),
    )(xc, g_pre_l, mod)


def _prenorm_bwd(dm, xc, g_pre_l, mod, dh, dxn):
    tr, d, n = dm.TR, dm.D, dm.N

    def body(x_ref, g_ref, mod_ref, dh_ref, dxn_ref, dx_ref, dg_ref, dss_ref):
        i = pl.program_id(0)
        is_ctx = i * tr >= n
        sh = _seg_select(is_ctx, mod_ref, 0, d)
        sc = _seg_select(is_ctx, mod_ref, d, d)
        _, vjp = jax.vjp(_prenorm_fn, x_ref[...], g_ref[...], sc, sh)
        dx, dg, dsc, dsh = vjp(dh_ref[...].astype(F32))
        dx_ref[...] = dxn_ref[...] + dx

        @pl.when(i == 0)
        def _():
            dg_ref[...] = jnp.zeros_like(dg_ref)
            dss_ref[...] = jnp.zeros_like(dss_ref)

        dg_ref[...] += dg
        dss_ref[...] += _seg_rows(is_ctx, jnp.concatenate([dsh, dsc], axis=1))

    return pl.pallas_call(
        body, name="prenorm_bwd", grid=(dm.R // tr,),
        in_specs=[_rows(tr, d), _whole((1, d)), _whole((8, 3 * d)), _rows(tr, d), _rows(tr, d)],
        out_specs=(_rows(tr, d), _whole((1, d)), _whole((8, 2 * d))),
        out_shape=(jax.ShapeDtypeStruct((dm.R, d), F32), jax.ShapeDtypeStruct((1, d), F32),
                   jax.ShapeDtypeStruct((8, 2 * d), F32)),
        compiler_params=_params(("arbitrary",)),
    )(xc, g_pre_l, mod, dh, dxn)


def _post(dm, xc, outp, g_post_l, mod):
    tr, d, n = dm.TR, dm.D, dm.N

    def body(x_ref, o_ref, g_ref, mod_ref, y_ref):
        is_ctx = pl.program_id(0) * tr >= n
        gt = _seg_select(is_ctx, mod_ref, 2 * d, d)
        y_ref[...] = x_ref[...] + _post_fn(o_ref[...].astype(F32), g_ref[...], gt)

    return pl.pallas_call(
        body, name="post", grid=(dm.R // tr,),
        in_specs=[_rows(tr, d), _rows(tr, d), _whole((1, d)), _whole((8, 3 * d))], out_specs=_rows(tr, d),
        out_shape=jax.ShapeDtypeStruct((dm.R, d), F32), compiler_params=_params(("parallel",)),
    )(xc, outp, g_post_l, mod)


def _post_bwd(dm, outp, g_post_l, mod, dxn):
    tr, d, n = dm.TR, dm.D, dm.N

    def body(o_ref, g_ref, mod_ref, dxn_ref, do_ref, dg_ref, dgt_ref):
        i = pl.program_id(0)
        is_ctx = i * tr >= n
        gt = _seg_select(is_ctx, mod_ref, 2 * d, d)
        _, vjp = jax.vjp(_post_fn, o_ref[...].astype(F32), g_ref[...], gt)
        do, dg, dgt = vjp(dxn_ref[...])
        do_ref[...] = do.astype(BF16)

        @pl.when(i == 0)
        def _():
            dg_ref[...] = jnp.zeros_like(dg_ref)
            dgt_ref[...] = jnp.zeros_like(dgt_ref)

        dg_ref[...] += dg
        dgt_ref[...] += _seg_rows(is_ctx, dgt)

    return pl.pallas_call(
        body, name="post_bwd", grid=(dm.R // tr,),
        in_specs=[_rows(tr, d), _whole((1, d)), _whole((8, 3 * d)), _rows(tr, d)],
        out_specs=(_rows(tr, d), _whole((1, d)), _whole((8, d))),
        out_shape=(jax.ShapeDtypeStruct((dm.R, d), BF16), jax.ShapeDtypeStruct((1, d), F32),
                   jax.ShapeDtypeStruct((8, d), F32)),
        compiler_params=_params(("arbitrary",)),
    )(outp, g_post_l, mod, dxn)


def _group_sum(x, bd):
    hi = x.astype(BF16)
    lo = (x - hi.astype(F32)).astype(BF16)
    return jnp.dot(hi, bd, preferred_element_type=F32) + jnp.dot(lo, bd, preferred_element_type=F32)


def _swap_halves(x):
    lane = lax.broadcasted_iota(jnp.int32, x.shape, 1)
    return jnp.where((lane & 32) == 0, pltpu.roll(x, LANES - 32, 1), pltpu.roll(x, 32, 1))


def _qk_prep(dm, p, cos, sins, gq, gk, bd):
    tr, hd = dm.TR, dm.HD
    oq, ok, ov = 3 * dm.C, 3 * dm.C + dm.A, 3 * dm.C + dm.A + dm.KVW
    scale = hd ** -0.5 * LOG2E

    def body(q_ref, k_ref, v_ref, cos_ref, sin_ref, gq_ref, gk_ref, bd_ref, qh_ref, qt_ref, kh_ref, kt_ref, vh_ref, vx_ref):
        cosv, sinv, bdv = cos_ref[...], sin_ref[...], bd_ref[...]

        def chunk(x, g):
            rstd = lax.rsqrt(_group_sum(x * x, bdv) * (1.0 / hd) + EPS)
            xn = x * rstd * g
            return xn * cosv + _swap_halves(xn) * sinv

        for c in range(dm.A // LANES):
            y = chunk(q_ref[:, c * LANES:(c + 1) * LANES].astype(F32), gq_ref[...]) * scale
            qh_ref[2 * c] = y[:, :hd].astype(BF16)
            qh_ref[2 * c + 1] = y[:, hd:].astype(BF16)
            qt_ref[c * LANES:(c + 1) * LANES, :] = y.T.astype(BF16)
        for c in range(dm.KVW // LANES):
            y = chunk(k_ref[:, c * LANES:(c + 1) * LANES].astype(F32), gk_ref[...])
            kh_ref[2 * c] = y[:, :hd].astype(BF16)
            kh_ref[2 * c + 1] = y[:, hd:].astype(BF16)
            kt_ref[c * LANES:(c + 1) * LANES, :] = y.T.astype(BF16)
            v = v_ref[:, c * LANES:(c + 1) * LANES].astype(F32)
            vh_ref[2 * c] = v[:, :hd].astype(BF16)
            vh_ref[2 * c + 1] = v[:, hd:].astype(BF16)
            lane = lax.broadcasted_iota(jnp.int32, v.shape, 1)
            one_at_hd = (lane == hd).astype(F32)
            vx_ref[2 * c] = jnp.where(lane < hd, v, one_at_hd).astype(BF16)
            vx_ref[2 * c + 1] = jnp.where(lane < hd, pltpu.roll(v, hd, 1), one_at_hd).astype(BF16)

    hspec = lambda nh, w=hd: pl.BlockSpec((nh, tr, w), lambda i: (0, i, 0))
    return pl.pallas_call(
        body, name="qk_prep", grid=(dm.R // tr,),
        in_specs=[_rows(tr, dm.A, _col(oq, dm.A)), _rows(tr, dm.KVW, _col(ok, dm.KVW)), _rows(tr, dm.KVW, _col(ov, dm.KVW)),
                  _rows(tr, LANES), _rows(tr, LANES), _whole((1, LANES)), _whole((1, LANES)), _whole((LANES, LANES))],
        out_specs=(hspec(dm.NH), pl.BlockSpec((dm.A, tr), lambda i: (0, i)), hspec(dm.NKV),
                   pl.BlockSpec((dm.KVW, tr), lambda i: (0, i)), hspec(dm.NKV), hspec(dm.NKV, LANES)),
        out_shape=(jax.ShapeDtypeStruct((dm.NH, dm.R, hd), BF16), jax.ShapeDtypeStruct((dm.A, dm.R), BF16),
                   jax.ShapeDtypeStruct((dm.NKV, dm.R, hd), BF16), jax.ShapeDtypeStruct((dm.KVW, dm.R), BF16),
                   jax.ShapeDtypeStruct((dm.NKV, dm.R, hd), BF16), jax.ShapeDtypeStruct((dm.NKV, dm.R, LANES), BF16)),
        compiler_params=_params(("parallel",)),
    )(p, p, p, cos, sins, gq, gk, bd)


def _qk_bwd(dm, p, dq, dkt, dvt, cos, sins, gq, gk, bd):
    tr, hd = dm.TR, dm.HD
    oq, ok = 3 * dm.C, 3 * dm.C + dm.A
    scale = hd ** -0.5

    def body(q_ref, k_ref, dq_ref, dkh_ref, dvh_ref, cos_ref, sin_ref, gq_ref, gk_ref, bd_ref,
             dpq_ref, dpkv_ref, dgq_ref, dgk_ref):
        i = pl.program_id(0)
        cosv, sinv, bdv = cos_ref[...], sin_ref[...], bd_ref[...]

        def chunk(x, g, dy):
            rstd = lax.rsqrt(_group_sum(x * x, bdv) * (1.0 / hd) + EPS)
            xhat = x * rstd
            dxn = dy * cosv + _swap_halves(dy * sinv)
            dxhat = dxn * g
            dx = rstd * (dxhat - xhat * (_group_sum(dxhat * xhat, bdv) * (1.0 / hd)))
            return dx, jnp.sum(dxn * xhat, axis=0, keepdims=True)

        @pl.when(i == 0)
        def _():
            dgq_ref[...] = jnp.zeros_like(dgq_ref)
            dgk_ref[...] = jnp.zeros_like(dgk_ref)

        dgq = jnp.zeros((1, LANES), F32)
        for c in range(dm.A // LANES):
            sl = slice(c * LANES, (c + 1) * LANES)
            dx, dg = chunk(q_ref[:, sl].astype(F32), gq_ref[...], dq_ref[:, sl] * scale)
            dpq_ref[:, sl] = dx.astype(BF16)
            dgq = dgq + dg
        dgq_ref[...] += dgq + pltpu.roll(dgq, hd, 1)
        dgk = jnp.zeros((1, LANES), F32)
        for c in range(dm.KVW // LANES):
            sl = slice(c * LANES, (c + 1) * LANES)
            dx, dg = chunk(k_ref[:, sl].astype(F32), gk_ref[...], dkh_ref[sl, :].T)
            dpkv_ref[:, sl] = dx.astype(BF16)
            dgk = dgk + dg
            dpkv_ref[:, dm.KVW + c * LANES:dm.KVW + (c + 1) * LANES] = dvh_ref[sl, :].T.astype(BF16)
        dgk_ref[...] += dgk + pltpu.roll(dgk, hd, 1)

    hspec = pl.BlockSpec((dm.KVW, tr), lambda i: (0, i))
    return pl.pallas_call(
        body, name="qk_bwd", grid=(dm.R // tr,),
        in_specs=[_rows(tr, dm.A, _col(oq, dm.A)), _rows(tr, dm.KVW, _col(ok, dm.KVW)), _rows(tr, dm.A), hspec, hspec,
                  _rows(tr, LANES), _rows(tr, LANES), _whole((1, LANES)), _whole((1, LANES)), _whole((LANES, LANES))],
        out_specs=(_rows(tr, dm.A), _rows(tr, 2 * dm.KVW), _whole((1, LANES)), _whole((1, LANES))),
        out_shape=(jax.ShapeDtypeStruct((dm.R, dm.A), BF16), jax.ShapeDtypeStruct((dm.R, 2 * dm.KVW), BF16),
                   jax.ShapeDtypeStruct((1, LANES), F32), jax.ShapeDtypeStruct((1, LANES), F32)),
        compiler_params=_params(("arbitrary",)),
    )(p, p, dq, dkt, dvt, cos, sins, gq, gk, bd)


def _flash_fwd(dm, qh, kh, vx):
    tq, ks = dm.TR, dm.TR
    rs = min(tq, LANES)
    g, hd = dm.G, dm.HD
    nq, nq_lat = dm.R // tq, dm.N // tq
    n_chunks, ctx_chunks = dm.R // ks, dm.CTX // ks

    def body(q_ref, k_ref, v_ref, o_ref, lse_ref):
        qi = pl.program_id(1)

        def attend(chunks):
            for gg in range(g):
                for r in range(tq // rs):
                    rows = slice(r * rs, (r + 1) * rs)
                    q = q_ref[gg, rows, :]
                    m_prev = jnp.full((rs, LANES), -1e30, F32)
                    acc = jnp.zeros((rs, LANES), F32)
                    for kc in chunks:
                        keys = slice(kc * ks, (kc + 1) * ks)
                        s = lax.dot_general(q, k_ref[0, keys, :], _DN["nt"], preferred_element_type=F32)
                        m_new = jnp.maximum(m_prev, jnp.max(s, axis=-1, keepdims=True))
                        p = jnp.exp2(s - jnp.tile(m_new, (1, ks // LANES)))
                        acc = acc * jnp.exp2(m_prev - m_new) + jnp.dot(p.astype(BF16), v_ref[0, keys, :],
                                                                        preferred_element_type=F32)
                        m_prev = m_new
                    l = acc[:, hd:hd + 1]
                    o_ref[rows, gg * hd:(gg + 1) * hd] = acc[:, :hd] / l
                    lse_ref[gg, rows, :] = m_prev + jnp.log2(l)

        @pl.when(qi < nq_lat)
        def _():
            attend(range(n_chunks))

        @pl.when(qi >= nq_lat)
        def _():
            attend(range(n_chunks - ctx_chunks, n_chunks))

    return pl.pallas_call(
        body, name="flash_fwd", grid=(dm.NKV, nq),
        in_specs=[pl.BlockSpec((g, tq, hd), lambda h, qi: (h, qi, 0)), pl.BlockSpec((1, dm.R, hd), lambda h, qi: (h, 0, 0)),
                  pl.BlockSpec((1, dm.R, LANES), lambda h, qi: (h, 0, 0))],
        out_specs=(pl.BlockSpec((tq, g * hd), lambda h, qi: (qi, h)), pl.BlockSpec((g, tq, LANES), lambda h, qi: (h, qi, 0))),
        out_shape=(jax.ShapeDtypeStruct((dm.R, dm.A), F32), jax.ShapeDtypeStruct((dm.NH, dm.R, LANES), F32)),
        compiler_params=_params(("parallel", "parallel")),
    )(qh, kh, vx)


def _flash_bwd(dm, qh, qt, kh, kt, vh, doh, dot, lse, delta):
    tq, ks = dm.TR, dm.TR
    g, hd = dm.G, dm.HD
    nq, nq_lat = dm.R // tq, dm.N // tq
    n_chunks, ctx_chunks = dm.R // ks, dm.CTX // ks

    def body(q_ref, qt_ref, k_ref, kt_ref, v_ref, do_ref, dot_ref, lse_ref, dl_ref, dq_ref, dk_acc, dv_acc):
        qi = pl.program_id(1)

        @pl.when(qi == 0)
        def _():
            dk_acc[...] = jnp.zeros_like(dk_acc)
            dv_acc[...] = jnp.zeros_like(dv_acc)

        def attend(chunks):
            q = q_ref[...].reshape(g * tq, hd)
            do = do_ref[...].reshape(g * tq, hd)
            q_t = jnp.concatenate([qt_ref[gg * hd:(gg + 1) * hd, :] for gg in range(g)], axis=1)
            do_t = jnp.concatenate([dot_ref[gg * hd:(gg + 1) * hd, :] for gg in range(g)], axis=1)
            wide = lambda ref: jnp.tile(ref[...].reshape(g * tq, LANES), (1, ks // LANES))
            dq_t = jnp.zeros((hd, g * tq), F32)
            for kc in chunks:
                keys = slice(kc * ks, (kc + 1) * ks)
                k, v = k_ref[0, keys, :], v_ref[0, keys, :]
                s = lax.dot_general(q, k, _DN["nt"], preferred_element_type=F32)
                p = jnp.exp2(s - wide(lse_ref))
                dp = lax.dot_general(do, v, _DN["nt"], preferred_element_type=F32)
                ds = p * (dp - wide(dl_ref))
                dv_acc[:, keys] += jnp.dot(do_t, p.astype(BF16), preferred_element_type=F32)
                dk_acc[:, keys] += jnp.dot(q_t, ds.astype(BF16), preferred_element_type=F32)
                dq_t = dq_t + jnp.dot(kt_ref[:, keys], ds.T.astype(BF16), preferred_element_type=F32)
            for gg in range(g):
                dq_ref[:, gg * hd:(gg + 1) * hd] = dq_t[:, gg * tq:(gg + 1) * tq].T

        @pl.when(qi < nq_lat)
        def _():
            attend(range(n_chunks))

        @pl.when(qi >= nq_lat)
        def _():
            attend(range(n_chunks - ctx_chunks, n_chunks))

        @pl.when(qi == nq - 1)
        def _():
            dk_acc[...] = dk_acc[...] * LN2

    qspec = pl.BlockSpec((g, tq, hd), lambda h, qi: (h, qi, 0))
    tspec = pl.BlockSpec((g * hd, tq), lambda h, qi: (h, qi))
    kspec = pl.BlockSpec((1, dm.R, hd), lambda h, qi: (h, 0, 0))
    sspec = pl.BlockSpec((g, tq, LANES), lambda h, qi: (h, qi, 0))
    ospec = pl.BlockSpec((hd, dm.R), lambda h, qi: (h, 0))
    return pl.pallas_call(
        body, name="flash_bwd", grid=(dm.NKV, nq),
        in_specs=[qspec, tspec, kspec, ospec, kspec, qspec, tspec, sspec, sspec],
        out_specs=(pl.BlockSpec((tq, g * hd), lambda h, qi: (qi, h)), ospec, ospec),
        out_shape=(jax.ShapeDtypeStruct((dm.R, dm.A), F32), jax.ShapeDtypeStruct((dm.KVW, dm.R), F32),
                   jax.ShapeDtypeStruct((dm.KVW, dm.R), F32)),
        compiler_params=_params(("parallel", "arbitrary")),
    )(qh, qt, kh, kt, vh, doh, dot, lse, delta)


def _gate_b_specs(dm):
    ogb = 3 * dm.C + dm.A + 2 * dm.KVW
    half = dm.A // 2
    return [_rows(dm.TR, half, _col(ogb, half)), _rows(dm.TR, half, _col(ogb, half) + 1)]


def _attn_gate(dm, o, p):
    tr = dm.TR

    def body(o_ref, g0_ref, g1_ref, za_ref):
        gate = jnp.concatenate([g0_ref[...], g1_ref[...]], axis=1).astype(F32)
        za_ref[...] = (o_ref[...] * jax.nn.silu(gate)).astype(BF16)

    return pl.pallas_call(
        body, name="attn_gate", grid=(dm.R // tr,),
        in_specs=[_rows(tr, dm.A)] + _gate_b_specs(dm), out_specs=_rows(tr, dm.A),
        out_shape=jax.ShapeDtypeStruct((dm.R, dm.A), BF16), compiler_params=_params(("parallel",)),
    )(o, p, p)


def _attn_gate_bwd(dm, dza, o, p):
    tr, hd = dm.TR, dm.HD

    def body(dza_ref, o_ref, g0_ref, g1_ref, doh_ref, dot_ref, dl_ref, dgb_ref):
        gate = jnp.concatenate([g0_ref[...], g1_ref[...]], axis=1).astype(F32)
        dza, o = dza_ref[...].astype(F32), o_ref[...]
        do = dza * jax.nn.silu(gate)
        dgb_ref[...] = (dza * o * _dsilu(gate)).astype(BF16)
        dot_ref[...] = do.T.astype(BF16)
        prod = do * o
        for h in range(dm.NH):
            sl = slice(h * hd, (h + 1) * hd)
            doh_ref[h] = do[:, sl].astype(BF16)
            dl_ref[h] = jnp.broadcast_to(jnp.sum(prod[:, sl], axis=-1, keepdims=True), (tr, LANES))

    return pl.pallas_call(
        body, name="attn_gate_bwd", grid=(dm.R // tr,),
        in_specs=[_rows(tr, dm.A), _rows(tr, dm.A)] + _gate_b_specs(dm),
        out_specs=(pl.BlockSpec((dm.NH, tr, hd), lambda i: (0, i, 0)), pl.BlockSpec((dm.A, tr), lambda i: (0, i)),
                   pl.BlockSpec((dm.NH, tr, LANES), lambda i: (0, i, 0)), _rows(tr, dm.A)),
        out_shape=(jax.ShapeDtypeStruct((dm.NH, dm.R, hd), BF16), jax.ShapeDtypeStruct((dm.A, dm.R), BF16),
                   jax.ShapeDtypeStruct((dm.NH, dm.R, LANES), F32), jax.ShapeDtypeStruct((dm.R, dm.A), BF16)),
        compiler_params=_params(("parallel",)),
    )(dza, o, p, p)


def _halo_specs(dm, col):
    per = dm.TR // HALO
    last = dm.R // HALO - 1
    prev = pl.BlockSpec((HALO, dm.C), lambda i: (jnp.maximum(i * per - 1, 0), col))
    nxt = pl.BlockSpec((HALO, dm.C), lambda i: (jnp.minimum((i + 1) * per, last), col))
    return prev, nxt


def _seg_valid(dm, i):
    tr = dm.TR
    r = i * tr - HALO + lax.broadcasted_iota(jnp.int32, (tr + 2 * HALO, 1), 0)
    is_ctx = i * tr >= dm.N
    lo = jnp.where(is_ctx, dm.N, 0)
    hi = jnp.where(is_ctx, dm.R, dm.N)
    return jnp.logical_and(r >= lo, r < hi)


CONV_RB = 64


def _shifted_copies(src_ref, sl, sh_ref, tr):
    rows = tr + 2 * HALO - SUBLANES
    padded = src_ref[:, sl]
    for ph in range(SUBLANES):
        sh_ref[ph, 0:rows, :] = padded[ph:ph + rows]


def _tap_rows(sh_ref, off, n):
    ph = off % SUBLANES
    return sh_ref[ph, off - ph:off - ph + n, :]


def _conv_fwd(dm, p, conv_w_l, conv_b_l, ln_g_l, ln_b_l):
    tr, c = dm.TR, dm.C

    def body(a_ref, g_ref, gate_ref, ap_ref, an_ref, gp_ref, gn_ref, w_ref, b_ref, lg_ref, lb_ref,
             zc_ref, y1_ref, u_ref, sh_ref):
        i = pl.program_id(0)
        a = jnp.concatenate([ap_ref[...], a_ref[...], an_ref[...]], axis=0).astype(F32)
        g = jnp.concatenate([gp_ref[...], g_ref[...], gn_ref[...]], axis=0).astype(F32)
        u_ref[...] = jnp.where(_seg_valid(dm, i), a * jax.nn.sigmoid(g), 0.0)
        for cc in range(c // LANES):
            sl = slice(cc * LANES, (cc + 1) * LANES)
            _shifted_copies(u_ref, sl, sh_ref, tr)
            for r0 in range(0, tr, CONV_RB):
                acc = jnp.zeros((CONV_RB, LANES), F32)
                for j in range(CONV_KERNEL):
                    acc = acc + w_ref[j:j + 1, sl] * _tap_rows(sh_ref, r0 + HALO - CONV_PAD + j, CONV_RB)
                y1_ref[r0:r0 + CONV_RB, sl] = acc + b_ref[:, sl]
        zc_ref[...] = _convact_fn(y1_ref[...], gate_ref[...].astype(F32), lg_ref[...], lb_ref[...]).astype(BF16)

    ap, an = _halo_specs(dm, 0)
    gp, gn = _halo_specs(dm, 1)
    return pl.pallas_call(
        body, name="conv_fwd", grid=(dm.R // tr,),
        in_specs=[_rows(tr, c, 0), _rows(tr, c, 1), _rows(tr, c, 2), ap, an, gp, gn,
                  _whole((32, c)), _whole((1, c)), _whole((1, c)), _whole((1, c))],
        out_specs=(_rows(tr, c), _rows(tr, c)),
        out_shape=(jax.ShapeDtypeStruct((dm.R, c), BF16), jax.ShapeDtypeStruct((dm.R, c), F32)),
        scratch_shapes=[pltpu.VMEM((tr + 2 * HALO, c), F32), pltpu.VMEM((SUBLANES, tr + 2 * HALO, LANES), F32)],
        compiler_params=_params(("parallel",)),
    )(p, p, p, p, p, p, p, conv_w_l, conv_b_l, ln_g_l, ln_b_l)


def _conv_act_bwd(dm, dzc, y1, p, ln_g_l, ln_b_l):
    tr, c = dm.TR, dm.C

    def body(dz_ref, y1_ref, gate_ref, lg_ref, lb_ref, dy_ref, dgate_ref, dlg_ref, dlb_ref):
        i = pl.program_id(0)
        _, vjp = jax.vjp(_convact_fn, y1_ref[...], gate_ref[...].astype(F32), lg_ref[...], lb_ref[...])
        dy, dgate, dlg, dlb = vjp(dz_ref[...].astype(F32))
        dy_ref[...] = dy
        dgate_ref[...] = dgate.astype(BF16)

        @pl.when(i == 0)
        def _():
            dlg_ref[...] = jnp.zeros_like(dlg_ref)
            dlb_ref[...] = jnp.zeros_like(dlb_ref)

        dlg_ref[...] += dlg
        dlb_ref[...] += dlb

    return pl.pallas_call(
        body, name="conv_act_bwd", grid=(dm.R // tr,),
        in_specs=[_rows(tr, c), _rows(tr, c), _rows(tr, c, 2), _whole((1, c)), _whole((1, c))],
        out_specs=(_rows(tr, c), _rows(tr, c), _whole((1, c)), _whole((1, c))),
        out_shape=(jax.ShapeDtypeStruct((dm.R, c), F32), jax.ShapeDtypeStruct((dm.R, c), BF16),
                   jax.ShapeDtypeStruct((1, c), F32), jax.ShapeDtypeStruct((1, c), F32)),
        compiler_params=_params(("arbitrary",)),
    )(dzc, y1, p, ln_g_l, ln_b_l)


def _conv_bwd(dm, dy1, p, conv_w_l):
    tr, c = dm.TR, dm.C

    def body(dy_ref, dyp_ref, dyn_ref, a_ref, g_ref, ap_ref, an_ref, gp_ref, gn_ref, w_ref,
             dag_ref, dw_ref, db_ref, u_ref, dyp_scr, du_ref, dwp_ref, shu_ref, shd_ref):
        i = pl.program_id(0)
        valid = _seg_valid(dm, i)
        a = jnp.concatenate([ap_ref[...], a_ref[...], an_ref[...]], axis=0).astype(F32)
        g = jnp.concatenate([gp_ref[...], g_ref[...], gn_ref[...]], axis=0).astype(F32)
        u_ref[...] = jnp.where(valid, a * jax.nn.sigmoid(g), 0.0)
        dyp_scr[...] = jnp.where(valid, jnp.concatenate([dyp_ref[...], dy_ref[...], dyn_ref[...]], axis=0), 0.0)

        @pl.when(i == 0)
        def _():
            dwp_ref[...] = jnp.zeros_like(dwp_ref)
            db_ref[...] = jnp.zeros_like(db_ref)

        db_ref[...] += jnp.sum(dy_ref[...], axis=0, keepdims=True)
        for cc in range(c // LANES):
            sl = slice(cc * LANES, (cc + 1) * LANES)
            _shifted_copies(u_ref, sl, shu_ref, tr)
            _shifted_copies(dyp_scr, sl, shd_ref, tr)
            for r0 in range(0, tr, CONV_RB):
                dy_cur = dy_ref[r0:r0 + CONV_RB, sl]
                acc = jnp.zeros((CONV_RB, LANES), F32)
                for j in range(CONV_KERNEL):
                    acc = acc + w_ref[j:j + 1, sl] * _tap_rows(shd_ref, r0 + HALO + CONV_PAD - j, CONV_RB)
                    prod = dy_cur * _tap_rows(shu_ref, r0 + HALO - CONV_PAD + j, CONV_RB)
                    dwp_ref[j, :, sl] += jnp.sum(prod.reshape(CONV_RB // SUBLANES, SUBLANES, LANES), axis=0)
                du_ref[r0:r0 + CONV_RB, sl] = acc
        du = du_ref[...]
        sg = jax.nn.sigmoid(g_ref[...].astype(F32))
        dag_ref[:, :c] = (du * sg).astype(BF16)
        dag_ref[:, c:] = (du * a_ref[...].astype(F32) * sg * (1.0 - sg)).astype(BF16)

        @pl.when(i == dm.R // tr - 1)
        def _():
            dw_ref[...] = jnp.sum(dwp_ref[...], axis=1)

    ap, an = _halo_specs(dm, 0)
    gp, gn = _halo_specs(dm, 1)
    dyp, dyn = _halo_specs(dm, 0)
    return pl.pallas_call(
        body, name="conv_bwd", grid=(dm.R // tr,),
        in_specs=[_rows(tr, c), dyp, dyn, _rows(tr, c, 0), _rows(tr, c, 1), ap, an, gp, gn, _whole((32, c))],
        out_specs=(_rows(tr, 2 * c), _whole((32, c)), _whole((1, c))),
        out_shape=(jax.ShapeDtypeStruct((dm.R, 2 * c), BF16), jax.ShapeDtypeStruct((32, c), F32),
                   jax.ShapeDtypeStruct((1, c), F32)),
        scratch_shapes=[pltpu.VMEM((tr + 2 * HALO, c), F32), pltpu.VMEM((tr + 2 * HALO, c), F32), pltpu.VMEM((tr, c), F32),
                        pltpu.VMEM((32, SUBLANES, c), F32), pltpu.VMEM((SUBLANES, tr + 2 * HALO, LANES), F32),
                        pltpu.VMEM((SUBLANES, tr + 2 * HALO, LANES), F32)],
        compiler_params=_params(("arbitrary",)),
    )(dy1, dy1, dy1, p, p, p, p, p, p, conv_w_l)


def _gm_specs(dm):
    half = dm.D // 2
    first = _col(3 * dm.C + 2 * dm.A + 2 * dm.KVW, half)
    return [_rows(dm.TR, half, first + k) for k in range(4)]


def _gm_load(a0, a1, b0, b1):
    return (jnp.concatenate([a0[...], a1[...]], axis=1).astype(F32),
            jnp.concatenate([b0[...], b1[...]], axis=1).astype(F32))


def _merge(dm, yc, ya, p):
    tr, d = dm.TR, dm.D

    def body(yc_ref, ya_ref, a0, a1, b0, b1, m_ref):
        ga, gb = _gm_load(a0, a1, b0, b1)
        yc, ya = yc_ref[...].astype(F32), ya_ref[...].astype(F32)
        m_ref[...] = (jax.nn.sigmoid(ga) * yc + jax.nn.sigmoid(gb) * ya).astype(BF16)

    return pl.pallas_call(
        body, name="merge", grid=(dm.R // tr,),
        in_specs=[_rows(tr, d), _rows(tr, d)] + _gm_specs(dm), out_specs=_rows(tr, d),
        out_shape=jax.ShapeDtypeStruct((dm.R, d), BF16), compiler_params=_params(("parallel",)),
    )(yc, ya, p, p, p, p)


def _merge_bwd(dm, dmerged, yc, ya, p):
    tr, d = dm.TR, dm.D

    def body(dm_ref, yc_ref, ya_ref, a0, a1, b0, b1, dyc_ref, dya_ref, dgm_ref):
        dmv = dm_ref[...].astype(F32)
        ga, gb = _gm_load(a0, a1, b0, b1)
        sa, sb = jax.nn.sigmoid(ga), jax.nn.sigmoid(gb)
        dyc_ref[...] = (dmv * sa).astype(BF16)
        dya_ref[...] = (dmv * sb).astype(BF16)
        dgm_ref[:, :d] = (dmv * yc_ref[...].astype(F32) * sa * (1.0 - sa)).astype(BF16)
        dgm_ref[:, d:] = (dmv * ya_ref[...].astype(F32) * sb * (1.0 - sb)).astype(BF16)

    return pl.pallas_call(
        body, name="merge_bwd", grid=(dm.R // tr,),
        in_specs=[_rows(tr, d), _rows(tr, d), _rows(tr, d)] + _gm_specs(dm),
        out_specs=(_rows(tr, d), _rows(tr, d), _rows(tr, 2 * d)),
        out_shape=(jax.ShapeDtypeStruct((dm.R, d), BF16), jax.ShapeDtypeStruct((dm.R, d), BF16),
                   jax.ShapeDtypeStruct((dm.R, 2 * d), BF16)),
        compiler_params=_params(("parallel",)),
    )(dmerged, yc, ya, p, p, p, p)


def _loss_head(dm, xc, target):
    tr, d = dm.TR, dm.D
    n_lat = dm.N // tr

    def body(x_ref, t_ref, dy_ref, l_ref):
        i = pl.program_id(0)

        @pl.when(i == 0)
        def _():
            l_ref[...] = jnp.zeros_like(l_ref)

        @pl.when(i < n_lat)
        def _():
            e = x_ref[...] - t_ref[...]
            dy_ref[...] = e * (1.0 / d)
            l_ref[...] += 0.5 * jnp.sum(jnp.mean(e * e, axis=-1, keepdims=True), axis=0, keepdims=True)

        @pl.when(i >= n_lat)
        def _():
            dy_ref[...] = jnp.zeros_like(dy_ref)

    return pl.pallas_call(
        body, name="loss_head", grid=(dm.R // tr,),
        in_specs=[_rows(tr, d), pl.BlockSpec((tr, d), lambda i: (jnp.minimum(i, n_lat - 1), 0))],
        out_specs=(_rows(tr, d), _whole((1, 1))),
        out_shape=(jax.ShapeDtypeStruct((dm.R, d), F32), jax.ShapeDtypeStruct((1, 1), F32)),
        compiler_params=_params(("arbitrary",)),
    )(xc, target)


def _adamw(w, g, m, v, name):
    rows, cols = w.shape
    fits = lambda t: t * cols * 4 <= ADAM_BLOCK_BYTES
    tr = rows if fits(rows) else _pick(rows, [t for t in (512, 256, 128, 64, 32, 16, 8) if fits(t)])

    def body(w_ref, g_ref, m_ref, v_ref, d_ref, mo_ref, vo_ref):
        d_ref[...], mo_ref[...], vo_ref[...] = _adam_update(w_ref[...], g_ref[...], m_ref[...], v_ref[...])

    spec = _rows(tr, cols)
    shp = jax.ShapeDtypeStruct(w.shape, F32)
    return pl.pallas_call(
        body, name=name, grid=(rows // tr,), in_specs=[spec] * 4, out_specs=(spec,) * 3, out_shape=(shp,) * 3,
        compiler_params=_params(("parallel",)),
    )(w, g, m, v)


def _adam_update(w, g, m, v):
    mn = ADAM_B1 * m + (1.0 - ADAM_B1) * g
    vn = ADAM_B2 * v + (1.0 - ADAM_B2) * jnp.square(g)
    m_hat = mn / (1.0 - ADAM_B1 ** ADAM_STEP)
    v_hat = vn / (1.0 - ADAM_B2 ** ADAM_STEP)
    return -ADAM_LR * (m_hat / (jnp.sqrt(v_hat) + ADAM_EPS) + ADAM_WD * w), mn, vn


def _rope_tables(dm):
    n = dm.N
    rows = n // GRID_W
    row = jnp.repeat(jnp.arange(rows, dtype=F32), GRID_W)
    col = jnp.tile(jnp.arange(GRID_W, dtype=F32), rows)
    half = dm.HD // 2
    inv_freq = ROPE_THETA ** (-jnp.arange(0, half, 2, dtype=F32) / half)
    ang = jnp.concatenate([row[:, None] * inv_freq, col[:, None] * inv_freq], axis=-1)
    cos, sin = jnp.cos(ang), jnp.sin(ang)
    cos = jnp.concatenate([cos, jnp.ones((dm.CTX, half), F32)], axis=0)
    sin = jnp.concatenate([sin, jnp.zeros((dm.CTX, half), F32)], axis=0)
    reps = LANES // dm.HD
    return jnp.tile(jnp.concatenate([cos, cos], axis=1), (1, reps)), jnp.tile(jnp.concatenate([-sin, sin], axis=1), (1, reps))


def _lane_vec(g, hd):
    return jnp.tile(g.reshape(1, hd), (1, LANES // hd))


def _layer_fwd(dm, xc, cv, wl, tabs):
    mod = _mod_fwd(cv, wl["w_mod"], wl["b_mod"])
    h = _prenorm(dm, xc, wl["g_pre"], mod)
    p = _matmul(h, wl["w_in"], "nn", "in_proj", BF16)
    qh, qt, kh, kt, vh, vx = _qk_prep(dm, p, tabs["cos"], tabs["sins"], wl["gq"], wl["gk"], tabs["bd"])
    o, lse = _flash_fwd(dm, qh, kh, vx)
    za = _attn_gate(dm, o, p)
    zc, y1 = _conv_fwd(dm, p, wl["conv_w"], wl["conv_b"], wl["ln_g"], wl["ln_b"])
    yc = _matmul(zc, wl["w_conv_out"], "nn", "conv_out", BF16)
    ya = _matmul(za, wl["w_attn_out"], "nn", "attn_out", BF16)
    merged = _merge(dm, yc, ya, p)
    outp = _matmul(merged, wl["w_out"], "nn", "out_proj", BF16)
    x_new = _post(dm, xc, outp, wl["g_post"], mod)
    saved = dict(xc=xc, mod=mod, h=h, p=p, qh=qh, qt=qt, kh=kh, kt=kt, vh=vh, o=o, lse=lse, za=za, zc=zc, y1=y1, yc=yc, ya=ya,
                 merged=merged, outp=outp)
    return x_new, saved


def _layer_bwd(dm, dxn, cv, wl, tabs, s):
    p = s["p"]
    d_outp, dg_post, dgt = _post_bwd(dm, s["outp"], wl["g_post"], s["mod"], dxn)
    dmerged = _matmul(d_outp, wl["w_out"], "nt", "out_proj_dx", BF16)
    dw_out = _matmul(s["merged"], d_outp, "tn", "out_proj_dw")
    dyc, dya, dgm = _merge_bwd(dm, dmerged, s["yc"], s["ya"], p)
    dzc = _matmul(dyc, wl["w_conv_out"], "nt", "conv_out_dx", BF16)
    dw_conv_out = _matmul(s["zc"], dyc, "tn", "conv_out_dw")
    dza = _matmul(dya, wl["w_attn_out"], "nt", "attn_out_dx", BF16)
    dw_attn_out = _matmul(s["za"], dya, "tn", "attn_out_dw")
    doh, dot, delta, dgate_b = _attn_gate_bwd(dm, dza, s["o"], p)
    dq, dkt, dvt = _flash_bwd(dm, s["qh"], s["qt"], s["kh"], s["kt"], s["vh"], doh, dot, s["lse"], delta)
    dpq, dpkv, dgq, dgk = _qk_bwd(dm, p, dq, dkt, dvt, tabs["cos"], tabs["sins"], wl["gq"], wl["gk"], tabs["bd"])
    dy1, dgate_a, dln_g, dln_b = _conv_act_bwd(dm, dzc, s["y1"], p, wl["ln_g"], wl["ln_b"])
    dpag, dconv_w, dconv_b = _conv_bwd(dm, dy1, p, wl["conv_w"])
    dp = jnp.concatenate([dpag, dgate_a, dpq, dpkv, dgate_b, dgm], axis=1)
    dh = _matmul(dp, wl["w_in"], "nt", "in_proj_dx", BF16)
    dw_in = _matmul(s["h"], dp, "tn", "in_proj_dw")
    dxc, dg_pre, dshsc = _prenorm_bwd(dm, s["xc"], wl["g_pre"], s["mod"], dh, dxn)
    dmod = jnp.concatenate([dshsc, dgt], axis=1)
    dw_mod, db_mod, dcv = _mod_bwd(cv, wl["w_mod"], dmod)
    grads = dict(w_mod=dw_mod, b_mod=db_mod[0], g_pre=dg_pre[0], g_post=dg_post[0], w_in=dw_in,
                 conv_w=dconv_w[:CONV_KERNEL], conv_b=dconv_b[0], ln_g=dln_g[0], ln_b=dln_b[0],
                 w_conv_out=dw_conv_out, q_norm_g=dgq[0, :dm.HD], k_norm_g=dgk[0, :dm.HD],
                 w_attn_out=dw_attn_out, w_out=dw_out, c_ctx=dcv[1])
    return dxc, grads


def _local_step(dm, x, c, ctx, c_ctx, target, wf):
    depth = wf["w_in"].shape[0]
    cos, sins = _rope_tables(dm)
    lane = np.arange(LANES)
    bd = jnp.asarray((lane[:, None] // dm.HD == lane[None, :] // dm.HD), dtype=BF16)
    tabs = dict(cos=cos, sins=sins, bd=bd)
    cv = jnp.concatenate([c.reshape(1, dm.D), c_ctx.reshape(1, dm.D), jnp.zeros((6, dm.D), F32)], axis=0)
    xc = jnp.concatenate([x, ctx], axis=0)
    layers, saved = [], []
    for l in range(depth):
        wl = dict(
            w_mod=wf["w_mod"][l], b_mod=wf["b_mod"][l].reshape(1, -1), g_pre=wf["g_pre"][l].reshape(1, -1),
            g_post=wf["g_post"][l].reshape(1, -1), w_in=wf["w_in"][l],
            conv_w=jnp.concatenate([wf["conv_w"][l], jnp.zeros((1, dm.C), F32)], axis=0),
            conv_b=wf["conv_b"][l].reshape(1, -1), ln_g=wf["ln_g"][l].reshape(1, -1), ln_b=wf["ln_b"][l].reshape(1, -1),
            w_conv_out=wf["w_conv_out"][l], gq=_lane_vec(wf["q_norm_g"][l], dm.HD), gk=_lane_vec(wf["k_norm_g"][l], dm.HD),
            w_attn_out=wf["w_attn_out"][l], w_out=wf["w_out"][l])
        layers.append(wl)
        xc, s = _layer_fwd(dm, xc, cv, wl, tabs)
        saved.append(s)
    dxc, loss = _loss_head(dm, xc, target)
    grads = [None] * depth
    for l in reversed(range(depth)):
        dxc, grads[l] = _layer_bwd(dm, dxc, cv, layers[l], tabs, saved[l])
    big = ("w_mod", "w_in", "w_conv_out", "w_attn_out", "w_out")
    gw = {k: [g[k] for g in grads] if k in big else jnp.stack([g[k] for g in grads]) for k in grads[0] if k != "c_ctx"}
    gw["c_ctx"] = grads[0]["c_ctx"] + grads[1]["c_ctx"] if depth == 2 else sum(g["c_ctx"] for g in grads)
    return loss[0, 0], dxc[:dm.N], gw


def _dims(x, ctx, w_in_full_cols, conv_c, attn_w, hd):
    n, d = x.shape
    ctx_len = ctx.shape[0]
    c, a = conv_c, attn_w
    kvw = (w_in_full_cols - 3 * c - 2 * a - 2 * d) // 2
    r = n + ctx_len
    tr = _pick(np.gcd(n, ctx_len), (256, 128))
    return Dims(N=n, CTX=ctx_len, R=r, D=d, C=c, A=a, KVW=kvw, HD=hd, NH=a // hd, NKV=kvw // hd,
                G=a // kvw, W=w_in_full_cols, TR=tr)


_HBM = pl.BlockSpec(memory_space=pltpu.HBM)
_VMEM = pl.BlockSpec(memory_space=pltpu.VMEM)


def _place():
    x, y, c = lax.axis_index("x"), lax.axis_index("y"), lax.axis_index("c")
    chips = [(1 - x, y), (x, 1 - y), (1 - x, 1 - y)]
    return x, y, c, chips


def _row_pieces(rows, want):
    n = max(k for k in range(1, want + 1) if rows % (16 * k) == 0)
    return [pl.ds(i * (rows // n), rows // n) for i in range(n)]


def _allgather_chips(own, name):
    _, h, w = own.shape
    pieces = _row_pieces(h, 4)
    npc = len(pieces)

    def body(own_ref, out_ref, send_sems, recv_sems):
        x, y, c, chips = _place()
        s_me = 2 * x + y
        sibling = (x, y, 1 - c)

        def copy(k, src, dst, to):
            return pltpu.make_async_remote_copy(src_ref=src, dst_ref=dst, send_sem=send_sems.at[k], recv_sem=recv_sems.at[k],
                                                device_id=to, device_id_type=MESH)

        sends = []
        for i, rows in enumerate(pieces):
            for j, chip in enumerate(chips):
                sends.append(copy(i * 3 + j, own_ref.at[c, rows], out_ref.at[s_me, c, rows], (*chip, c)))
        own_slots = [(6 * npc + hf * npc + i, hf, rows) for hf in range(2) for i, rows in enumerate(pieces)]
        for k, hf, rows in own_slots:
            sends.append(copy(k, own_ref.at[hf, rows], out_ref.at[s_me, hf, rows], sibling))
        for cp in sends:
            cp.start()
        for i, rows in enumerate(pieces):
            for j, (cx, cy) in enumerate(chips):
                landed = out_ref.at[2 * cx + cy, c, rows]
                copy(i * 3 + j, landed, landed, (cx, cy, c)).wait_recv()
                fwd = copy(3 * npc + i * 3 + j, landed, landed, sibling)
                fwd.start()
                sends.append(fwd)
        for i, rows in enumerate(pieces):
            for j, (cx, cy) in enumerate(chips):
                theirs = out_ref.at[2 * cx + cy, 1 - c, rows]
                copy(3 * npc + i * 3 + j, theirs, theirs, sibling).wait_recv()
        for k, hf, rows in own_slots:
            mine = out_ref.at[s_me, hf, rows]
            copy(k, mine, mine, sibling).wait_recv()
        for cp in sends:
            cp.wait_send()

    return pl.pallas_call(
        body, name=name, in_specs=[_HBM], out_specs=_HBM,
        out_shape=jax.ShapeDtypeStruct((N_CHIPS, 2, h, w), own.dtype),
        scratch_shapes=[pltpu.SemaphoreType.DMA((8 * npc,)), pltpu.SemaphoreType.DMA((8 * npc,))],
    )(own)


def _pair_exchange(buf, name):
    _, m, w = buf.shape
    pieces = _row_pieces(m, 16)

    def body(buf_ref, out_ref, send_sems, recv_sems):
        x, y, c, _ = _place()
        cps = [pltpu.make_async_remote_copy(src_ref=buf_ref.at[1 - c, rows], dst_ref=out_ref.at[rows], send_sem=send_sems.at[i],
                                            recv_sem=recv_sems.at[i], device_id=(x, y, 1 - c), device_id_type=MESH)
               for i, rows in enumerate(pieces)]
        for cp in cps:
            cp.start()
        for cp in cps:
            cp.wait()

    return pl.pallas_call(
        body, name=name, in_specs=[_HBM], out_specs=_HBM, out_shape=jax.ShapeDtypeStruct((m, w), buf.dtype),
        scratch_shapes=[pltpu.SemaphoreType.DMA((len(pieces),)), pltpu.SemaphoreType.DMA((len(pieces),))],
    )(buf)


def _pair_swap(t, name):
    m, w = t.shape
    pieces = _row_pieces(m, 8)
    n = len(pieces)

    def body(t_ref, out_ref, send_sems, recv_sems):
        x, y, c, _ = _place()
        cps = [pltpu.make_async_remote_copy(src_ref=t_ref.at[rows], dst_ref=out_ref.at[rows], send_sem=send_sems.at[i],
                                            recv_sem=recv_sems.at[i], device_id=(x, y, 1 - c), device_id_type=MESH)
               for i, rows in enumerate(pieces)]
        for cp in cps:
            cp.start()
        for cp in cps:
            cp.wait()

    return pl.pallas_call(
        body, name=name, in_specs=[_HBM], out_specs=_HBM, out_shape=jax.ShapeDtypeStruct((m, w), t.dtype),
        scratch_shapes=[pltpu.SemaphoreType.DMA((n,)), pltpu.SemaphoreType.DMA((n,))],
    )(t)


def _scatter_chips(s, name):
    _, h, w = s.shape

    def body(s_ref, out_ref, send_sems, recv_sems):
        _, _, c, chips = _place()
        cps = [pltpu.make_async_remote_copy(src_ref=s_ref.at[2 * cx + cy], dst_ref=out_ref.at[j], send_sem=send_sems.at[j],
                                            recv_sem=recv_sems.at[j], device_id=(cx, cy, c), device_id_type=MESH)
               for j, (cx, cy) in enumerate(chips)]
        for cp in cps:
            cp.start()
        for cp in cps:
            cp.wait()

    return pl.pallas_call(
        body, name=name, in_specs=[_HBM], out_specs=_HBM, out_shape=jax.ShapeDtypeStruct((3, h, w), s.dtype),
        scratch_shapes=[pltpu.SemaphoreType.DMA((3,)), pltpu.SemaphoreType.DMA((3,))],
    )(s)


def _sum_pair(buf, recv, core, name):
    m, w = recv.shape
    tr = _pick(m, (512, 432, 256, 128, 64, 32, 16))

    def body(c_ref, b_ref, r_ref, o_ref, ob_ref):
        v = b_ref[0] + r_ref[...]
        o_ref[...] = v
        ob_ref[...] = v.astype(BF16)

    spec = pl.BlockSpec((tr, w), lambda i, c: (i, 0))
    return pl.pallas_call(
        body, name=name,
        grid_spec=pltpu.PrefetchScalarGridSpec(
            num_scalar_prefetch=1, grid=(m // tr,),
            in_specs=[pl.BlockSpec((1, tr, w), lambda i, c: (c[0], i, 0)), spec], out_specs=(spec, spec)),
        out_shape=(jax.ShapeDtypeStruct((m, w), F32), jax.ShapeDtypeStruct((m, w), BF16)),
        compiler_params=_params(("parallel",)),
    )(core, buf, recv)


def _sum_chips(own, recv, chip, name):
    _, h, w = own.shape
    tr = _pick(h, (512, 432, 256, 128, 64, 32, 16))

    def body(s_ref, a_ref, r_ref, o_ref):
        acc = a_ref[0]
        for j in range(3):
            acc = acc + r_ref[j].astype(F32)
        o_ref[...] = acc

    return pl.pallas_call(
        body, name=name,
        grid_spec=pltpu.PrefetchScalarGridSpec(
            num_scalar_prefetch=1, grid=(h // tr,),
            in_specs=[pl.BlockSpec((1, tr, w), lambda i, s: (s[0], i, 0)), pl.BlockSpec((3, tr, w), lambda i, s: (0, i, 0))],
            out_specs=pl.BlockSpec((tr, w), lambda i, s: (i, 0))),
        out_shape=jax.ShapeDtypeStruct((h, w), F32), compiler_params=_params(("parallel",)),
    )(chip, own, recv)


def _allreduce_small(own, name):
    m, w = own.shape

    def body(own_ref, sum_ref, all_ref, send_sems, recv_sems):
        x, y, c, _ = _place()
        me = 4 * x + 2 * y + c
        all_ref[me] = own_ref[...]
        sends = []
        for k in range(1, N_DEV):
            peer = (x ^ ((k >> 2) & 1), y ^ ((k >> 1) & 1), c ^ (k & 1))
            cp = pltpu.make_async_remote_copy(src_ref=own_ref, dst_ref=all_ref.at[me], send_sem=send_sems.at[k - 1],
                                              recv_sem=recv_sems.at[k - 1], device_id=peer, device_id_type=MESH)
            cp.start()
            sends.append(cp)
        for k in range(1, N_DEV):
            px, py, pc = x ^ ((k >> 2) & 1), y ^ ((k >> 1) & 1), c ^ (k & 1)
            slot = all_ref.at[4 * px + 2 * py + pc]
            pltpu.make_async_remote_copy(src_ref=slot, dst_ref=slot, send_sem=send_sems.at[k - 1], recv_sem=recv_sems.at[k - 1],
                                         device_id=(px, py, pc), device_id_type=MESH).wait_recv()
        for cp in sends:
            cp.wait_send()
        acc = all_ref[0]
        for d in range(1, N_DEV):
            acc = acc + all_ref[d]
        sum_ref[...] = acc

    return pl.pallas_call(
        body, name=name, in_specs=[_VMEM], out_specs=_VMEM, out_shape=jax.ShapeDtypeStruct((m, w), F32),
        scratch_shapes=[pltpu.VMEM((N_DEV, m, w), F32), pltpu.SemaphoreType.DMA((N_DEV - 1,)), pltpu.SemaphoreType.DMA((N_DEV - 1,))],
        compiler_params=pltpu.CompilerParams(vmem_limit_bytes=VMEM_LIMIT),
    )(own)


_COL_SHARDED = ("w_mod", "w_in")
_ROW_SHARDED = ("w_conv_out", "w_attn_out", "w_out")
_BIG = _COL_SHARDED + _ROW_SHARDED
_SMALL = ("c_ctx", "b_mod", "g_pre", "g_post", "conv_b", "ln_g", "ln_b", "q_norm_g", "k_norm_g")
FLAT_W = 1024


def _unflatten_shard(flat, like):
    out, off = {}, 0
    flat = flat.reshape(-1)
    for k in _BIG:
        n = int(np.prod(like[k].shape))
        out[k] = flat[off:off + n].reshape(like[k].shape)
        off += n
    return out


def _by_shard(k, layers):
    parts = []
    for full in layers:
        if k in _COL_SHARDED:
            d, cols = full.shape
            parts.append(full.reshape(d, N_CHIPS, cols // N_CHIPS).transpose(1, 0, 2).reshape(N_CHIPS, -1))
        else:
            parts.append(full.reshape(N_CHIPS, -1))
    return jnp.concatenate(parts, axis=1)


def _from_shards(k, flat4, like):
    l = like.shape[0]
    if k in _COL_SHARDED:
        d, cs = like.shape[1:]
        return flat4.reshape(N_CHIPS, l, d, cs).transpose(1, 2, 0, 3).reshape(l, d, N_CHIPS * cs)
    rs, d = like.shape[1:]
    return flat4.reshape(N_CHIPS, l, rs, d).transpose(1, 0, 2, 3).reshape(l, N_CHIPS * rs, d)


def _pad_rows(flat, mult=8):
    n = flat.shape[0]
    rows = -(-n // LANES)
    rows = -(-rows // mult) * mult
    return jnp.concatenate([flat, jnp.zeros((rows * LANES - n,), F32)]).reshape(rows, LANES)


def kernel(x, c, ctx, c_ctx, w_mod, b_mod, g_pre, g_post, w_in, conv_w, conv_b, ln_g, ln_b, w_conv_out, q_norm_g, k_norm_g, w_attn_out, w_out, loss_target, m_c_ctx, m_w_mod, m_b_mod, m_g_pre, m_g_post, m_w_in, m_conv_w, m_conv_b, m_ln_g, m_ln_b, m_w_conv_out, m_q_norm_g, m_k_norm_g, m_w_attn_out, m_w_out, v_c_ctx, v_w_mod, v_b_mod, v_g_pre, v_g_post, v_w_in, v_conv_w, v_conv_b, v_ln_g, v_ln_b, v_w_conv_out, v_q_norm_g, v_k_norm_g, v_w_attn_out, v_w_out):
    w = dict(c_ctx=c_ctx, w_mod=w_mod, b_mod=b_mod, g_pre=g_pre, g_post=g_post, w_in=w_in, conv_w=conv_w, conv_b=conv_b,
             ln_g=ln_g, ln_b=ln_b, w_conv_out=w_conv_out, q_norm_g=q_norm_g, k_norm_g=k_norm_g, w_attn_out=w_attn_out, w_out=w_out)
    m = dict(c_ctx=m_c_ctx, w_mod=m_w_mod, b_mod=m_b_mod, g_pre=m_g_pre, g_post=m_g_post, w_in=m_w_in, conv_w=m_conv_w,
             conv_b=m_conv_b, ln_g=m_ln_g, ln_b=m_ln_b, w_conv_out=m_w_conv_out, q_norm_g=m_q_norm_g, k_norm_g=m_k_norm_g,
             w_attn_out=m_w_attn_out, w_out=m_w_out)
    v = dict(c_ctx=v_c_ctx, w_mod=v_w_mod, b_mod=v_b_mod, g_pre=v_g_pre, g_post=v_g_post, w_in=v_w_in, conv_w=v_conv_w,
             conv_b=v_conv_b, ln_g=v_ln_g, ln_b=v_ln_b, w_conv_out=v_w_conv_out, q_norm_g=v_q_norm_g, k_norm_g=v_k_norm_g,
             w_attn_out=v_w_attn_out, w_out=v_w_out)
    order = ("c_ctx", "w_mod", "b_mod", "g_pre", "g_post", "w_in", "conv_w", "conv_b", "ln_g", "ln_b", "w_conv_out",
             "q_norm_g", "k_norm_g", "w_attn_out", "w_out")
    my_chip = 2 * lax.axis_index("x") + lax.axis_index("y")
    my_core = lax.axis_index("c")
    depth, _, conv_cs = conv_w.shape
    conv_c = conv_cs * N_CHIPS
    dm = _dims(x[0], ctx[0], w_in.shape[2] * N_CHIPS, conv_c, w_attn_out.shape[1] * N_CHIPS, q_norm_g.shape[1])

    own = jnp.concatenate([w[k].astype(BF16).reshape(-1) for k in _BIG])
    rows2 = own.shape[0] // FLAT_W
    own = own.reshape(2, rows2 // 2, FLAT_W)
    gathered = _allgather_chips(own, "gather_weights").reshape(N_CHIPS, -1)
    wf, off = {}, 0
    for k in _BIG:
        n = int(np.prod(w[k].shape))
        wf[k] = _from_shards(k, gathered[:, off:off + n], w[k])
        off += n
    cw = jnp.zeros((depth, CONV_KERNEL, N_CHIPS, conv_cs), F32)
    cw = lax.dynamic_update_slice(cw, conv_w[:, :, None, :], (0, 0, my_chip, 0))
    cw = jnp.where(my_core == 0, cw, 0.0).reshape(-1)
    wf["conv_w"] = _allreduce_small(_pad_rows(cw), "gather_conv_w").reshape(-1)[:cw.shape[0]].reshape(depth, CONV_KERNEL, conv_c)
    for k in _SMALL:
        wf[k] = w[k]

    loss_part, grad_x, gw = _local_step(dm, x[0], c, ctx[0], c_ctx, loss_target[0], wf)

    g4 = jnp.concatenate([_by_shard(k, gw[k]) for k in _BIG], axis=1)
    hrows = rows2 // 2
    g24 = g4.reshape(N_CHIPS, 2, hrows, FLAT_W).transpose(1, 0, 2, 3).reshape(2, N_CHIPS * hrows, FLAT_W)
    core_i = my_core.astype(jnp.int32).reshape(1)
    chip_i = my_chip.astype(jnp.int32).reshape(1)
    theirs = _pair_exchange(g24, "reduce_pair")
    chip_sum, chip_sum_bf = _sum_pair(g24, theirs, core_i, "reduce_pair_add")
    others = _scatter_chips(chip_sum_bf.reshape(N_CHIPS, hrows, FLAT_W), "reduce_chips")
    mine = _sum_chips(chip_sum.reshape(N_CHIPS, hrows, FLAT_W), others, chip_i, "reduce_chips_add")
    sibling_half = _pair_swap(mine, "reduce_share")

    small = jnp.concatenate([gw[k].reshape(-1) for k in _SMALL] + [gw["conv_w"].reshape(-1), loss_part.reshape(1)])
    small_sum = _allreduce_small(_pad_rows(small), "reduce_small").reshape(-1)
    loss = small_sum[small.shape[0] - 1]
    gs, off = {}, 0
    for k in _SMALL:
        n = int(np.prod(w[k].shape))
        gs[k] = small_sum[off:off + n].reshape(w[k].shape)
        off += n
    gcw = small_sum[off:off + depth * CONV_KERNEL * conv_c].reshape(depth, CONV_KERNEL, N_CHIPS, conv_cs)
    gs["conv_w"] = lax.dynamic_index_in_dim(gcw, my_chip, axis=2, keepdims=False)

    g_flat = jnp.where(my_core == 0, jnp.concatenate([mine, sibling_half]), jnp.concatenate([sibling_half, mine]))
    grads = _unflatten_shard(g_flat, w)
    delta, new_m, new_v = {}, {}, {}
    for k in _BIG:
        as2d = lambda t: t.reshape(-1, t.shape[-1])
        d_k, m_k, v_k = _adamw(as2d(w[k]), as2d(grads[k]), as2d(m[k]), as2d(v[k]), "adamw_" + k)
        delta[k], new_m[k], new_v[k] = d_k.reshape(w[k].shape), m_k.reshape(w[k].shape), v_k.reshape(w[k].shape)
    small_keys = _SMALL + ("conv_w",)
    join = lambda t: _pad_rows(jnp.concatenate([t[k].reshape(-1) for k in small_keys]))
    d_s, m_s, v_s = _adamw(join(w), join(gs), join(m), join(v), "adamw_small")
    off = 0
    for k in small_keys:
        n = int(np.prod(w[k].shape))
        grads[k] = gs[k]
        for dst, src in ((delta, d_s), (new_m, m_s), (new_v, v_s)):
            dst[k] = src.reshape(-1)[off:off + n].reshape(w[k].shape)
        off += n
    return (loss, grad_x[None], *[grads[k] for k in order], *[delta[k] for k in order],
            *[new_m[k] for k in order], *[new_v[k] for k in order])
```
